```python
import math
import jax, jax.numpy as jnp
from jax import lax
import numpy as np

D_MODEL = 1024
BATCH = 2
SEQ = 8192
DEPTH = 1

SB_HEADS = 8
SB_HEAD_DIM = 64
SB_WIDTH = SB_HEADS * SB_HEAD_DIM
POOL_WINDOWS = (2, 4, 8, 16)
N_POOL_GROUPS = len(POOL_WINDOWS)
POOL_WIDTH = D_MODEL // 2
POOL_GROUP_DIM = POOL_WIDTH // N_POOL_GROUPS
MIX_WIDTH = SB_WIDTH + POOL_WIDTH
IN_PROJ_WIDTH = 3 * SB_WIDTH + POOL_WIDTH
Q_BLOCK = 128
N_EXPERTS = 32
TOP_K = 4
D_EXPERT = D_MODEL
SWIGLU_LIMIT = 7.0
SWIGLU_ALPHA = 1.702
EXPERT_BLOCK = 128
RMS_EPS = 1e-5

kernel_name = "hybrid_stickbreak_pool_moe"


def rms_norm(x, g):
    xf = x.astype(jnp.float32)
    y = xf * lax.rsqrt(jnp.mean(xf * xf, axis=-1, keepdims=True) + RMS_EPS)
    return (y * g.astype(jnp.float32)).astype(x.dtype)


def stick_breaking_attention(q, k, v):
    B, H, S, Dh = q.shape
    n_blocks = S // Q_BLOCK
    scale = 1.0 / math.sqrt(Dh)
    q_blocks = q.reshape(B, H, n_blocks, Q_BLOCK, Dh).transpose(2, 0, 1, 3, 4)
    kf = k.astype(jnp.float32)
    vf = v.astype(jnp.float32)
    key_pos = jnp.arange(S, dtype=jnp.int32)

    def one_block(args):
        qi, bi = args
        z = jnp.einsum('bhqd,bhkd->bhqk', qi.astype(jnp.float32), kf) * scale
        q_pos = bi * Q_BLOCK + jnp.arange(Q_BLOCK, dtype=jnp.int32)
        causal = key_pos[None, :] < q_pos[:, None]
        log_1m_beta = jnp.where(causal, jax.nn.log_sigmoid(-z), 0.0)
        shifted = jnp.pad(log_1m_beta[..., 1:], ((0, 0), (0, 0), (0, 0), (0, 1)))
        tail = lax.cumsum(shifted, axis=3, reverse=True)
        weights = jnp.where(causal, jnp.exp(jax.nn.log_sigmoid(z) + tail), 0.0)
        return jnp.einsum('bhqk,bhkd->bhqd', weights, vf)

    out = lax.map(one_block, (q_blocks, jnp.arange(n_blocks, dtype=jnp.int32)))
    out = out.transpose(1, 0, 3, 2, 4).reshape(B, S, H * Dh)
    return out.astype(q.dtype)


def multiscale_pool(u, pool_w, pool_scale):
    B, S, C = u.shape
    uf = u.astype(jnp.float32)
    csum = jnp.pad(jnp.cumsum(uf, axis=1), ((0, 0), (1, 0), (0, 0)))
    t = jnp.arange(S, dtype=jnp.int32)
    groups = []
    for g, w in enumerate(POOL_WINDOWS):
        sl = slice(g * POOL_GROUP_DIM, (g + 1) * POOL_GROUP_DIM)
        start = jnp.maximum(t + 1 - w, 0)
        count = (t + 1 - start).astype(jnp.float32)
        window_sum = csum[:, 1:, sl] - csum[:, start, sl]
        groups.append(window_sum / count[None, :, None] - uf[:, :, sl])
    pooled = jnp.stack(groups, axis=2)
    mixed = jnp.einsum('bsgc,gcd->bsgd', pooled, pool_w.astype(jnp.float32))
    out = mixed.reshape(B, S, C) * pool_scale.astype(jnp.float32)
    return out.astype(u.dtype)


def moe_ffn(h, w_router, b_router, w_gate_up, b_gate_up, w_down, b_down):
    B, S, D = h.shape
    T = B * S
    hf = h.reshape(T, D)
    logits = hf.astype(jnp.float32) @ w_router.astype(jnp.float32) + b_router.astype(jnp.float32)
    top_vals, top_idx = lax.top_k(logits, TOP_K)
    gates = jax.nn.softmax(top_vals, axis=-1).astype(h.dtype)

    TK = T * TOP_K
    expert_flat = top_idx.reshape(TK).astype(jnp.int32)
    token_flat = jnp.arange(TK, dtype=jnp.int32) // TOP_K
    gate_flat = gates.reshape(TK)
    order = jnp.argsort(expert_flat, stable=True)
    sorted_expert = expert_flat[order]
    counts = jnp.bincount(expert_flat, length=N_EXPERTS).astype(jnp.int32)
    offsets = jnp.cumsum(counts) - counts
    padded_counts = ((counts + EXPERT_BLOCK - 1) // EXPERT_BLOCK) * EXPERT_BLOCK
    padded_end = jnp.cumsum(padded_counts)
    padded_start = padded_end - padded_counts
    dest = padded_start[sorted_expert] + jnp.arange(TK, dtype=jnp.int32) - offsets[sorted_expert]

    n_blocks = -(-(TK + N_EXPERTS * (EXPERT_BLOCK - 1)) // EXPERT_BLOCK)
    n_pad = n_blocks * EXPERT_BLOCK
    row_token = jnp.zeros((n_pad,), jnp.int32).at[dest].set(token_flat[order])
    row_gate = jnp.zeros((n_pad,), h.dtype).at[dest].set(gate_flat[order])
    x_disp = hf[row_token].reshape(n_blocks, EXPERT_BLOCK, D)
    block_start = jnp.arange(n_blocks, dtype=jnp.int32) * EXPERT_BLOCK
    block_expert = jnp.minimum(
        jnp.sum(block_start[:, None] >= padded_end[None, :], axis=1), N_EXPERTS - 1
    ).astype(jnp.int32)

    def expert_block(args):
        xb, e = args
        gu = xb @ w_gate_up[e] + b_gate_up[e]
        gate = jnp.minimum(gu[:, :D_EXPERT], SWIGLU_LIMIT)
        up = jnp.clip(gu[:, D_EXPERT:], -SWIGLU_LIMIT, SWIGLU_LIMIT)
        act = gate * jax.nn.sigmoid(SWIGLU_ALPHA * gate) * (up + 1.0)
        return act @ w_down[e] + b_down[e]

    y = lax.map(expert_block, (x_disp, block_expert)).reshape(n_pad, D)
    out = jax.ops.segment_sum(y * row_gate[:, None], row_token, num_segments=T)
    return out.reshape(B, S, D).astype(h.dtype)


def setup_inputs(seed: int = 0) -> dict:
    key = jax.random.key(seed)
    ks = jax.random.split(key, 16)
    f32 = jnp.float32
    L = DEPTH
    x = jax.random.normal(ks[0], (BATCH, SEQ, D_MODEL), f32)
    g_mix = 1.0 + 0.05 * jax.random.normal(ks[1], (L, D_MODEL), f32)
    w_in = jax.random.normal(ks[2], (L, D_MODEL, IN_PROJ_WIDTH), f32) * D_MODEL ** -0.5
    pool_w = jax.random.normal(ks[3], (L, N_POOL_GROUPS, POOL_GROUP_DIM, POOL_GROUP_DIM), f32) * POOL_GROUP_DIM ** -0.5
    pool_scale = 1.0 + 0.1 * jax.random.normal(ks[4], (L, POOL_WIDTH), f32)
    w_out = jax.random.normal(ks[5], (L, MIX_WIDTH, D_MODEL), f32) * MIX_WIDTH ** -0.5
    g_moe = 1.0 + 0.05 * jax.random.normal(ks[6], (L, D_MODEL), f32)
    w_router = jax.random.normal(ks[7], (L, D_MODEL, N_EXPERTS), f32) * D_MODEL ** -0.5
    b_router = 0.01 * jax.random.normal(ks[8], (L, N_EXPERTS), f32)
    w_gate_up = jax.random.normal(ks[9], (L, N_EXPERTS, D_MODEL, 2 * D_EXPERT), f32) * D_MODEL ** -0.5
    b_gate_up = 0.02 * jax.random.normal(ks[10], (L, N_EXPERTS, 2 * D_EXPERT), f32)
    w_down = jax.random.normal(ks[11], (L, N_EXPERTS, D_EXPERT, D_MODEL), f32) * D_EXPERT ** -0.5
    b_down = 0.02 * jax.random.normal(ks[12], (L, N_EXPERTS, D_MODEL), f32)
    g_final = 1.0 + 0.05 * jax.random.normal(ks[13], (D_MODEL,), f32)
    return {"x": x, "g_mix": g_mix, "w_in": w_in, "pool_w": pool_w,
            "pool_scale": pool_scale, "w_out": w_out, "g_moe": g_moe,
            "w_router": w_router, "b_router": b_router, "w_gate_up": w_gate_up,
            "b_gate_up": b_gate_up, "w_down": w_down, "b_down": b_down,
            "g_final": g_final}


def reference(x, g_mix, w_in, pool_w, pool_scale, w_out, g_moe, w_router,
              b_router, w_gate_up, b_gate_up, w_down, b_down, g_final):
    B, S, D = x.shape
    for l in range(DEPTH):
        h = rms_norm(x, g_mix[l])
        proj = h @ w_in[l]
        q = proj[..., 0:SB_WIDTH]
        k = proj[..., SB_WIDTH:2 * SB_WIDTH]
        v = proj[..., 2 * SB_WIDTH:3 * SB_WIDTH]
        u = proj[..., 3 * SB_WIDTH:]
        to_heads = lambda t: t.reshape(B, S, SB_HEADS, SB_HEAD_DIM).transpose(0, 2, 1, 3)
        attn_out = stick_breaking_attention(to_heads(q), to_heads(k), to_heads(v))
        pool_out = multiscale_pool(u, pool_w[l], pool_scale[l])
        mixed = jnp.concatenate([attn_out, pool_out], axis=-1)
        x = x + mixed @ w_out[l]
        x = x + moe_ffn(rms_norm(x, g_moe[l]), w_router[l], b_router[l],
                        w_gate_up[l], b_gate_up[l], w_down[l], b_down[l])
    return rms_norm(x, g_final)
```

```python
import functools

import jax
import jax.numpy as jnp
from jax import lax
from jax.experimental import pallas as pl
from jax.experimental.pallas import tpu as pltpu

F32 = jnp.float32
BF16 = jnp.bfloat16
I32 = jnp.int32

D_MODEL = 1024
SB_HEADS = 8
SB_HEAD_DIM = 64
SB_WIDTH = SB_HEADS * SB_HEAD_DIM
POOL_WINDOWS = (2, 4, 8, 16)
POOL_WIDTH = 512
POOL_GROUP_DIM = 128
N_EXPERTS = 32
TOP_K = 4
D_EXPERT = 1024
SWIGLU_LIMIT = 7.0
SWIGLU_ALPHA = 1.702
RMS_EPS = 1e-5

LANES = 128
HALO = 16
PROJ_TILE = 512
ATTN_TILE = 256
EXPERT_TILE = 256
COMBINE_TILE = 256
ATTN_SKIP_LOG = -112.0
VMEM_LIMIT = 56 * 1024 * 1024

_NT = (((1,), (1,)), ((), ()))


def _rms(x, g):
    ms = jnp.mean(x * x, axis=-1, keepdims=True)
    return x * lax.rsqrt(ms + RMS_EPS) * g


def _in_proj_kernel(x_ref, g_ref, w_ref, pw_ref, ps_ref, q_ref, k_ref, v_ref, p_ref, uext_ref):
    s = pl.program_id(1)
    tm = x_ref.shape[1]
    h = _rms(x_ref[0], g_ref[...])
    proj = jnp.dot(h.astype(BF16), w_ref[...], preferred_element_type=F32)
    q_ref[0] = (proj[:, 0:SB_WIDTH] * (SB_HEAD_DIM ** -0.5)).astype(BF16)
    k_ref[0] = proj[:, SB_WIDTH:2 * SB_WIDTH].astype(BF16)
    v_ref[0] = proj[:, 2 * SB_WIDTH:3 * SB_WIDTH].astype(BF16)
    u = proj[:, 3 * SB_WIDTH:]

    @pl.when(s == 0)
    def _():
        uext_ref[0:HALO, :] = jnp.zeros((HALO, POOL_WIDTH), F32)

    uext_ref[HALO:, :] = u
    t = s * tm + lax.broadcasted_iota(I32, (tm, 1), 0)
    for g, w in enumerate(POOL_WINDOWS):
        sl = slice(g * POOL_GROUP_DIM, (g + 1) * POOL_GROUP_DIM)
        ug = u[:, sl]
        acc = ug
        for i in range(1, w):
            acc = acc + uext_ref[HALO - i:HALO - i + tm, sl]
        count = jnp.minimum(t + 1, w).astype(F32)
        pooled = acc / count - ug
        mixed = jnp.dot(pooled.astype(BF16), pw_ref[g], preferred_element_type=F32)
        p_ref[0, :, sl] = (mixed * ps_ref[:, sl]).astype(BF16)
    uext_ref[0:HALO, :] = u[tm - HALO:, :]


def _in_proj(x, g_mix, w_in, pool_w, pool_scale):
    B, S, D = x.shape
    tm = PROJ_TILE
    n_out = w_in.shape[1]
    out_sd = jax.ShapeDtypeStruct((B, S, SB_WIDTH), BF16)
    blk = pl.BlockSpec((1, tm, SB_WIDTH), lambda b, s: (b, s, 0))
    return pl.pallas_call(
        _in_proj_kernel,
        grid=(B, S // tm),
        in_specs=[
            pl.BlockSpec((1, tm, D), lambda b, s: (b, s, 0)),
            pl.BlockSpec((1, D), lambda b, s: (0, 0)),
            pl.BlockSpec((D, n_out), lambda b, s: (0, 0)),
            pl.BlockSpec(pool_w.shape, lambda b, s: (0, 0, 0)),
            pl.BlockSpec((1, POOL_WIDTH), lambda b, s: (0, 0)),
        ],
        out_specs=[blk, blk, blk, blk],
        out_shape=[out_sd, out_sd, out_sd, out_sd],
        scratch_shapes=[pltpu.VMEM((HALO + tm, POOL_WIDTH), F32)],
        compiler_params=pltpu.CompilerParams(
            dimension_semantics=("arbitrary", "arbitrary"), vmem_limit_bytes=VMEM_LIMIT),
        name="in_proj_pool",
    )(x, g_mix.reshape(1, D), w_in.astype(BF16), pool_w.astype(BF16), pool_scale.reshape(1, POOL_WIDTH))


def _attn_kernel(q_ref, k_ref, v_ref, o_ref):
    tq = q_ref.shape[1]
    tk = tq
    qi = pl.program_id(2)
    q = q_ref[0]
    lane = lax.broadcasted_iota(I32, (tq, LANES), 1)
    row = lax.broadcasted_iota(I32, (tq, tk), 0)
    col = lax.broadcasted_iota(I32, (tq, tk), 1)
    causal = col < row
    tri = (row > col).astype(BF16)

    def tile(qh, j, carry, acc, diag):
        start = pl.multiple_of(j * tk, tk)
        kb = k_ref[0, pl.ds(start, tk), :]
        vb = v_ref[0, pl.ds(start, tk), :]
        z = lax.dot_general(qh, kb, _NT, preferred_element_type=F32)
        lsm = -(jnp.maximum(z, 0.0) + jnp.log(1.0 + jnp.exp(-jnp.abs(z))))
        if diag:
            lsm = jnp.where(causal, lsm, 0.0)
        tail = jnp.dot(lsm.astype(BF16), tri, preferred_element_type=F32)
        w = jnp.exp(lsm + z + tail + carry)
        if diag:
            w = jnp.where(causal, w, 0.0)
        acc = acc + jnp.dot(w.astype(BF16), vb, preferred_element_type=F32)
        carry = carry + tail[:, 0:1] + lsm[:, 0:1]
        return carry, acc

    accs = []
    for h in range(2):
        in_head = (lane >= h * SB_HEAD_DIM) & (lane < (h + 1) * SB_HEAD_DIM)
        qh = jnp.where(in_head, q, jnp.zeros_like(q))
        carry, acc = tile(qh, qi, jnp.zeros((tq, 1), F32), jnp.zeros((tq, LANES), F32), True)

        def cond(st):
            j, c, _ = st
            return jnp.logical_and(j >= 0, jnp.max(c) > ATTN_SKIP_LOG)

        def body(st, qh=qh):
            j, c, a = st
            c, a = tile(qh, j, c, a, False)
            return j - 1, c, a

        _, _, acc = lax.while_loop(cond, body, (qi - 1, carry, acc))
        accs.append(acc)
    o_ref[0] = jnp.where(lane < SB_HEAD_DIM, accs[0], accs[1]).astype(o_ref.dtype)


def _attention(q, k, v):
    B, S, W = q.shape
    tq = ATTN_TILE
    n_pairs = W // LANES
    return pl.pallas_call(
        _attn_kernel,
        grid=(B, n_pairs, S // tq),
        in_specs=[
            pl.BlockSpec((1, tq, LANES), lambda b, p, i: (b, i, p)),
            pl.BlockSpec((1, S, LANES), lambda b, p, i: (b, 0, p)),
            pl.BlockSpec((1, S, LANES), lambda b, p, i: (b, 0, p)),
        ],
        out_specs=pl.BlockSpec((1, tq, LANES), lambda b, p, i: (b, i, p)),
        out_shape=jax.ShapeDtypeStruct((B, S, W), BF16),
        compiler_params=pltpu.CompilerParams(
            dimension_semantics=("arbitrary", "arbitrary", "arbitrary"), vmem_limit_bytes=VMEM_LIMIT),
        name="stickbreak_attn",
    )(q, k, v)


def _out_proj_router_kernel(attn_ref, pool_ref, x_ref, wo_ref, g_ref, wr_ref, br_ref, tri_ref,
                            x1_ref, h2_ref, idx_ref, gate_ref, rank_ref, cnt_ref, carry_ref):
    i = pl.program_id(0)
    tm = x_ref.shape[0]

    @pl.when(i == 0)
    def _():
        carry_ref[...] = jnp.zeros_like(carry_ref)

    mixed = jnp.dot(attn_ref[...], wo_ref[0:SB_WIDTH, :], preferred_element_type=F32)
    mixed = mixed + jnp.dot(pool_ref[...], wo_ref[SB_WIDTH:, :], preferred_element_type=F32)
    x1 = x_ref[...] + mixed
    h2 = _rms(x1, g_ref[...])
    x1_ref[...] = x1
    h2_ref[...] = h2

    hh = h2.astype(BF16)
    hl = (h2 - hh.astype(F32)).astype(BF16)
    wr = wr_ref[...]
    wh = wr.astype(BF16)
    wl = (wr - wh.astype(F32)).astype(BF16)
    logits = (lax.dot_general(wh, hh, _NT, preferred_element_type=F32)
              + lax.dot_general(wh, hl, _NT, preferred_element_type=F32)
              + lax.dot_general(wl, hh, _NT, preferred_element_type=F32)) + br_ref[...]

    eid = lax.broadcasted_iota(I32, (N_EXPERTS, tm), 0).astype(F32)
    work = logits
    vals, ids = [], []
    for _ in range(TOP_K):
        m = jnp.max(work, axis=0, keepdims=True)
        sel = jnp.min(jnp.where(work == m, eid, float(N_EXPERTS)), axis=0, keepdims=True)
        vals.append(m)
        ids.append(sel)
        work = jnp.where(eid == sel, -jnp.inf, work)
    exps = [jnp.exp(v - vals[0]) for v in vals]
    denom = exps[0] + exps[1] + exps[2] + exps[3]

    onehot = jnp.zeros((N_EXPERTS, tm), F32)
    for sel in ids:
        onehot = onehot + (eid == sel).astype(F32)
    before = jnp.dot(onehot.astype(BF16), tri_ref[...], preferred_element_type=F32) + carry_ref[...]
    for kk in range(TOP_K):
        idx_ref[kk:kk + 1, :] = ids[kk].astype(I32)
        gate_ref[kk:kk + 1, :] = exps[kk] / denom
        rk = jnp.sum(jnp.where(eid == ids[kk], before, 0.0), axis=0, keepdims=True)
        rank_ref[kk:kk + 1, :] = rk.astype(I32)
    carry_ref[...] = carry_ref[...] + jnp.sum(onehot, axis=1, keepdims=True)
    cnt_ref[...] = jnp.broadcast_to(carry_ref[...], cnt_ref.shape)


def _out_proj_router(attn, pool, x, w_out, g_moe, w_router, b_router):
    T, D = x.shape
    tm = PROJ_TILE
    r = lax.broadcasted_iota(I32, (tm, tm), 0)
    c = lax.broadcasted_iota(I32, (tm, tm), 1)
    tri = (r < c).astype(BF16)
    row_blk = lambda w: pl.BlockSpec((tm, w), lambda i: (i, 0))
    fixed = lambda shape: pl.BlockSpec(shape, lambda i: tuple(0 for _ in shape))
    sel_blk = pl.BlockSpec((TOP_K, tm), lambda i: (0, i))
    return pl.pallas_call(
        _out_proj_router_kernel,
        grid=(T // tm,),
        in_specs=[row_blk(SB_WIDTH), row_blk(POOL_WIDTH), row_blk(D), fixed((D, D)), fixed((1, D)),
                  fixed((N_EXPERTS, D)), fixed((N_EXPERTS, 1)), fixed((tm, tm))],
        out_specs=[row_blk(D), row_blk(D), sel_blk, sel_blk, sel_blk, fixed((N_EXPERTS, LANES))],
        out_shape=[jax.ShapeDtypeStruct((T, D), F32), jax.ShapeDtypeStruct((T, D), F32),
                   jax.ShapeDtypeStruct((TOP_K, T), I32), jax.ShapeDtypeStruct((TOP_K, T), F32),
                   jax.ShapeDtypeStruct((TOP_K, T), I32), jax.ShapeDtypeStruct((N_EXPERTS, LANES), F32)],
        scratch_shapes=[pltpu.VMEM((N_EXPERTS, 1), F32)],
        compiler_params=pltpu.CompilerParams(
            dimension_semantics=("arbitrary",), vmem_limit_bytes=VMEM_LIMIT),
        name="out_proj_router",
    )(attn, pool, x, w_out.astype(BF16), g_moe.reshape(1, D), w_router.T, b_router.reshape(N_EXPERTS, 1), tri)


def _dispatch_kernel(dest_ref, h_ref, out_ref, sem):
    tm = h_ref.shape[0]

    def issue(r, _):
        for kk in range(TOP_K):
            pltpu.make_async_copy(h_ref.at[pl.ds(r, 1)], out_ref.at[pl.ds(dest_ref[kk, r], 1)], sem).start()
        return 0

    lax.fori_loop(0, tm, issue, 0)
    for _ in range(TOP_K):
        pltpu.make_async_copy(h_ref, out_ref.at[pl.ds(0, tm)], sem).wait()


def _dispatch(h2, dest):
    T, D = h2.shape
    tm = PROJ_TILE
    dest3 = dest.reshape(TOP_K, T // tm, tm).transpose(1, 0, 2)
    return pl.pallas_call(
        _dispatch_kernel,
        grid=(T // tm,),
        in_specs=[pl.BlockSpec((None, TOP_K, tm), lambda i: (i, 0, 0), memory_space=pltpu.SMEM),
                  pl.BlockSpec((tm, D), lambda i: (i, 0))],
        out_specs=pl.BlockSpec(memory_space=pl.ANY),
        out_shape=jax.ShapeDtypeStruct((T * TOP_K, D), F32),
        scratch_shapes=[pltpu.SemaphoreType.DMA],
        compiler_params=pltpu.CompilerParams(
            dimension_semantics=("arbitrary",), vmem_limit_bytes=VMEM_LIMIT),
        name="dispatch_rows",
    )(dest3, h2)


def _expert_kernel(vb_ref, ve_ref, vlo_ref, vhi_ref, x_ref, wgu_ref, bgu_ref, wd_ref, bd_ref, y_ref):
    v = pl.program_id(0)
    tm = x_ref.shape[0]
    lo = vlo_ref[v]
    hi = vhi_ref[v]
    blk = vb_ref[v]
    first = jnp.logical_or(v == 0, blk != vb_ref[jnp.maximum(v - 1, 0)])

    @pl.when(hi > lo)
    def _():
        gu = jnp.dot(x_ref[...].astype(BF16), wgu_ref[0], preferred_element_type=F32) + bgu_ref[0]
        gate = jnp.minimum(gu[:, :D_EXPERT], SWIGLU_LIMIT)
        up = jnp.clip(gu[:, D_EXPERT:], -SWIGLU_LIMIT, SWIGLU_LIMIT)
        act = gate * jax.nn.sigmoid(SWIGLU_ALPHA * gate) * (up + 1.0)
        y = jnp.dot(act.astype(BF16), wd_ref[0], preferred_element_type=F32) + bd_ref[0]
        rows = blk * tm + lax.broadcasted_iota(I32, (tm, 1), 0)
        valid = jnp.logical_and(rows >= lo, rows < hi)

        @pl.when(first)
        def _():
            y_ref[...] = jnp.where(valid, y, 0.0)

        @pl.when(jnp.logical_not(first))
        def _():
            y_ref[...] = jnp.where(valid, y, y_ref[...])


def _experts(x_sorted, visits, w_gate_up, b_gate_up, w_down, b_down):
    n_rows, D = x_sorted.shape
    tm = EXPERT_TILE
    vb, ve, vlo, vhi = visits
    n_visits = vb.shape[0]
    grid_spec = pltpu.PrefetchScalarGridSpec(
        num_scalar_prefetch=4,
        grid=(n_visits,),
        in_specs=[
            pl.BlockSpec((tm, D), lambda v, vb, ve, lo, hi: (vb[v], 0)),
            pl.BlockSpec((1, D, 2 * D_EXPERT), lambda v, vb, ve, lo, hi: (ve[v], 0, 0)),
            pl.BlockSpec((1, 1, 2 * D_EXPERT), lambda v, vb, ve, lo, hi: (ve[v], 0, 0)),
            pl.BlockSpec((1, D_EXPERT, D), lambda v, vb, ve, lo, hi: (ve[v], 0, 0)),
            pl.BlockSpec((1, 1, D), lambda v, vb, ve, lo, hi: (ve[v], 0, 0)),
        ],
        out_specs=pl.BlockSpec((tm, D), lambda v, vb, ve, lo, hi: (vb[v], 0)),
    )
    return pl.pallas_call(
        _expert_kernel,
        grid_spec=grid_spec,
        out_shape=jax.ShapeDtypeStruct((n_rows, D), F32),
        compiler_params=pltpu.CompilerParams(
            dimension_semantics=("arbitrary",), vmem_limit_bytes=VMEM_LIMIT),
        name="expert_gmm",
    )(vb, ve, vlo, vhi, x_sorted, w_gate_up.astype(BF16), b_gate_up.reshape(N_EXPERTS, 1, -1),
      w_down.astype(BF16), b_down.reshape(N_EXPERTS, 1, -1))


def _visit_schedule(counts, n_rows):
    tm = EXPERT_TILE
    n_blocks = n_rows // tm
    n_visits = n_blocks + N_EXPERTS - 1
    ends = jnp.cumsum(counts)
    starts = ends - counts
    first_blk = starts // tm
    last_blk = jnp.maximum(ends - 1, 0) // tm
    per_e = jnp.where(counts > 0, last_blk - first_blk + 1, 0)
    vend = jnp.cumsum(per_e)
    vstart = vend - per_e
    total = vend[-1]
    v = jnp.arange(n_visits, dtype=I32)
    vc = jnp.minimum(v, total - 1)
    e = jnp.sum(vc[:, None] >= vend[None, :], axis=1).astype(I32)
    blk = first_blk[e] + vc - vstart[e]
    lo = jnp.maximum(starts[e], blk * tm)
    hi = jnp.minimum(ends[e], (blk + 1) * tm)
    real = v < total
    lo = jnp.where(real, lo, 0)
    hi = jnp.where(real, hi, 0)
    return blk.astype(I32), e, lo.astype(I32), hi.astype(I32)


def _combine_kernel(dest_ref, y_ref, x1_ref, gate_ref, g_ref, o_ref, buf_ref, sem):
    tm = x1_ref.shape[0]

    def issue(r, _):
        for kk in range(TOP_K):
            pltpu.make_async_copy(y_ref.at[pl.ds(dest_ref[kk, r], 1)], buf_ref.at[kk, pl.ds(r, 1)], sem).start()
        return 0

    lax.fori_loop(0, tm, issue, 0)
    for kk in range(TOP_K):
        pltpu.make_async_copy(y_ref.at[pl.ds(0, tm)], buf_ref.at[kk], sem).wait()
    acc = x1_ref[...]
    gates = gate_ref[...]
    for kk in range(TOP_K):
        acc = acc + gates[:, kk:kk + 1] * buf_ref[kk]
    o_ref[...] = _rms(acc, g_ref[...])


def _combine(y, dest, x1, gates_t, g_final):
    T, D = x1.shape
    tm = COMBINE_TILE
    dest3 = dest.reshape(TOP_K, T // tm, tm).transpose(1, 0, 2)
    return pl.pallas_call(
        _combine_kernel,
        grid=(T // tm,),
        in_specs=[pl.BlockSpec((None, TOP_K, tm), lambda i: (i, 0, 0), memory_space=pltpu.SMEM),
                  pl.BlockSpec(memory_space=pl.ANY),
                  pl.BlockSpec((tm, D), lambda i: (i, 0)),
                  pl.BlockSpec((tm, TOP_K), lambda i: (i, 0)),
                  pl.BlockSpec((1, D), lambda i: (0, 0))],
        out_specs=pl.BlockSpec((tm, D), lambda i: (i, 0)),
        out_shape=jax.ShapeDtypeStruct((T, D), F32),
        scratch_shapes=[pltpu.VMEM((TOP_K, tm, D), F32), pltpu.SemaphoreType.DMA],
        compiler_params=pltpu.CompilerParams(
            dimension_semantics=("arbitrary",), vmem_limit_bytes=VMEM_LIMIT),
        name="combine_norm",
    )(dest3, y, x1, gates_t, g_final.reshape(1, D))


def kernel(x, g_mix, w_in, pool_w, pool_scale, w_out, g_moe, w_router, b_router, w_gate_up, b_gate_up,
           w_down, b_down, g_final):
    B, S, D = x.shape
    T = B * S
    assert g_mix.shape[0] == 1, "single-layer problem: the final norm is fused into the combine step"
    q, k, v, pool = _in_proj(x, g_mix[0], w_in[0], pool_w[0], pool_scale[0])
    attn = _attention(q, k, v)
    x1, h2, idx, gates, rank, cnt = _out_proj_router(
        attn.reshape(T, SB_WIDTH), pool.reshape(T, POOL_WIDTH), x.reshape(T, D),
        w_out[0], g_moe[0], w_router[0], b_router[0])
    counts = cnt[:, 0].astype(I32)
    starts = jnp.cumsum(counts) - counts
    dest = rank + jnp.sum(jnp.where(idx[:, :, None] == jnp.arange(N_EXPERTS, dtype=I32), starts, 0), axis=-1)
    x_sorted = _dispatch(h2, dest)
    visits = _visit_schedule(counts, T * TOP_K)
    y = _experts(x_sorted, visits, w_gate_up[0], b_gate_up[0], w_down[0], b_down[0])
    return _combine(y, dest, x1, gates.T, g_final).reshape(B, S, D)
```

```python
import functools

import jax
import jax.numpy as jnp
from jax import lax
from jax.experimental import pallas as pl
from jax.experimental.pallas import tpu as pltpu

F32 = jnp.float32
BF16 = jnp.bfloat16
I32 = jnp.int32

D_MODEL = 1024
SB_HEADS = 8
SB_HEAD_DIM = 64
SB_WIDTH = SB_HEADS * SB_HEAD_DIM
POOL_WINDOWS = (2, 4, 8, 16)
POOL_WIDTH = 512
POOL_GROUP_DIM = 128
N_EXPERTS = 32
TOP_K = 4
D_EXPERT = 1024
SWIGLU_LIMIT = 7.0
SWIGLU_ALPHA = 1.702
RMS_EPS = 1e-5

LANES = 128
HALO = 16
PROJ_TILE = 512
ATTN_TILE = 256
EXPERT_TILE = 256
COMBINE_TILE = 256
ATTN_SKIP_LOG2 = -160.0
LOG2_E = 1.4426950408889634
VMEM_LIMIT = 56 * 1024 * 1024

_NT = (((1,), (1,)), ((), ()))


def _rms(x, g):
    ms = jnp.mean(x * x, axis=-1, keepdims=True)
    return x * lax.rsqrt(ms + RMS_EPS) * g


def _in_proj_kernel(x_ref, g_ref, w_ref, pw_ref, ps_ref, q_ref, k_ref, v_ref, p_ref, uext_ref):
    s = pl.program_id(1)
    tm = x_ref.shape[1]
    h = _rms(x_ref[0], g_ref[...])
    proj = jnp.dot(h.astype(BF16), w_ref[...], preferred_element_type=F32)
    q_ref[0] = (proj[:, 0:SB_WIDTH] * (LOG2_E * SB_HEAD_DIM ** -0.5)).astype(BF16)
    k_ref[0] = proj[:, SB_WIDTH:2 * SB_WIDTH].astype(BF16)
    v_ref[0] = proj[:, 2 * SB_WIDTH:3 * SB_WIDTH].astype(BF16)
    u = proj[:, 3 * SB_WIDTH:]

    @pl.when(s == 0)
    def _():
        uext_ref[0:HALO, :] = jnp.zeros((HALO, POOL_WIDTH), F32)

    uext_ref[HALO:, :] = u
    t = s * tm + lax.broadcasted_iota(I32, (tm, 1), 0)
    for g, w in enumerate(POOL_WINDOWS):
        sl = slice(g * POOL_GROUP_DIM, (g + 1) * POOL_GROUP_DIM)
        ug = u[:, sl]
        acc = ug
        for i in range(1, w):
            acc = acc + uext_ref[HALO - i:HALO - i + tm, sl]
        count = jnp.minimum(t + 1, w).astype(F32)
        pooled = acc / count - ug
        mixed = jnp.dot(pooled.astype(BF16), pw_ref[g], preferred_element_type=F32)
        p_ref[0, :, sl] = (mixed * ps_ref[:, sl]).astype(BF16)
    uext_ref[0:HALO, :] = u[tm - HALO:, :]


def _in_proj(x, g_mix, w_in, pool_w, pool_scale):
    B, S, D = x.shape
    tm = PROJ_TILE
    n_out = w_in.shape[1]
    out_sd = jax.ShapeDtypeStruct((B, S, SB_WIDTH), BF16)
    blk = pl.BlockSpec((1, tm, SB_WIDTH), lambda b, s: (b, s, 0))
    return pl.pallas_call(
        _in_proj_kernel,
        grid=(B, S // tm),
        in_specs=[
            pl.BlockSpec((1, tm, D), lambda b, s: (b, s, 0)),
            pl.BlockSpec((1, D), lambda b, s: (0, 0)),
            pl.BlockSpec((D, n_out), lambda b, s: (0, 0)),
            pl.BlockSpec(pool_w.shape, lambda b, s: (0, 0, 0)),
            pl.BlockSpec((1, POOL_WIDTH), lambda b, s: (0, 0)),
        ],
        out_specs=[blk, blk, blk, blk],
        out_shape=[out_sd, out_sd, out_sd, out_sd],
        scratch_shapes=[pltpu.VMEM((HALO + tm, POOL_WIDTH), F32)],
        compiler_params=pltpu.CompilerParams(
            dimension_semantics=("arbitrary", "arbitrary"), vmem_limit_bytes=VMEM_LIMIT),
        name="in_proj_pool",
    )(x, g_mix.reshape(1, D), w_in.astype(BF16), pool_w.astype(BF16), pool_scale.reshape(1, POOL_WIDTH))


def _attn_kernel(q_ref, k_ref, v_ref, o_ref):
    tq = q_ref.shape[1]
    tk = tq
    qi = pl.program_id(2)
    q = q_ref[0]
    lane = lax.broadcasted_iota(I32, (tq, LANES), 1)
    row = lax.broadcasted_iota(I32, (tq, tk), 0)
    col = lax.broadcasted_iota(I32, (tq, tk), 1)
    causal = col < row
    tri = (row > col).astype(BF16)
    qs = [jnp.where((lane >= h * SB_HEAD_DIM) & (lane < (h + 1) * SB_HEAD_DIM), q, jnp.zeros_like(q))
          for h in range(2)]

    def tile(qh, j, mask):
        start = pl.multiple_of(j * tk, tk)
        kb = k_ref[0, pl.ds(start, tk), :]
        vb = v_ref[0, pl.ds(start, tk), :]
        z = lax.dot_general(qh, kb, _NT, preferred_element_type=F32)
        lsm = -(jnp.maximum(z, 0.0) + jnp.log2(1.0 + jnp.exp2(-jnp.abs(z))))
        if mask:
            lsm = jnp.where(causal, lsm, 0.0)
        tail = jnp.dot(lsm.astype(BF16), tri, preferred_element_type=F32)
        total = tail[:, 0:1] + lsm[:, 0:1]
        return lsm + z + tail, total, vb

    prev = jnp.maximum(qi - 1, 0)
    has_prev = qi > 0
    accs, carries = [], []
    for h in range(2):
        lw_d, tot_d, v_d = tile(qs[h], qi, True)
        lw_p, tot_p, v_p = tile(qs[h], prev, False)
        w_d = jnp.where(causal, jnp.exp2(lw_d), 0.0).astype(BF16)
        w_p = jnp.where(has_prev, jnp.exp2(lw_p + tot_d), 0.0).astype(BF16)
        accs.append(jnp.dot(w_d, v_d, preferred_element_type=F32) + jnp.dot(w_p, v_p, preferred_element_type=F32))
        carries.append(tot_d + tot_p)

    def cond(st):
        j, c0, c1, _, _ = st
        return jnp.logical_and(j >= 0, jnp.maximum(jnp.max(c0), jnp.max(c1)) > ATTN_SKIP_LOG2)

    def body(st):
        j, c0, c1, a0, a1 = st
        out = []
        for qh, c, a in ((qs[0], c0, a0), (qs[1], c1, a1)):
            lw, tot, vb = tile(qh, j, False)
            a = a + jnp.dot(jnp.exp2(lw + c).astype(BF16), vb, preferred_element_type=F32)
            out.append((c + tot, a))
        return j - 1, out[0][0], out[1][0], out[0][1], out[1][1]

    _, _, _, acc0, acc1 = lax.while_loop(cond, body, (qi - 2, carries[0], carries[1], accs[0], accs[1]))
    o_ref[0] = jnp.where(lane < SB_HEAD_DIM, acc0, acc1).astype(o_ref.dtype)


def _attention(q, k, v):
    B, S, W = q.shape
    tq = ATTN_TILE
    n_pairs = W // LANES
    return pl.pallas_call(
        _attn_kernel,
        grid=(B, n_pairs, S // tq),
        in_specs=[
            pl.BlockSpec((1, tq, LANES), lambda b, p, i: (b, i, p)),
            pl.BlockSpec((1, S, LANES), lambda b, p, i: (b, 0, p)),
            pl.BlockSpec((1, S, LANES), lambda b, p, i: (b, 0, p)),
        ],
        out_specs=pl.BlockSpec((1, tq, LANES), lambda b, p, i: (b, i, p)),
        out_shape=jax.ShapeDtypeStruct((B, S, W), BF16),
        compiler_params=pltpu.CompilerParams(
            dimension_semantics=("arbitrary", "arbitrary", "arbitrary"), vmem_limit_bytes=VMEM_LIMIT),
        name="stickbreak_attn",
    )(q, k, v)


def _out_proj_router_kernel(attn_ref, pool_ref, x_ref, wo_ref, g_ref, wr_ref, br_ref, tri_ref,
                            x1_ref, h2_ref, idx_ref, gate_ref, rank_ref, cnt_ref, carry_ref):
    i = pl.program_id(0)
    tm = x_ref.shape[0]

    @pl.when(i == 0)
    def _():
        carry_ref[...] = jnp.zeros_like(carry_ref)

    mixed = jnp.dot(attn_ref[...], wo_ref[0:SB_WIDTH, :], preferred_element_type=F32)
    mixed = mixed + jnp.dot(pool_ref[...], wo_ref[SB_WIDTH:, :], preferred_element_type=F32)
    x1 = x_ref[...] + mixed
    h2 = _rms(x1, g_ref[...])
    x1_ref[...] = x1
    h2_ref[...] = h2

    hh = h2.astype(BF16)
    hl = (h2 - hh.astype(F32)).astype(BF16)
    wr = wr_ref[...]
    wh = wr.astype(BF16)
    wl = (wr - wh.astype(F32)).astype(BF16)
    logits = (lax.dot_general(wh, hh, _NT, preferred_element_type=F32)
              + lax.dot_general(wh, hl, _NT, preferred_element_type=F32)
              + lax.dot_general(wl, hh, _NT, preferred_element_type=F32)) + br_ref[...]

    eid = lax.broadcasted_iota(I32, (N_EXPERTS, tm), 0).astype(F32)
    work = logits
    vals, ids = [], []
    for _ in range(TOP_K):
        m = jnp.max(work, axis=0, keepdims=True)
        sel = jnp.min(jnp.where(work == m, eid, float(N_EXPERTS)), axis=0, keepdims=True)
        vals.append(m)
        ids.append(sel)
        work = jnp.where(eid == sel, -jnp.inf, work)
    exps = [jnp.exp(v - vals[0]) for v in vals]
    denom = exps[0] + exps[1] + exps[2] + exps[3]

    onehot = jnp.zeros((N_EXPERTS, tm), F32)
    for sel in ids:
        onehot = onehot + (eid == sel).astype(F32)
    before = jnp.dot(onehot.astype(BF16), tri_ref[...], preferred_element_type=F32) + carry_ref[...]
    for kk in range(TOP_K):
        idx_ref[kk:kk + 1, :] = ids[kk].astype(I32)
        gate_ref[kk:kk + 1, :] = exps[kk] / denom
        rk = jnp.sum(jnp.where(eid == ids[kk], before, 0.0), axis=0, keepdims=True)
        rank_ref[kk:kk + 1, :] = rk.astype(I32)
    carry_ref[...] = carry_ref[...] + jnp.sum(onehot, axis=1, keepdims=True)
    cnt_ref[...] = jnp.broadcast_to(carry_ref[...], cnt_ref.shape)


def _out_proj_router(attn, pool, x, w_out, g_moe, w_router, b_router):
    T, D = x.shape
    tm = PROJ_TILE
    r = lax.broadcasted_iota(I32, (tm, tm), 0)
    c = lax.broadcasted_iota(I32, (tm, tm), 1)
    tri = (r < c).astype(BF16)
    row_blk = lambda w: pl.BlockSpec((tm, w), lambda i: (i, 0))
    fixed = lambda shape: pl.BlockSpec(shape, lambda i: tuple(0 for _ in shape))
    sel_blk = pl.BlockSpec((TOP_K, tm), lambda i: (0, i))
    return pl.pallas_call(
        _out_proj_router_kernel,
        grid=(T // tm,),
        in_specs=[row_blk(SB_WIDTH), row_blk(POOL_WIDTH), row_blk(D), fixed((D, D)), fixed((1, D)),
                  fixed((N_EXPERTS, D)), fixed((N_EXPERTS, 1)), fixed((tm, tm))],
        out_specs=[row_blk(D), row_blk(D), sel_blk, sel_blk, sel_blk, fixed((N_EXPERTS, LANES))],
        out_shape=[jax.ShapeDtypeStruct((T, D), F32), jax.ShapeDtypeStruct((T, D), F32),
                   jax.ShapeDtypeStruct((TOP_K, T), I32), jax.ShapeDtypeStruct((TOP_K, T), F32),
                   jax.ShapeDtypeStruct((TOP_K, T), I32), jax.ShapeDtypeStruct((N_EXPERTS, LANES), F32)],
        scratch_shapes=[pltpu.VMEM((N_EXPERTS, 1), F32)],
        compiler_params=pltpu.CompilerParams(
            dimension_semantics=("arbitrary",), vmem_limit_bytes=VMEM_LIMIT),
        name="out_proj_router",
    )(attn, pool, x, w_out.astype(BF16), g_moe.reshape(1, D), w_router.T, b_router.reshape(N_EXPERTS, 1), tri)


def _dispatch_kernel(dest_ref, h_ref, out_ref, sem):
    tm = h_ref.shape[0]

    def issue(r, _):
        for kk in range(TOP_K):
            pltpu.make_async_copy(h_ref.at[pl.ds(r, 1)], out_ref.at[pl.ds(dest_ref[kk, r], 1)], sem).start()
        return 0

    lax.fori_loop(0, tm, issue, 0)
    for _ in range(TOP_K):
        pltpu.make_async_copy(h_ref, out_ref.at[pl.ds(0, tm)], sem).wait()


def _dispatch(h2, dest):
    T, D = h2.shape
    tm = PROJ_TILE
    dest3 = dest.reshape(TOP_K, T // tm, tm).transpose(1, 0, 2)
    return pl.pallas_call(
        _dispatch_kernel,
        grid=(T // tm,),
        in_specs=[pl.BlockSpec((None, TOP_K, tm), lambda i: (i, 0, 0), memory_space=pltpu.SMEM),
                  pl.BlockSpec((tm, D), lambda i: (i, 0))],
        out_specs=pl.BlockSpec(memory_space=pl.ANY),
        out_shape=jax.ShapeDtypeStruct((T * TOP_K, D), F32),
        scratch_shapes=[pltpu.SemaphoreType.DMA],
        compiler_params=pltpu.CompilerParams(
            dimension_semantics=("arbitrary",), vmem_limit_bytes=VMEM_LIMIT),
        name="dispatch_rows",
    )(dest3, h2)


def _expert_kernel(vb_ref, ve_ref, vlo_ref, vhi_ref, x_ref, wgu_ref, bgu_ref, wd_ref, bd_ref, y_ref):
    v = pl.program_id(0)
    tm = x_ref.shape[0]
    lo = vlo_ref[v]
    hi = vhi_ref[v]
    blk = vb_ref[v]
    first = jnp.logical_or(v == 0, blk != vb_ref[jnp.maximum(v - 1, 0)])

    @pl.when(hi > lo)
    def _():
        gu = jnp.dot(x_ref[...].astype(BF16), wgu_ref[0], preferred_element_type=F32) + bgu_ref[0]
        gate = jnp.minimum(gu[:, :D_EXPERT], SWIGLU_LIMIT)
        up = jnp.clip(gu[:, D_EXPERT:], -SWIGLU_LIMIT, SWIGLU_LIMIT)
        act = gate * jax.nn.sigmoid(SWIGLU_ALPHA * gate) * (up + 1.0)
        y = jnp.dot(act.astype(BF16), wd_ref[0], preferred_element_type=F32) + bd_ref[0]
        rows = blk * tm + lax.broadcasted_iota(I32, (tm, 1), 0)
        valid = jnp.logical_and(rows >= lo, rows < hi)

        @pl.when(first)
        def _():
            y_ref[...] = jnp.where(valid, y, 0.0)

        @pl.when(jnp.logical_not(first))
        def _():
            y_ref[...] = jnp.where(valid, y, y_ref[...])


def _experts(x_sorted, visits, w_gate_up, b_gate_up, w_down, b_down):
    n_rows, D = x_sorted.shape
    tm = EXPERT_TILE
    vb, ve, vlo, vhi = visits
    n_visits = vb.shape[0]
    grid_spec = pltpu.PrefetchScalarGridSpec(
        num_scalar_prefetch=4,
        grid=(n_visits,),
        in_specs=[
            pl.BlockSpec((tm, D), lambda v, vb, ve, lo, hi: (vb[v], 0)),
            pl.BlockSpec((1, D, 2 * D_EXPERT), lambda v, vb, ve, lo, hi: (ve[v], 0, 0)),
            pl.BlockSpec((1, 1, 2 * D_EXPERT), lambda v, vb, ve, lo, hi: (ve[v], 0, 0)),
            pl.BlockSpec((1, D_EXPERT, D), lambda v, vb, ve, lo, hi: (ve[v], 0, 0)),
            pl.BlockSpec((1, 1, D), lambda v, vb, ve, lo, hi: (ve[v], 0, 0)),
        ],
        out_specs=pl.BlockSpec((tm, D), lambda v, vb, ve, lo, hi: (vb[v], 0)),
    )
    return pl.pallas_call(
        _expert_kernel,
        grid_spec=grid_spec,
        out_shape=jax.ShapeDtypeStruct((n_rows, D), F32),
        compiler_params=pltpu.CompilerParams(
            dimension_semantics=("arbitrary",), vmem_limit_bytes=VMEM_LIMIT),
        name="expert_gmm",
    )(vb, ve, vlo, vhi, x_sorted, w_gate_up.astype(BF16), b_gate_up.reshape(N_EXPERTS, 1, -1),
      w_down.astype(BF16), b_down.reshape(N_EXPERTS, 1, -1))


def _visit_schedule(counts, n_rows):
    tm = EXPERT_TILE
    n_blocks = n_rows // tm
    n_visits = n_blocks + N_EXPERTS - 1
    ends = jnp.cumsum(counts)
    starts = ends - counts
    first_blk = starts // tm
    last_blk = jnp.maximum(ends - 1, 0) // tm
    per_e = jnp.where(counts > 0, last_blk - first_blk + 1, 0)
    vend = jnp.cumsum(per_e)
    vstart = vend - per_e
    total = vend[-1]
    v = jnp.arange(n_visits, dtype=I32)
    vc = jnp.minimum(v, total - 1)
    e = jnp.sum(vc[:, None] >= vend[None, :], axis=1).astype(I32)
    blk = first_blk[e] + vc - vstart[e]
    lo = jnp.maximum(starts[e], blk * tm)
    hi = jnp.minimum(ends[e], (blk + 1) * tm)
    real = v < total
    lo = jnp.where(real, lo, 0)
    hi = jnp.where(real, hi, 0)
    return blk.astype(I32), e, lo.astype(I32), hi.astype(I32)


def _combine_kernel(dest_ref, y_ref, x1_ref, gate_ref, g_ref, o_ref, buf_ref, sem):
    tm = x1_ref.shape[0]

    def issue(r, _):
        for kk in range(TOP_K):
            pltpu.make_async_copy(y_ref.at[pl.ds(dest_ref[kk, r], 1)], buf_ref.at[kk, pl.ds(r, 1)], sem).start()
        return 0

    lax.fori_loop(0, tm, issue, 0)
    for kk in range(TOP_K):
        pltpu.make_async_copy(y_ref.at[pl.ds(0, tm)], buf_ref.at[kk], sem).wait()
    acc = x1_ref[...]
    gates = gate_ref[...]
    for kk in range(TOP_K):
        acc = acc + gates[:, kk:kk + 1] * buf_ref[kk]
    o_ref[...] = _rms(acc, g_ref[...])


def _combine(y, dest, x1, gates_t, g_final):
    T, D = x1.shape
    tm = COMBINE_TILE
    dest3 = dest.reshape(TOP_K, T // tm, tm).transpose(1, 0, 2)
    return pl.pallas_call(
        _combine_kernel,
        grid=(T // tm,),
        in_specs=[pl.BlockSpec((None, TOP_K, tm), lambda i: (i, 0, 0), memory_space=pltpu.SMEM),
                  pl.BlockSpec(memory_space=pl.ANY),
                  pl.BlockSpec((tm, D), lambda i: (i, 0)),
                  pl.BlockSpec((tm, TOP_K), lambda i: (i, 0)),
                  pl.BlockSpec((1, D), lambda i: (0, 0))],
        out_specs=pl.BlockSpec((tm, D), lambda i: (i, 0)),
        out_shape=jax.ShapeDtypeStruct((T, D), F32),
        scratch_shapes=[pltpu.VMEM((TOP_K, tm, D), F32), pltpu.SemaphoreType.DMA],
        compiler_params=pltpu.CompilerParams(
            dimension_semantics=("arbitrary",), vmem_limit_bytes=VMEM_LIMIT),
        name="combine_norm",
    )(dest3, y, x1, gates_t, g_final.reshape(1, D))


def kernel(x, g_mix, w_in, pool_w, pool_scale, w_out, g_moe, w_router, b_router, w_gate_up, b_gate_up,
           w_down, b_down, g_final):
    B, S, D = x.shape
    T = B * S
    assert g_mix.shape[0] == 1, "single-layer problem: the final norm is fused into the combine step"
    q, k, v, pool = _in_proj(x, g_mix[0], w_in[0], pool_w[0], pool_scale[0])
    attn = _attention(q, k, v)
    x1, h2, idx, gates, rank, cnt = _out_proj_router(
        attn.reshape(T, SB_WIDTH), pool.reshape(T, POOL_WIDTH), x.reshape(T, D),
        w_out[0], g_moe[0], w_router[0], b_router[0])
    counts = cnt[:, 0].astype(I32)
    starts = jnp.cumsum(counts) - counts
    dest = rank + jnp.sum(jnp.where(idx[:, :, None] == jnp.arange(N_EXPERTS, dtype=I32), starts, 0), axis=-1)
    x_sorted = _dispatch(h2, dest)
    visits = _visit_schedule(counts, T * TOP_K)
    y = _experts(x_sorted, visits, w_gate_up[0], b_gate_up[0], w_down[0], b_down[0])
    return _combine(y, dest, x1, gates.T, g_final).reshape(B, S, D)
```

```python
import functools

import jax
import jax.numpy as jnp
from jax import lax
from jax.experimental import pallas as pl
from jax.experimental.pallas import tpu as pltpu

F32 = jnp.float32
BF16 = jnp.bfloat16
I32 = jnp.int32

D_MODEL = 1024
SB_HEADS = 8
SB_HEAD_DIM = 64
SB_WIDTH = SB_HEADS * SB_HEAD_DIM
POOL_WINDOWS = (2, 4, 8, 16)
POOL_WIDTH = 512
POOL_GROUP_DIM = 128
N_EXPERTS = 32
TOP_K = 4
D_EXPERT = 1024
SWIGLU_LIMIT = 7.0
SWIGLU_ALPHA = 1.702
RMS_EPS = 1e-5

LANES = 128
HALO = 16
PROJ_TILE = 512
ATTN_TILE = 256
EXPERT_TILE = 256
COMBINE_TILE = 256
ATTN_SKIP_LOG2 = -160.0
LOG2_E = 1.4426950408889634
VMEM_LIMIT = 56 * 1024 * 1024

_NT = (((1,), (1,)), ((), ()))


def _rms(x, g):
    ms = jnp.mean(x * x, axis=-1, keepdims=True)
    return x * lax.rsqrt(ms + RMS_EPS) * g


def _in_proj_kernel(x_ref, g_ref, w_ref, pw_ref, ps_ref, q_ref, k_ref, v_ref, p_ref, uext_ref):
    s = pl.program_id(1)
    tm = x_ref.shape[1]
    h = _rms(x_ref[0], g_ref[...])
    proj = jnp.dot(h.astype(BF16), w_ref[...], preferred_element_type=F32)
    q_ref[0] = (proj[:, 0:SB_WIDTH] * (LOG2_E * SB_HEAD_DIM ** -0.5)).astype(BF16)
    k_ref[0] = proj[:, SB_WIDTH:2 * SB_WIDTH].astype(BF16)
    v_ref[0] = proj[:, 2 * SB_WIDTH:3 * SB_WIDTH].astype(BF16)
    u = proj[:, 3 * SB_WIDTH:]

    @pl.when(s == 0)
    def _():
        uext_ref[0:HALO, :] = jnp.zeros((HALO, POOL_WIDTH), F32)

    uext_ref[HALO:, :] = u
    t = s * tm + lax.broadcasted_iota(I32, (tm, 1), 0)
    for g, w in enumerate(POOL_WINDOWS):
        sl = slice(g * POOL_GROUP_DIM, (g + 1) * POOL_GROUP_DIM)
        ug = u[:, sl]
        acc = ug
        for i in range(1, w):
            acc = acc + uext_ref[HALO - i:HALO - i + tm, sl]
        count = jnp.minimum(t + 1, w).astype(F32)
        pooled = acc / count - ug
        mixed = jnp.dot(pooled.astype(BF16), pw_ref[g], preferred_element_type=F32)
        p_ref[0, :, sl] = (mixed * ps_ref[:, sl]).astype(BF16)
    uext_ref[0:HALO, :] = u[tm - HALO:, :]


def _in_proj(x, g_mix, w_in, pool_w, pool_scale):
    B, S, D = x.shape
    tm = PROJ_TILE
    n_out = w_in.shape[1]
    out_sd = jax.ShapeDtypeStruct((B, S, SB_WIDTH), BF16)
    blk = pl.BlockSpec((1, tm, SB_WIDTH), lambda b, s: (b, s, 0))
    return pl.pallas_call(
        _in_proj_kernel,
        grid=(B, S // tm),
        in_specs=[
            pl.BlockSpec((1, tm, D), lambda b, s: (b, s, 0)),
            pl.BlockSpec((1, D), lambda b, s: (0, 0)),
            pl.BlockSpec((D, n_out), lambda b, s: (0, 0)),
            pl.BlockSpec(pool_w.shape, lambda b, s: (0, 0, 0)),
            pl.BlockSpec((1, POOL_WIDTH), lambda b, s: (0, 0)),
        ],
        out_specs=[blk, blk, blk, blk],
        out_shape=[out_sd, out_sd, out_sd, out_sd],
        scratch_shapes=[pltpu.VMEM((HALO + tm, POOL_WIDTH), F32)],
        compiler_params=pltpu.CompilerParams(
            dimension_semantics=("arbitrary", "arbitrary"), vmem_limit_bytes=VMEM_LIMIT),
        name="in_proj_pool",
    )(x, g_mix.reshape(1, D), w_in.astype(BF16), pool_w.astype(BF16), pool_scale.reshape(1, POOL_WIDTH))


def _attn_kernel(q_ref, k_ref, v_ref, o_ref):
    tq = q_ref.shape[1]
    tk = tq
    qi = pl.program_id(2)
    q = q_ref[0]
    lane = lax.broadcasted_iota(I32, (tq, LANES), 1)
    row = lax.broadcasted_iota(I32, (tq, tk), 0)
    col = lax.broadcasted_iota(I32, (tq, tk), 1)
    causal = col < row
    tri = (row > col).astype(BF16)
    qs = [jnp.where((lane >= h * SB_HEAD_DIM) & (lane < (h + 1) * SB_HEAD_DIM), q, jnp.zeros_like(q))
          for h in range(2)]

    def tile(qh, j, mask):
        start = pl.multiple_of(j * tk, tk)
        kb = k_ref[0, pl.ds(start, tk), :]
        vb = v_ref[0, pl.ds(start, tk), :]
        z = lax.dot_general(qh, kb, _NT, preferred_element_type=F32)
        lsm = -(jnp.maximum(z, 0.0) + jnp.log2(1.0 + jnp.exp2(-jnp.abs(z))))
        if mask:
            lsm = jnp.where(causal, lsm, 0.0)
        tail = jnp.dot(lsm.astype(BF16), tri, preferred_element_type=F32)
        total = tail[:, 0:1] + lsm[:, 0:1]
        return lsm + z + tail, total, vb

    prev = jnp.maximum(qi - 1, 0)
    has_prev = qi > 0
    accs, carries = [], []
    for h in range(2):
        lw_d, tot_d, v_d = tile(qs[h], qi, True)
        lw_p, tot_p, v_p = tile(qs[h], prev, False)
        w_d = jnp.where(causal, jnp.exp2(lw_d), 0.0).astype(BF16)
        w_p = jnp.where(has_prev, jnp.exp2(lw_p + tot_d), 0.0).astype(BF16)
        accs.append(jnp.dot(w_d, v_d, preferred_element_type=F32) + jnp.dot(w_p, v_p, preferred_element_type=F32))
        carries.append(tot_d + tot_p)

    def cond(st):
        j, c0, c1, _, _ = st
        return jnp.logical_and(j >= 0, jnp.maximum(jnp.max(c0), jnp.max(c1)) > ATTN_SKIP_LOG2)

    def body(st):
        j, c0, c1, a0, a1 = st
        out = []
        for qh, c, a in ((qs[0], c0, a0), (qs[1], c1, a1)):
            lw, tot, vb = tile(qh, j, False)
            a = a + jnp.dot(jnp.exp2(lw + c).astype(BF16), vb, preferred_element_type=F32)
            out.append((c + tot, a))
        return j - 1, out[0][0], out[1][0], out[0][1], out[1][1]

    _, _, _, acc0, acc1 = lax.while_loop(cond, body, (qi - 2, carries[0], carries[1], accs[0], accs[1]))
    o_ref[0] = jnp.where(lane < SB_HEAD_DIM, acc0, acc1).astype(o_ref.dtype)


def _attention(q, k, v):
    B, S, W = q.shape
    tq = ATTN_TILE
    n_pairs = W // LANES
    return pl.pallas_call(
        _attn_kernel,
        grid=(B, n_pairs, S // tq),
        in_specs=[
            pl.BlockSpec((1, tq, LANES), lambda b, p, i: (b, i, p)),
            pl.BlockSpec((1, S, LANES), lambda b, p, i: (b, 0, p)),
            pl.BlockSpec((1, S, LANES), lambda b, p, i: (b, 0, p)),
        ],
        out_specs=pl.BlockSpec((1, tq, LANES), lambda b, p, i: (b, i, p)),
        out_shape=jax.ShapeDtypeStruct((B, S, W), BF16),
        compiler_params=pltpu.CompilerParams(
            dimension_semantics=("arbitrary", "arbitrary", "arbitrary"), vmem_limit_bytes=VMEM_LIMIT),
        name="stickbreak_attn",
    )(q, k, v)


def _out_proj_router_kernel(attn_ref, pool_ref, x_ref, wo_ref, g_ref, wr_ref, br_ref, tri_ref,
                            x1_ref, h2_ref, idx_ref, gate_ref, rank_ref, cnt_ref, carry_ref):
    i = pl.program_id(0)
    tm = x_ref.shape[0]

    @pl.when(i == 0)
    def _():
        carry_ref[...] = jnp.zeros_like(carry_ref)

    mixed = jnp.dot(attn_ref[...], wo_ref[0:SB_WIDTH, :], preferred_element_type=F32)
    mixed = mixed + jnp.dot(pool_ref[...], wo_ref[SB_WIDTH:, :], preferred_element_type=F32)
    x1 = x_ref[...] + mixed
    h2 = _rms(x1, g_ref[...])
    x1_ref[...] = x1
    h2_ref[...] = h2

    hh = h2.astype(BF16)
    hl = (h2 - hh.astype(F32)).astype(BF16)
    wr = wr_ref[...]
    wh = wr.astype(BF16)
    wl = (wr - wh.astype(F32)).astype(BF16)
    logits = (lax.dot_general(wh, hh, _NT, preferred_element_type=F32)
              + lax.dot_general(wh, hl, _NT, preferred_element_type=F32)
              + lax.dot_general(wl, hh, _NT, preferred_element_type=F32)) + br_ref[...]

    eid = lax.broadcasted_iota(I32, (N_EXPERTS, tm), 0).astype(F32)
    work = logits
    vals, ids = [], []
    for _ in range(TOP_K):
        m = jnp.max(work, axis=0, keepdims=True)
        sel = jnp.min(jnp.where(work == m, eid, float(N_EXPERTS)), axis=0, keepdims=True)
        vals.append(m)
        ids.append(sel)
        work = jnp.where(eid == sel, -jnp.inf, work)
    exps = [jnp.exp(v - vals[0]) for v in vals]
    denom = exps[0] + exps[1] + exps[2] + exps[3]

    onehot = jnp.zeros((N_EXPERTS, tm), F32)
    for sel in ids:
        onehot = onehot + (eid == sel).astype(F32)
    before = jnp.dot(onehot.astype(BF16), tri_ref[...], preferred_element_type=F32) + carry_ref[...]
    for kk in range(TOP_K):
        idx_ref[kk:kk + 1, :] = ids[kk].astype(I32)
        gate_ref[kk:kk + 1, :] = exps[kk] / denom
        rk = jnp.sum(jnp.where(eid == ids[kk], before, 0.0), axis=0, keepdims=True)
        rank_ref[kk:kk + 1, :] = rk.astype(I32)
    carry_ref[...] = carry_ref[...] + jnp.sum(onehot, axis=1, keepdims=True)
    cnt_ref[...] = jnp.broadcast_to(carry_ref[...], cnt_ref.shape)


def _out_proj_router(attn, pool, x, w_out, g_moe, w_router, b_router):
    T, D = x.shape
    tm = PROJ_TILE
    r = lax.broadcasted_iota(I32, (tm, tm), 0)
    c = lax.broadcasted_iota(I32, (tm, tm), 1)
    tri = (r < c).astype(BF16)
    row_blk = lambda w: pl.BlockSpec((tm, w), lambda i: (i, 0))
    fixed = lambda shape: pl.BlockSpec(shape, lambda i: tuple(0 for _ in shape))
    sel_blk = pl.BlockSpec((TOP_K, tm), lambda i: (0, i))
    return pl.pallas_call(
        _out_proj_router_kernel,
        grid=(T // tm,),
        in_specs=[row_blk(SB_WIDTH), row_blk(POOL_WIDTH), row_blk(D), fixed((D, D)), fixed((1, D)),
                  fixed((N_EXPERTS, D)), fixed((N_EXPERTS, 1)), fixed((tm, tm))],
        out_specs=[row_blk(D), row_blk(D), sel_blk, sel_blk, sel_blk, fixed((N_EXPERTS, LANES))],
        out_shape=[jax.ShapeDtypeStruct((T, D), F32), jax.ShapeDtypeStruct((T, D), F32),
                   jax.ShapeDtypeStruct((TOP_K, T), I32), jax.ShapeDtypeStruct((TOP_K, T), F32),
                   jax.ShapeDtypeStruct((TOP_K, T), I32), jax.ShapeDtypeStruct((N_EXPERTS, LANES), F32)],
        scratch_shapes=[pltpu.VMEM((N_EXPERTS, 1), F32)],
        compiler_params=pltpu.CompilerParams(
            dimension_semantics=("arbitrary",), vmem_limit_bytes=VMEM_LIMIT),
        name="out_proj_router",
    )(attn, pool, x, w_out.astype(BF16), g_moe.reshape(1, D), w_router.T, b_router.reshape(N_EXPERTS, 1), tri)


EXPERT_BLOCKS_MAX_EXTRA = N_EXPERTS


def _expert_kernel(be_ref, nblk_ref, ig_ref, is_ref, h_ref, wgu_ref, bgu_ref, wd_ref, bd_ref, y_ref,
                   xbuf, ybuf, gsem, ssem):
    g = pl.program_id(0)
    tm = xbuf.shape[1]
    n_tok = h_ref.shape[0]
    nblk = nblk_ref[0]
    cur = g % 2
    oth = 1 - cur

    def gather_wait(slot):
        pltpu.make_async_copy(h_ref.at[pl.ds(0, tm)], xbuf.at[slot], gsem.at[slot]).wait()

    def scatter_wait(slot):
        pltpu.make_async_copy(ybuf.at[slot], y_ref.at[pl.ds(0, tm)], ssem.at[slot]).wait()

    @pl.when(g == 0)
    def _():
        xbuf[...] = jnp.zeros_like(xbuf)
        ybuf[...] = jnp.zeros_like(ybuf)

    @pl.when(g <= nblk + 1)
    def _():
        @pl.when(g >= 1)
        def _():
            gather_wait(oth)
            scatter_wait(oth)

        for r in range(tm):
            tok = ig_ref[0, r] & (n_tok - 1)
            pltpu.make_async_copy(h_ref.at[pl.ds(tok, 1)], xbuf.at[cur, pl.ds(r, 1)], gsem.at[cur]).start()
        for r in range(tm):
            pltpu.make_async_copy(ybuf.at[cur, pl.ds(r, 1)], y_ref.at[pl.ds(is_ref[0, r], 1)], ssem.at[cur]).start()

        gu = jnp.dot(xbuf[oth].astype(BF16), wgu_ref[0], preferred_element_type=F32) + bgu_ref[0]
        gate = jnp.minimum(gu[:, :D_EXPERT], SWIGLU_LIMIT)
        up = jnp.clip(gu[:, D_EXPERT:], -SWIGLU_LIMIT, SWIGLU_LIMIT)
        act = gate * jax.nn.sigmoid(SWIGLU_ALPHA * gate) * (up + 1.0)
        ybuf[oth] = jnp.dot(act.astype(BF16), wd_ref[0], preferred_element_type=F32) + bd_ref[0]

        @pl.when(g == nblk + 1)
        def _():
            gather_wait(cur)
            scatter_wait(cur)


def _experts(h2, row_slot, block_expert, n_blocks_used, w_gate_up, b_gate_up, w_down, b_down):
    T, D = h2.shape
    tm = EXPERT_TILE
    n_steps = block_expert.shape[0]
    assert T & (T - 1) == 0, "token index is recovered from the slot with a mask"
    idx_spec = lambda off: pl.BlockSpec((None, 1, tm), lambda g, be, nb: (g + off, 0, 0), memory_space=pltpu.SMEM)
    grid_spec = pltpu.PrefetchScalarGridSpec(
        num_scalar_prefetch=2,
        grid=(n_steps,),
        in_specs=[
            idx_spec(2), idx_spec(0),
            pl.BlockSpec(memory_space=pl.ANY),
            pl.BlockSpec((1, D, 2 * D_EXPERT), lambda g, be, nb: (be[g], 0, 0)),
            pl.BlockSpec((1, 1, 2 * D_EXPERT), lambda g, be, nb: (be[g], 0, 0)),
            pl.BlockSpec((1, D_EXPERT, D), lambda g, be, nb: (be[g], 0, 0)),
            pl.BlockSpec((1, 1, D), lambda g, be, nb: (be[g], 0, 0)),
        ],
        out_specs=pl.BlockSpec(memory_space=pl.ANY),
        scratch_shapes=[pltpu.VMEM((2, tm, D), F32), pltpu.VMEM((2, tm, D), F32),
                        pltpu.SemaphoreType.DMA((2,)), pltpu.SemaphoreType.DMA((2,))],
    )
    return pl.pallas_call(
        _expert_kernel,
        grid_spec=grid_spec,
        out_shape=jax.ShapeDtypeStruct((T * TOP_K + tm, D), F32),
        compiler_params=pltpu.CompilerParams(
            dimension_semantics=("arbitrary",), vmem_limit_bytes=VMEM_LIMIT),
        name="expert_gmm",
    )(block_expert, n_blocks_used, row_slot, row_slot, h2, w_gate_up.astype(BF16),
      b_gate_up.reshape(N_EXPERTS, 1, -1), w_down.astype(BF16), b_down.reshape(N_EXPERTS, 1, -1))


def _expert_schedule(idx, rank, counts, n_tok):
    tm = EXPERT_TILE
    n_assign = n_tok * TOP_K
    max_blocks = n_assign // tm + EXPERT_BLOCKS_MAX_EXTRA
    padded = ((counts + tm - 1) // tm) * tm
    ends = jnp.cumsum(padded)
    starts = ends - padded
    dest = rank + jnp.sum(jnp.where(idx[:, :, None] == jnp.arange(N_EXPERTS, dtype=I32), starts, 0), axis=-1)
    n_used = (ends[-1] // tm).astype(I32)
    spare = n_assign + jnp.arange(tm, dtype=I32)
    slots = jnp.tile(spare, max_blocks + 2)
    slots = slots.at[dest.reshape(-1)].set(jnp.arange(n_assign, dtype=I32), unique_indices=True)
    row_slot = jnp.concatenate([jnp.tile(spare, 2), slots]).reshape(max_blocks + 4, 1, tm)
    steps = jnp.arange(max_blocks + 2, dtype=I32)
    blk = jnp.clip(steps - 1, 0, n_used - 1)
    block_expert = jnp.minimum(jnp.sum((blk * tm)[:, None] >= ends[None, :], axis=1), N_EXPERTS - 1).astype(I32)
    return row_slot, block_expert, n_used.reshape(1)


def _combine_kernel(y0_ref, y1_ref, y2_ref, y3_ref, x1_ref, gate_ref, g_ref, o_ref):
    gates = gate_ref[...]
    acc = x1_ref[...]
    for kk, y_ref in enumerate((y0_ref, y1_ref, y2_ref, y3_ref)):
        acc = acc + gates[:, kk:kk + 1] * y_ref[...]
    o_ref[...] = _rms(acc, g_ref[...])


def _combine(y, x1, gates_t, g_final):
    T, D = x1.shape
    tm = COMBINE_TILE
    nt = T // tm
    y_spec = lambda kk: pl.BlockSpec((tm, D), lambda i: (kk * nt + i, 0))
    return pl.pallas_call(
        _combine_kernel,
        grid=(nt,),
        in_specs=[y_spec(0), y_spec(1), y_spec(2), y_spec(3),
                  pl.BlockSpec((tm, D), lambda i: (i, 0)),
                  pl.BlockSpec((tm, TOP_K), lambda i: (i, 0)),
                  pl.BlockSpec((1, D), lambda i: (0, 0))],
        out_specs=pl.BlockSpec((tm, D), lambda i: (i, 0)),
        out_shape=jax.ShapeDtypeStruct((T, D), F32),
        compiler_params=pltpu.CompilerParams(
            dimension_semantics=("arbitrary",), vmem_limit_bytes=VMEM_LIMIT),
        name="combine_norm",
    )(y, y, y, y, x1, gates_t, g_final.reshape(1, D))


def kernel(x, g_mix, w_in, pool_w, pool_scale, w_out, g_moe, w_router, b_router, w_gate_up, b_gate_up,
           w_down, b_down, g_final):
    B, S, D = x.shape
    T = B * S
    assert g_mix.shape[0] == 1, "single-layer problem: the final norm is fused into the combine step"
    q, k, v, pool = _in_proj(x, g_mix[0], w_in[0], pool_w[0], pool_scale[0])
    attn = _attention(q, k, v)
    x1, h2, idx, gates, rank, cnt = _out_proj_router(
        attn.reshape(T, SB_WIDTH), pool.reshape(T, POOL_WIDTH), x.reshape(T, D),
        w_out[0], g_moe[0], w_router[0], b_router[0])
    row_slot, block_expert, n_used = _expert_schedule(idx, rank, cnt[:, 0].astype(I32), T)
    y = _experts(h2, row_slot, block_expert, n_used, w_gate_up[0], b_gate_up[0], w_down[0], b_down[0])
    return _combine(y, x1, gates.T, g_final).reshape(B, S, D)
```

```python
import functools

import jax
import jax.numpy as jnp
from jax import lax
from jax.experimental import pallas as pl
from jax.experimental.pallas import tpu as pltpu

F32 = jnp.float32
BF16 = jnp.bfloat16
I32 = jnp.int32

D_MODEL = 1024
SB_HEADS = 8
SB_HEAD_DIM = 64
SB_WIDTH = SB_HEADS * SB_HEAD_DIM
POOL_WINDOWS = (2, 4, 8, 16)
POOL_WIDTH = 512
POOL_GROUP_DIM = 128
N_EXPERTS = 32
TOP_K = 4
D_EXPERT = 1024
SWIGLU_LIMIT = 7.0
SWIGLU_ALPHA = 1.702
RMS_EPS = 1e-5

LANES = 128
HALO = 16
PROJ_TILE = 512
ATTN_TILE = 256
EXPERT_TILE = 256
COMBINE_TILE = 256
ATTN_SKIP_LOG2 = -160.0
LOG2_E = 1.4426950408889634
VMEM_LIMIT = 56 * 1024 * 1024

_NT = (((1,), (1,)), ((), ()))


def _rms(x, g):
    ms = jnp.mean(x * x, axis=-1, keepdims=True)
    return x * lax.rsqrt(ms + RMS_EPS) * g


def _in_proj_kernel(x_ref, g_ref, w_ref, pw_ref, ps_ref, q_ref, k_ref, v_ref, p_ref, uext_ref):
    s = pl.program_id(1)
    tm = x_ref.shape[1]
    h = _rms(x_ref[0], g_ref[...])
    proj = jnp.dot(h.astype(BF16), w_ref[...], preferred_element_type=F32)
    q_ref[0] = (proj[:, 0:SB_WIDTH] * (LOG2_E * SB_HEAD_DIM ** -0.5)).astype(BF16)
    k_ref[0] = proj[:, SB_WIDTH:2 * SB_WIDTH].astype(BF16)
    v_ref[0] = proj[:, 2 * SB_WIDTH:3 * SB_WIDTH].astype(BF16)
    u = proj[:, 3 * SB_WIDTH:]

    @pl.when(s == 0)
    def _():
        uext_ref[0:HALO, :] = jnp.zeros((HALO, POOL_WIDTH), F32)

    uext_ref[HALO:, :] = u
    t = s * tm + lax.broadcasted_iota(I32, (tm, 1), 0)
    for g, w in enumerate(POOL_WINDOWS):
        sl = slice(g * POOL_GROUP_DIM, (g + 1) * POOL_GROUP_DIM)
        ug = u[:, sl]
        acc = ug
        for i in range(1, w):
            acc = acc + uext_ref[HALO - i:HALO - i + tm, sl]
        count = jnp.minimum(t + 1, w).astype(F32)
        pooled = acc / count - ug
        mixed = jnp.dot(pooled.astype(BF16), pw_ref[g], preferred_element_type=F32)
        p_ref[0, :, sl] = (mixed * ps_ref[:, sl]).astype(BF16)
    uext_ref[0:HALO, :] = u[tm - HALO:, :]


def _in_proj(x, g_mix, w_in, pool_w, pool_scale):
    B, S, D = x.shape
    tm = PROJ_TILE
    n_out = w_in.shape[1]
    out_sd = jax.ShapeDtypeStruct((B, S, SB_WIDTH), BF16)
    blk = pl.BlockSpec((1, tm, SB_WIDTH), lambda b, s: (b, s, 0))
    return pl.pallas_call(
        _in_proj_kernel,
        grid=(B, S // tm),
        in_specs=[
            pl.BlockSpec((1, tm, D), lambda b, s: (b, s, 0)),
            pl.BlockSpec((1, D), lambda b, s: (0, 0)),
            pl.BlockSpec((D, n_out), lambda b, s: (0, 0)),
            pl.BlockSpec(pool_w.shape, lambda b, s: (0, 0, 0)),
            pl.BlockSpec((1, POOL_WIDTH), lambda b, s: (0, 0)),
        ],
        out_specs=[blk, blk, blk, blk],
        out_shape=[out_sd, out_sd, out_sd, out_sd],
        scratch_shapes=[pltpu.VMEM((HALO + tm, POOL_WIDTH), F32)],
        compiler_params=pltpu.CompilerParams(
            dimension_semantics=("arbitrary", "arbitrary"), vmem_limit_bytes=VMEM_LIMIT),
        name="in_proj_pool",
    )(x, g_mix.reshape(1, D), w_in.astype(BF16), pool_w.astype(BF16), pool_scale.reshape(1, POOL_WIDTH))


def _attn_kernel(q_ref, k_ref, v_ref, o_ref):
    tq = q_ref.shape[1]
    tk = tq
    qi = pl.program_id(2)
    q = q_ref[0]
    lane = lax.broadcasted_iota(I32, (tq, LANES), 1)
    row = lax.broadcasted_iota(I32, (tq, tk), 0)
    col = lax.broadcasted_iota(I32, (tq, tk), 1)
    causal = col < row
    tri = (row > col).astype(BF16)
    qs = [jnp.where((lane >= h * SB_HEAD_DIM) & (lane < (h + 1) * SB_HEAD_DIM), q, jnp.zeros_like(q))
          for h in range(2)]

    def tile(qh, j, mask):
        start = pl.multiple_of(j * tk, tk)
        kb = k_ref[0, pl.ds(start, tk), :]
        vb = v_ref[0, pl.ds(start, tk), :]
        z = lax.dot_general(qh, kb, _NT, preferred_element_type=F32)
        lsm = -(jnp.maximum(z, 0.0) + jnp.log2(1.0 + jnp.exp2(-jnp.abs(z))))
        if mask:
            lsm = jnp.where(causal, lsm, 0.0)
        tail = jnp.dot(lsm.astype(BF16), tri, preferred_element_type=F32)
        total = tail[:, 0:1] + lsm[:, 0:1]
        return lsm + z + tail, total, vb

    prev = jnp.maximum(qi - 1, 0)
    has_prev = qi > 0
    accs, carries = [], []
    for h in range(2):
        lw_d, tot_d, v_d = tile(qs[h], qi, True)
        lw_p, tot_p, v_p = tile(qs[h], prev, False)
        w_d = jnp.where(causal, jnp.exp2(lw_d), 0.0).astype(BF16)
        w_p = jnp.where(has_prev, jnp.exp2(lw_p + tot_d), 0.0).astype(BF16)
        accs.append(jnp.dot(w_d, v_d, preferred_element_type=F32) + jnp.dot(w_p, v_p, preferred_element_type=F32))
        carries.append(tot_d + tot_p)

    def cond(st):
        j, c0, c1, _, _ = st
        return jnp.logical_and(j >= 0, jnp.maximum(jnp.max(c0), jnp.max(c1)) > ATTN_SKIP_LOG2)

    def body(st):
        j, c0, c1, a0, a1 = st
        out = []
        for qh, c, a in ((qs[0], c0, a0), (qs[1], c1, a1)):
            lw, tot, vb = tile(qh, j, False)
            a = a + jnp.dot(jnp.exp2(lw + c).astype(BF16), vb, preferred_element_type=F32)
            out.append((c + tot, a))
        return j - 1, out[0][0], out[1][0], out[0][1], out[1][1]

    _, _, _, acc0, acc1 = lax.while_loop(cond, body, (qi - 2, carries[0], carries[1], accs[0], accs[1]))
    o_ref[0] = jnp.where(lane < SB_HEAD_DIM, acc0, acc1).astype(o_ref.dtype)


def _attention(q, k, v):
    B, S, W = q.shape
    tq = ATTN_TILE
    n_pairs = W // LANES
    return pl.pallas_call(
        _attn_kernel,
        grid=(B, n_pairs, S // tq),
        in_specs=[
            pl.BlockSpec((1, tq, LANES), lambda b, p, i: (b, i, p)),
            pl.BlockSpec((1, S, LANES), lambda b, p, i: (b, 0, p)),
            pl.BlockSpec((1, S, LANES), lambda b, p, i: (b, 0, p)),
        ],
        out_specs=pl.BlockSpec((1, tq, LANES), lambda b, p, i: (b, i, p)),
        out_shape=jax.ShapeDtypeStruct((B, S, W), BF16),
        compiler_params=pltpu.CompilerParams(
            dimension_semantics=("arbitrary", "arbitrary", "arbitrary"), vmem_limit_bytes=VMEM_LIMIT),
        name="stickbreak_attn",
    )(q, k, v)


def _out_proj_router_kernel(attn_ref, pool_ref, x_ref, wo_ref, g_ref, wr_ref, br_ref, tri_ref,
                            x1_ref, h2_ref, idx_ref, gate_ref, rank_ref, cnt_ref, carry_ref):
    i = pl.program_id(0)
    tm = x_ref.shape[0]

    @pl.when(i == 0)
    def _():
        carry_ref[...] = jnp.zeros_like(carry_ref)

    mixed = jnp.dot(attn_ref[...], wo_ref[0:SB_WIDTH, :], preferred_element_type=F32)
    mixed = mixed + jnp.dot(pool_ref[...], wo_ref[SB_WIDTH:, :], preferred_element_type=F32)
    x1 = x_ref[...] + mixed
    h2 = _rms(x1, g_ref[...])
    x1_ref[...] = x1
    h2_ref[...] = h2

    hh = h2.astype(BF16)
    hl = (h2 - hh.astype(F32)).astype(BF16)
    wr = wr_ref[...]
    wh = wr.astype(BF16)
    wl = (wr - wh.astype(F32)).astype(BF16)
    logits = (lax.dot_general(wh, hh, _NT, preferred_element_type=F32)
              + lax.dot_general(wh, hl, _NT, preferred_element_type=F32)
              + lax.dot_general(wl, hh, _NT, preferred_element_type=F32)) + br_ref[...]

    eid = lax.broadcasted_iota(I32, (N_EXPERTS, tm), 0).astype(F32)
    work = logits
    vals, ids = [], []
    for _ in range(TOP_K):
        m = jnp.max(work, axis=0, keepdims=True)
        sel = jnp.min(jnp.where(work == m, eid, float(N_EXPERTS)), axis=0, keepdims=True)
        vals.append(m)
        ids.append(sel)
        work = jnp.where(eid == sel, -jnp.inf, work)
    exps = [jnp.exp(v - vals[0]) for v in vals]
    denom = exps[0] + exps[1] + exps[2] + exps[3]

    onehot = jnp.zeros((N_EXPERTS, tm), F32)
    for sel in ids:
        onehot = onehot + (eid == sel).astype(F32)
    before = jnp.dot(onehot.astype(BF16), tri_ref[...], preferred_element_type=F32) + carry_ref[...]
    for kk in range(TOP_K):
        idx_ref[kk:kk + 1, :] = ids[kk].astype(I32)
        gate_ref[kk:kk + 1, :] = exps[kk] / denom
        rk = jnp.sum(jnp.where(eid == ids[kk], before, 0.0), axis=0, keepdims=True)
        rank_ref[kk:kk + 1, :] = rk.astype(I32)
    carry_ref[...] = carry_ref[...] + jnp.sum(onehot, axis=1, keepdims=True)
    cnt_ref[...] = jnp.broadcast_to(carry_ref[...], cnt_ref.shape)


def _out_proj_router(attn, pool, x, w_out, g_moe, w_router, b_router):
    T, D = x.shape
    tm = PROJ_TILE
    r = lax.broadcasted_iota(I32, (tm, tm), 0)
    c = lax.broadcasted_iota(I32, (tm, tm), 1)
    tri = (r < c).astype(BF16)
    row_blk = lambda w: pl.BlockSpec((tm, w), lambda i: (i, 0))
    fixed = lambda shape: pl.BlockSpec(shape, lambda i: tuple(0 for _ in shape))
    sel_blk = pl.BlockSpec((TOP_K, tm), lambda i: (0, i))
    return pl.pallas_call(
        _out_proj_router_kernel,
        grid=(T // tm,),
        in_specs=[row_blk(SB_WIDTH), row_blk(POOL_WIDTH), row_blk(D), fixed((D, D)), fixed((1, D)),
                  fixed((N_EXPERTS, D)), fixed((N_EXPERTS, 1)), fixed((tm, tm))],
        out_specs=[row_blk(D), row_blk(D), sel_blk, sel_blk, sel_blk, fixed((N_EXPERTS, LANES))],
        out_shape=[jax.ShapeDtypeStruct((T, D), F32), jax.ShapeDtypeStruct((T, D), F32),
                   jax.ShapeDtypeStruct((TOP_K, T), I32), jax.ShapeDtypeStruct((TOP_K, T), F32),
                   jax.ShapeDtypeStruct((TOP_K, T), I32), jax.ShapeDtypeStruct((N_EXPERTS, LANES), F32)],
        scratch_shapes=[pltpu.VMEM((N_EXPERTS, 1), F32)],
        compiler_params=pltpu.CompilerParams(
            dimension_semantics=("arbitrary",), vmem_limit_bytes=VMEM_LIMIT),
        name="out_proj_router",
    )(attn, pool, x, w_out.astype(BF16), g_moe.reshape(1, D), w_router.T, b_router.reshape(N_EXPERTS, 1), tri)


def _dispatch_kernel(dest_ref, h_ref, out_ref, sem):
    tm = h_ref.shape[0]

    def issue(r8, _):
        for dr in range(8):
            r = r8 * 8 + dr
            for kk in range(TOP_K):
                pltpu.make_async_copy(h_ref.at[pl.ds(r, 1)], out_ref.at[pl.ds(dest_ref[kk, r], 1)], sem).start()
        return 0

    lax.fori_loop(0, tm // 8, issue, 0)
    for _ in range(TOP_K):
        pltpu.make_async_copy(h_ref, out_ref.at[pl.ds(0, tm)], sem).wait()


def _dispatch(h2, dest):
    T, D = h2.shape
    tm = PROJ_TILE
    dest3 = dest.reshape(TOP_K, T // tm, tm).transpose(1, 0, 2)
    return pl.pallas_call(
        _dispatch_kernel,
        grid=(T // tm,),
        in_specs=[pl.BlockSpec((None, TOP_K, tm), lambda i: (i, 0, 0), memory_space=pltpu.SMEM),
                  pl.BlockSpec((tm, D), lambda i: (i, 0))],
        out_specs=pl.BlockSpec(memory_space=pl.ANY),
        out_shape=jax.ShapeDtypeStruct((T * TOP_K, D), F32),
        scratch_shapes=[pltpu.SemaphoreType.DMA],
        compiler_params=pltpu.CompilerParams(
            dimension_semantics=("arbitrary",), vmem_limit_bytes=VMEM_LIMIT),
        name="dispatch_rows",
    )(dest3, h2)


def _expert_kernel(vb_ref, ve_ref, vlo_ref, vhi_ref, nxt_ref, par_ref,
                   x_ref, wgu_hbm, bgu_ref, wd_hbm, bd_ref, y_ref,
                   wgu_stage, wd_stage, wgu_bf, wd_bf, sem):
    v = pl.program_id(0)
    tm = x_ref.shape[0]
    lo = vlo_ref[v]
    hi = vhi_ref[v]
    blk = vb_ref[v]
    e = ve_ref[v]
    prev = jnp.maximum(v - 1, 0)
    first_of_block = jnp.logical_or(v == 0, blk != vb_ref[prev])
    first_of_expert = jnp.logical_or(v == 0, e != ve_ref[prev])
    slot = par_ref[e]

    def weight_copies(expert, s):
        return (pltpu.make_async_copy(wgu_hbm.at[expert], wgu_stage.at[s], sem.at[0, s]),
                pltpu.make_async_copy(wd_hbm.at[expert], wd_stage.at[s], sem.at[1, s]))

    @pl.when(v == 0)
    def _():
        for c in weight_copies(e, slot):
            c.start()

    @pl.when(jnp.logical_and(first_of_expert, hi > lo))
    def _():
        for c in weight_copies(e, slot):
            c.wait()
        nxt = nxt_ref[e]

        @pl.when(nxt >= 0)
        def _():
            for c in weight_copies(nxt, 1 - slot):
                c.start()

        wgu_bf[...] = wgu_stage[slot].astype(BF16)
        wd_bf[...] = wd_stage[slot].astype(BF16)

    @pl.when(hi > lo)
    def _():
        gu = jnp.dot(x_ref[...].astype(BF16), wgu_bf[...], preferred_element_type=F32) + bgu_ref[0]
        gate = jnp.minimum(gu[:, :D_EXPERT], SWIGLU_LIMIT)
        up = jnp.clip(gu[:, D_EXPERT:], -SWIGLU_LIMIT, SWIGLU_LIMIT)
        act = gate * jax.nn.sigmoid(SWIGLU_ALPHA * gate) * (up + 1.0)
        y = jnp.dot(act.astype(BF16), wd_bf[...], preferred_element_type=F32) + bd_ref[0]
        rows = blk * tm + lax.broadcasted_iota(I32, (tm, 1), 0)
        valid = jnp.logical_and(rows >= lo, rows < hi)

        @pl.when(first_of_block)
        def _():
            y_ref[...] = jnp.where(valid, y, 0.0)

        @pl.when(jnp.logical_not(first_of_block))
        def _():
            y_ref[...] = jnp.where(valid, y, y_ref[...])


def _experts(x_sorted, visits, w_gate_up, b_gate_up, w_down, b_down):
    n_rows, D = x_sorted.shape
    tm = EXPERT_TILE
    n_visits = visits[0].shape[0]
    n_prefetch = len(visits)
    by_block = lambda v, vb, *_: (vb[v], 0)
    by_expert = lambda v, vb, ve, *_: (ve[v], 0, 0)
    grid_spec = pltpu.PrefetchScalarGridSpec(
        num_scalar_prefetch=n_prefetch,
        grid=(n_visits,),
        in_specs=[
            pl.BlockSpec((tm, D), by_block),
            pl.BlockSpec(memory_space=pl.ANY),
            pl.BlockSpec((1, 1, 2 * D_EXPERT), by_expert),
            pl.BlockSpec(memory_space=pl.ANY),
            pl.BlockSpec((1, 1, D), by_expert),
        ],
        out_specs=pl.BlockSpec((tm, D), by_block),
        scratch_shapes=[pltpu.VMEM((2, D, 2 * D_EXPERT), F32), pltpu.VMEM((2, D_EXPERT, D), F32),
                        pltpu.VMEM((D, 2 * D_EXPERT), BF16), pltpu.VMEM((D_EXPERT, D), BF16),
                        pltpu.SemaphoreType.DMA((2, 2))],
    )
    return pl.pallas_call(
        _expert_kernel,
        grid_spec=grid_spec,
        out_shape=jax.ShapeDtypeStruct((n_rows, D), F32),
        compiler_params=pltpu.CompilerParams(
            dimension_semantics=("arbitrary",), vmem_limit_bytes=VMEM_LIMIT),
        name="expert_gmm",
    )(*visits, x_sorted, w_gate_up, b_gate_up.reshape(N_EXPERTS, 1, -1), w_down, b_down.reshape(N_EXPERTS, 1, -1))


def _visit_schedule(counts, n_rows):
    tm = EXPERT_TILE
    n_blocks = n_rows // tm
    n_visits = n_blocks + N_EXPERTS - 1
    ends = jnp.cumsum(counts)
    starts = ends - counts
    first_blk = starts // tm
    last_blk = jnp.maximum(ends - 1, 0) // tm
    per_e = jnp.where(counts > 0, last_blk - first_blk + 1, 0)
    vend = jnp.cumsum(per_e)
    vstart = vend - per_e
    total = vend[-1]
    v = jnp.arange(n_visits, dtype=I32)
    vc = jnp.minimum(v, total - 1)
    e = jnp.sum(vc[:, None] >= vend[None, :], axis=1).astype(I32)
    blk = first_blk[e] + vc - vstart[e]
    lo = jnp.maximum(starts[e], blk * tm)
    hi = jnp.minimum(ends[e], (blk + 1) * tm)
    real = v < total
    lo = jnp.where(real, lo, 0)
    hi = jnp.where(real, hi, 0)
    ids = jnp.arange(N_EXPERTS, dtype=I32)
    nonempty = counts > 0
    later = jnp.logical_and(nonempty[None, :], ids[None, :] > ids[:, None])
    nxt = jnp.where(jnp.any(later, axis=1), jnp.argmax(later, axis=1), -1).astype(I32)
    parity = ((jnp.cumsum(nonempty.astype(I32)) - nonempty.astype(I32)) % 2).astype(I32)
    return blk.astype(I32), e, lo.astype(I32), hi.astype(I32), nxt, parity


def _combine_kernel(dest_ref, dest_next_ref, y_ref, x1_ref, gate_ref, g_ref, o_ref, buf_ref, sem):
    i = pl.program_id(0)
    tm = x1_ref.shape[0]
    slot = i % 2

    def gather(idx_ref, s):
        def issue(r8, _):
            for dr in range(8):
                r = r8 * 8 + dr
                for kk in range(TOP_K):
                    pltpu.make_async_copy(y_ref.at[pl.ds(idx_ref[kk, r], 1)], buf_ref.at[s, kk, pl.ds(r, 1)],
                                          sem.at[s]).start()
            return 0
        lax.fori_loop(0, tm // 8, issue, 0)

    @pl.when(i == 0)
    def _():
        gather(dest_ref, slot)

    @pl.when(i + 1 < pl.num_programs(0))
    def _():
        gather(dest_next_ref, 1 - slot)

    for kk in range(TOP_K):
        pltpu.make_async_copy(y_ref.at[pl.ds(0, tm)], buf_ref.at[slot, kk], sem.at[slot]).wait()
    acc = x1_ref[...]
    gates = gate_ref[...]
    for kk in range(TOP_K):
        acc = acc + gates[:, kk:kk + 1] * buf_ref[slot, kk]
    o_ref[...] = _rms(acc, g_ref[...])


def _combine(y, dest, x1, gates_t, g_final):
    T, D = x1.shape
    tm = COMBINE_TILE
    nt = T // tm
    dest3 = dest.reshape(TOP_K, nt, tm).transpose(1, 0, 2)
    return pl.pallas_call(
        _combine_kernel,
        grid=(nt,),
        in_specs=[pl.BlockSpec((None, TOP_K, tm), lambda i: (i, 0, 0), memory_space=pltpu.SMEM),
                  pl.BlockSpec((None, TOP_K, tm), lambda i: (jnp.minimum(i + 1, nt - 1), 0, 0),
                               memory_space=pltpu.SMEM),
                  pl.BlockSpec(memory_space=pl.ANY),
                  pl.BlockSpec((tm, D), lambda i: (i, 0)),
                  pl.BlockSpec((tm, TOP_K), lambda i: (i, 0)),
                  pl.BlockSpec((1, D), lambda i: (0, 0))],
        out_specs=pl.BlockSpec((tm, D), lambda i: (i, 0)),
        out_shape=jax.ShapeDtypeStruct((T, D), F32),
        scratch_shapes=[pltpu.VMEM((2, TOP_K, tm, D), F32), pltpu.SemaphoreType.DMA((2,))],
        compiler_params=pltpu.CompilerParams(
            dimension_semantics=("arbitrary",), vmem_limit_bytes=VMEM_LIMIT),
        name="combine_norm",
    )(dest3, dest3, y, x1, gates_t, g_final.reshape(1, D))


def kernel(x, g_mix, w_in, pool_w, pool_scale, w_out, g_moe, w_router, b_router, w_gate_up, b_gate_up,
           w_down, b_down, g_final):
    B, S, D = x.shape
    T = B * S
    assert g_mix.shape[0] == 1, "single-layer problem: the final norm is fused into the combine step"
    q, k, v, pool = _in_proj(x, g_mix[0], w_in[0], pool_w[0], pool_scale[0])
    attn = _attention(q, k, v)
    x1, h2, idx, gates, rank, cnt = _out_proj_router(
        attn.reshape(T, SB_WIDTH), pool.reshape(T, POOL_WIDTH), x.reshape(T, D),
        w_out[0], g_moe[0], w_router[0], b_router[0])
    counts = cnt[:, 0].astype(I32)
    starts = jnp.cumsum(counts) - counts
    dest = rank
    for e in range(N_EXPERTS):
        dest = dest + jnp.where(idx == e, starts[e], 0)
    x_sorted = _dispatch(h2, dest)
    visits = _visit_schedule(counts, T * TOP_K)
    y = _experts(x_sorted, visits, w_gate_up[0], b_gate_up[0], w_down[0], b_down[0])
    return _combine(y, dest, x1, gates.T, g_final).reshape(B, S, D)
```

```python
import functools

import jax
import jax.numpy as jnp
from jax import lax
from jax.experimental import pallas as pl
from jax.experimental.pallas import tpu as pltpu
from jax.experimental.pallas import tpu_sc as plsc

F32 = jnp.float32
BF16 = jnp.bfloat16
I32 = jnp.int32
U32 = jnp.uint32

D_MODEL = 1024
SB_HEADS = 8
SB_HEAD_DIM = 64
SB_WIDTH = SB_HEADS * SB_HEAD_DIM
POOL_WINDOWS = (2, 4, 8, 16)
POOL_WIDTH = 512
POOL_GROUP_DIM = 128
N_EXPERTS = 32
TOP_K = 4
D_EXPERT = 1024
SWIGLU_LIMIT = 7.0
SWIGLU_ALPHA = 1.702
RMS_EPS = 1e-5

LANES = 128
HALO = 16
PROJ_TILE = 512
ATTN_TILE = 256
EXPERT_TILE = 256
COMBINE_TILE = 256
ATTN_SKIP_LOG2 = -160.0
LOG2_E = 1.4426950408889634
VMEM_LIMIT = 56 * 1024 * 1024

_NT = (((1,), (1,)), ((), ()))


def _rms(x, g):
    ms = jnp.mean(x * x, axis=-1, keepdims=True)
    return x * lax.rsqrt(ms + RMS_EPS) * g


def _pack_rows(x):
    n = x.shape[1] // 2
    lo = lax.bitcast_convert_type(x[:, :n].astype(BF16).astype(F32), U32)
    hi = lax.bitcast_convert_type(x[:, n:].astype(BF16).astype(F32), U32)
    return (lo >> 16) | (hi & jnp.uint32(0xFFFF0000))


def _unpack_rows(w):
    lo = lax.bitcast_convert_type(w << 16, F32)
    hi = lax.bitcast_convert_type(w & jnp.uint32(0xFFFF0000), F32)
    return lo, hi


def _in_proj_kernel(x_ref, g_ref, w_ref, pw_ref, ps_ref, q_ref, k_ref, v_ref, p_ref, uext_ref):
    s = pl.program_id(1)
    tm = x_ref.shape[1]
    h = _rms(x_ref[0], g_ref[...])
    proj = jnp.dot(h.astype(BF16), w_ref[...], preferred_element_type=F32)
    q_ref[0] = (proj[:, 0:SB_WIDTH] * (LOG2_E * SB_HEAD_DIM ** -0.5)).astype(BF16)
    k_ref[0] = proj[:, SB_WIDTH:2 * SB_WIDTH].astype(BF16)
    v_ref[0] = proj[:, 2 * SB_WIDTH:3 * SB_WIDTH].astype(BF16)
    u = proj[:, 3 * SB_WIDTH:]

    @pl.when(s == 0)
    def _():
        uext_ref[0:HALO, :] = jnp.zeros((HALO, POOL_WIDTH), F32)

    uext_ref[HALO:, :] = u
    t = s * tm + lax.broadcasted_iota(I32, (tm, 1), 0)
    for g, w in enumerate(POOL_WINDOWS):
        sl = slice(g * POOL_GROUP_DIM, (g + 1) * POOL_GROUP_DIM)
        ug = u[:, sl]
        acc = ug
        for i in range(1, w):
            acc = acc + uext_ref[HALO - i:HALO - i + tm, sl]
        count = jnp.minimum(t + 1, w).astype(F32)
        pooled = acc / count - ug
        mixed = jnp.dot(pooled.astype(BF16), pw_ref[g], preferred_element_type=F32)
        p_ref[0, :, sl] = (mixed * ps_ref[:, sl]).astype(BF16)
    uext_ref[0:HALO, :] = u[tm - HALO:, :]


def _in_proj(x, g_mix, w_in, pool_w, pool_scale):
    B, S, D = x.shape
    tm = PROJ_TILE
    n_out = w_in.shape[1]
    out_sd = jax.ShapeDtypeStruct((B, S, SB_WIDTH), BF16)
    blk = pl.BlockSpec((1, tm, SB_WIDTH), lambda b, s: (b, s, 0))
    return pl.pallas_call(
        _in_proj_kernel,
        grid=(B, S // tm),
        in_specs=[
            pl.BlockSpec((1, tm, D), lambda b, s: (b, s, 0)),
            pl.BlockSpec((1, D), lambda b, s: (0, 0)),
            pl.BlockSpec((D, n_out), lambda b, s: (0, 0)),
            pl.BlockSpec(pool_w.shape, lambda b, s: (0, 0, 0)),
            pl.BlockSpec((1, POOL_WIDTH), lambda b, s: (0, 0)),
        ],
        out_specs=[blk, blk, blk, blk],
        out_shape=[out_sd, out_sd, out_sd, out_sd],
        scratch_shapes=[pltpu.VMEM((HALO + tm, POOL_WIDTH), F32)],
        compiler_params=pltpu.CompilerParams(
            dimension_semantics=("arbitrary", "arbitrary"), vmem_limit_bytes=VMEM_LIMIT),
        name="in_proj_pool",
    )(x, g_mix.reshape(1, D), w_in.astype(BF16), pool_w.astype(BF16), pool_scale.reshape(1, POOL_WIDTH))


def _attn_kernel(q_ref, k_ref, v_ref, o_ref):
    tq = q_ref.shape[1]
    tk = tq
    qi = pl.program_id(2)
    q = q_ref[0]
    lane = lax.broadcasted_iota(I32, (tq, LANES), 1)
    row = lax.broadcasted_iota(I32, (tq, tk), 0)
    col = lax.broadcasted_iota(I32, (tq, tk), 1)
    causal = col < row
    tri = (row > col).astype(BF16)
    qs = [jnp.where((lane >= h * SB_HEAD_DIM) & (lane < (h + 1) * SB_HEAD_DIM), q, jnp.zeros_like(q))
          for h in range(2)]

    def tile(qh, j, mask):
        start = pl.multiple_of(j * tk, tk)
        kb = k_ref[0, pl.ds(start, tk), :]
        vb = v_ref[0, pl.ds(start, tk), :]
        z = lax.dot_general(qh, kb, _NT, preferred_element_type=F32)
        lsm = -(jnp.maximum(z, 0.0) + jnp.log2(1.0 + jnp.exp2(-jnp.abs(z))))
        if mask:
            lsm = jnp.where(causal, lsm, 0.0)
        tail = jnp.dot(lsm.astype(BF16), tri, preferred_element_type=F32)
        total = tail[:, 0:1] + lsm[:, 0:1]
        return lsm + z + tail, total, vb

    prev = jnp.maximum(qi - 1, 0)
    has_prev = qi > 0
    accs, carries = [], []
    for h in range(2):
        lw_d, tot_d, v_d = tile(qs[h], qi, True)
        lw_p, tot_p, v_p = tile(qs[h], prev, False)
        w_d = jnp.where(causal, jnp.exp2(lw_d), 0.0).astype(BF16)
        w_p = jnp.where(has_prev, jnp.exp2(lw_p + tot_d), 0.0).astype(BF16)
        accs.append(jnp.dot(w_d, v_d, preferred_element_type=F32) + jnp.dot(w_p, v_p, preferred_element_type=F32))
        carries.append(tot_d + tot_p)

    def cond(st):
        j, c0, c1, _, _ = st
        return jnp.logical_and(j >= 0, jnp.maximum(jnp.max(c0), jnp.max(c1)) > ATTN_SKIP_LOG2)

    def body(st):
        j, c0, c1, a0, a1 = st
        out = []
        for qh, c, a in ((qs[0], c0, a0), (qs[1], c1, a1)):
            lw, tot, vb = tile(qh, j, False)
            a = a + jnp.dot(jnp.exp2(lw + c).astype(BF16), vb, preferred_element_type=F32)
            out.append((c + tot, a))
        return j - 1, out[0][0], out[1][0], out[0][1], out[1][1]

    _, _, _, acc0, acc1 = lax.while_loop(cond, body, (qi - 2, carries[0], carries[1], accs[0], accs[1]))
    o_ref[0] = jnp.where(lane < SB_HEAD_DIM, acc0, acc1).astype(o_ref.dtype)


def _attention(q, k, v):
    B, S, W = q.shape
    tq = ATTN_TILE
    n_pairs = W // LANES
    return pl.pallas_call(
        _attn_kernel,
        grid=(B, n_pairs, S // tq),
        in_specs=[
            pl.BlockSpec((1, tq, LANES), lambda b, p, i: (b, i, p)),
            pl.BlockSpec((1, S, LANES), lambda b, p, i: (b, 0, p)),
            pl.BlockSpec((1, S, LANES), lambda b, p, i: (b, 0, p)),
        ],
        out_specs=pl.BlockSpec((1, tq, LANES), lambda b, p, i: (b, i, p)),
        out_shape=jax.ShapeDtypeStruct((B, S, W), BF16),
        compiler_params=pltpu.CompilerParams(
            dimension_semantics=("arbitrary", "arbitrary", "arbitrary"), vmem_limit_bytes=VMEM_LIMIT),
        name="stickbreak_attn",
    )(q, k, v)


def _out_proj_router_kernel(attn_ref, pool_ref, x_ref, wo_ref, g_ref, wr_ref, br_ref, tri_ref,
                            x1_ref, h2_ref, idx_ref, gate_ref, rank_ref, cnt_ref, carry_ref):
    i = pl.program_id(0)
    tm = x_ref.shape[0]

    @pl.when(i == 0)
    def _():
        carry_ref[...] = jnp.zeros_like(carry_ref)

    mixed = jnp.dot(attn_ref[...], wo_ref[0:SB_WIDTH, :], preferred_element_type=F32)
    mixed = mixed + jnp.dot(pool_ref[...], wo_ref[SB_WIDTH:, :], preferred_element_type=F32)
    x1 = x_ref[...] + mixed
    h2 = _rms(x1, g_ref[...])
    x1_ref[...] = x1
    h2_ref[...] = _pack_rows(h2)

    hh = h2.astype(BF16)
    hl = (h2 - hh.astype(F32)).astype(BF16)
    wr = wr_ref[...]
    wh = wr.astype(BF16)
    wl = (wr - wh.astype(F32)).astype(BF16)
    logits = (lax.dot_general(wh, hh, _NT, preferred_element_type=F32)
              + lax.dot_general(wh, hl, _NT, preferred_element_type=F32)
              + lax.dot_general(wl, hh, _NT, preferred_element_type=F32)) + br_ref[...]

    eid = lax.broadcasted_iota(I32, (N_EXPERTS, tm), 0).astype(F32)
    work = logits
    vals, ids = [], []
    for _ in range(TOP_K):
        m = jnp.max(work, axis=0, keepdims=True)
        sel = jnp.min(jnp.where(work == m, eid, float(N_EXPERTS)), axis=0, keepdims=True)
        vals.append(m)
        ids.append(sel)
        work = jnp.where(eid == sel, -jnp.inf, work)
    exps = [jnp.exp(v - vals[0]) for v in vals]
    denom = exps[0] + exps[1] + exps[2] + exps[3]

    onehot = jnp.zeros((N_EXPERTS, tm), F32)
    for sel in ids:
        onehot = onehot + (eid == sel).astype(F32)
    before = jnp.dot(onehot.astype(BF16), tri_ref[...], preferred_element_type=F32) + carry_ref[...]
    for kk in range(TOP_K):
        idx_ref[kk:kk + 1, :] = ids[kk].astype(I32)
        gate_ref[kk:kk + 1, :] = exps[kk] / denom
        rk = jnp.sum(jnp.where(eid == ids[kk], before, 0.0), axis=0, keepdims=True)
        rank_ref[kk:kk + 1, :] = rk.astype(I32)
    carry_ref[...] = carry_ref[...] + jnp.sum(onehot, axis=1, keepdims=True)
    cnt_ref[...] = jnp.broadcast_to(carry_ref[...], cnt_ref.shape)


def _out_proj_router(attn, pool, x, w_out, g_moe, w_router, b_router):
    T, D = x.shape
    tm = PROJ_TILE
    r = lax.broadcasted_iota(I32, (tm, tm), 0)
    c = lax.broadcasted_iota(I32, (tm, tm), 1)
    tri = (r < c).astype(BF16)
    row_blk = lambda w: pl.BlockSpec((tm, w), lambda i: (i, 0))
    fixed = lambda shape: pl.BlockSpec(shape, lambda i: tuple(0 for _ in shape))
    sel_blk = pl.BlockSpec((TOP_K, tm), lambda i: (0, i))
    return pl.pallas_call(
        _out_proj_router_kernel,
        grid=(T // tm,),
        in_specs=[row_blk(SB_WIDTH), row_blk(POOL_WIDTH), row_blk(D), fixed((D, D)), fixed((1, D)),
                  fixed((N_EXPERTS, D)), fixed((N_EXPERTS, 1)), fixed((tm, tm))],
        out_specs=[row_blk(D), row_blk(D // 2), sel_blk, sel_blk, sel_blk, fixed((N_EXPERTS, LANES))],
        out_shape=[jax.ShapeDtypeStruct((T, D), F32), jax.ShapeDtypeStruct((T, D // 2), U32),
                   jax.ShapeDtypeStruct((TOP_K, T), I32), jax.ShapeDtypeStruct((TOP_K, T), F32),
                   jax.ShapeDtypeStruct((TOP_K, T), I32), jax.ShapeDtypeStruct((N_EXPERTS, LANES), F32)],
        scratch_shapes=[pltpu.VMEM((N_EXPERTS, 1), F32)],
        compiler_params=pltpu.CompilerParams(
            dimension_semantics=("arbitrary",), vmem_limit_bytes=VMEM_LIMIT),
        name="out_proj_router",
    )(attn, pool, x, w_out.astype(BF16), g_moe.reshape(1, D), w_router.T, b_router.reshape(N_EXPERTS, 1), tri)


SC_ROWS_PER_COPY = 64


def _sc_workers():
    info = plsc.get_sparse_core_info()
    return info.num_cores, info.num_cores * info.num_subcores


def _sc_dispatch(rows, dest):
    T, W = rows.shape
    K = dest.shape[0]
    sub = SC_ROWS_PER_COPY
    n_cores, n_workers = _sc_workers()
    per_w = T // n_workers
    n_chunks = per_w // sub
    assert per_w * n_workers == T and n_chunks * sub == per_w and n_chunks % 2 == 0
    idx = dest.reshape(K, n_workers, n_chunks, sub).transpose(1, 2, 0, 3).reshape(n_workers, n_chunks * K, sub)
    mesh = plsc.VectorSubcoreMesh(core_axis_name="core", subcore_axis_name="subcore")

    @functools.partial(
        pl.kernel, out_type=jax.ShapeDtypeStruct((K * T, W), rows.dtype), mesh=mesh,
        scratch_types=[pltpu.VMEM((n_chunks * K, sub), I32), pltpu.VMEM((2, sub, W), rows.dtype),
                       pltpu.SemaphoreType.DMA((2,)), pltpu.SemaphoreType.DMA((2,))])
    def scatter_rows(x_hbm, i_hbm, o_hbm, idx_v, buf, rsem, wsem):
        wid = lax.axis_index("subcore") * n_cores + lax.axis_index("core")
        base = wid * per_w
        pltpu.sync_copy(i_hbm.at[wid], idx_v)

        def read(c, slot):
            return pltpu.make_async_copy(x_hbm.at[pl.ds(base + c * sub, sub)], buf.at[slot], rsem.at[slot])

        def write(c, kk, slot):
            return pltpu.make_async_copy(buf.at[slot], o_hbm.at[idx_v.at[c * K + kk]], wsem.at[slot])

        read(0, 0).start()

        @pl.loop(0, n_chunks, step=2)
        def _(c0):
            for b in range(2):
                c = c0 + b
                read(c, b).wait()
                for kk in range(K):
                    write(c, kk, b).start()

                @pl.when(c + 1 < n_chunks)
                def _():
                    @pl.when(c >= 1)
                    def _():
                        for kk in range(K):
                            write(c - 1, kk, 1 - b).wait()
                    read(c + 1, 1 - b).start()

        for kk in range(K):
            write(n_chunks - 2, kk, 0).wait()
            write(n_chunks - 1, kk, 1).wait()

    return scatter_rows(rows, idx)


def _sc_gather(table, indices):
    M = indices.shape[0]
    W = table.shape[1]
    sub = SC_ROWS_PER_COPY
    n_cores, n_workers = _sc_workers()
    per_w = M // n_workers
    n_steps = per_w // sub
    assert per_w * n_workers == M and n_steps * sub == per_w and n_steps % 2 == 0
    mesh = plsc.VectorSubcoreMesh(core_axis_name="core", subcore_axis_name="subcore")

    @functools.partial(
        pl.kernel, out_type=jax.ShapeDtypeStruct((M, W), table.dtype), mesh=mesh,
        scratch_types=[pltpu.VMEM((n_steps, sub), I32), pltpu.VMEM((2, sub, W), table.dtype),
                       pltpu.SemaphoreType.DMA((2,)), pltpu.SemaphoreType.DMA((2,))])
    def gather_rows(x_hbm, i_hbm, o_hbm, idx_v, buf, gsem, wsem):
        wid = lax.axis_index("subcore") * n_cores + lax.axis_index("core")
        base = wid * per_w
        pltpu.sync_copy(i_hbm.at[wid], idx_v)

        def gather(s, slot):
            return pltpu.make_async_copy(x_hbm.at[idx_v.at[s]], buf.at[slot], gsem.at[slot])

        def write(s, slot):
            return pltpu.make_async_copy(buf.at[slot], o_hbm.at[pl.ds(base + s * sub, sub)], wsem.at[slot])

        gather(0, 0).start()

        @pl.loop(0, n_steps, step=2)
        def _(s0):
            for b in range(2):
                s = s0 + b
                gather(s, b).wait()
                write(s, b).start()

                @pl.when(s + 1 < n_steps)
                def _():
                    @pl.when(s >= 1)
                    def _():
                        write(s - 1, 1 - b).wait()
                    gather(s + 1, 1 - b).start()

        write(n_steps - 2, 0).wait()
        write(n_steps - 1, 1).wait()

    return gather_rows(table, indices.reshape(n_workers, n_steps, sub))


def _expert_kernel(vb_ref, ve_ref, vlo_ref, vhi_ref, nxt_ref, par_ref,
                   x_ref, wgu_hbm, bgu_ref, wd_hbm, bd_ref, y_ref,
                   wgu_stage, wd_stage, wgu_bf, wd_bf, sem):
    v = pl.program_id(0)
    tm = x_ref.shape[0]
    lo = vlo_ref[v]
    hi = vhi_ref[v]
    blk = vb_ref[v]
    e = ve_ref[v]
    prev = jnp.maximum(v - 1, 0)
    first_of_block = jnp.logical_or(v == 0, blk != vb_ref[prev])
    first_of_expert = jnp.logical_or(v == 0, e != ve_ref[prev])
    slot = par_ref[e]

    def weight_copies(expert, s):
        return (pltpu.make_async_copy(wgu_hbm.at[expert], wgu_stage.at[s], sem.at[0, s]),
                pltpu.make_async_copy(wd_hbm.at[expert], wd_stage.at[s], sem.at[1, s]))

    @pl.when(v == 0)
    def _():
        for c in weight_copies(e, slot):
            c.start()

    @pl.when(jnp.logical_and(first_of_expert, hi > lo))
    def _():
        for c in weight_copies(e, slot):
            c.wait()
        nxt = nxt_ref[e]

        @pl.when(nxt >= 0)
        def _():
            for c in weight_copies(nxt, 1 - slot):
                c.start()

        wgu_bf[...] = wgu_stage[slot].astype(BF16)
        wd_bf[...] = wd_stage[slot].astype(BF16)

    @pl.when(hi > lo)
    def _():
        half = wgu_bf.shape[0] // 2
        x_lo, x_hi = _unpack_rows(x_ref[...])
        gu = (jnp.dot(x_lo.astype(BF16), wgu_bf[:half, :], preferred_element_type=F32)
              + jnp.dot(x_hi.astype(BF16), wgu_bf[half:, :], preferred_element_type=F32)) + bgu_ref[0]
        gate = jnp.minimum(gu[:, :D_EXPERT], SWIGLU_LIMIT)
        up = jnp.clip(gu[:, D_EXPERT:], -SWIGLU_LIMIT, SWIGLU_LIMIT)
        act = gate * jax.nn.sigmoid(SWIGLU_ALPHA * gate) * (up + 1.0)
        y = _pack_rows(jnp.dot(act.astype(BF16), wd_bf[...], preferred_element_type=F32) + bd_ref[0])
        rows = blk * tm + lax.broadcasted_iota(I32, (tm, 1), 0)
        valid = jnp.logical_and(rows >= lo, rows < hi)

        @pl.when(first_of_block)
        def _():
            y_ref[...] = jnp.where(valid, y, jnp.uint32(0))

        @pl.when(jnp.logical_not(first_of_block))
        def _():
            y_ref[...] = jnp.where(valid, y, y_ref[...])


def _experts(x_sorted, visits, w_gate_up, b_gate_up, w_down, b_down):
    n_rows, W = x_sorted.shape
    D = 2 * W
    tm = EXPERT_TILE
    n_visits = visits[0].shape[0]
    n_prefetch = len(visits)
    by_block = lambda v, vb, *_: (vb[v], 0)
    by_expert = lambda v, vb, ve, *_: (ve[v], 0, 0)
    grid_spec = pltpu.PrefetchScalarGridSpec(
        num_scalar_prefetch=n_prefetch,
        grid=(n_visits,),
        in_specs=[
            pl.BlockSpec((tm, W), by_block),
            pl.BlockSpec(memory_space=pl.ANY),
            pl.BlockSpec((1, 1, 2 * D_EXPERT), by_expert),
            pl.BlockSpec(memory_space=pl.ANY),
            pl.BlockSpec((1, 1, D), by_expert),
        ],
        out_specs=pl.BlockSpec((tm, W), by_block),
        scratch_shapes=[pltpu.VMEM((2, D, 2 * D_EXPERT), F32), pltpu.VMEM((2, D_EXPERT, D), F32),
                        pltpu.VMEM((D, 2 * D_EXPERT), BF16), pltpu.VMEM((D_EXPERT, D), BF16),
                        pltpu.SemaphoreType.DMA((2, 2))],
    )
    return pl.pallas_call(
        _expert_kernel,
        grid_spec=grid_spec,
        out_shape=jax.ShapeDtypeStruct((n_rows, W), U32),
        compiler_params=pltpu.CompilerParams(
            dimension_semantics=("arbitrary",), vmem_limit_bytes=VMEM_LIMIT),
        name="expert_gmm",
    )(*visits, x_sorted, w_gate_up, b_gate_up.reshape(N_EXPERTS, 1, -1), w_down, b_down.reshape(N_EXPERTS, 1, -1))


def _visit_schedule(counts, n_rows):
    tm = EXPERT_TILE
    n_blocks = n_rows // tm
    n_visits = n_blocks + N_EXPERTS - 1
    ends = jnp.cumsum(counts)
    starts = ends - counts
    first_blk = starts // tm
    last_blk = jnp.maximum(ends - 1, 0) // tm
    per_e = jnp.where(counts > 0, last_blk - first_blk + 1, 0)
    vend = jnp.cumsum(per_e)
    vstart = vend - per_e
    total = vend[-1]
    v = jnp.arange(n_visits, dtype=I32)
    vc = jnp.minimum(v, total - 1)
    e = jnp.sum(vc[:, None] >= vend[None, :], axis=1).astype(I32)
    blk = first_blk[e] + vc - vstart[e]
    lo = jnp.maximum(starts[e], blk * tm)
    hi = jnp.minimum(ends[e], (blk + 1) * tm)
    real = v < total
    lo = jnp.where(real, lo, 0)
    hi = jnp.where(real, hi, 0)
    ids = jnp.arange(N_EXPERTS, dtype=I32)
    nonempty = counts > 0
    later = jnp.logical_and(nonempty[None, :], ids[None, :] > ids[:, None])
    nxt = jnp.where(jnp.any(later, axis=1), jnp.argmax(later, axis=1), -1).astype(I32)
    parity = ((jnp.cumsum(nonempty.astype(I32)) - nonempty.astype(I32)) % 2).astype(I32)
    return blk.astype(I32), e, lo.astype(I32), hi.astype(I32), nxt, parity


def _combine_kernel(y0_ref, y1_ref, y2_ref, y3_ref, x1_ref, gate_ref, g_ref, o_ref):
    half = x1_ref.shape[1] // 2
    gates = gate_ref[...]
    lo = x1_ref[:, :half]
    hi = x1_ref[:, half:]
    for kk, y_ref in enumerate((y0_ref, y1_ref, y2_ref, y3_ref)):
        y_lo, y_hi = _unpack_rows(y_ref[...])
        lo = lo + gates[:, kk:kk + 1] * y_lo
        hi = hi + gates[:, kk:kk + 1] * y_hi
    ms = (jnp.sum(lo * lo, axis=-1, keepdims=True) + jnp.sum(hi * hi, axis=-1, keepdims=True)) / (2 * half)
    scale = lax.rsqrt(ms + RMS_EPS)
    o_ref[:, :half] = lo * scale * g_ref[:, :half]
    o_ref[:, half:] = hi * scale * g_ref[:, half:]


def _combine(y_tok, x1, gates_t, g_final):
    T, D = x1.shape
    tm = COMBINE_TILE
    nt = T // tm
    y_spec = lambda kk: pl.BlockSpec((tm, D // 2), lambda i: (kk * nt + i, 0))
    return pl.pallas_call(
        _combine_kernel,
        grid=(nt,),
        in_specs=[y_spec(0), y_spec(1), y_spec(2), y_spec(3),
                  pl.BlockSpec((tm, D), lambda i: (i, 0)),
                  pl.BlockSpec((tm, TOP_K), lambda i: (i, 0)),
                  pl.BlockSpec((1, D), lambda i: (0, 0))],
        out_specs=pl.BlockSpec((tm, D), lambda i: (i, 0)),
        out_shape=jax.ShapeDtypeStruct((T, D), F32),
        compiler_params=pltpu.CompilerParams(
            dimension_semantics=("arbitrary",), vmem_limit_bytes=VMEM_LIMIT),
        name="combine_norm",
    )(y_tok, y_tok, y_tok, y_tok, x1, gates_t, g_final.reshape(1, D))


def kernel(x, g_mix, w_in, pool_w, pool_scale, w_out, g_moe, w_router, b_router, w_gate_up, b_gate_up,
           w_down, b_down, g_final):
    B, S, D = x.shape
    T = B * S
    assert g_mix.shape[0] == 1, "single-layer problem: the final norm is fused into the combine step"
    q, k, v, pool = _in_proj(x, g_mix[0], w_in[0], pool_w[0], pool_scale[0])
    attn = _attention(q, k, v)
    x1, h2, idx, gates, rank, cnt = _out_proj_router(
        attn.reshape(T, SB_WIDTH), pool.reshape(T, POOL_WIDTH), x.reshape(T, D),
        w_out[0], g_moe[0], w_router[0], b_router[0])
    counts = cnt[:, 0].astype(I32)
    starts = jnp.cumsum(counts) - counts
    dest = rank
    for e in range(N_EXPERTS):
        dest = dest + jnp.where(idx == e, starts[e], 0)
    x_sorted = _sc_dispatch(h2, dest)
    visits = _visit_schedule(counts, T * TOP_K)
    y = _experts(x_sorted, visits, w_gate_up[0], b_gate_up[0], w_down[0], b_down[0])
    y_tok = _sc_gather(y, dest.reshape(-1))
    return _combine(y_tok, x1, gates.T, g_final).reshape(B, S, D)
```

```python
import functools

import jax
import jax.numpy as jnp
from jax import lax
from jax.experimental import pallas as pl
from jax.experimental.pallas import tpu as pltpu
from jax.experimental.pallas import tpu_sc as plsc

F32 = jnp.float32
BF16 = jnp.bfloat16
I32 = jnp.int32
U32 = jnp.uint32

D_MODEL = 1024
SB_HEADS = 8
SB_HEAD_DIM = 64
SB_WIDTH = SB_HEADS * SB_HEAD_DIM
POOL_WINDOWS = (2, 4, 8, 16)
POOL_WIDTH = 512
POOL_GROUP_DIM = 128
N_EXPERTS = 32
TOP_K = 4
D_EXPERT = 1024
SWIGLU_LIMIT = 7.0
SWIGLU_ALPHA = 1.702
RMS_EPS = 1e-5

LANES = 128
HALO = 16
PROJ_TILE = 512
ATTN_TILE = 256
EXPERT_TILE = 512
COMBINE_TILE = 256
ATTN_SKIP_LOG2 = -160.0
LOG2_E = 1.4426950408889634
VMEM_LIMIT = 56 * 1024 * 1024

_NT = (((1,), (1,)), ((), ()))


def _rms(x, g):
    ms = jnp.mean(x * x, axis=-1, keepdims=True)
    return x * lax.rsqrt(ms + RMS_EPS) * g


def _pack_rows(x):
    n = x.shape[1] // 2
    lo = lax.bitcast_convert_type(x[:, :n].astype(BF16).astype(F32), U32)
    hi = lax.bitcast_convert_type(x[:, n:].astype(BF16).astype(F32), U32)
    return (lo >> 16) | (hi & jnp.uint32(0xFFFF0000))


def _unpack_rows(w):
    lo = lax.bitcast_convert_type(w << 16, F32)
    hi = lax.bitcast_convert_type(w & jnp.uint32(0xFFFF0000), F32)
    return lo, hi


def _in_proj_kernel(x_ref, g_ref, w_ref, pw_ref, ps_ref, q_ref, k_ref, v_ref, p_ref, uext_ref):
    s = pl.program_id(1)
    tm = x_ref.shape[1]
    h = _rms(x_ref[0], g_ref[...])
    proj = jnp.dot(h.astype(BF16), w_ref[...], preferred_element_type=F32)
    q_ref[0] = (proj[:, 0:SB_WIDTH] * (LOG2_E * SB_HEAD_DIM ** -0.5)).astype(BF16)
    k_ref[0] = proj[:, SB_WIDTH:2 * SB_WIDTH].astype(BF16)
    v_ref[0] = proj[:, 2 * SB_WIDTH:3 * SB_WIDTH].astype(BF16)
    u = proj[:, 3 * SB_WIDTH:]

    @pl.when(s == 0)
    def _():
        uext_ref[0:HALO, :] = jnp.zeros((HALO, POOL_WIDTH), F32)

    uext_ref[HALO:, :] = u
    t = s * tm + lax.broadcasted_iota(I32, (tm, 1), 0)
    for g, w in enumerate(POOL_WINDOWS):
        sl = slice(g * POOL_GROUP_DIM, (g + 1) * POOL_GROUP_DIM)
        ug = u[:, sl]
        acc = ug
        for i in range(1, w):
            acc = acc + uext_ref[HALO - i:HALO - i + tm, sl]
        count = jnp.minimum(t + 1, w).astype(F32)
        pooled = acc / count - ug
        mixed = jnp.dot(pooled.astype(BF16), pw_ref[g], preferred_element_type=F32)
        p_ref[0, :, sl] = (mixed * ps_ref[:, sl]).astype(BF16)
    uext_ref[0:HALO, :] = u[tm - HALO:, :]


def _in_proj(x, g_mix, w_in, pool_w, pool_scale):
    B, S, D = x.shape
    tm = PROJ_TILE
    n_out = w_in.shape[1]
    out_sd = jax.ShapeDtypeStruct((B, S, SB_WIDTH), BF16)
    blk = pl.BlockSpec((1, tm, SB_WIDTH), lambda b, s: (b, s, 0))
    return pl.pallas_call(
        _in_proj_kernel,
        grid=(B, S // tm),
        in_specs=[
            pl.BlockSpec((1, tm, D), lambda b, s: (b, s, 0)),
            pl.BlockSpec((1, D), lambda b, s: (0, 0)),
            pl.BlockSpec((D, n_out), lambda b, s: (0, 0)),
            pl.BlockSpec(pool_w.shape, lambda b, s: (0, 0, 0)),
            pl.BlockSpec((1, POOL_WIDTH), lambda b, s: (0, 0)),
        ],
        out_specs=[blk, blk, blk, blk],
        out_shape=[out_sd, out_sd, out_sd, out_sd],
        scratch_shapes=[pltpu.VMEM((HALO + tm, POOL_WIDTH), F32)],
        compiler_params=pltpu.CompilerParams(
            dimension_semantics=("arbitrary", "arbitrary"), vmem_limit_bytes=VMEM_LIMIT),
        name="in_proj_pool",
    )(x, g_mix.reshape(1, D), w_in.astype(BF16), pool_w.astype(BF16), pool_scale.reshape(1, POOL_WIDTH))


def _attn_kernel(q_ref, k_ref, v_ref, o_ref):
    tq = q_ref.shape[1]
    tk = tq
    qi = pl.program_id(2)
    q = q_ref[0]
    lane = lax.broadcasted_iota(I32, (tq, LANES), 1)
    row = lax.broadcasted_iota(I32, (tq, tk), 0)
    col = lax.broadcasted_iota(I32, (tq, tk), 1)
    causal = col < row
    tri = (row > col).astype(BF16)
    qs = [jnp.where((lane >= h * SB_HEAD_DIM) & (lane < (h + 1) * SB_HEAD_DIM), q, jnp.zeros_like(q))
          for h in range(2)]

    def tile(qh, j, mask):
        start = pl.multiple_of(j * tk, tk)
        kb = k_ref[0, pl.ds(start, tk), :]
        vb = v_ref[0, pl.ds(start, tk), :]
        z = lax.dot_general(qh, kb, _NT, preferred_element_type=F32)
        lsm = -(jnp.maximum(z, 0.0) + jnp.log2(1.0 + jnp.exp2(-jnp.abs(z))))
        if mask:
            lsm = jnp.where(causal, lsm, 0.0)
        tail = jnp.dot(lsm.astype(BF16), tri, preferred_element_type=F32)
        total = tail[:, 0:1] + lsm[:, 0:1]
        return lsm + z + tail, total, vb

    prev = jnp.maximum(qi - 1, 0)
    has_prev = qi > 0
    accs, carries = [], []
    for h in range(2):
        lw_d, tot_d, v_d = tile(qs[h], qi, True)
        lw_p, tot_p, v_p = tile(qs[h], prev, False)
        w_d = jnp.where(causal, jnp.exp2(lw_d), 0.0).astype(BF16)
        w_p = jnp.where(has_prev, jnp.exp2(lw_p + tot_d), 0.0).astype(BF16)
        accs.append(jnp.dot(w_d, v_d, preferred_element_type=F32) + jnp.dot(w_p, v_p, preferred_element_type=F32))
        carries.append(tot_d + tot_p)

    def cond(st):
        j, c0, c1, _, _ = st
        return jnp.logical_and(j >= 0, jnp.maximum(jnp.max(c0), jnp.max(c1)) > ATTN_SKIP_LOG2)

    def body(st):
        j, c0, c1, a0, a1 = st
        out = []
        for qh, c, a in ((qs[0], c0, a0), (qs[1], c1, a1)):
            lw, tot, vb = tile(qh, j, False)
            a = a + jnp.dot(jnp.exp2(lw + c).astype(BF16), vb, preferred_element_type=F32)
            out.append((c + tot, a))
        return j - 1, out[0][0], out[1][0], out[0][1], out[1][1]

    _, _, _, acc0, acc1 = lax.while_loop(cond, body, (qi - 2, carries[0], carries[1], accs[0], accs[1]))
    o_ref[0] = jnp.where(lane < SB_HEAD_DIM, acc0, acc1).astype(o_ref.dtype)


def _attention(q, k, v):
    B, S, W = q.shape
    tq = ATTN_TILE
    n_pairs = W // LANES
    return pl.pallas_call(
        _attn_kernel,
        grid=(B, n_pairs, S // tq),
        in_specs=[
            pl.BlockSpec((1, tq, LANES), lambda b, p, i: (b, i, p)),
            pl.BlockSpec((1, S, LANES), lambda b, p, i: (b, 0, p)),
            pl.BlockSpec((1, S, LANES), lambda b, p, i: (b, 0, p)),
        ],
        out_specs=pl.BlockSpec((1, tq, LANES), lambda b, p, i: (b, i, p)),
        out_shape=jax.ShapeDtypeStruct((B, S, W), BF16),
        compiler_params=pltpu.CompilerParams(
            dimension_semantics=("arbitrary", "arbitrary", "arbitrary"), vmem_limit_bytes=VMEM_LIMIT),
        name="stickbreak_attn",
    )(q, k, v)


def _out_proj_router_kernel(attn_ref, pool_ref, x_ref, wo_ref, g_ref, wr_ref, br_ref, tri_ref,
                            x1_ref, h2_ref, gate_ref, dest_ref, cnt_ref, carry_ref, *, capacity):
    i = pl.program_id(0)
    tm = x_ref.shape[0]

    @pl.when(i == 0)
    def _():
        carry_ref[...] = jnp.zeros_like(carry_ref)

    mixed = jnp.dot(attn_ref[...], wo_ref[0:SB_WIDTH, :], preferred_element_type=F32)
    mixed = mixed + jnp.dot(pool_ref[...], wo_ref[SB_WIDTH:, :], preferred_element_type=F32)
    x1 = x_ref[...] + mixed
    h2 = _rms(x1, g_ref[...])
    x1_ref[...] = x1
    h2_ref[...] = _pack_rows(h2)

    hh = h2.astype(BF16)
    hl = (h2 - hh.astype(F32)).astype(BF16)
    wr = wr_ref[...]
    wh = wr.astype(BF16)
    wl = (wr - wh.astype(F32)).astype(BF16)
    logits = (lax.dot_general(wh, hh, _NT, preferred_element_type=F32)
              + lax.dot_general(wh, hl, _NT, preferred_element_type=F32)
              + lax.dot_general(wl, hh, _NT, preferred_element_type=F32)) + br_ref[...]

    eid = lax.broadcasted_iota(I32, (N_EXPERTS, tm), 0).astype(F32)
    work = logits
    vals, ids = [], []
    for _ in range(TOP_K):
        m = jnp.max(work, axis=0, keepdims=True)
        sel = jnp.min(jnp.where(work == m, eid, float(N_EXPERTS)), axis=0, keepdims=True)
        vals.append(m)
        ids.append(sel)
        work = jnp.where(eid == sel, -jnp.inf, work)
    exps = [jnp.exp(v - vals[0]) for v in vals]
    denom = exps[0] + exps[1] + exps[2] + exps[3]

    onehot = jnp.zeros((N_EXPERTS, tm), F32)
    for sel in ids:
        onehot = onehot + (eid == sel).astype(F32)
    before = jnp.dot(onehot.astype(BF16), tri_ref[...], preferred_element_type=F32) + carry_ref[...]
    for kk in range(TOP_K):
        gate_ref[kk:kk + 1, :] = exps[kk] / denom
        rk = jnp.sum(jnp.where(eid == ids[kk], before, 0.0), axis=0, keepdims=True)
        dest_ref[kk:kk + 1, :] = (ids[kk] * float(capacity) + rk).astype(I32)
    carry_ref[...] = carry_ref[...] + jnp.sum(onehot, axis=1, keepdims=True)
    cnt_ref[...] = jnp.broadcast_to(carry_ref[...], cnt_ref.shape)


def _out_proj_router(attn, pool, x, w_out, g_moe, w_router, b_router, capacity):
    T, D = x.shape
    tm = PROJ_TILE
    r = lax.broadcasted_iota(I32, (tm, tm), 0)
    c = lax.broadcasted_iota(I32, (tm, tm), 1)
    tri = (r < c).astype(BF16)
    row_blk = lambda w: pl.BlockSpec((tm, w), lambda i: (i, 0))
    fixed = lambda shape: pl.BlockSpec(shape, lambda i: tuple(0 for _ in shape))
    sel_blk = pl.BlockSpec((TOP_K, tm), lambda i: (0, i))
    assert N_EXPERTS * capacity < 2 ** 24
    return pl.pallas_call(
        functools.partial(_out_proj_router_kernel, capacity=capacity),
        grid=(T // tm,),
        in_specs=[row_blk(SB_WIDTH), row_blk(POOL_WIDTH), row_blk(D), fixed((D, D)), fixed((1, D)),
                  fixed((N_EXPERTS, D)), fixed((N_EXPERTS, 1)), fixed((tm, tm))],
        out_specs=[row_blk(D), row_blk(D // 2), sel_blk, sel_blk, fixed((N_EXPERTS, LANES))],
        out_shape=[jax.ShapeDtypeStruct((T, D), F32), jax.ShapeDtypeStruct((T, D // 2), U32),
                   jax.ShapeDtypeStruct((TOP_K, T), F32), jax.ShapeDtypeStruct((TOP_K, T), I32),
                   jax.ShapeDtypeStruct((N_EXPERTS, LANES), F32)],
        scratch_shapes=[pltpu.VMEM((N_EXPERTS, 1), F32)],
        compiler_params=pltpu.CompilerParams(
            dimension_semantics=("arbitrary",), vmem_limit_bytes=VMEM_LIMIT),
        name="out_proj_router",
    )(attn, pool, x, w_out.astype(BF16), g_moe.reshape(1, D), w_router.T, b_router.reshape(N_EXPERTS, 1), tri)


SC_ROWS_PER_COPY = 64


def _sc_workers():
    info = plsc.get_sparse_core_info()
    return info.num_cores, info.num_cores * info.num_subcores


def _sc_dispatch(rows, dest, n_out):
    T, W = rows.shape
    K = dest.shape[0]
    sub = SC_ROWS_PER_COPY
    n_cores, n_workers = _sc_workers()
    per_w = T // n_workers
    n_chunks = per_w // sub
    assert per_w * n_workers == T and n_chunks * sub == per_w and n_chunks % 2 == 0
    idx = dest.reshape(K, n_workers, n_chunks, sub).transpose(1, 2, 0, 3).reshape(n_workers, n_chunks * K, sub)
    mesh = plsc.VectorSubcoreMesh(core_axis_name="core", subcore_axis_name="subcore")

    @functools.partial(
        pl.kernel, out_type=jax.ShapeDtypeStruct((n_out, W), rows.dtype), mesh=mesh,
        scratch_types=[pltpu.VMEM((n_chunks * K, sub), I32), pltpu.VMEM((2, sub, W), rows.dtype),
                       pltpu.SemaphoreType.DMA((2,)), pltpu.SemaphoreType.DMA((2,))])
    def scatter_rows(x_hbm, i_hbm, o_hbm, idx_v, buf, rsem, wsem):
        wid = lax.axis_index("subcore") * n_cores + lax.axis_index("core")
        base = wid * per_w
        pltpu.sync_copy(i_hbm.at[wid], idx_v)

        def read(c, slot):
            return pltpu.make_async_copy(x_hbm.at[pl.ds(base + c * sub, sub)], buf.at[slot], rsem.at[slot])

        def write(c, kk, slot):
            return pltpu.make_async_copy(buf.at[slot], o_hbm.at[idx_v.at[c * K + kk]], wsem.at[slot])

        read(0, 0).start()

        @pl.loop(0, n_chunks, step=2)
        def _(c0):
            for b in range(2):
                c = c0 + b
                read(c, b).wait()
                for kk in range(K):
                    write(c, kk, b).start()

                @pl.when(c + 1 < n_chunks)
                def _():
                    @pl.when(c >= 1)
                    def _():
                        for kk in range(K):
                            write(c - 1, kk, 1 - b).wait()
                    read(c + 1, 1 - b).start()

        for kk in range(K):
            write(n_chunks - 2, kk, 0).wait()
            write(n_chunks - 1, kk, 1).wait()

    return scatter_rows(rows, idx)


def _sc_gather(table, indices):
    M = indices.shape[0]
    W = table.shape[1]
    sub = SC_ROWS_PER_COPY
    n_cores, n_workers = _sc_workers()
    per_w = M // n_workers
    n_steps = per_w // sub
    assert per_w * n_workers == M and n_steps * sub == per_w and n_steps % 2 == 0
    mesh = plsc.VectorSubcoreMesh(core_axis_name="core", subcore_axis_name="subcore")

    @functools.partial(
        pl.kernel, out_type=jax.ShapeDtypeStruct((M, W), table.dtype), mesh=mesh,
        scratch_types=[pltpu.VMEM((n_steps, sub), I32), pltpu.VMEM((2, sub, W), table.dtype),
                       pltpu.SemaphoreType.DMA((2,)), pltpu.SemaphoreType.DMA((2,))])
    def gather_rows(x_hbm, i_hbm, o_hbm, idx_v, buf, gsem, wsem):
        wid = lax.axis_index("subcore") * n_cores + lax.axis_index("core")
        base = wid * per_w
        pltpu.sync_copy(i_hbm.at[wid], idx_v)

        def gather(s, slot):
            return pltpu.make_async_copy(x_hbm.at[idx_v.at[s]], buf.at[slot], gsem.at[slot])

        def write(s, slot):
            return pltpu.make_async_copy(buf.at[slot], o_hbm.at[pl.ds(base + s * sub, sub)], wsem.at[slot])

        gather(0, 0).start()

        @pl.loop(0, n_steps, step=2)
        def _(s0):
            for b in range(2):
                s = s0 + b
                gather(s, b).wait()
                write(s, b).start()

                @pl.when(s + 1 < n_steps)
                def _():
                    @pl.when(s >= 1)
                    def _():
                        write(s - 1, 1 - b).wait()
                    gather(s + 1, 1 - b).start()

        write(n_steps - 2, 0).wait()
        write(n_steps - 1, 1).wait()

    return gather_rows(table, indices.reshape(n_workers, n_steps, sub))


def _expert_kernel(vb_ref, ve_ref, nvis_ref, nxt_ref, par_ref,
                   x_ref, wgu_hbm, bgu_ref, wd_hbm, bd_ref, y_ref,
                   wgu_stage, wd_stage, wgu_bf, wd_bf, sem):
    v = pl.program_id(0)
    real = v < nvis_ref[0]
    e = ve_ref[v]
    first_of_expert = jnp.logical_or(v == 0, e != ve_ref[jnp.maximum(v - 1, 0)])
    slot = par_ref[e]

    def weight_copies(expert, s):
        return (pltpu.make_async_copy(wgu_hbm.at[expert], wgu_stage.at[s], sem.at[0, s]),
                pltpu.make_async_copy(wd_hbm.at[expert], wd_stage.at[s], sem.at[1, s]))

    @pl.when(v == 0)
    def _():
        for c in weight_copies(e, slot):
            c.start()

    @pl.when(jnp.logical_and(first_of_expert, real))
    def _():
        for c in weight_copies(e, slot):
            c.wait()
        nxt = nxt_ref[e]

        @pl.when(nxt >= 0)
        def _():
            for c in weight_copies(nxt, 1 - slot):
                c.start()

        wgu_bf[...] = wgu_stage[slot].astype(BF16)
        wd_bf[...] = wd_stage[slot].astype(BF16)

    @pl.when(real)
    def _():
        half = wgu_bf.shape[0] // 2
        x_lo, x_hi = _unpack_rows(x_ref[...])
        gu = (jnp.dot(x_lo.astype(BF16), wgu_bf[:half, :], preferred_element_type=F32)
              + jnp.dot(x_hi.astype(BF16), wgu_bf[half:, :], preferred_element_type=F32)) + bgu_ref[0]
        gate = jnp.minimum(gu[:, :D_EXPERT], SWIGLU_LIMIT)
        up = jnp.clip(gu[:, D_EXPERT:], -SWIGLU_LIMIT, SWIGLU_LIMIT)
        act = gate * jax.nn.sigmoid(SWIGLU_ALPHA * gate) * (up + 1.0)
        y_ref[...] = _pack_rows(jnp.dot(act.astype(BF16), wd_bf[...], preferred_element_type=F32) + bd_ref[0])


def _experts(x_sorted, visits, w_gate_up, b_gate_up, w_down, b_down):
    n_rows, W = x_sorted.shape
    D = 2 * W
    tm = EXPERT_TILE
    n_visits = visits[0].shape[0]
    n_prefetch = len(visits)
    by_block = lambda v, vb, *_: (vb[v], 0)
    by_expert = lambda v, vb, ve, *_: (ve[v], 0, 0)
    grid_spec = pltpu.PrefetchScalarGridSpec(
        num_scalar_prefetch=n_prefetch,
        grid=(n_visits,),
        in_specs=[
            pl.BlockSpec((tm, W), by_block),
            pl.BlockSpec(memory_space=pl.ANY),
            pl.BlockSpec((1, 1, 2 * D_EXPERT), by_expert),
            pl.BlockSpec(memory_space=pl.ANY),
            pl.BlockSpec((1, 1, D), by_expert),
        ],
        out_specs=pl.BlockSpec((tm, W), by_block),
        scratch_shapes=[pltpu.VMEM((2, D, 2 * D_EXPERT), F32), pltpu.VMEM((2, D_EXPERT, D), F32),
                        pltpu.VMEM((D, 2 * D_EXPERT), BF16), pltpu.VMEM((D_EXPERT, D), BF16),
                        pltpu.SemaphoreType.DMA((2, 2))],
    )
    return pl.pallas_call(
        _expert_kernel,
        grid_spec=grid_spec,
        out_shape=jax.ShapeDtypeStruct((n_rows, W), U32),
        compiler_params=pltpu.CompilerParams(
            dimension_semantics=("arbitrary",), vmem_limit_bytes=VMEM_LIMIT),
        name="expert_gmm",
    )(*visits, x_sorted, w_gate_up, b_gate_up.reshape(N_EXPERTS, 1, -1), w_down, b_down.reshape(N_EXPERTS, 1, -1))


def _visit_schedule(counts, n_assign, capacity):
    tm = EXPERT_TILE
    n_visits = n_assign // tm + N_EXPERTS
    per_e = (counts + tm - 1) // tm
    vend = jnp.cumsum(per_e)
    nvis = vend[-1]
    vc = jnp.minimum(jnp.arange(n_visits, dtype=I32), nvis - 1)
    done = vc[:, None] >= vend[None, :]
    e = jnp.sum(done, axis=1).astype(I32)
    blk = e * (capacity // tm) + vc - jnp.sum(jnp.where(done, per_e[None, :], 0), axis=1)
    ids = jnp.arange(N_EXPERTS, dtype=I32)
    nonempty = counts > 0
    later = jnp.logical_and(nonempty[None, :], ids[None, :] > ids[:, None])
    nxt = jnp.where(jnp.any(later, axis=1), jnp.argmax(later, axis=1), -1).astype(I32)
    parity = ((jnp.cumsum(nonempty.astype(I32)) - nonempty.astype(I32)) % 2).astype(I32)
    return blk.astype(I32), e, nvis.reshape(1).astype(I32), nxt, parity


def _combine_kernel(y0_ref, y1_ref, y2_ref, y3_ref, x1_ref, gate_ref, g_ref, o_ref):
    half = x1_ref.shape[1] // 2
    gates = gate_ref[...]
    lo = x1_ref[:, :half]
    hi = x1_ref[:, half:]
    for kk, y_ref in enumerate((y0_ref, y1_ref, y2_ref, y3_ref)):
        y_lo, y_hi = _unpack_rows(y_ref[...])
        lo = lo + gates[:, kk:kk + 1] * y_lo
        hi = hi + gates[:, kk:kk + 1] * y_hi
    ms = (jnp.sum(lo * lo, axis=-1, keepdims=True) + jnp.sum(hi * hi, axis=-1, keepdims=True)) / (2 * half)
    scale = lax.rsqrt(ms + RMS_EPS)
    o_ref[:, :half] = lo * scale * g_ref[:, :half]
    o_ref[:, half:] = hi * scale * g_ref[:, half:]


def _combine(y_tok, x1, gates_t, g_final):
    T, D = x1.shape
    tm = COMBINE_TILE
    nt = T // tm
    y_spec = lambda kk: pl.BlockSpec((tm, D // 2), lambda i: (kk * nt + i, 0))
    return pl.pallas_call(
        _combine_kernel,
        grid=(nt,),
        in_specs=[y_spec(0), y_spec(1), y_spec(2), y_spec(3),
                  pl.BlockSpec((tm, D), lambda i: (i, 0)),
                  pl.BlockSpec((tm, TOP_K), lambda i: (i, 0)),
                  pl.BlockSpec((1, D), lambda i: (0, 0))],
        out_specs=pl.BlockSpec((tm, D), lambda i: (i, 0)),
        out_shape=jax.ShapeDtypeStruct((T, D), F32),
        compiler_params=pltpu.CompilerParams(
            dimension_semantics=("arbitrary",), vmem_limit_bytes=VMEM_LIMIT),
        name="combine_norm",
    )(y_tok, y_tok, y_tok, y_tok, x1, gates_t, g_final.reshape(1, D))


def kernel(x, g_mix, w_in, pool_w, pool_scale, w_out, g_moe, w_router, b_router, w_gate_up, b_gate_up,
           w_down, b_down, g_final):
    B, S, D = x.shape
    T = B * S
    assert g_mix.shape[0] == 1, "single-layer problem: the final norm is fused into the combine step"
    q, k, v, pool = _in_proj(x, g_mix[0], w_in[0], pool_w[0], pool_scale[0])
    attn = _attention(q, k, v)
    capacity = T
    x1, h2, gates, dest, cnt = _out_proj_router(
        attn.reshape(T, SB_WIDTH), pool.reshape(T, POOL_WIDTH), x.reshape(T, D),
        w_out[0], g_moe[0], w_router[0], b_router[0], capacity)
    x_sorted = _sc_dispatch(h2, dest, N_EXPERTS * capacity)
    visits = _visit_schedule(cnt[:, 0].astype(I32), T * TOP_K, capacity)
    y = _experts(x_sorted, visits, w_gate_up[0], b_gate_up[0], w_down[0], b_down[0])
    y_tok = _sc_gather(y, dest.reshape(-1))
    return _combine(y_tok, x1, gates.T, g_final).reshape(B, S, D)
```

```python
import functools

import jax
import jax.numpy as jnp
from jax import lax
from jax.experimental import pallas as pl
from jax.experimental.pallas import tpu as pltpu
from jax.experimental.pallas import tpu_sc as plsc

F32 = jnp.float32
BF16 = jnp.bfloat16
I32 = jnp.int32
U32 = jnp.uint32

D_MODEL = 1024
SB_HEADS = 8
SB_HEAD_DIM = 64
SB_WIDTH = SB_HEADS * SB_HEAD_DIM
POOL_WINDOWS = (2, 4, 8, 16)
POOL_WIDTH = 512
POOL_GROUP_DIM = 128
N_EXPERTS = 32
TOP_K = 4
D_EXPERT = 1024
SWIGLU_LIMIT = 7.0
SWIGLU_ALPHA = 1.702
RMS_EPS = 1e-5

LANES = 128
HALO = 16
PROJ_TILE = 512
ATTN_TILE = 256
EXPERT_TILE = 512
COMBINE_TILE = 256
ATTN_SKIP_LOG2 = -160.0
LOG2_E = 1.4426950408889634
VMEM_LIMIT = 56 * 1024 * 1024

_NT = (((1,), (1,)), ((), ()))


def _rms(x, g):
    ms = jnp.mean(x * x, axis=-1, keepdims=True)
    return x * lax.rsqrt(ms + RMS_EPS) * g


def _pack_rows(x):
    n = x.shape[1] // 2
    lo = lax.bitcast_convert_type(x[:, :n].astype(BF16).astype(F32), U32)
    hi = lax.bitcast_convert_type(x[:, n:].astype(BF16).astype(F32), U32)
    return (lo >> 16) | (hi & jnp.uint32(0xFFFF0000))


def _unpack_rows(w):
    lo = lax.bitcast_convert_type(w << 16, F32)
    hi = lax.bitcast_convert_type(w & jnp.uint32(0xFFFF0000), F32)
    return lo, hi


def _in_proj_kernel(x_ref, g_ref, w_ref, pw_ref, ps_ref, q_ref, k_ref, v_ref, p_ref, uext_ref):
    s = pl.program_id(1)
    tm = x_ref.shape[1]
    h = _rms(x_ref[0], g_ref[...])
    proj = jnp.dot(h.astype(BF16), w_ref[...], preferred_element_type=F32)
    q_ref[0] = (proj[:, 0:SB_WIDTH] * (LOG2_E * SB_HEAD_DIM ** -0.5)).astype(BF16)
    k_ref[0] = proj[:, SB_WIDTH:2 * SB_WIDTH].astype(BF16)
    v_ref[0] = proj[:, 2 * SB_WIDTH:3 * SB_WIDTH].astype(BF16)
    u = proj[:, 3 * SB_WIDTH:]

    @pl.when(s == 0)
    def _():
        uext_ref[0:HALO, :] = jnp.zeros((HALO, POOL_WIDTH), F32)

    uext_ref[HALO:, :] = u
    t = s * tm + lax.broadcasted_iota(I32, (tm, 1), 0)
    for g, w in enumerate(POOL_WINDOWS):
        sl = slice(g * POOL_GROUP_DIM, (g + 1) * POOL_GROUP_DIM)
        ug = u[:, sl]
        acc = ug
        for i in range(1, w):
            acc = acc + uext_ref[HALO - i:HALO - i + tm, sl]
        count = jnp.minimum(t + 1, w).astype(F32)
        pooled = acc / count - ug
        mixed = jnp.dot(pooled.astype(BF16), pw_ref[g], preferred_element_type=F32)
        p_ref[0, :, sl] = (mixed * ps_ref[:, sl]).astype(BF16)
    uext_ref[0:HALO, :] = u[tm - HALO:, :]


def _in_proj(x, g_mix, w_in, pool_w, pool_scale):
    B, S, D = x.shape
    tm = PROJ_TILE
    n_out = w_in.shape[1]
    out_sd = jax.ShapeDtypeStruct((B, S, SB_WIDTH), BF16)
    blk = pl.BlockSpec((1, tm, SB_WIDTH), lambda b, s: (b, s, 0))
    return pl.pallas_call(
        _in_proj_kernel,
        grid=(B, S // tm),
        in_specs=[
            pl.BlockSpec((1, tm, D), lambda b, s: (b, s, 0)),
            pl.BlockSpec((1, D), lambda b, s: (0, 0)),
            pl.BlockSpec((D, n_out), lambda b, s: (0, 0)),
            pl.BlockSpec(pool_w.shape, lambda b, s: (0, 0, 0)),
            pl.BlockSpec((1, POOL_WIDTH), lambda b, s: (0, 0)),
        ],
        out_specs=[blk, blk, blk, blk],
        out_shape=[out_sd, out_sd, out_sd, out_sd],
        scratch_shapes=[pltpu.VMEM((HALO + tm, POOL_WIDTH), F32)],
        compiler_params=pltpu.CompilerParams(
            dimension_semantics=("arbitrary", "arbitrary"), vmem_limit_bytes=VMEM_LIMIT),
        name="in_proj_pool",
    )(x, g_mix.reshape(1, D), w_in.astype(BF16), pool_w.astype(BF16), pool_scale.reshape(1, POOL_WIDTH))


def _attn_kernel(q_ref, k_ref, v_ref, o_ref):
    tq = ATTN_TILE
    tk = tq
    qi = pl.program_id(2)
    lane = lax.broadcasted_iota(I32, (tq, LANES), 1)
    row = lax.broadcasted_iota(I32, (tq, tk), 0)
    col = lax.broadcasted_iota(I32, (tq, tk), 1)
    causal = col < row
    tri = (row > col).astype(BF16)

    def head_rows(half, h):
        rows = q_ref[0, half * tq:(half + 1) * tq, :]
        return jnp.where((lane >= h * SB_HEAD_DIM) & (lane < (h + 1) * SB_HEAD_DIM), rows, jnp.zeros_like(rows))

    def tile(qh, j, mask):
        start = pl.multiple_of(j * tk, tk)
        kb = k_ref[0, pl.ds(start, tk), :]
        vb = v_ref[0, pl.ds(start, tk), :]
        z = lax.dot_general(qh, kb, _NT, preferred_element_type=F32)
        neg_abs = lax.bitcast_convert_type(lax.bitcast_convert_type(z, U32) | jnp.uint32(0x80000000), F32)
        nl = jnp.maximum(z, 0.0) + jnp.log2(1.0 + jnp.exp2(neg_abs))
        if mask:
            nl = jnp.where(causal, nl, 0.0)
        ntail = jnp.dot(nl.astype(BF16), tri, preferred_element_type=F32)
        total = -(ntail[:, 0:1] + nl[:, 0:1])
        return (z - nl) - ntail, total, vb

    has_prev = qi > 0
    qs, accs, carries = [], [], []
    for half in range(2):
        diag = 2 * qi + half
        prev = jnp.maximum(diag - 1, 0)
        for h in range(2):
            qh = head_rows(half, h)
            lw_d, tot_d, v_d = tile(qh, diag, True)
            lw_p, tot_p, v_p = tile(qh, prev, False)
            w_d = jnp.where(causal, jnp.exp2(lw_d), 0.0).astype(BF16)
            w_p = jnp.exp2(lw_p + tot_d)
            if half == 0:
                w_p = jnp.where(has_prev, w_p, 0.0)
            acc = jnp.dot(w_d, v_d, preferred_element_type=F32)
            qs.append(qh)
            accs.append(acc + jnp.dot(w_p.astype(BF16), v_p, preferred_element_type=F32))
            carries.append(tot_d + tot_p)

    def cond(st):
        j, cs, _ = st
        live = functools.reduce(jnp.maximum, [jnp.max(c) for c in cs])
        return jnp.logical_and(j >= 0, live > ATTN_SKIP_LOG2)

    def body(st):
        j, cs, acs = st
        new_c, new_a = [], []
        for n, (qh, c, a) in enumerate(zip(qs, cs, acs)):
            if n < 2:
                c = jnp.where(j >= 1, c, -jnp.inf)
                lw, tot, vb = tile(qh, jnp.maximum(j - 1, 0), False)
            else:
                lw, tot, vb = tile(qh, j, False)
            new_a.append(a + jnp.dot(jnp.exp2(lw + c).astype(BF16), vb, preferred_element_type=F32))
            new_c.append(c + tot)
        return j - 1, tuple(new_c), tuple(new_a)

    _, _, accs = lax.while_loop(cond, body, (2 * qi - 1, tuple(carries), tuple(accs)))
    for half in range(2):
        o_ref[0, half * tq:(half + 1) * tq, :] = jnp.where(
            lane < SB_HEAD_DIM, accs[2 * half], accs[2 * half + 1]).astype(o_ref.dtype)


def _attention(q, k, v):
    B, S, W = q.shape
    tq = 2 * ATTN_TILE
    n_pairs = W // LANES
    return pl.pallas_call(
        _attn_kernel,
        grid=(B, n_pairs, S // tq),
        in_specs=[
            pl.BlockSpec((1, tq, LANES), lambda b, p, i: (b, i, p)),
            pl.BlockSpec((1, S, LANES), lambda b, p, i: (b, 0, p)),
            pl.BlockSpec((1, S, LANES), lambda b, p, i: (b, 0, p)),
        ],
        out_specs=pl.BlockSpec((1, tq, LANES), lambda b, p, i: (b, i, p)),
        out_shape=jax.ShapeDtypeStruct((B, S, W), BF16),
        compiler_params=pltpu.CompilerParams(
            dimension_semantics=("arbitrary", "arbitrary", "arbitrary"), vmem_limit_bytes=VMEM_LIMIT),
        name="stickbreak_attn",
    )(q, k, v)


def _out_proj_router_kernel(attn_ref, pool_ref, x_ref, wo_ref, g_ref, wr_ref, br_ref, tri_ref,
                            x1_ref, h2_ref, gate_ref, dest_ref, cnt_ref, carry_ref, *, capacity):
    i = pl.program_id(0)
    tm = x_ref.shape[0]

    @pl.when(i == 0)
    def _():
        carry_ref[...] = jnp.zeros_like(carry_ref)

    mixed = jnp.dot(attn_ref[...], wo_ref[0:SB_WIDTH, :], preferred_element_type=F32)
    mixed = mixed + jnp.dot(pool_ref[...], wo_ref[SB_WIDTH:, :], preferred_element_type=F32)
    x1 = x_ref[...] + mixed
    h2 = _rms(x1, g_ref[...])
    x1_ref[...] = x1
    h2_ref[...] = _pack_rows(h2)

    hh = h2.astype(BF16)
    hl = (h2 - hh.astype(F32)).astype(BF16)
    wr = wr_ref[...]
    wh = wr.astype(BF16)
    wl = (wr - wh.astype(F32)).astype(BF16)
    logits = (lax.dot_general(wh, hh, _NT, preferred_element_type=F32)
              + lax.dot_general(wh, hl, _NT, preferred_element_type=F32)
              + lax.dot_general(wl, hh, _NT, preferred_element_type=F32)) + br_ref[...]

    eid = lax.broadcasted_iota(I32, (N_EXPERTS, tm), 0).astype(F32)
    work = logits
    vals, ids = [], []
    for _ in range(TOP_K):
        m = jnp.max(work, axis=0, keepdims=True)
        sel = jnp.min(jnp.where(work == m, eid, float(N_EXPERTS)), axis=0, keepdims=True)
        vals.append(m)
        ids.append(sel)
        work = jnp.where(eid == sel, -jnp.inf, work)
    exps = [jnp.exp(v - vals[0]) for v in vals]
    denom = exps[0] + exps[1] + exps[2] + exps[3]

    onehot = jnp.zeros((N_EXPERTS, tm), F32)
    for sel in ids:
        onehot = onehot + (eid == sel).astype(F32)
    before = jnp.dot(onehot.astype(BF16), tri_ref[...], preferred_element_type=F32) + carry_ref[...]
    for kk in range(TOP_K):
        gate_ref[kk:kk + 1, :] = exps[kk] / denom
        rk = jnp.sum(jnp.where(eid == ids[kk], before, 0.0), axis=0, keepdims=True)
        dest_ref[kk:kk + 1, :] = (ids[kk] * float(capacity) + rk).astype(I32)
    carry_ref[...] = carry_ref[...] + jnp.sum(onehot, axis=1, keepdims=True)
    cnt_ref[...] = jnp.broadcast_to(carry_ref[...], cnt_ref.shape)


def _out_proj_router(attn, pool, x, w_out, g_moe, w_router, b_router, capacity):
    T, D = x.shape
    tm = PROJ_TILE
    r = lax.broadcasted_iota(I32, (tm, tm), 0)
    c = lax.broadcasted_iota(I32, (tm, tm), 1)
    tri = (r < c).astype(BF16)
    row_blk = lambda w: pl.BlockSpec((tm, w), lambda i: (i, 0))
    fixed = lambda shape: pl.BlockSpec(shape, lambda i: tuple(0 for _ in shape))
    sel_blk = pl.BlockSpec((TOP_K, tm), lambda i: (0, i))
    assert N_EXPERTS * capacity < 2 ** 24
    return pl.pallas_call(
        functools.partial(_out_proj_router_kernel, capacity=capacity),
        grid=(T // tm,),
        in_specs=[row_blk(SB_WIDTH), row_blk(POOL_WIDTH), row_blk(D), fixed((D, D)), fixed((1, D)),
                  fixed((N_EXPERTS, D)), fixed((N_EXPERTS, 1)), fixed((tm, tm))],
        out_specs=[row_blk(D), row_blk(D // 2), sel_blk, sel_blk, fixed((N_EXPERTS, LANES))],
        out_shape=[jax.ShapeDtypeStruct((T, D), F32), jax.ShapeDtypeStruct((T, D // 2), U32),
                   jax.ShapeDtypeStruct((TOP_K, T), F32), jax.ShapeDtypeStruct((TOP_K, T), I32),
                   jax.ShapeDtypeStruct((N_EXPERTS, LANES), F32)],
        scratch_shapes=[pltpu.VMEM((N_EXPERTS, 1), F32)],
        compiler_params=pltpu.CompilerParams(
            dimension_semantics=("arbitrary",), vmem_limit_bytes=VMEM_LIMIT),
        name="out_proj_router",
    )(attn, pool, x, w_out.astype(BF16), g_moe.reshape(1, D), w_router.T, b_router.reshape(N_EXPERTS, 1), tri)


SC_ROWS_PER_COPY = 64


def _sc_workers():
    info = plsc.get_sparse_core_info()
    return info.num_cores, info.num_cores * info.num_subcores


def _sc_dispatch(rows, dest, n_out):
    T, W = rows.shape
    K = dest.shape[0]
    sub = SC_ROWS_PER_COPY
    n_cores, n_workers = _sc_workers()
    per_w = T // n_workers
    n_chunks = per_w // sub
    assert per_w * n_workers == T and n_chunks * sub == per_w and n_chunks % 2 == 0
    idx = dest.reshape(K, n_workers, n_chunks, sub).transpose(1, 2, 0, 3).reshape(n_workers, n_chunks * K, sub)
    mesh = plsc.VectorSubcoreMesh(core_axis_name="core", subcore_axis_name="subcore")

    @functools.partial(
        pl.kernel, out_type=jax.ShapeDtypeStruct((n_out, W), rows.dtype), mesh=mesh,
        scratch_types=[pltpu.VMEM((n_chunks * K, sub), I32), pltpu.VMEM((2, sub, W), rows.dtype),
                       pltpu.SemaphoreType.DMA((2,)), pltpu.SemaphoreType.DMA((2,))])
    def scatter_rows(x_hbm, i_hbm, o_hbm, idx_v, buf, rsem, wsem):
        wid = lax.axis_index("subcore") * n_cores + lax.axis_index("core")
        base = wid * per_w
        pltpu.sync_copy(i_hbm.at[wid], idx_v)

        def read(c, slot):
            return pltpu.make_async_copy(x_hbm.at[pl.ds(base + c * sub, sub)], buf.at[slot], rsem.at[slot])

        def write(c, kk, slot):
            return pltpu.make_async_copy(buf.at[slot], o_hbm.at[idx_v.at[c * K + kk]], wsem.at[slot])

        read(0, 0).start()

        @pl.loop(0, n_chunks, step=2)
        def _(c0):
            for b in range(2):
                c = c0 + b
                read(c, b).wait()
                for kk in range(K):
                    write(c, kk, b).start()

                @pl.when(c + 1 < n_chunks)
                def _():
                    @pl.when(c >= 1)
                    def _():
                        for kk in range(K):
                            write(c - 1, kk, 1 - b).wait()
                    read(c + 1, 1 - b).start()

        for kk in range(K):
            write(n_chunks - 2, kk, 0).wait()
            write(n_chunks - 1, kk, 1).wait()

    return scatter_rows(rows, idx)


def _sc_gather(table, indices):
    M = indices.shape[0]
    W = table.shape[1]
    sub = SC_ROWS_PER_COPY
    n_cores, n_workers = _sc_workers()
    per_w = M // n_workers
    n_steps = per_w // sub
    assert per_w * n_workers == M and n_steps * sub == per_w and n_steps % 2 == 0
    mesh = plsc.VectorSubcoreMesh(core_axis_name="core", subcore_axis_name="subcore")

    @functools.partial(
        pl.kernel, out_type=jax.ShapeDtypeStruct((M, W), table.dtype), mesh=mesh,
        scratch_types=[pltpu.VMEM((n_steps, sub), I32), pltpu.VMEM((2, sub, W), table.dtype),
                       pltpu.SemaphoreType.DMA((2,)), pltpu.SemaphoreType.DMA((2,))])
    def gather_rows(x_hbm, i_hbm, o_hbm, idx_v, buf, gsem, wsem):
        wid = lax.axis_index("subcore") * n_cores + lax.axis_index("core")
        base = wid * per_w
        pltpu.sync_copy(i_hbm.at[wid], idx_v)

        def gather(s, slot):
            return pltpu.make_async_copy(x_hbm.at[idx_v.at[s]], buf.at[slot], gsem.at[slot])

        def write(s, slot):
            return pltpu.make_async_copy(buf.at[slot], o_hbm.at[pl.ds(base + s * sub, sub)], wsem.at[slot])

        gather(0, 0).start()

        @pl.loop(0, n_steps, step=2)
        def _(s0):
            for b in range(2):
                s = s0 + b
                gather(s, b).wait()
                write(s, b).start()

                @pl.when(s + 1 < n_steps)
                def _():
                    @pl.when(s >= 1)
                    def _():
                        write(s - 1, 1 - b).wait()
                    gather(s + 1, 1 - b).start()

        write(n_steps - 2, 0).wait()
        write(n_steps - 1, 1).wait()

    return gather_rows(table, indices.reshape(n_workers, n_steps, sub))


def _expert_kernel(vb_ref, ve_ref, nvis_ref, nxt_ref, par_ref,
                   x_ref, wgu_hbm, bgu_ref, wd_hbm, bd_ref, y_ref,
                   wgu_stage, wd_stage, wgu_bf, wd_bf, sem):
    v = pl.program_id(0)
    real = v < nvis_ref[0]
    e = ve_ref[v]
    first_of_expert = jnp.logical_or(v == 0, e != ve_ref[jnp.maximum(v - 1, 0)])
    slot = par_ref[e]

    def weight_copies(expert, s):
        return (pltpu.make_async_copy(wgu_hbm.at[expert], wgu_stage.at[s], sem.at[0, s]),
                pltpu.make_async_copy(wd_hbm.at[expert], wd_stage.at[s], sem.at[1, s]))

    @pl.when(v == 0)
    def _():
        for c in weight_copies(e, slot):
            c.start()

    @pl.when(jnp.logical_and(first_of_expert, real))
    def _():
        for c in weight_copies(e, slot):
            c.wait()
        nxt = nxt_ref[e]

        @pl.when(nxt >= 0)
        def _():
            for c in weight_copies(nxt, 1 - slot):
                c.start()

        wgu_bf[...] = wgu_stage[slot].astype(BF16)
        wd_bf[...] = wd_stage[slot].astype(BF16)

    @pl.when(real)
    def _():
        half = wgu_bf.shape[0] // 2
        x_lo, x_hi = _unpack_rows(x_ref[...])
        gu = (jnp.dot(x_lo.astype(BF16), wgu_bf[:half, :], preferred_element_type=F32)
              + jnp.dot(x_hi.astype(BF16), wgu_bf[half:, :], preferred_element_type=F32)) + bgu_ref[0]
        gate = jnp.minimum(gu[:, :D_EXPERT], SWIGLU_LIMIT)
        up = jnp.clip(gu[:, D_EXPERT:], -SWIGLU_LIMIT, SWIGLU_LIMIT)
        act = gate * jax.nn.sigmoid(SWIGLU_ALPHA * gate) * (up + 1.0)
        y_ref[...] = _pack_rows(jnp.dot(act.astype(BF16), wd_bf[...], preferred_element_type=F32) + bd_ref[0])


def _experts(x_sorted, visits, w_gate_up, b_gate_up, w_down, b_down):
    n_rows, W = x_sorted.shape
    D = 2 * W
    tm = EXPERT_TILE
    n_visits = visits[0].shape[0]
    n_prefetch = len(visits)
    by_block = lambda v, vb, *_: (vb[v], 0)
    by_expert = lambda v, vb, ve, *_: (ve[v], 0, 0)
    grid_spec = pltpu.PrefetchScalarGridSpec(
        num_scalar_prefetch=n_prefetch,
        grid=(n_visits,),
        in_specs=[
            pl.BlockSpec((tm, W), by_block),
            pl.BlockSpec(memory_space=pl.ANY),
            pl.BlockSpec((1, 1, 2 * D_EXPERT), by_expert),
            pl.BlockSpec(memory_space=pl.ANY),
            pl.BlockSpec((1, 1, D), by_expert),
        ],
        out_specs=pl.BlockSpec((tm, W), by_block),
        scratch_shapes=[pltpu.VMEM((2, D, 2 * D_EXPERT), F32), pltpu.VMEM((2, D_EXPERT, D), F32),
                        pltpu.VMEM((D, 2 * D_EXPERT), BF16), pltpu.VMEM((D_EXPERT, D), BF16),
                        pltpu.SemaphoreType.DMA((2, 2))],
    )
    return pl.pallas_call(
        _expert_kernel,
        grid_spec=grid_spec,
        out_shape=jax.ShapeDtypeStruct((n_rows, W), U32),
        compiler_params=pltpu.CompilerParams(
            dimension_semantics=("arbitrary",), vmem_limit_bytes=VMEM_LIMIT),
        name="expert_gmm",
    )(*visits, x_sorted, w_gate_up, b_gate_up.reshape(N_EXPERTS, 1, -1), w_down, b_down.reshape(N_EXPERTS, 1, -1))


def _visit_schedule(counts, n_assign, capacity):
    tm = EXPERT_TILE
    n_visits = n_assign // tm + N_EXPERTS
    per_e = (counts + tm - 1) // tm
    vend = jnp.cumsum(per_e)
    nvis = vend[-1]
    vc = jnp.minimum(jnp.arange(n_visits, dtype=I32), nvis - 1)
    done = vc[:, None] >= vend[None, :]
    e = jnp.sum(done, axis=1).astype(I32)
    blk = e * (capacity // tm) + vc - jnp.sum(jnp.where(done, per_e[None, :], 0), axis=1)
    ids = jnp.arange(N_EXPERTS, dtype=I32)
    nonempty = counts > 0
    later = jnp.logical_and(nonempty[None, :], ids[None, :] > ids[:, None])
    nxt = jnp.where(jnp.any(later, axis=1), jnp.argmax(later, axis=1), -1).astype(I32)
    parity = ((jnp.cumsum(nonempty.astype(I32)) - nonempty.astype(I32)) % 2).astype(I32)
    return blk.astype(I32), e, nvis.reshape(1).astype(I32), nxt, parity


def _combine_kernel(y0_ref, y1_ref, y2_ref, y3_ref, x1_ref, gate_ref, g_ref, o_ref):
    half = x1_ref.shape[1] // 2
    gates = gate_ref[...]
    lo = x1_ref[:, :half]
    hi = x1_ref[:, half:]
    for kk, y_ref in enumerate((y0_ref, y1_ref, y2_ref, y3_ref)):
        y_lo, y_hi = _unpack_rows(y_ref[...])
        lo = lo + gates[:, kk:kk + 1] * y_lo
        hi = hi + gates[:, kk:kk + 1] * y_hi
    ms = (jnp.sum(lo * lo, axis=-1, keepdims=True) + jnp.sum(hi * hi, axis=-1, keepdims=True)) / (2 * half)
    scale = lax.rsqrt(ms + RMS_EPS)
    o_ref[:, :half] = lo * scale * g_ref[:, :half]
    o_ref[:, half:] = hi * scale * g_ref[:, half:]


def _combine(y_tok, x1, gates_t, g_final):
    T, D = x1.shape
    tm = COMBINE_TILE
    nt = T // tm
    y_spec = lambda kk: pl.BlockSpec((tm, D // 2), lambda i: (kk * nt + i, 0))
    return pl.pallas_call(
        _combine_kernel,
        grid=(nt,),
        in_specs=[y_spec(0), y_spec(1), y_spec(2), y_spec(3),
                  pl.BlockSpec((tm, D), lambda i: (i, 0)),
                  pl.BlockSpec((tm, TOP_K), lambda i: (i, 0)),
                  pl.BlockSpec((1, D), lambda i: (0, 0))],
        out_specs=pl.BlockSpec((tm, D), lambda i: (i, 0)),
        out_shape=jax.ShapeDtypeStruct((T, D), F32),
        compiler_params=pltpu.CompilerParams(
            dimension_semantics=("arbitrary",), vmem_limit_bytes=VMEM_LIMIT),
        name="combine_norm",
    )(y_tok, y_tok, y_tok, y_tok, x1, gates_t, g_final.reshape(1, D))


def kernel(x, g_mix, w_in, pool_w, pool_scale, w_out, g_moe, w_router, b_router, w_gate_up, b_gate_up,
           w_down, b_down, g_final):
    B, S, D = x.shape
    T = B * S
    assert g_mix.shape[0] == 1, "single-layer problem: the final norm is fused into the combine step"
    q, k, v, pool = _in_proj(x, g_mix[0], w_in[0], pool_w[0], pool_scale[0])
    attn = _attention(q, k, v)
    capacity = T
    x1, h2, gates, dest, cnt = _out_proj_router(
        attn.reshape(T, SB_WIDTH), pool.reshape(T, POOL_WIDTH), x.reshape(T, D),
        w_out[0], g_moe[0], w_router[0], b_router[0], capacity)
    x_sorted = _sc_dispatch(h2, dest, N_EXPERTS * capacity)
    visits = _visit_schedule(cnt[:, 0].astype(I32), T * TOP_K, capacity)
    y = _experts(x_sorted, visits, w_gate_up[0], b_gate_up[0], w_down[0], b_down[0])
    y_tok = _sc_gather(y, dest.reshape(-1))
    return _combine(y_tok, x1, gates.T, g_final).reshape(B, S, D)
```

```python
import functools

import jax
import jax.numpy as jnp
from jax import lax
from jax.experimental import pallas as pl
from jax.experimental.pallas import tpu as pltpu
from jax.experimental.pallas import tpu_sc as plsc

F32 = jnp.float32
BF16 = jnp.bfloat16
I32 = jnp.int32
U32 = jnp.uint32

D_MODEL = 1024
SB_HEADS = 8
SB_HEAD_DIM = 64
SB_WIDTH = SB_HEADS * SB_HEAD_DIM
POOL_WINDOWS = (2, 4, 8, 16)
POOL_WIDTH = 512
POOL_GROUP_DIM = 128
N_EXPERTS = 32
TOP_K = 4
D_EXPERT = 1024
SWIGLU_LIMIT = 7.0
SWIGLU_ALPHA = 1.702
RMS_EPS = 1e-5

LANES = 128
HALO = 16
PROJ_TILE = 512
PROJ_SUB_ROWS = 256
ATTN_TILE = 256
EXPERT_TILE = 512
COMBINE_TILE = 256
COMBINE_CHUNKS = 4
ATTN_SKIP_LOG2 = -160.0
LOG2_E = 1.4426950408889634
VMEM_LIMIT = 56 * 1024 * 1024

_NT = (((1,), (1,)), ((), ()))


def _rms(x, g):
    ms = jnp.mean(x * x, axis=-1, keepdims=True)
    return x * lax.rsqrt(ms + RMS_EPS) * g


def _pack_rows(x):
    n = x.shape[1] // 2
    lo = lax.bitcast_convert_type(x[:, :n].astype(BF16).astype(F32), U32)
    hi = lax.bitcast_convert_type(x[:, n:].astype(BF16).astype(F32), U32)
    return (lo >> 16) | (hi & jnp.uint32(0xFFFF0000))


def _unpack_rows(w):
    lo = lax.bitcast_convert_type(w << 16, F32)
    hi = lax.bitcast_convert_type(w & jnp.uint32(0xFFFF0000), F32)
    return lo, hi


def _in_proj_kernel(x_ref, g_ref, w_ref, pw_ref, ps_ref, q_ref, k_ref, v_ref, p_ref, uext_ref):
    s = pl.program_id(1)
    tm = x_ref.shape[1]
    sub = PROJ_SUB_ROWS

    @pl.when(s == 0)
    def _():
        uext_ref[0:HALO, :] = jnp.zeros((HALO, POOL_WIDTH), F32)

    for r0 in range(0, tm, sub):
        rows = slice(r0, r0 + sub)
        h = _rms(x_ref[0, rows, :], g_ref[...])
        proj = jnp.dot(h.astype(BF16), w_ref[...], preferred_element_type=F32)
        q_ref[0, rows, :] = (proj[:, 0:SB_WIDTH] * (LOG2_E * SB_HEAD_DIM ** -0.5)).astype(BF16)
        k_ref[0, rows, :] = proj[:, SB_WIDTH:2 * SB_WIDTH].astype(BF16)
        v_ref[0, rows, :] = proj[:, 2 * SB_WIDTH:3 * SB_WIDTH].astype(BF16)
        u = proj[:, 3 * SB_WIDTH:]
        uext_ref[HALO + r0:HALO + r0 + sub, :] = u
        t = s * tm + r0 + lax.broadcasted_iota(I32, (sub, 1), 0)
        for g, w in enumerate(POOL_WINDOWS):
            sl = slice(g * POOL_GROUP_DIM, (g + 1) * POOL_GROUP_DIM)
            ug = u[:, sl]
            acc = ug
            for i in range(1, w):
                acc = acc + uext_ref[HALO + r0 - i:HALO + r0 - i + sub, sl]
            count = jnp.minimum(t + 1, w).astype(F32)
            pooled = acc / count - ug
            mixed = jnp.dot(pooled.astype(BF16), pw_ref[g], preferred_element_type=F32)
            p_ref[0, rows, sl] = (mixed * ps_ref[:, sl]).astype(BF16)
    uext_ref[0:HALO, :] = uext_ref[tm:tm + HALO, :]


def _in_proj(x, g_mix, w_in, pool_w, pool_scale):
    B, S, D = x.shape
    tm = PROJ_TILE
    n_out = w_in.shape[1]
    out_sd = jax.ShapeDtypeStruct((B, S, SB_WIDTH), BF16)
    blk = pl.BlockSpec((1, tm, SB_WIDTH), lambda b, s: (b, s, 0))
    return pl.pallas_call(
        _in_proj_kernel,
        grid=(B, S // tm),
        in_specs=[
            pl.BlockSpec((1, tm, D), lambda b, s: (b, s, 0)),
            pl.BlockSpec((1, D), lambda b, s: (0, 0)),
            pl.BlockSpec((D, n_out), lambda b, s: (0, 0)),
            pl.BlockSpec(pool_w.shape, lambda b, s: (0, 0, 0)),
            pl.BlockSpec((1, POOL_WIDTH), lambda b, s: (0, 0)),
        ],
        out_specs=[blk, blk, blk, blk],
        out_shape=[out_sd, out_sd, out_sd, out_sd],
        scratch_shapes=[pltpu.VMEM((HALO + tm, POOL_WIDTH), F32)],
        compiler_params=pltpu.CompilerParams(
            dimension_semantics=("arbitrary", "arbitrary"), vmem_limit_bytes=VMEM_LIMIT),
        name="in_proj_pool",
    )(x, g_mix.reshape(1, D), w_in.astype(BF16), pool_w.astype(BF16), pool_scale.reshape(1, POOL_WIDTH))


def _attn_kernel(q_ref, k_ref, v_ref, o_ref):
    tq = ATTN_TILE
    tk = tq
    qi = pl.program_id(2)
    lane = lax.broadcasted_iota(I32, (tq, LANES), 1)
    row = lax.broadcasted_iota(I32, (tq, tk), 0)
    col = lax.broadcasted_iota(I32, (tq, tk), 1)
    causal = col < row
    tri = (row > col).astype(BF16)

    def head_rows(half, h):
        rows = q_ref[0, half * tq:(half + 1) * tq, :]
        return jnp.where((lane >= h * SB_HEAD_DIM) & (lane < (h + 1) * SB_HEAD_DIM), rows, jnp.zeros_like(rows))

    def tile(qh, j, mask):
        start = pl.multiple_of(j * tk, tk)
        kb = k_ref[0, pl.ds(start, tk), :]
        vb = v_ref[0, pl.ds(start, tk), :]
        z = lax.dot_general(qh, kb, _NT, preferred_element_type=F32)
        neg_abs = lax.bitcast_convert_type(lax.bitcast_convert_type(z, U32) | jnp.uint32(0x80000000), F32)
        nl = jnp.maximum(z, 0.0) + jnp.log2(1.0 + jnp.exp2(neg_abs))
        if mask:
            nl = jnp.where(causal, nl, 0.0)
        ntail = jnp.dot(nl.astype(BF16), tri, preferred_element_type=F32)
        total = -(ntail[:, 0:1] + nl[:, 0:1])
        return (z - nl) - ntail, total, vb

    has_prev = qi > 0
    qs, accs, carries = [], [], []
    for half in range(2):
        diag = 2 * qi + half
        prev = jnp.maximum(diag - 1, 0)
        for h in range(2):
            qh = head_rows(half, h)
            lw_d, tot_d, v_d = tile(qh, diag, True)
            lw_p, tot_p, v_p = tile(qh, prev, False)
            w_d = jnp.where(causal, jnp.exp2(lw_d), 0.0).astype(BF16)
            w_p = jnp.exp2(lw_p + tot_d)
            if half == 0:
                w_p = jnp.where(has_prev, w_p, 0.0)
            acc = jnp.dot(w_d, v_d, preferred_element_type=F32)
            qs.append(qh)
            accs.append(acc + jnp.dot(w_p.astype(BF16), v_p, preferred_element_type=F32))
            carries.append(tot_d + tot_p)

    def cond(st):
        j, cs, _ = st
        live = functools.reduce(jnp.maximum, [jnp.max(c) for c in cs])
        return jnp.logical_and(j >= 0, live > ATTN_SKIP_LOG2)

    def body(st):
        j, cs, acs = st
        new_c, new_a = [], []
        for n, (qh, c, a) in enumerate(zip(qs, cs, acs)):
            if n < 2:
                c = jnp.where(j >= 1, c, -jnp.inf)
                lw, tot, vb = tile(qh, jnp.maximum(j - 1, 0), False)
            else:
                lw, tot, vb = tile(qh, j, False)
            new_a.append(a + jnp.dot(jnp.exp2(lw + c).astype(BF16), vb, preferred_element_type=F32))
            new_c.append(c + tot)
        return j - 1, tuple(new_c), tuple(new_a)

    _, _, accs = lax.while_loop(cond, body, (2 * qi - 1, tuple(carries), tuple(accs)))
    for half in range(2):
        o_ref[0, half * tq:(half + 1) * tq, :] = jnp.where(
            lane < SB_HEAD_DIM, accs[2 * half], accs[2 * half + 1]).astype(o_ref.dtype)


def _attention(q, k, v):
    B, S, W = q.shape
    tq = 2 * ATTN_TILE
    n_pairs = W // LANES
    return pl.pallas_call(
        _attn_kernel,
        grid=(B, n_pairs, S // tq),
        in_specs=[
            pl.BlockSpec((1, tq, LANES), lambda b, p, i: (b, i, p)),
            pl.BlockSpec((1, S, LANES), lambda b, p, i: (b, 0, p)),
            pl.BlockSpec((1, S, LANES), lambda b, p, i: (b, 0, p)),
        ],
        out_specs=pl.BlockSpec((1, tq, LANES), lambda b, p, i: (b, i, p)),
        out_shape=jax.ShapeDtypeStruct((B, S, W), BF16),
        compiler_params=pltpu.CompilerParams(
            dimension_semantics=("arbitrary", "arbitrary", "arbitrary"), vmem_limit_bytes=VMEM_LIMIT),
        name="stickbreak_attn",
    )(q, k, v)


def _out_proj_router_kernel(attn_ref, pool_ref, x_ref, wo_ref, g_ref, wr_ref, br_ref, tri_ref,
                            x1_ref, h2_ref, gate_ref, dest_ref, cnt_ref, carry_ref, *, capacity):
    i = pl.program_id(0)
    tm = x_ref.shape[0]

    @pl.when(i == 0)
    def _():
        carry_ref[...] = jnp.zeros_like(carry_ref)

    mixed = jnp.dot(attn_ref[...], wo_ref[0:SB_WIDTH, :], preferred_element_type=F32)
    mixed = mixed + jnp.dot(pool_ref[...], wo_ref[SB_WIDTH:, :], preferred_element_type=F32)
    x1 = x_ref[...] + mixed
    h2 = _rms(x1, g_ref[...])
    x1_ref[...] = x1
    h2_ref[...] = _pack_rows(h2)

    hh = h2.astype(BF16)
    hl = (h2 - hh.astype(F32)).astype(BF16)
    wr = wr_ref[...]
    wh = wr.astype(BF16)
    wl = (wr - wh.astype(F32)).astype(BF16)
    logits = (lax.dot_general(wh, hh, _NT, preferred_element_type=F32)
              + lax.dot_general(wh, hl, _NT, preferred_element_type=F32)
              + lax.dot_general(wl, hh, _NT, preferred_element_type=F32)) + br_ref[...]

    eid = lax.broadcasted_iota(I32, (N_EXPERTS, tm), 0).astype(F32)
    work = logits
    vals, ids = [], []
    for _ in range(TOP_K):
        m = jnp.max(work, axis=0, keepdims=True)
        sel = jnp.min(jnp.where(work == m, eid, float(N_EXPERTS)), axis=0, keepdims=True)
        vals.append(m)
        ids.append(sel)
        work = jnp.where(eid == sel, -jnp.inf, work)
    exps = [jnp.exp(v - vals[0]) for v in vals]
    denom = exps[0] + exps[1] + exps[2] + exps[3]

    onehot = jnp.zeros((N_EXPERTS, tm), F32)
    for sel in ids:
        onehot = onehot + (eid == sel).astype(F32)
    before = jnp.dot(onehot.astype(BF16), tri_ref[...], preferred_element_type=F32) + carry_ref[...]
    for kk in range(TOP_K):
        gate_ref[kk:kk + 1, :] = exps[kk] / denom
        rk = jnp.sum(jnp.where(eid == ids[kk], before, 0.0), axis=0, keepdims=True)
        dest_ref[kk:kk + 1, :] = (ids[kk] * float(capacity) + rk).astype(I32)
    carry_ref[...] = carry_ref[...] + jnp.sum(onehot, axis=1, keepdims=True)
    cnt_ref[...] = jnp.broadcast_to(carry_ref[...], cnt_ref.shape)


def _out_proj_router(attn, pool, x, w_out, g_moe, w_router, b_router, capacity):
    T, D = x.shape
    tm = PROJ_TILE
    r = lax.broadcasted_iota(I32, (tm, tm), 0)
    c = lax.broadcasted_iota(I32, (tm, tm), 1)
    tri = (r < c).astype(BF16)
    row_blk = lambda w: pl.BlockSpec((tm, w), lambda i: (i, 0))
    fixed = lambda shape: pl.BlockSpec(shape, lambda i: tuple(0 for _ in shape))
    sel_blk = pl.BlockSpec((TOP_K, tm), lambda i: (0, i))
    assert N_EXPERTS * capacity < 2 ** 24
    return pl.pallas_call(
        functools.partial(_out_proj_router_kernel, capacity=capacity),
        grid=(T // tm,),
        in_specs=[row_blk(SB_WIDTH), row_blk(POOL_WIDTH), row_blk(D), fixed((D, D)), fixed((1, D)),
                  fixed((N_EXPERTS, D)), fixed((N_EXPERTS, 1)), fixed((tm, tm))],
        out_specs=[row_blk(D), row_blk(D // 2), sel_blk, sel_blk, fixed((N_EXPERTS, LANES))],
        out_shape=[jax.ShapeDtypeStruct((T, D), F32), jax.ShapeDtypeStruct((T, D // 2), U32),
                   jax.ShapeDtypeStruct((TOP_K, T), F32), jax.ShapeDtypeStruct((TOP_K, T), I32),
                   jax.ShapeDtypeStruct((N_EXPERTS, LANES), F32)],
        scratch_shapes=[pltpu.VMEM((N_EXPERTS, 1), F32)],
        compiler_params=pltpu.CompilerParams(
            dimension_semantics=("arbitrary",), vmem_limit_bytes=VMEM_LIMIT),
        name="out_proj_router",
    )(attn, pool, x, w_out.astype(BF16), g_moe.reshape(1, D), w_router.T, b_router.reshape(N_EXPERTS, 1), tri)


SC_ROWS_PER_COPY = 64


def _sc_workers():
    info = plsc.get_sparse_core_info()
    return info.num_cores, info.num_cores * info.num_subcores


def _sc_dispatch(rows, dest, n_out):
    T, W = rows.shape
    K = dest.shape[0]
    sub = SC_ROWS_PER_COPY
    n_cores, n_workers = _sc_workers()
    per_w = T // n_workers
    n_chunks = per_w // sub
    assert per_w * n_workers == T and n_chunks * sub == per_w and n_chunks % 2 == 0
    idx = dest.reshape(K, n_workers, n_chunks, sub).transpose(1, 2, 0, 3).reshape(n_workers, n_chunks * K, sub)
    mesh = plsc.VectorSubcoreMesh(core_axis_name="core", subcore_axis_name="subcore")

    @functools.partial(
        pl.kernel, out_type=jax.ShapeDtypeStruct((n_out, W), rows.dtype), mesh=mesh,
        scratch_types=[pltpu.VMEM((n_chunks * K, sub), I32), pltpu.VMEM((2, sub, W), rows.dtype),
                       pltpu.SemaphoreType.DMA((2,)), pltpu.SemaphoreType.DMA((2,))])
    def scatter_rows(x_hbm, i_hbm, o_hbm, idx_v, buf, rsem, wsem):
        wid = lax.axis_index("subcore") * n_cores + lax.axis_index("core")
        base = wid * per_w
        pltpu.sync_copy(i_hbm.at[wid], idx_v)

        def read(c, slot):
            return pltpu.make_async_copy(x_hbm.at[pl.ds(base + c * sub, sub)], buf.at[slot], rsem.at[slot])

        def write(c, kk, slot):
            return pltpu.make_async_copy(buf.at[slot], o_hbm.at[idx_v.at[c * K + kk]], wsem.at[slot])

        read(0, 0).start()

        @pl.loop(0, n_chunks, step=2)
        def _(c0):
            for b in range(2):
                c = c0 + b
                read(c, b).wait()
                for kk in range(K):
                    write(c, kk, b).start()

                @pl.when(c + 1 < n_chunks)
                def _():
                    @pl.when(c >= 1)
                    def _():
                        for kk in range(K):
                            write(c - 1, kk, 1 - b).wait()
                    read(c + 1, 1 - b).start()

        for kk in range(K):
            write(n_chunks - 2, kk, 0).wait()
            write(n_chunks - 1, kk, 1).wait()

    return scatter_rows(rows, idx)


def _sc_gather(table, indices):
    M = indices.shape[0]
    W = table.shape[1]
    sub = SC_ROWS_PER_COPY
    n_cores, n_workers = _sc_workers()
    per_w = M // n_workers
    n_steps = per_w // sub
    assert per_w * n_workers == M and n_steps * sub == per_w and n_steps % 2 == 0
    mesh = plsc.VectorSubcoreMesh(core_axis_name="core", subcore_axis_name="subcore")

    @functools.partial(
        pl.kernel, out_type=jax.ShapeDtypeStruct((M, W), table.dtype), mesh=mesh,
        scratch_types=[pltpu.VMEM((n_steps, sub), I32), pltpu.VMEM((2, sub, W), table.dtype),
                       pltpu.SemaphoreType.DMA((2,)), pltpu.SemaphoreType.DMA((2,))])
    def gather_rows(x_hbm, i_hbm, o_hbm, idx_v, buf, gsem, wsem):
        wid = lax.axis_index("subcore") * n_cores + lax.axis_index("core")
        base = wid * per_w
        pltpu.sync_copy(i_hbm.at[wid], idx_v)

        def gather(s, slot):
            return pltpu.make_async_copy(x_hbm.at[idx_v.at[s]], buf.at[slot], gsem.at[slot])

        def write(s, slot):
            return pltpu.make_async_copy(buf.at[slot], o_hbm.at[pl.ds(base + s * sub, sub)], wsem.at[slot])

        gather(0, 0).start()

        @pl.loop(0, n_steps, step=2)
        def _(s0):
            for b in range(2):
                s = s0 + b
                gather(s, b).wait()
                write(s, b).start()

                @pl.when(s + 1 < n_steps)
                def _():
                    @pl.when(s >= 1)
                    def _():
                        write(s - 1, 1 - b).wait()
                    gather(s + 1, 1 - b).start()

        write(n_steps - 2, 0).wait()
        write(n_steps - 1, 1).wait()

    return gather_rows(table, indices.reshape(n_workers, n_steps, sub))


def _expert_kernel(vb_ref, ve_ref, nvis_ref, nxt_ref, par_ref,
                   x_ref, wgu_hbm, bgu_ref, wd_hbm, bd_ref, y_ref,
                   wgu_stage, wd_stage, wgu_bf, wd_bf, sem):
    v = pl.program_id(0)
    real = v < nvis_ref[0]
    e = ve_ref[v]
    first_of_expert = jnp.logical_or(v == 0, e != ve_ref[jnp.maximum(v - 1, 0)])
    slot = par_ref[e]

    def weight_copies(expert, s):
        return (pltpu.make_async_copy(wgu_hbm.at[expert], wgu_stage.at[s], sem.at[0, s]),
                pltpu.make_async_copy(wd_hbm.at[expert], wd_stage.at[s], sem.at[1, s]))

    @pl.when(v == 0)
    def _():
        for c in weight_copies(e, slot):
            c.start()

    @pl.when(jnp.logical_and(first_of_expert, real))
    def _():
        for c in weight_copies(e, slot):
            c.wait()
        nxt = nxt_ref[e]

        @pl.when(nxt >= 0)
        def _():
            for c in weight_copies(nxt, 1 - slot):
                c.start()

        wgu_bf[...] = wgu_stage[slot].astype(BF16)
        wd_bf[...] = wd_stage[slot].astype(BF16)

    @pl.when(real)
    def _():
        half = wgu_bf.shape[0] // 2
        x_lo, x_hi = _unpack_rows(x_ref[...])
        gu = (jnp.dot(x_lo.astype(BF16), wgu_bf[:half, :], preferred_element_type=F32)
              + jnp.dot(x_hi.astype(BF16), wgu_bf[half:, :], preferred_element_type=F32)) + bgu_ref[0]
        gate = jnp.minimum(gu[:, :D_EXPERT], SWIGLU_LIMIT)
        up = jnp.clip(gu[:, D_EXPERT:], -SWIGLU_LIMIT, SWIGLU_LIMIT)
        act = gate * jax.nn.sigmoid(SWIGLU_ALPHA * gate) * (up + 1.0)
        y_ref[...] = _pack_rows(jnp.dot(act.astype(BF16), wd_bf[...], preferred_element_type=F32) + bd_ref[0])


def _experts(x_sorted, visits, w_gate_up, b_gate_up, w_down, b_down):
    n_rows, W = x_sorted.shape
    D = 2 * W
    tm = EXPERT_TILE
    n_visits = visits[0].shape[0]
    n_prefetch = len(visits)
    by_block = lambda v, vb, *_: (vb[v], 0)
    by_expert = lambda v, vb, ve, *_: (ve[v], 0, 0)
    grid_spec = pltpu.PrefetchScalarGridSpec(
        num_scalar_prefetch=n_prefetch,
        grid=(n_visits,),
        in_specs=[
            pl.BlockSpec((tm, W), by_block),
            pl.BlockSpec(memory_space=pl.ANY),
            pl.BlockSpec((1, 1, 2 * D_EXPERT), by_expert),
            pl.BlockSpec(memory_space=pl.ANY),
            pl.BlockSpec((1, 1, D), by_expert),
        ],
        out_specs=pl.BlockSpec((tm, W), by_block),
        scratch_shapes=[pltpu.VMEM((2, D, 2 * D_EXPERT), F32), pltpu.VMEM((2, D_EXPERT, D), F32),
                        pltpu.VMEM((D, 2 * D_EXPERT), BF16), pltpu.VMEM((D_EXPERT, D), BF16),
                        pltpu.SemaphoreType.DMA((2, 2))],
    )
    return pl.pallas_call(
        _expert_kernel,
        grid_spec=grid_spec,
        out_shape=jax.ShapeDtypeStruct((n_rows, W), U32),
        compiler_params=pltpu.CompilerParams(
            dimension_semantics=("arbitrary",), vmem_limit_bytes=VMEM_LIMIT),
        name="expert_gmm",
    )(*visits, x_sorted, w_gate_up, b_gate_up.reshape(N_EXPERTS, 1, -1), w_down, b_down.reshape(N_EXPERTS, 1, -1))


def _visit_schedule(counts, n_assign, capacity):
    tm = EXPERT_TILE
    n_visits = n_assign // tm + N_EXPERTS
    per_e = (counts + tm - 1) // tm
    vend = jnp.cumsum(per_e)
    nvis = vend[-1]
    vc = jnp.minimum(jnp.arange(n_visits, dtype=I32), nvis - 1)
    done = vc[:, None] >= vend[None, :]
    e = jnp.sum(done, axis=1).astype(I32)
    blk = e * (capacity // tm) + vc - jnp.sum(jnp.where(done, per_e[None, :], 0), axis=1)
    ids = jnp.arange(N_EXPERTS, dtype=I32)
    nonempty = counts > 0
    later = jnp.logical_and(nonempty[None, :], ids[None, :] > ids[:, None])
    nxt = jnp.where(jnp.any(later, axis=1), jnp.argmax(later, axis=1), -1).astype(I32)
    parity = ((jnp.cumsum(nonempty.astype(I32)) - nonempty.astype(I32)) % 2).astype(I32)
    return blk.astype(I32), e, nvis.reshape(1).astype(I32), nxt, parity


def _combine_kernel(y0_ref, y1_ref, y2_ref, y3_ref, x1_ref, gate_ref, g_ref, *out_refs):
    o_ref = out_refs[-1]
    half = x1_ref.shape[1] // 2
    gates = gate_ref[...]
    lo = x1_ref[:, :half]
    hi = x1_ref[:, half:]
    for kk, y_ref in enumerate((y0_ref, y1_ref, y2_ref, y3_ref)):
        y_lo, y_hi = _unpack_rows(y_ref[...])
        lo = lo + gates[:, kk:kk + 1] * y_lo
        hi = hi + gates[:, kk:kk + 1] * y_hi
    ms = (jnp.sum(lo * lo, axis=-1, keepdims=True) + jnp.sum(hi * hi, axis=-1, keepdims=True)) / (2 * half)
    scale = lax.rsqrt(ms + RMS_EPS)
    o_ref[:, :half] = lo * scale * g_ref[:, :half]
    o_ref[:, half:] = hi * scale * g_ref[:, half:]


def _combine(y_tok, x1, gates_t, g_final, chunk, n_chunks, out_so_far):
    T, D = x1.shape
    tm = COMBINE_TILE
    ntc = T // n_chunks // tm
    y_spec = lambda kk: pl.BlockSpec((tm, D // 2), lambda i: (kk * ntc + i, 0))
    in_chunk = lambda i: (chunk * ntc + i, 0)
    in_specs = [y_spec(0), y_spec(1), y_spec(2), y_spec(3),
                pl.BlockSpec((tm, D), in_chunk),
                pl.BlockSpec((tm, TOP_K), in_chunk),
                pl.BlockSpec((1, D), lambda i: (0, 0))]
    args = [y_tok, y_tok, y_tok, y_tok, x1, gates_t, g_final.reshape(1, D)]
    aliases = {}
    if out_so_far is not None:
        in_specs.append(pl.BlockSpec(memory_space=pl.ANY))
        args.append(out_so_far)
        aliases = {len(args) - 1: 0}
    return pl.pallas_call(
        _combine_kernel,
        grid=(ntc,),
        in_specs=in_specs,
        out_specs=pl.BlockSpec((tm, D), in_chunk),
        out_shape=jax.ShapeDtypeStruct((T, D), F32),
        input_output_aliases=aliases,
        compiler_params=pltpu.CompilerParams(
            dimension_semantics=("arbitrary",), vmem_limit_bytes=VMEM_LIMIT),
        name="combine_norm",
    )(*args)


def kernel(x, g_mix, w_in, pool_w, pool_scale, w_out, g_moe, w_router, b_router, w_gate_up, b_gate_up,
           w_down, b_down, g_final):
    B, S, D = x.shape
    T = B * S
    assert g_mix.shape[0] == 1, "single-layer problem: the final norm is fused into the combine step"
    q, k, v, pool = _in_proj(x, g_mix[0], w_in[0], pool_w[0], pool_scale[0])
    attn = _attention(q, k, v)
    capacity = T
    x1, h2, gates, dest, cnt = _out_proj_router(
        attn.reshape(T, SB_WIDTH), pool.reshape(T, POOL_WIDTH), x.reshape(T, D),
        w_out[0], g_moe[0], w_router[0], b_router[0], capacity)
    x_sorted = _sc_dispatch(h2, dest, N_EXPERTS * capacity)
    visits = _visit_schedule(cnt[:, 0].astype(I32), T * TOP_K, capacity)
    y = _experts(x_sorted, visits, w_gate_up[0], b_gate_up[0], w_down[0], b_down[0])
    gates_t = gates.T
    tc = T // COMBINE_CHUNKS
    out = None
    for c in range(COMBINE_CHUNKS):
        y_tok = _sc_gather(y, dest[:, c * tc:(c + 1) * tc].reshape(-1))
        out = _combine(y_tok, x1, gates_t, g_final, c, COMBINE_CHUNKS, out)
    return out.reshape(B, S, D)
```

```python
import functools

import jax
import jax.numpy as jnp
from jax import lax
from jax.experimental import pallas as pl
from jax.experimental.pallas import tpu as pltpu
from jax.experimental.pallas import tpu_sc as plsc

F32 = jnp.float32
BF16 = jnp.bfloat16
I32 = jnp.int32
U32 = jnp.uint32

D_MODEL = 1024
SB_HEADS = 8
SB_HEAD_DIM = 64
SB_WIDTH = SB_HEADS * SB_HEAD_DIM
POOL_WINDOWS = (2, 4, 8, 16)
POOL_WIDTH = 512
POOL_GROUP_DIM = 128
N_EXPERTS = 32
TOP_K = 4
D_EXPERT = 1024
SWIGLU_LIMIT = 7.0
SWIGLU_ALPHA = 1.702
RMS_EPS = 1e-5

LANES = 128
HALO = 16
PROJ_TILE = 512
ATTN_TILE = 256
EXPERT_TILE = 512
COMBINE_TILE = 256
ATTN_SKIP_LOG2 = -160.0
LOG2_E = 1.4426950408889634
VMEM_LIMIT = 56 * 1024 * 1024

_NT = (((1,), (1,)), ((), ()))


def _rms(x, g):
    ms = jnp.mean(x * x, axis=-1, keepdims=True)
    return x * lax.rsqrt(ms + RMS_EPS) * g


def _pack_rows(x):
    n = x.shape[1] // 2
    lo = lax.bitcast_convert_type(x[:, :n].astype(BF16).astype(F32), U32)
    hi = lax.bitcast_convert_type(x[:, n:].astype(BF16).astype(F32), U32)
    return (lo >> 16) | (hi & jnp.uint32(0xFFFF0000))


def _unpack_rows(w):
    lo = lax.bitcast_convert_type(w << 16, F32)
    hi = lax.bitcast_convert_type(w & jnp.uint32(0xFFFF0000), F32)
    return lo, hi


def _in_proj_kernel(x_ref, g_ref, w_ref, pw_ref, ps_ref, q_ref, k_ref, v_ref, p_ref, uext_ref):
    s = pl.program_id(1)
    tm = x_ref.shape[1]
    h = _rms(x_ref[0], g_ref[...])
    proj = jnp.dot(h.astype(BF16), w_ref[...], preferred_element_type=F32)
    q_ref[0] = (proj[:, 0:SB_WIDTH] * (LOG2_E * SB_HEAD_DIM ** -0.5)).astype(BF16)
    k_ref[0] = proj[:, SB_WIDTH:2 * SB_WIDTH].astype(BF16)
    v_ref[0] = proj[:, 2 * SB_WIDTH:3 * SB_WIDTH].astype(BF16)
    u = proj[:, 3 * SB_WIDTH:]

    @pl.when(s == 0)
    def _():
        uext_ref[0:HALO, :] = jnp.zeros((HALO, POOL_WIDTH), F32)

    uext_ref[HALO:, :] = u
    t = s * tm + lax.broadcasted_iota(I32, (tm, 1), 0)
    for g, w in enumerate(POOL_WINDOWS):
        sl = slice(g * POOL_GROUP_DIM, (g + 1) * POOL_GROUP_DIM)
        ug = u[:, sl]
        acc = ug
        for i in range(1, w):
            acc = acc + uext_ref[HALO - i:HALO - i + tm, sl]
        count = jnp.minimum(t + 1, w).astype(F32)
        pooled = acc / count - ug
        mixed = jnp.dot(pooled.astype(BF16), pw_ref[g], preferred_element_type=F32)
        p_ref[0, :, sl] = (mixed * ps_ref[:, sl]).astype(BF16)
    uext_ref[0:HALO, :] = u[tm - HALO:, :]


def _in_proj(x, g_mix, w_in, pool_w, pool_scale):
    B, S, D = x.shape
    tm = PROJ_TILE
    n_out = w_in.shape[1]
    out_sd = jax.ShapeDtypeStruct((B, S, SB_WIDTH), BF16)
    blk = pl.BlockSpec((1, tm, SB_WIDTH), lambda b, s: (b, s, 0))
    return pl.pallas_call(
        _in_proj_kernel,
        grid=(B, S // tm),
        in_specs=[
            pl.BlockSpec((1, tm, D), lambda b, s: (b, s, 0)),
            pl.BlockSpec((1, D), lambda b, s: (0, 0)),
            pl.BlockSpec((D, n_out), lambda b, s: (0, 0)),
            pl.BlockSpec(pool_w.shape, lambda b, s: (0, 0, 0)),
            pl.BlockSpec((1, POOL_WIDTH), lambda b, s: (0, 0)),
        ],
        out_specs=[blk, blk, blk, blk],
        out_shape=[out_sd, out_sd, out_sd, out_sd],
        scratch_shapes=[pltpu.VMEM((HALO + tm, POOL_WIDTH), F32)],
        compiler_params=pltpu.CompilerParams(
            dimension_semantics=("arbitrary", "arbitrary"), vmem_limit_bytes=VMEM_LIMIT),
        name="in_proj_pool",
    )(x, g_mix.reshape(1, D), w_in.astype(BF16), pool_w.astype(BF16), pool_scale.reshape(1, POOL_WIDTH))


def _attn_kernel(q_ref, k_ref, v_ref, o_ref):
    tq = ATTN_TILE
    tk = tq
    qi = pl.program_id(2)
    lane = lax.broadcasted_iota(I32, (tq, LANES), 1)
    row = lax.broadcasted_iota(I32, (tq, tk), 0)
    col = lax.broadcasted_iota(I32, (tq, tk), 1)
    causal = col < row
    tri = (row > col).astype(BF16)

    def head_rows(half, h):
        rows = q_ref[0, half * tq:(half + 1) * tq, :]
        return jnp.where((lane >= h * SB_HEAD_DIM) & (lane < (h + 1) * SB_HEAD_DIM), rows, jnp.zeros_like(rows))

    def scores(qh, j):
        kb = k_ref[0, pl.ds(pl.multiple_of(j * tk, tk), tk), :]
        return lax.dot_general(qh, kb, _NT, preferred_element_type=F32)

    def values(j):
        return v_ref[0, pl.ds(pl.multiple_of(j * tk, tk), tk), :]

    def neg_log(z, mask):
        neg_abs = lax.bitcast_convert_type(lax.bitcast_convert_type(z, U32) | jnp.uint32(0x80000000), F32)
        nl = jnp.maximum(z, 0.0) + jnp.log2(1.0 + jnp.exp2(neg_abs))
        return jnp.where(causal, nl, 0.0) if mask else nl

    def tile(qh, j, mask):
        z = scores(qh, j)
        nl = neg_log(z, mask)
        ntail = jnp.dot(nl.astype(BF16), tri, preferred_element_type=F32)
        return (z - nl) - ntail, -(ntail[:, 0:1] + nl[:, 0:1]), values(j)

    has_prev = qi > 0
    qs = [head_rows(half, h) for half in range(2) for h in range(2)]
    chains = []
    for n, qh in enumerate(qs):
        diag = 2 * qi + n // 2
        chains += [(qh, diag, True), (qh, jnp.maximum(diag - 1, 0), False)]
    zs = [scores(qh, j) for qh, j, _ in chains]
    nls = [neg_log(z, is_diag) for z, (_, _, is_diag) in zip(zs, chains)]
    nts = [jnp.dot(nl.astype(BF16), tri, preferred_element_type=F32) for nl in nls]
    tots = [-(nt[:, 0:1] + nl[:, 0:1]) for nt, nl in zip(nts, nls)]
    ws = []
    for n in range(4):
        d, p = 2 * n, 2 * n + 1
        w_d = jnp.where(causal, jnp.exp2((zs[d] - nls[d]) - nts[d]), 0.0)
        w_p = jnp.exp2((zs[p] - nls[p]) - nts[p] + tots[d])
        if n < 2:
            w_p = jnp.where(has_prev, w_p, 0.0)
        ws += [w_d.astype(BF16), w_p.astype(BF16)]
    pvs = [jnp.dot(w, values(j), preferred_element_type=F32) for w, (_, j, _) in zip(ws, chains)]
    accs = [pvs[2 * n] + pvs[2 * n + 1] for n in range(4)]
    carries = [tots[2 * n] + tots[2 * n + 1] for n in range(4)]

    def cond(st):
        j, cs, _ = st
        live = functools.reduce(jnp.maximum, [jnp.max(c) for c in cs])
        return jnp.logical_and(j >= 0, live > ATTN_SKIP_LOG2)

    def body(st):
        j, cs, acs = st
        new_c, new_a = [], []
        for n, (qh, c, a) in enumerate(zip(qs, cs, acs)):
            if n < 2:
                c = jnp.where(j >= 1, c, -jnp.inf)
                lw, tot, vb = tile(qh, jnp.maximum(j - 1, 0), False)
            else:
                lw, tot, vb = tile(qh, j, False)
            new_a.append(a + jnp.dot(jnp.exp2(lw + c).astype(BF16), vb, preferred_element_type=F32))
            new_c.append(c + tot)
        return j - 1, tuple(new_c), tuple(new_a)

    _, _, accs = lax.while_loop(cond, body, (2 * qi - 1, tuple(carries), tuple(accs)))
    for half in range(2):
        o_ref[0, half * tq:(half + 1) * tq, :] = jnp.where(
            lane < SB_HEAD_DIM, accs[2 * half], accs[2 * half + 1]).astype(o_ref.dtype)


def _attention(q, k, v):
    B, S, W = q.shape
    tq = 2 * ATTN_TILE
    n_pairs = W // LANES
    return pl.pallas_call(
        _attn_kernel,
        grid=(B, n_pairs, S // tq),
        in_specs=[
            pl.BlockSpec((1, tq, LANES), lambda b, p, i: (b, i, p)),
            pl.BlockSpec((1, S, LANES), lambda b, p, i: (b, 0, p)),
            pl.BlockSpec((1, S, LANES), lambda b, p, i: (b, 0, p)),
        ],
        out_specs=pl.BlockSpec((1, tq, LANES), lambda b, p, i: (b, i, p)),
        out_shape=jax.ShapeDtypeStruct((B, S, W), BF16),
        compiler_params=pltpu.CompilerParams(
            dimension_semantics=("arbitrary", "arbitrary", "arbitrary"), vmem_limit_bytes=VMEM_LIMIT),
        name="stickbreak_attn",
    )(q, k, v)


def _out_proj_router_kernel(attn_ref, pool_ref, x_ref, wo_ref, g_ref, wr_ref, br_ref, tri_ref,
                            x1_ref, h2_ref, gate_ref, dest_ref, cnt_ref, carry_ref, *, capacity):
    i = pl.program_id(0)
    tm = x_ref.shape[0]

    @pl.when(i == 0)
    def _():
        carry_ref[...] = jnp.zeros_like(carry_ref)

    mixed = jnp.dot(attn_ref[...], wo_ref[0:SB_WIDTH, :], preferred_element_type=F32)
    mixed = mixed + jnp.dot(pool_ref[...], wo_ref[SB_WIDTH:, :], preferred_element_type=F32)
    x1 = x_ref[...] + mixed
    h2 = _rms(x1, g_ref[...])
    x1_ref[...] = x1
    h2_ref[...] = _pack_rows(h2)

    hh = h2.astype(BF16)
    hl = (h2 - hh.astype(F32)).astype(BF16)
    wr = wr_ref[...]
    wh = wr.astype(BF16)
    wl = (wr - wh.astype(F32)).astype(BF16)
    logits = (lax.dot_general(wh, hh, _NT, preferred_element_type=F32)
              + lax.dot_general(wh, hl, _NT, preferred_element_type=F32)
              + lax.dot_general(wl, hh, _NT, preferred_element_type=F32)) + br_ref[...]

    eid = lax.broadcasted_iota(I32, (N_EXPERTS, tm), 0).astype(F32)
    work = logits
    vals, ids = [], []
    for _ in range(TOP_K):
        m = jnp.max(work, axis=0, keepdims=True)
        sel = jnp.min(jnp.where(work == m, eid, float(N_EXPERTS)), axis=0, keepdims=True)
        vals.append(m)
        ids.append(sel)
        work = jnp.where(eid == sel, -jnp.inf, work)
    exps = [jnp.exp(v - vals[0]) for v in vals]
    denom = exps[0] + exps[1] + exps[2] + exps[3]

    onehot = jnp.zeros((N_EXPERTS, tm), F32)
    for sel in ids:
        onehot = onehot + (eid == sel).astype(F32)
    before = jnp.dot(onehot.astype(BF16), tri_ref[...], preferred_element_type=F32) + carry_ref[...]
    for kk in range(TOP_K):
        gate_ref[kk:kk + 1, :] = exps[kk] / denom
        rk = jnp.sum(jnp.where(eid == ids[kk], before, 0.0), axis=0, keepdims=True)
        dest_ref[kk:kk + 1, :] = (ids[kk] * float(capacity) + rk).astype(I32)
    carry_ref[...] = carry_ref[...] + jnp.sum(onehot, axis=1, keepdims=True)
    cnt_ref[...] = jnp.broadcast_to(carry_ref[...], cnt_ref.shape)


def _out_proj_router(attn, pool, x, w_out, g_moe, w_router, b_router, capacity):
    T, D = x.shape
    tm = PROJ_TILE
    r = lax.broadcasted_iota(I32, (tm, tm), 0)
    c = lax.broadcasted_iota(I32, (tm, tm), 1)
    tri = (r < c).astype(BF16)
    row_blk = lambda w: pl.BlockSpec((tm, w), lambda i: (i, 0))
    fixed = lambda shape: pl.BlockSpec(shape, lambda i: tuple(0 for _ in shape))
    sel_blk = pl.BlockSpec((TOP_K, tm), lambda i: (0, i))
    assert N_EXPERTS * capacity < 2 ** 24
    return pl.pallas_call(
        functools.partial(_out_proj_router_kernel, capacity=capacity),
        grid=(T // tm,),
        in_specs=[row_blk(SB_WIDTH), row_blk(POOL_WIDTH), row_blk(D), fixed((D, D)), fixed((1, D)),
                  fixed((N_EXPERTS, D)), fixed((N_EXPERTS, 1)), fixed((tm, tm))],
        out_specs=[row_blk(D), row_blk(D // 2), sel_blk, sel_blk, fixed((N_EXPERTS, LANES))],
        out_shape=[jax.ShapeDtypeStruct((T, D), F32), jax.ShapeDtypeStruct((T, D // 2), U32),
                   jax.ShapeDtypeStruct((TOP_K, T), F32), jax.ShapeDtypeStruct((TOP_K, T), I32),
                   jax.ShapeDtypeStruct((N_EXPERTS, LANES), F32)],
        scratch_shapes=[pltpu.VMEM((N_EXPERTS, 1), F32)],
        compiler_params=pltpu.CompilerParams(
            dimension_semantics=("arbitrary",), vmem_limit_bytes=VMEM_LIMIT),
        name="out_proj_router",
    )(attn, pool, x, w_out.astype(BF16), g_moe.reshape(1, D), w_router.T, b_router.reshape(N_EXPERTS, 1), tri)


SC_ROWS_PER_COPY = 64


def _sc_workers():
    info = plsc.get_sparse_core_info()
    return info.num_cores, info.num_cores * info.num_subcores


def _sc_dispatch(rows, dest, n_out):
    T, W = rows.shape
    K = dest.shape[0]
    sub = SC_ROWS_PER_COPY
    n_cores, n_workers = _sc_workers()
    per_w = T // n_workers
    n_chunks = per_w // sub
    assert per_w * n_workers == T and n_chunks * sub == per_w and n_chunks % 2 == 0
    idx = dest.reshape(K, n_workers, n_chunks, sub).transpose(1, 2, 0, 3).reshape(n_workers, n_chunks * K, sub)
    mesh = plsc.VectorSubcoreMesh(core_axis_name="core", subcore_axis_name="subcore")

    @functools.partial(
        pl.kernel, out_type=jax.ShapeDtypeStruct((n_out, W), rows.dtype), mesh=mesh,
        scratch_types=[pltpu.VMEM((n_chunks * K, sub), I32), pltpu.VMEM((2, sub, W), rows.dtype),
                       pltpu.SemaphoreType.DMA((2,)), pltpu.SemaphoreType.DMA((2,))])
    def scatter_rows(x_hbm, i_hbm, o_hbm, idx_v, buf, rsem, wsem):
        wid = lax.axis_index("subcore") * n_cores + lax.axis_index("core")
        base = wid * per_w
        pltpu.sync_copy(i_hbm.at[wid], idx_v)

        def read(c, slot):
            return pltpu.make_async_copy(x_hbm.at[pl.ds(base + c * sub, sub)], buf.at[slot], rsem.at[slot])

        def write(c, kk, slot):
            return pltpu.make_async_copy(buf.at[slot], o_hbm.at[idx_v.at[c * K + kk]], wsem.at[slot])

        read(0, 0).start()

        @pl.loop(0, n_chunks, step=2)
        def _(c0):
            for b in range(2):
                c = c0 + b
                read(c, b).wait()
                for kk in range(K):
                    write(c, kk, b).start()

                @pl.when(c + 1 < n_chunks)
                def _():
                    @pl.when(c >= 1)
                    def _():
                        for kk in range(K):
                            write(c - 1, kk, 1 - b).wait()
                    read(c + 1, 1 - b).start()

        for kk in range(K):
            write(n_chunks - 2, kk, 0).wait()
            write(n_chunks - 1, kk, 1).wait()

    return scatter_rows(rows, idx)


def _sc_gather(table, indices):
    M = indices.shape[0]
    W = table.shape[1]
    sub = SC_ROWS_PER_COPY
    n_cores, n_workers = _sc_workers()
    per_w = M // n_workers
    n_steps = per_w // sub
    assert per_w * n_workers == M and n_steps * sub == per_w and n_steps % 2 == 0
    mesh = plsc.VectorSubcoreMesh(core_axis_name="core", subcore_axis_name="subcore")

    @functools.partial(
        pl.kernel, out_type=jax.ShapeDtypeStruct((M, W), table.dtype), mesh=mesh,
        scratch_types=[pltpu.VMEM((n_steps, sub), I32), pltpu.VMEM((2, sub, W), table.dtype),
                       pltpu.SemaphoreType.DMA((2,)), pltpu.SemaphoreType.DMA((2,))])
    def gather_rows(x_hbm, i_hbm, o_hbm, idx_v, buf, gsem, wsem):
        wid = lax.axis_index("subcore") * n_cores + lax.axis_index("core")
        base = wid * per_w
        pltpu.sync_copy(i_hbm.at[wid], idx_v)

        def gather(s, slot):
            return pltpu.make_async_copy(x_hbm.at[idx_v.at[s]], buf.at[slot], gsem.at[slot])

        def write(s, slot):
            return pltpu.make_async_copy(buf.at[slot], o_hbm.at[pl.ds(base + s * sub, sub)], wsem.at[slot])

        gather(0, 0).start()

        @pl.loop(0, n_steps, step=2)
        def _(s0):
            for b in range(2):
                s = s0 + b
                gather(s, b).wait()
                write(s, b).start()

                @pl.when(s + 1 < n_steps)
                def _():
                    @pl.when(s >= 1)
                    def _():
                        write(s - 1, 1 - b).wait()
                    gather(s + 1, 1 - b).start()

        write(n_steps - 2, 0).wait()
        write(n_steps - 1, 1).wait()

    return gather_rows(table, indices.reshape(n_workers, n_steps, sub))


def _expert_kernel(vb_ref, ve_ref, nvis_ref, nxt_ref, par_ref,
                   x_ref, wgu_hbm, bgu_ref, wd_hbm, bd_ref, y_ref,
                   wgu_stage, wd_stage, wgu_bf, wd_bf, sem):
    v = pl.program_id(0)
    real = v < nvis_ref[0]
    e = ve_ref[v]
    first_of_expert = jnp.logical_or(v == 0, e != ve_ref[jnp.maximum(v - 1, 0)])
    slot = par_ref[e]

    def weight_copies(expert, s):
        return (pltpu.make_async_copy(wgu_hbm.at[expert], wgu_stage.at[s], sem.at[0, s]),
                pltpu.make_async_copy(wd_hbm.at[expert], wd_stage.at[s], sem.at[1, s]))

    @pl.when(v == 0)
    def _():
        for c in weight_copies(e, slot):
            c.start()

    @pl.when(jnp.logical_and(first_of_expert, real))
    def _():
        for c in weight_copies(e, slot):
            c.wait()
        nxt = nxt_ref[e]

        @pl.when(nxt >= 0)
        def _():
            for c in weight_copies(nxt, 1 - slot):
                c.start()

        wgu_bf[...] = wgu_stage[slot].astype(BF16)
        wd_bf[...] = wd_stage[slot].astype(BF16)

    @pl.when(real)
    def _():
        half = wgu_bf.shape[0] // 2
        x_lo, x_hi = _unpack_rows(x_ref[...])
        gu = (jnp.dot(x_lo.astype(BF16), wgu_bf[:half, :], preferred_element_type=F32)
              + jnp.dot(x_hi.astype(BF16), wgu_bf[half:, :], preferred_element_type=F32)) + bgu_ref[0]
        gate = jnp.minimum(gu[:, :D_EXPERT], SWIGLU_LIMIT)
        up = jnp.clip(gu[:, D_EXPERT:], -SWIGLU_LIMIT, SWIGLU_LIMIT)
        act = gate * jax.nn.sigmoid(SWIGLU_ALPHA * gate) * (up + 1.0)
        y_ref[...] = _pack_rows(jnp.dot(act.astype(BF16), wd_bf[...], preferred_element_type=F32) + bd_ref[0])


def _experts(x_sorted, visits, w_gate_up, b_gate_up, w_down, b_down):
    n_rows, W = x_sorted.shape
    D = 2 * W
    tm = EXPERT_TILE
    n_visits = visits[0].shape[0]
    n_prefetch = len(visits)
    by_block = lambda v, vb, *_: (vb[v], 0)
    by_expert = lambda v, vb, ve, *_: (ve[v], 0, 0)
    grid_spec = pltpu.PrefetchScalarGridSpec(
        num_scalar_prefetch=n_prefetch,
        grid=(n_visits,),
        in_specs=[
            pl.BlockSpec((tm, W), by_block),
            pl.BlockSpec(memory_space=pl.ANY),
            pl.BlockSpec((1, 1, 2 * D_EXPERT), by_expert),
            pl.BlockSpec(memory_space=pl.ANY),
            pl.BlockSpec((1, 1, D), by_expert),
        ],
        out_specs=pl.BlockSpec((tm, W), by_block),
        scratch_shapes=[pltpu.VMEM((2, D, 2 * D_EXPERT), F32), pltpu.VMEM((2, D_EXPERT, D), F32),
                        pltpu.VMEM((D, 2 * D_EXPERT), BF16), pltpu.VMEM((D_EXPERT, D), BF16),
                        pltpu.SemaphoreType.DMA((2, 2))],
    )
    return pl.pallas_call(
        _expert_kernel,
        grid_spec=grid_spec,
        out_shape=jax.ShapeDtypeStruct((n_rows, W), U32),
        compiler_params=pltpu.CompilerParams(
            dimension_semantics=("arbitrary",), vmem_limit_bytes=VMEM_LIMIT),
        name="expert_gmm",
    )(*visits, x_sorted, w_gate_up, b_gate_up.reshape(N_EXPERTS, 1, -1), w_down, b_down.reshape(N_EXPERTS, 1, -1))


def _visit_schedule(counts, n_assign, capacity):
    tm = EXPERT_TILE
    n_visits = n_assign // tm + N_EXPERTS
    per_e = (counts + tm - 1) // tm
    vend = jnp.cumsum(per_e)
    nvis = vend[-1]
    vc = jnp.minimum(jnp.arange(n_visits, dtype=I32), nvis - 1)
    done = vc[:, None] >= vend[None, :]
    e = jnp.sum(done, axis=1).astype(I32)
    blk = e * (capacity // tm) + vc - jnp.sum(jnp.where(done, per_e[None, :], 0), axis=1)
    ids = jnp.arange(N_EXPERTS, dtype=I32)
    nonempty = counts > 0
    later = jnp.logical_and(nonempty[None, :], ids[None, :] > ids[:, None])
    nxt = jnp.where(jnp.any(later, axis=1), jnp.argmax(later, axis=1), -1).astype(I32)
    parity = ((jnp.cumsum(nonempty.astype(I32)) - nonempty.astype(I32)) % 2).astype(I32)
    return blk.astype(I32), e, nvis.reshape(1).astype(I32), nxt, parity


def _combine_kernel(y0_ref, y1_ref, y2_ref, y3_ref, x1_ref, gate_ref, g_ref, o_ref):
    half = x1_ref.shape[1] // 2
    gates = gate_ref[...]
    lo = x1_ref[:, :half]
    hi = x1_ref[:, half:]
    for kk, y_ref in enumerate((y0_ref, y1_ref, y2_ref, y3_ref)):
        y_lo, y_hi = _unpack_rows(y_ref[...])
        lo = lo + gates[:, kk:kk + 1] * y_lo
        hi = hi + gates[:, kk:kk + 1] * y_hi
    ms = (jnp.sum(lo * lo, axis=-1, keepdims=True) + jnp.sum(hi * hi, axis=-1, keepdims=True)) / (2 * half)
    scale = lax.rsqrt(ms + RMS_EPS)
    o_ref[:, :half] = lo * scale * g_ref[:, :half]
    o_ref[:, half:] = hi * scale * g_ref[:, half:]


def _combine(y_tok, x1, gates_t, g_final):
    T, D = x1.shape
    tm = COMBINE_TILE
    nt = T // tm
    y_spec = lambda kk: pl.BlockSpec((tm, D // 2), lambda i: (kk * nt + i, 0))
    return pl.pallas_call(
        _combine_kernel,
        grid=(nt,),
        in_specs=[y_spec(0), y_spec(1), y_spec(2), y_spec(3),
                  pl.BlockSpec((tm, D), lambda i: (i, 0)),
                  pl.BlockSpec((tm, TOP_K), lambda i: (i, 0)),
                  pl.BlockSpec((1, D), lambda i: (0, 0))],
        out_specs=pl.BlockSpec((tm, D), lambda i: (i, 0)),
        out_shape=jax.ShapeDtypeStruct((T, D), F32),
        compiler_params=pltpu.CompilerParams(
            dimension_semantics=("arbitrary",), vmem_limit_bytes=VMEM_LIMIT),
        name="combine_norm",
    )(y_tok, y_tok, y_tok, y_tok, x1, gates_t, g_final.reshape(1, D))


def kernel(x, g_mix, w_in, pool_w, pool_scale, w_out, g_moe, w_router, b_router, w_gate_up, b_gate_up,
           w_down, b_down, g_final):
    B, S, D = x.shape
    T = B * S
    assert g_mix.shape[0] == 1, "single-layer problem: the final norm is fused into the combine step"
    q, k, v, pool = _in_proj(x, g_mix[0], w_in[0], pool_w[0], pool_scale[0])
    attn = _attention(q, k, v)
    capacity = T
    x1, h2, gates, dest, cnt = _out_proj_router(
        attn.reshape(T, SB_WIDTH), pool.reshape(T, POOL_WIDTH), x.reshape(T, D),
        w_out[0], g_moe[0], w_router[0], b_router[0], capacity)
    x_sorted = _sc_dispatch(h2, dest, N_EXPERTS * capacity)
    visits = _visit_schedule(cnt[:, 0].astype(I32), T * TOP_K, capacity)
    y = _experts(x_sorted, visits, w_gate_up[0], b_gate_up[0], w_down[0], b_down[0])
    y_tok = _sc_gather(y, dest.reshape(-1))
    return _combine(y_tok, x1, gates.T, g_final).reshape(B, S, D)
```

```python
import functools

import jax
import jax.numpy as jnp
from jax import lax
from jax.experimental import pallas as pl
from jax.experimental.pallas import tpu as pltpu
from jax.experimental.pallas import tpu_sc as plsc

F32 = jnp.float32
BF16 = jnp.bfloat16
I32 = jnp.int32
U32 = jnp.uint32

D_MODEL = 1024
SB_HEADS = 8
SB_HEAD_DIM = 64
SB_WIDTH = SB_HEADS * SB_HEAD_DIM
POOL_WINDOWS = (2, 4, 8, 16)
POOL_WIDTH = 512
POOL_GROUP_DIM = 128
N_EXPERTS = 32
TOP_K = 4
D_EXPERT = 1024
SWIGLU_LIMIT = 7.0
SWIGLU_ALPHA = 1.702
RMS_EPS = 1e-5

LANES = 128
HALO = 16
PROJ_TILE = 512
ATTN_TILE = 256
ATTN_SUBTILES = 4
EXPERT_TILE = 512
COMBINE_TILE = 256
ATTN_SKIP_LOG2 = -160.0
LOG2_E = 1.4426950408889634
VMEM_LIMIT = 56 * 1024 * 1024

_NT = (((1,), (1,)), ((), ()))


def _rms(x, g):
    ms = jnp.mean(x * x, axis=-1, keepdims=True)
    return x * lax.rsqrt(ms + RMS_EPS) * g


def _pack_rows(x):
    n = x.shape[1] // 2
    lo = lax.bitcast_convert_type(x[:, :n].astype(BF16).astype(F32), U32)
    hi = lax.bitcast_convert_type(x[:, n:].astype(BF16).astype(F32), U32)
    return (lo >> 16) | (hi & jnp.uint32(0xFFFF0000))


def _unpack_rows(w):
    lo = lax.bitcast_convert_type(w << 16, F32)
    hi = lax.bitcast_convert_type(w & jnp.uint32(0xFFFF0000), F32)
    return lo, hi


def _in_proj_kernel(x_ref, g_ref, w_ref, pw_ref, ps_ref, q_ref, k_ref, v_ref, p_ref, uext_ref):
    s = pl.program_id(1)
    tm = x_ref.shape[1]
    h = _rms(x_ref[0], g_ref[...])
    proj = jnp.dot(h.astype(BF16), w_ref[...], preferred_element_type=F32)
    q_ref[0] = (proj[:, 0:SB_WIDTH] * (LOG2_E * SB_HEAD_DIM ** -0.5)).astype(BF16)
    k_ref[0] = proj[:, SB_WIDTH:2 * SB_WIDTH].astype(BF16)
    v_ref[0] = proj[:, 2 * SB_WIDTH:3 * SB_WIDTH].astype(BF16)
    u = proj[:, 3 * SB_WIDTH:]

    @pl.when(s == 0)
    def _():
        uext_ref[0:HALO, :] = jnp.zeros((HALO, POOL_WIDTH), F32)

    uext_ref[HALO:, :] = u
    t = s * tm + lax.broadcasted_iota(I32, (tm, 1), 0)
    for g, w in enumerate(POOL_WINDOWS):
        sl = slice(g * POOL_GROUP_DIM, (g + 1) * POOL_GROUP_DIM)
        ug = u[:, sl]
        acc = ug
        for i in range(1, w):
            acc = acc + uext_ref[HALO - i:HALO - i + tm, sl]
        count = jnp.minimum(t + 1, w).astype(F32)
        pooled = acc / count - ug
        mixed = jnp.dot(pooled.astype(BF16), pw_ref[g], preferred_element_type=F32)
        p_ref[0, :, sl] = (mixed * ps_ref[:, sl]).astype(BF16)
    uext_ref[0:HALO, :] = u[tm - HALO:, :]


def _in_proj(x, g_mix, w_in, pool_w, pool_scale):
    B, S, D = x.shape
    tm = PROJ_TILE
    n_out = w_in.shape[1]
    out_sd = jax.ShapeDtypeStruct((B, S, SB_WIDTH), BF16)
    blk = pl.BlockSpec((1, tm, SB_WIDTH), lambda b, s: (b, s, 0))
    return pl.pallas_call(
        _in_proj_kernel,
        grid=(B, S // tm),
        in_specs=[
            pl.BlockSpec((1, tm, D), lambda b, s: (b, s, 0)),
            pl.BlockSpec((1, D), lambda b, s: (0, 0)),
            pl.BlockSpec((D, n_out), lambda b, s: (0, 0)),
            pl.BlockSpec(pool_w.shape, lambda b, s: (0, 0, 0)),
            pl.BlockSpec((1, POOL_WIDTH), lambda b, s: (0, 0)),
        ],
        out_specs=[blk, blk, blk, blk],
        out_shape=[out_sd, out_sd, out_sd, out_sd],
        scratch_shapes=[pltpu.VMEM((HALO + tm, POOL_WIDTH), F32)],
        compiler_params=pltpu.CompilerParams(
            dimension_semantics=("arbitrary", "arbitrary"), vmem_limit_bytes=VMEM_LIMIT),
        name="in_proj_pool",
    )(x, g_mix.reshape(1, D), w_in.astype(BF16), pool_w.astype(BF16), pool_scale.reshape(1, POOL_WIDTH))


def _attn_kernel(q_ref, k_ref, v_ref, o_ref):
    tq = ATTN_TILE
    tk = tq
    qi = pl.program_id(2)
    lane = lax.broadcasted_iota(I32, (tq, LANES), 1)
    row = lax.broadcasted_iota(I32, (tq, tk), 0)
    col = lax.broadcasted_iota(I32, (tq, tk), 1)
    causal = col < row
    tri = (row > col).astype(BF16)

    def head_rows(r, h):
        rows = q_ref[0, r * tq:(r + 1) * tq, :]
        return jnp.where((lane >= h * SB_HEAD_DIM) & (lane < (h + 1) * SB_HEAD_DIM), rows, jnp.zeros_like(rows))

    def scores(qh, j):
        kb = k_ref[0, pl.ds(pl.multiple_of(j * tk, tk), tk), :]
        return lax.dot_general(qh, kb, _NT, preferred_element_type=F32)

    def values(j):
        return v_ref[0, pl.ds(pl.multiple_of(j * tk, tk), tk), :]

    def neg_log(z, mask):
        neg_abs = lax.bitcast_convert_type(lax.bitcast_convert_type(z, U32) | jnp.uint32(0x80000000), F32)
        nl = jnp.maximum(z, 0.0) + jnp.log2(1.0 + jnp.exp2(neg_abs))
        return jnp.where(causal, nl, 0.0) if mask else nl

    def tile(qh, j, mask):
        z = scores(qh, j)
        nl = neg_log(z, mask)
        ntail = jnp.dot(nl.astype(BF16), tri, preferred_element_type=F32)
        return (z - nl) - ntail, -(ntail[:, 0:1] + nl[:, 0:1]), values(j)

    nq = ATTN_SUBTILES
    has_prev = qi > 0
    qs = [head_rows(r, h) for r in range(nq) for h in range(2)]
    chains = []
    for n, qh in enumerate(qs):
        diag = nq * qi + n // 2
        chains += [(qh, diag, True), (qh, jnp.maximum(diag - 1, 0), False)]
    zs = [scores(qh, j) for qh, j, _ in chains]
    nls = [neg_log(z, is_diag) for z, (_, _, is_diag) in zip(zs, chains)]
    nts = [jnp.dot(nl.astype(BF16), tri, preferred_element_type=F32) for nl in nls]
    tots = [-(nt[:, 0:1] + nl[:, 0:1]) for nt, nl in zip(nts, nls)]
    ws = []
    for n in range(2 * nq):
        d, p = 2 * n, 2 * n + 1
        w_d = jnp.where(causal, jnp.exp2((zs[d] - nls[d]) - nts[d]), 0.0)
        w_p = jnp.exp2((zs[p] - nls[p]) - nts[p] + tots[d])
        if n < 2:
            w_p = jnp.where(has_prev, w_p, 0.0)
        ws += [w_d.astype(BF16), w_p.astype(BF16)]
    pvs = [jnp.dot(w, values(j), preferred_element_type=F32) for w, (_, j, _) in zip(ws, chains)]
    accs = [pvs[2 * n] + pvs[2 * n + 1] for n in range(2 * nq)]
    carries = [tots[2 * n] + tots[2 * n + 1] for n in range(2 * nq)]

    def cond(st):
        j, cs, _ = st
        live = functools.reduce(jnp.maximum, [jnp.max(c) for c in cs])
        return jnp.logical_and(j >= 0, live > ATTN_SKIP_LOG2)

    def body(st):
        j, cs, acs = st
        new_c, new_a = [], []
        for n, (qh, c, a) in enumerate(zip(qs, cs, acs)):
            lag = nq - 1 - n // 2
            if lag:
                c = jnp.where(j >= lag, c, -jnp.inf)
            lw, tot, vb = tile(qh, jnp.maximum(j - lag, 0), False)
            new_a.append(a + jnp.dot(jnp.exp2(lw + c).astype(BF16), vb, preferred_element_type=F32))
            new_c.append(c + tot)
        return j - 1, tuple(new_c), tuple(new_a)

    _, _, accs = lax.while_loop(cond, body, (nq * qi + nq - 3, tuple(carries), tuple(accs)))
    for r in range(nq):
        o_ref[0, r * tq:(r + 1) * tq, :] = jnp.where(
            lane < SB_HEAD_DIM, accs[2 * r], accs[2 * r + 1]).astype(o_ref.dtype)


def _attention(q, k, v):
    B, S, W = q.shape
    tq = ATTN_SUBTILES * ATTN_TILE
    n_pairs = W // LANES
    return pl.pallas_call(
        _attn_kernel,
        grid=(B, n_pairs, S // tq),
        in_specs=[
            pl.BlockSpec((1, tq, LANES), lambda b, p, i: (b, i, p)),
            pl.BlockSpec((1, S, LANES), lambda b, p, i: (b, 0, p)),
            pl.BlockSpec((1, S, LANES), lambda b, p, i: (b, 0, p)),
        ],
        out_specs=pl.BlockSpec((1, tq, LANES), lambda b, p, i: (b, i, p)),
        out_shape=jax.ShapeDtypeStruct((B, S, W), BF16),
        compiler_params=pltpu.CompilerParams(
            dimension_semantics=("arbitrary", "arbitrary", "arbitrary"), vmem_limit_bytes=VMEM_LIMIT),
        name="stickbreak_attn",
    )(q, k, v)


def _out_proj_router_kernel(attn_ref, pool_ref, x_ref, wo_ref, g_ref, wr_ref, br_ref, tri_ref,
                            x1_ref, h2_ref, gate_ref, dest_ref, cnt_ref, carry_ref, *, capacity):
    i = pl.program_id(0)
    tm = x_ref.shape[0]

    @pl.when(i == 0)
    def _():
        carry_ref[...] = jnp.zeros_like(carry_ref)

    mixed = jnp.dot(attn_ref[...], wo_ref[0:SB_WIDTH, :], preferred_element_type=F32)
    mixed = mixed + jnp.dot(pool_ref[...], wo_ref[SB_WIDTH:, :], preferred_element_type=F32)
    x1 = x_ref[...] + mixed
    h2 = _rms(x1, g_ref[...])
    x1_ref[...] = x1
    h2_ref[...] = _pack_rows(h2)

    hh = h2.astype(BF16)
    hl = (h2 - hh.astype(F32)).astype(BF16)
    wr = wr_ref[...]
    wh = wr.astype(BF16)
    wl = (wr - wh.astype(F32)).astype(BF16)
    logits = (lax.dot_general(wh, hh, _NT, preferred_element_type=F32)
              + lax.dot_general(wh, hl, _NT, preferred_element_type=F32)
              + lax.dot_general(wl, hh, _NT, preferred_element_type=F32)) + br_ref[...]

    eid = lax.broadcasted_iota(I32, (N_EXPERTS, tm), 0).astype(F32)
    work = logits
    vals, ids = [], []
    for _ in range(TOP_K):
        m = jnp.max(work, axis=0, keepdims=True)
        sel = jnp.min(jnp.where(work == m, eid, float(N_EXPERTS)), axis=0, keepdims=True)
        vals.append(m)
        ids.append(sel)
        work = jnp.where(eid == sel, -jnp.inf, work)
    exps = [jnp.exp(v - vals[0]) for v in vals]
    denom = exps[0] + exps[1] + exps[2] + exps[3]

    onehot = jnp.zeros((N_EXPERTS, tm), F32)
    for sel in ids:
        onehot = onehot + (eid == sel).astype(F32)
    before = jnp.dot(onehot.astype(BF16), tri_ref[...], preferred_element_type=F32) + carry_ref[...]
    for kk in range(TOP_K):
        gate_ref[kk:kk + 1, :] = exps[kk] / denom
        rk = jnp.sum(jnp.where(eid == ids[kk], before, 0.0), axis=0, keepdims=True)
        dest_ref[kk:kk + 1, :] = (ids[kk] * float(capacity) + rk).astype(I32)
    carry_ref[...] = carry_ref[...] + jnp.sum(onehot, axis=1, keepdims=True)
    cnt_ref[...] = jnp.broadcast_to(carry_ref[...], cnt_ref.shape)


def _out_proj_router(attn, pool, x, w_out, g_moe, w_router, b_router, capacity):
    T, D = x.shape
    tm = PROJ_TILE
    r = lax.broadcasted_iota(I32, (tm, tm), 0)
    c = lax.broadcasted_iota(I32, (tm, tm), 1)
    tri = (r < c).astype(BF16)
    row_blk = lambda w: pl.BlockSpec((tm, w), lambda i: (i, 0))
    fixed = lambda shape: pl.BlockSpec(shape, lambda i: tuple(0 for _ in shape))
    sel_blk = pl.BlockSpec((TOP_K, tm), lambda i: (0, i))
    assert N_EXPERTS * capacity < 2 ** 24
    return pl.pallas_call(
        functools.partial(_out_proj_router_kernel, capacity=capacity),
        grid=(T // tm,),
        in_specs=[row_blk(SB_WIDTH), row_blk(POOL_WIDTH), row_blk(D), fixed((D, D)), fixed((1, D)),
                  fixed((N_EXPERTS, D)), fixed((N_EXPERTS, 1)), fixed((tm, tm))],
        out_specs=[row_blk(D), row_blk(D // 2), sel_blk, sel_blk, fixed((N_EXPERTS, LANES))],
        out_shape=[jax.ShapeDtypeStruct((T, D), F32), jax.ShapeDtypeStruct((T, D // 2), U32),
                   jax.ShapeDtypeStruct((TOP_K, T), F32), jax.ShapeDtypeStruct((TOP_K, T), I32),
                   jax.ShapeDtypeStruct((N_EXPERTS, LANES), F32)],
        scratch_shapes=[pltpu.VMEM((N_EXPERTS, 1), F32)],
        compiler_params=pltpu.CompilerParams(
            dimension_semantics=("arbitrary",), vmem_limit_bytes=VMEM_LIMIT),
        name="out_proj_router",
    )(attn, pool, x, w_out.astype(BF16), g_moe.reshape(1, D), w_router.T, b_router.reshape(N_EXPERTS, 1), tri)


SC_ROWS_PER_COPY = 64


def _sc_workers():
    info = plsc.get_sparse_core_info()
    return info.num_cores, info.num_cores * info.num_subcores


def _sc_dispatch(rows, dest, n_out):
    T, W = rows.shape
    K = dest.shape[0]
    sub = SC_ROWS_PER_COPY
    n_cores, n_workers = _sc_workers()
    per_w = T // n_workers
    n_chunks = per_w // sub
    assert per_w * n_workers == T and n_chunks * sub == per_w and n_chunks % 2 == 0
    idx = dest.reshape(K, n_workers, n_chunks, sub).transpose(1, 2, 0, 3).reshape(n_workers, n_chunks * K, sub)
    mesh = plsc.VectorSubcoreMesh(core_axis_name="core", subcore_axis_name="subcore")

    @functools.partial(
        pl.kernel, out_type=jax.ShapeDtypeStruct((n_out, W), rows.dtype), mesh=mesh,
        scratch_types=[pltpu.VMEM((n_chunks * K, sub), I32), pltpu.VMEM((2, sub, W), rows.dtype),
                       pltpu.SemaphoreType.DMA((2,)), pltpu.SemaphoreType.DMA((2,))])
    def scatter_rows(x_hbm, i_hbm, o_hbm, idx_v, buf, rsem, wsem):
        wid = lax.axis_index("subcore") * n_cores + lax.axis_index("core")
        base = wid * per_w
        pltpu.sync_copy(i_hbm.at[wid], idx_v)

        def read(c, slot):
            return pltpu.make_async_copy(x_hbm.at[pl.ds(base + c * sub, sub)], buf.at[slot], rsem.at[slot])

        def write(c, kk, slot):
            return pltpu.make_async_copy(buf.at[slot], o_hbm.at[idx_v.at[c * K + kk]], wsem.at[slot])

        read(0, 0).start()

        @pl.loop(0, n_chunks, step=2)
        def _(c0):
            for b in range(2):
                c = c0 + b
                read(c, b).wait()
                for kk in range(K):
                    write(c, kk, b).start()

                @pl.when(c + 1 < n_chunks)
                def _():
                    @pl.when(c >= 1)
                    def _():
                        for kk in range(K):
                            write(c - 1, kk, 1 - b).wait()
                    read(c + 1, 1 - b).start()

        for kk in range(K):
            write(n_chunks - 2, kk, 0).wait()
            write(n_chunks - 1, kk, 1).wait()

    return scatter_rows(rows, idx)


def _sc_gather(table, indices):
    M = indices.shape[0]
    W = table.shape[1]
    sub = SC_ROWS_PER_COPY
    n_cores, n_workers = _sc_workers()
    per_w = M // n_workers
    n_steps = per_w // sub
    assert per_w * n_workers == M and n_steps * sub == per_w and n_steps % 2 == 0
    mesh = plsc.VectorSubcoreMesh(core_axis_name="core", subcore_axis_name="subcore")

    @functools.partial(
        pl.kernel, out_type=jax.ShapeDtypeStruct((M, W), table.dtype), mesh=mesh,
        scratch_types=[pltpu.VMEM((n_steps, sub), I32), pltpu.VMEM((2, sub, W), table.dtype),
                       pltpu.SemaphoreType.DMA((2,)), pltpu.SemaphoreType.DMA((2,))])
    def gather_rows(x_hbm, i_hbm, o_hbm, idx_v, buf, gsem, wsem):
        wid = lax.axis_index("subcore") * n_cores + lax.axis_index("core")
        base = wid * per_w
        pltpu.sync_copy(i_hbm.at[wid], idx_v)

        def gather(s, slot):
            return pltpu.make_async_copy(x_hbm.at[idx_v.at[s]], buf.at[slot], gsem.at[slot])

        def write(s, slot):
            return pltpu.make_async_copy(buf.at[slot], o_hbm.at[pl.ds(base + s * sub, sub)], wsem.at[slot])

        gather(0, 0).start()

        @pl.loop(0, n_steps, step=2)
        def _(s0):
            for b in range(2):
                s = s0 + b
                gather(s, b).wait()
                write(s, b).start()

                @pl.when(s + 1 < n_steps)
                def _():
                    @pl.when(s >= 1)
                    def _():
                        write(s - 1, 1 - b).wait()
                    gather(s + 1, 1 - b).start()

        write(n_steps - 2, 0).wait()
        write(n_steps - 1, 1).wait()

    return gather_rows(table, indices.reshape(n_workers, n_steps, sub))


def _expert_kernel(vb_ref, ve_ref, nvis_ref, nxt_ref, par_ref,
                   x_ref, wgu_hbm, bgu_ref, wd_hbm, bd_ref, y_ref,
                   wgu_stage, wd_stage, wgu_bf, wd_bf, sem):
    v = pl.program_id(0)
    real = v < nvis_ref[0]
    e = ve_ref[v]
    first_of_expert = jnp.logical_or(v == 0, e != ve_ref[jnp.maximum(v - 1, 0)])
    slot = par_ref[e]

    def weight_copies(expert, s):
        return (pltpu.make_async_copy(wgu_hbm.at[expert], wgu_stage.at[s], sem.at[0, s]),
                pltpu.make_async_copy(wd_hbm.at[expert], wd_stage.at[s], sem.at[1, s]))

    @pl.when(v == 0)
    def _():
        for c in weight_copies(e, slot):
            c.start()

    @pl.when(jnp.logical_and(first_of_expert, real))
    def _():
        for c in weight_copies(e, slot):
            c.wait()
        nxt = nxt_ref[e]

        @pl.when(nxt >= 0)
        def _():
            for c in weight_copies(nxt, 1 - slot):
                c.start()

        wgu_bf[...] = wgu_stage[slot].astype(BF16)
        wd_bf[...] = wd_stage[slot].astype(BF16)

    @pl.when(real)
    def _():
        half = wgu_bf.shape[0] // 2
        x_lo, x_hi = _unpack_rows(x_ref[...])
        gu = (jnp.dot(x_lo.astype(BF16), wgu_bf[:half, :], preferred_element_type=F32)
              + jnp.dot(x_hi.astype(BF16), wgu_bf[half:, :], preferred_element_type=F32)) + bgu_ref[0]
        gate = jnp.minimum(gu[:, :D_EXPERT], SWIGLU_LIMIT)
        up = jnp.clip(gu[:, D_EXPERT:], -SWIGLU_LIMIT, SWIGLU_LIMIT)
        act = gate * jax.nn.sigmoid(SWIGLU_ALPHA * gate) * (up + 1.0)
        y_ref[...] = _pack_rows(jnp.dot(act.astype(BF16), wd_bf[...], preferred_element_type=F32) + bd_ref[0])


def _experts(x_sorted, visits, w_gate_up, b_gate_up, w_down, b_down):
    n_rows, W = x_sorted.shape
    D = 2 * W
    tm = EXPERT_TILE
    n_visits = visits[0].shape[0]
    n_prefetch = len(visits)
    by_block = lambda v, vb, *_: (vb[v], 0)
    by_expert = lambda v, vb, ve, *_: (ve[v], 0, 0)
    grid_spec = pltpu.PrefetchScalarGridSpec(
        num_scalar_prefetch=n_prefetch,
        grid=(n_visits,),
        in_specs=[
            pl.BlockSpec((tm, W), by_block),
            pl.BlockSpec(memory_space=pl.ANY),
            pl.BlockSpec((1, 1, 2 * D_EXPERT), by_expert),
            pl.BlockSpec(memory_space=pl.ANY),
            pl.BlockSpec((1, 1, D), by_expert),
        ],
        out_specs=pl.BlockSpec((tm, W), by_block),
        scratch_shapes=[pltpu.VMEM((2, D, 2 * D_EXPERT), F32), pltpu.VMEM((2, D_EXPERT, D), F32),
                        pltpu.VMEM((D, 2 * D_EXPERT), BF16), pltpu.VMEM((D_EXPERT, D), BF16),
                        pltpu.SemaphoreType.DMA((2, 2))],
    )
    return pl.pallas_call(
        _expert_kernel,
        grid_spec=grid_spec,
        out_shape=jax.ShapeDtypeStruct((n_rows, W), U32),
        compiler_params=pltpu.CompilerParams(
            dimension_semantics=("arbitrary",), vmem_limit_bytes=VMEM_LIMIT),
        name="expert_gmm",
    )(*visits, x_sorted, w_gate_up, b_gate_up.reshape(N_EXPERTS, 1, -1), w_down, b_down.reshape(N_EXPERTS, 1, -1))


def _visit_schedule(counts, n_assign, capacity):
    tm = EXPERT_TILE
    n_visits = n_assign // tm + N_EXPERTS
    per_e = (counts + tm - 1) // tm
    vend = jnp.cumsum(per_e)
    nvis = vend[-1]
    vc = jnp.minimum(jnp.arange(n_visits, dtype=I32), nvis - 1)
    done = vc[:, None] >= vend[None, :]
    e = jnp.sum(done, axis=1).astype(I32)
    blk = e * (capacity // tm) + vc - jnp.sum(jnp.where(done, per_e[None, :], 0), axis=1)
    ids = jnp.arange(N_EXPERTS, dtype=I32)
    nonempty = counts > 0
    later = jnp.logical_and(nonempty[None, :], ids[None, :] > ids[:, None])
    nxt = jnp.where(jnp.any(later, axis=1), jnp.argmax(later, axis=1), -1).astype(I32)
    parity = ((jnp.cumsum(nonempty.astype(I32)) - nonempty.astype(I32)) % 2).astype(I32)
    return blk.astype(I32), e, nvis.reshape(1).astype(I32), nxt, parity


def _combine_kernel(y0_ref, y1_ref, y2_ref, y3_ref, x1_ref, gate_ref, g_ref, o_ref):
    half = x1_ref.shape[1] // 2
    gates = gate_ref[...]
    lo = x1_ref[:, :half]
    hi = x1_ref[:, half:]
    for kk, y_ref in enumerate((y0_ref, y1_ref, y2_ref, y3_ref)):
        y_lo, y_hi = _unpack_rows(y_ref[...])
        lo = lo + gates[:, kk:kk + 1] * y_lo
        hi = hi + gates[:, kk:kk + 1] * y_hi
    ms = (jnp.sum(lo * lo, axis=-1, keepdims=True) + jnp.sum(hi * hi, axis=-1, keepdims=True)) / (2 * half)
    scale = lax.rsqrt(ms + RMS_EPS)
    o_ref[:, :half] = lo * scale * g_ref[:, :half]
    o_ref[:, half:] = hi * scale * g_ref[:, half:]


def _combine(y_tok, x1, gates_t, g_final):
    T, D = x1.shape
    tm = COMBINE_TILE
    nt = T // tm
    y_spec = lambda kk: pl.BlockSpec((tm, D // 2), lambda i: (kk * nt + i, 0))
    return pl.pallas_call(
        _combine_kernel,
        grid=(nt,),
        in_specs=[y_spec(0), y_spec(1), y_spec(2), y_spec(3),
                  pl.BlockSpec((tm, D), lambda i: (i, 0)),
                  pl.BlockSpec((tm, TOP_K), lambda i: (i, 0)),
                  pl.BlockSpec((1, D), lambda i: (0, 0))],
        out_specs=pl.BlockSpec((tm, D), lambda i: (i, 0)),
        out_shape=jax.ShapeDtypeStruct((T, D), F32),
        compiler_params=pltpu.CompilerParams(
            dimension_semantics=("arbitrary",), vmem_limit_bytes=VMEM_LIMIT),
        name="combine_norm",
    )(y_tok, y_tok, y_tok, y_tok, x1, gates_t, g_final.reshape(1, D))


def kernel(x, g_mix, w_in, pool_w, pool_scale, w_out, g_moe, w_router, b_router, w_gate_up, b_gate_up,
           w_down, b_down, g_final):
    B, S, D = x.shape
    T = B * S
    assert g_mix.shape[0] == 1, "single-layer problem: the final norm is fused into the combine step"
    q, k, v, pool = _in_proj(x, g_mix[0], w_in[0], pool_w[0], pool_scale[0])
    attn = _attention(q, k, v)
    capacity = T
    x1, h2, gates, dest, cnt = _out_proj_router(
        attn.reshape(T, SB_WIDTH), pool.reshape(T, POOL_WIDTH), x.reshape(T, D),
        w_out[0], g_moe[0], w_router[0], b_router[0], capacity)
    x_sorted = _sc_dispatch(h2, dest, N_EXPERTS * capacity)
    visits = _visit_schedule(cnt[:, 0].astype(I32), T * TOP_K, capacity)
    y = _experts(x_sorted, visits, w_gate_up[0], b_gate_up[0], w_down[0], b_down[0])
    y_tok = _sc_gather(y, dest.reshape(-1))
    return _combine(y_tok, x1, gates.T, g_final).reshape(B, S, D)
```

```python
import functools

import jax
import jax.numpy as jnp
from jax import lax
from jax.experimental import pallas as pl
from jax.experimental.pallas import tpu as pltpu
from jax.experimental.pallas import tpu_sc as plsc

F32 = jnp.float32
BF16 = jnp.bfloat16
I32 = jnp.int32
U32 = jnp.uint32

D_MODEL = 1024
SB_HEADS = 8
SB_HEAD_DIM = 64
SB_WIDTH = SB_HEADS * SB_HEAD_DIM
POOL_WINDOWS = (2, 4, 8, 16)
POOL_WIDTH = 512
POOL_GROUP_DIM = 128
N_EXPERTS = 32
TOP_K = 4
D_EXPERT = 1024
SWIGLU_LIMIT = 7.0
SWIGLU_ALPHA = 1.702
RMS_EPS = 1e-5

LANES = 128
HALO = 32
PROJ_TILE = 512
ATTN_TILE = 256
ATTN_SUBTILES = 4
EXPERT_TILE = 512
COMBINE_TILE = 256
ATTN_SKIP_LOG2 = -160.0
LOG2_E = 1.4426950408889634
VMEM_LIMIT = 56 * 1024 * 1024

_NT = (((1,), (1,)), ((), ()))


def _rms(x, g):
    ms = jnp.mean(x * x, axis=-1, keepdims=True)
    return x * lax.rsqrt(ms + RMS_EPS) * g


def _pack_rows(x):
    n = x.shape[1] // 2
    lo = lax.bitcast_convert_type(x[:, :n].astype(BF16).astype(F32), U32)
    hi = lax.bitcast_convert_type(x[:, n:].astype(BF16).astype(F32), U32)
    return (lo >> 16) | (hi & jnp.uint32(0xFFFF0000))


def _unpack_rows(w):
    lo = lax.bitcast_convert_type(w << 16, F32)
    hi = lax.bitcast_convert_type(w & jnp.uint32(0xFFFF0000), F32)
    return lo, hi


def _in_proj_kernel(x_ref, w_ref, pw_ref, ps_ref, q_ref, k_ref, v_ref, p_ref, uext_ref, xb_ref, lvl_ref):
    s = pl.program_id(1)
    tm = x_ref.shape[1]
    x = x_ref[0]
    inv = lax.rsqrt(jnp.mean(x * x, axis=-1, keepdims=True) + RMS_EPS)
    xb_ref[...] = x.astype(BF16)
    proj = jnp.dot(xb_ref[...], w_ref[...], preferred_element_type=F32)
    q_ref[0] = (proj[:, 0:SB_WIDTH] * (inv * (LOG2_E * SB_HEAD_DIM ** -0.5))).astype(BF16)
    k_ref[0] = (proj[:, SB_WIDTH:2 * SB_WIDTH] * inv).astype(BF16)
    v_ref[0] = (proj[:, 2 * SB_WIDTH:3 * SB_WIDTH] * inv).astype(BF16)
    u = proj[:, 3 * SB_WIDTH:] * inv

    lo = HALO // 2

    @pl.when(s == 0)
    def _():
        uext_ref[0:HALO, :] = jnp.zeros((HALO, POOL_WIDTH), F32)
        lvl_ref[:, 0:lo, :] = jnp.zeros((2, lo, POOL_GROUP_DIM), F32)

    uext_ref[HALO:, :] = u
    t = s * tm + lax.broadcasted_iota(I32, (tm, 1), 0)
    for g, w in enumerate(POOL_WINDOWS):
        sl = slice(g * POOL_GROUP_DIM, (g + 1) * POOL_GROUP_DIM)
        lvl_ref[0, lo:, :] = uext_ref[lo:, sl] + uext_ref[lo - 1:HALO + tm - 1, sl]
        cur, k = 0, 2
        while k < w:
            lvl_ref[1 - cur, lo:, :] = lvl_ref[cur, lo:, :] + lvl_ref[cur, lo - k:HALO + tm - k, :]
            cur, k = 1 - cur, 2 * k
        ug = u[:, sl]
        count = jnp.minimum(t + 1, w).astype(F32)
        pooled = lvl_ref[cur, HALO:, :] / count - ug
        mixed = jnp.dot(pooled.astype(BF16), pw_ref[g], preferred_element_type=F32)
        p_ref[0, :, sl] = (mixed * ps_ref[:, sl]).astype(BF16)
    uext_ref[0:HALO, :] = u[tm - HALO:, :]


def _in_proj(x, g_mix, w_in, pool_w, pool_scale):
    B, S, D = x.shape
    tm = PROJ_TILE
    n_out = w_in.shape[1]
    out_sd = jax.ShapeDtypeStruct((B, S, SB_WIDTH), BF16)
    blk = pl.BlockSpec((1, tm, SB_WIDTH), lambda b, s: (b, s, 0))
    return pl.pallas_call(
        _in_proj_kernel,
        grid=(B, S // tm),
        in_specs=[
            pl.BlockSpec((1, tm, D), lambda b, s: (b, s, 0)),
            pl.BlockSpec((D, n_out), lambda b, s: (0, 0)),
            pl.BlockSpec(pool_w.shape, lambda b, s: (0, 0, 0)),
            pl.BlockSpec((1, POOL_WIDTH), lambda b, s: (0, 0)),
        ],
        out_specs=[blk, blk, blk, blk],
        out_shape=[out_sd, out_sd, out_sd, out_sd],
        scratch_shapes=[pltpu.VMEM((HALO + tm, POOL_WIDTH), F32), pltpu.VMEM((tm, D), BF16),
                        pltpu.VMEM((2, HALO + tm, POOL_GROUP_DIM), F32)],
        compiler_params=pltpu.CompilerParams(
            dimension_semantics=("arbitrary", "arbitrary"), vmem_limit_bytes=VMEM_LIMIT),
        name="in_proj_pool",
    )(x, (g_mix[:, None] * w_in).astype(BF16), pool_w.astype(BF16), pool_scale.reshape(1, POOL_WIDTH))


def _attn_kernel(q_ref, k_ref, v_ref, o_ref):
    tq = ATTN_TILE
    tk = tq
    qi = pl.program_id(2)
    lane = lax.broadcasted_iota(I32, (tq, LANES), 1)
    row = lax.broadcasted_iota(I32, (tq, tk), 0)
    col = lax.broadcasted_iota(I32, (tq, tk), 1)
    causal = col < row
    tri = (row > col).astype(BF16)

    def head_rows(r, h):
        rows = q_ref[0, r * tq:(r + 1) * tq, :]
        return jnp.where((lane >= h * SB_HEAD_DIM) & (lane < (h + 1) * SB_HEAD_DIM), rows, jnp.zeros_like(rows))

    def scores(qh, j):
        kb = k_ref[0, pl.ds(pl.multiple_of(j * tk, tk), tk), :]
        return lax.dot_general(qh, kb, _NT, preferred_element_type=F32)

    def values(j):
        return v_ref[0, pl.ds(pl.multiple_of(j * tk, tk), tk), :]

    def neg_log(z, mask):
        neg_abs = lax.bitcast_convert_type(lax.bitcast_convert_type(z, U32) | jnp.uint32(0x80000000), F32)
        nl = jnp.maximum(z, 0.0) + jnp.log2(1.0 + jnp.exp2(neg_abs))
        return jnp.where(causal, nl, 0.0) if mask else nl

    def tile(qh, j, mask):
        z = scores(qh, j)
        nl = neg_log(z, mask)
        ntail = jnp.dot(nl.astype(BF16), tri, preferred_element_type=F32)
        return (z - nl) - ntail, -(ntail[:, 0:1] + nl[:, 0:1]), values(j)

    nq = ATTN_SUBTILES
    has_prev = qi > 0
    qs = [head_rows(r, h) for r in range(nq) for h in range(2)]
    chains = []
    for n, qh in enumerate(qs):
        diag = nq * qi + n // 2
        chains += [(qh, diag, True), (qh, jnp.maximum(diag - 1, 0), False)]
    zs = [scores(qh, j) for qh, j, _ in chains]
    nls = [neg_log(z, is_diag) for z, (_, _, is_diag) in zip(zs, chains)]
    nts = [jnp.dot(nl.astype(BF16), tri, preferred_element_type=F32) for nl in nls]
    tots = [-(nt[:, 0:1] + nl[:, 0:1]) for nt, nl in zip(nts, nls)]
    ws = []
    for n in range(2 * nq):
        d, p = 2 * n, 2 * n + 1
        w_d = jnp.where(causal, jnp.exp2((zs[d] - nls[d]) - nts[d]), 0.0)
        w_p = jnp.exp2((zs[p] - nls[p]) - nts[p] + tots[d])
        if n < 2:
            w_p = jnp.where(has_prev, w_p, 0.0)
        ws += [w_d.astype(BF16), w_p.astype(BF16)]
    pvs = [jnp.dot(w, values(j), preferred_element_type=F32) for w, (_, j, _) in zip(ws, chains)]
    accs = [pvs[2 * n] + pvs[2 * n + 1] for n in range(2 * nq)]
    carries = [tots[2 * n] + tots[2 * n + 1] for n in range(2 * nq)]

    def cond(st):
        j, cs, _ = st
        live = functools.reduce(jnp.maximum, [jnp.max(c) for c in cs])
        return jnp.logical_and(j >= 0, live > ATTN_SKIP_LOG2)

    def body(st):
        j, cs, acs = st
        new_c, new_a = [], []
        for n, (qh, c, a) in enumerate(zip(qs, cs, acs)):
            lag = nq - 1 - n // 2
            if lag:
                c = jnp.where(j >= lag, c, -jnp.inf)
            lw, tot, vb = tile(qh, jnp.maximum(j - lag, 0), False)
            new_a.append(a + jnp.dot(jnp.exp2(lw + c).astype(BF16), vb, preferred_element_type=F32))
            new_c.append(c + tot)
        return j - 1, tuple(new_c), tuple(new_a)

    _, _, accs = lax.while_loop(cond, body, (nq * qi + nq - 3, tuple(carries), tuple(accs)))
    for r in range(nq):
        o_ref[0, r * tq:(r + 1) * tq, :] = jnp.where(
            lane < SB_HEAD_DIM, accs[2 * r], accs[2 * r + 1]).astype(o_ref.dtype)


def _attention(q, k, v):
    B, S, W = q.shape
    tq = ATTN_SUBTILES * ATTN_TILE
    n_pairs = W // LANES
    return pl.pallas_call(
        _attn_kernel,
        grid=(B, n_pairs, S // tq),
        in_specs=[
            pl.BlockSpec((1, tq, LANES), lambda b, p, i: (b, i, p)),
            pl.BlockSpec((1, S, LANES), lambda b, p, i: (b, 0, p)),
            pl.BlockSpec((1, S, LANES), lambda b, p, i: (b, 0, p)),
        ],
        out_specs=pl.BlockSpec((1, tq, LANES), lambda b, p, i: (b, i, p)),
        out_shape=jax.ShapeDtypeStruct((B, S, W), BF16),
        compiler_params=pltpu.CompilerParams(
            dimension_semantics=("arbitrary", "arbitrary", "arbitrary"), vmem_limit_bytes=VMEM_LIMIT),
        name="stickbreak_attn",
    )(q, k, v)


def _out_proj_router_kernel(attn_ref, pool_ref, x_ref, wo_ref, g_ref, wr_ref, br_ref, tri_ref,
                            x1_ref, h2_ref, gate_ref, dest_ref, cnt_ref, carry_ref, *, capacity):
    i = pl.program_id(0)
    tm = x_ref.shape[0]

    @pl.when(i == 0)
    def _():
        carry_ref[...] = jnp.zeros_like(carry_ref)

    mixed = jnp.dot(attn_ref[...], wo_ref[0:SB_WIDTH, :], preferred_element_type=F32)
    mixed = mixed + jnp.dot(pool_ref[...], wo_ref[SB_WIDTH:, :], preferred_element_type=F32)
    x1 = x_ref[...] + mixed
    h2 = _rms(x1, g_ref[...])
    x1_ref[...] = x1
    h2_ref[...] = _pack_rows(h2)

    hh = h2.astype(BF16)
    hl = (h2 - hh.astype(F32)).astype(BF16)
    wr = wr_ref[...]
    wh = wr.astype(BF16)
    wl = (wr - wh.astype(F32)).astype(BF16)
    logits = (lax.dot_general(wh, hh, _NT, preferred_element_type=F32)
              + lax.dot_general(wh, hl, _NT, preferred_element_type=F32)
              + lax.dot_general(wl, hh, _NT, preferred_element_type=F32)) + br_ref[...]

    eid = lax.broadcasted_iota(I32, (N_EXPERTS, tm), 0).astype(F32)
    work = logits
    vals, ids = [], []
    for _ in range(TOP_K):
        m = jnp.max(work, axis=0, keepdims=True)
        sel = jnp.min(jnp.where(work == m, eid, float(N_EXPERTS)), axis=0, keepdims=True)
        vals.append(m)
        ids.append(sel)
        work = jnp.where(eid == sel, -jnp.inf, work)
    exps = [jnp.exp(v - vals[0]) for v in vals]
    denom = exps[0] + exps[1] + exps[2] + exps[3]

    onehot = jnp.zeros((N_EXPERTS, tm), F32)
    for sel in ids:
        onehot = onehot + (eid == sel).astype(F32)
    before = jnp.dot(onehot.astype(BF16), tri_ref[...], preferred_element_type=F32) + carry_ref[...]
    for kk in range(TOP_K):
        gate_ref[kk:kk + 1, :] = exps[kk] / denom
        rk = jnp.sum(jnp.where(eid == ids[kk], before, 0.0), axis=0, keepdims=True)
        dest_ref[kk:kk + 1, :] = (ids[kk] * float(capacity) + rk).astype(I32)
    carry_ref[...] = carry_ref[...] + jnp.sum(onehot, axis=1, keepdims=True)
    cnt_ref[...] = jnp.broadcast_to(carry_ref[...], cnt_ref.shape)


def _out_proj_router(attn, pool, x, w_out, g_moe, w_router, b_router, capacity):
    T, D = x.shape
    tm = PROJ_TILE
    r = lax.broadcasted_iota(I32, (tm, tm), 0)
    c = lax.broadcasted_iota(I32, (tm, tm), 1)
    tri = (r < c).astype(BF16)
    row_blk = lambda w: pl.BlockSpec((tm, w), lambda i: (i, 0))
    fixed = lambda shape: pl.BlockSpec(shape, lambda i: tuple(0 for _ in shape))
    sel_blk = pl.BlockSpec((TOP_K, tm), lambda i: (0, i))
    assert N_EXPERTS * capacity < 2 ** 24
    return pl.pallas_call(
        functools.partial(_out_proj_router_kernel, capacity=capacity),
        grid=(T // tm,),
        in_specs=[row_blk(SB_WIDTH), row_blk(POOL_WIDTH), row_blk(D), fixed((D, D)), fixed((1, D)),
                  fixed((N_EXPERTS, D)), fixed((N_EXPERTS, 1)), fixed((tm, tm))],
        out_specs=[row_blk(D), row_blk(D // 2), sel_blk, sel_blk, fixed((N_EXPERTS, LANES))],
        out_shape=[jax.ShapeDtypeStruct((T, D), F32), jax.ShapeDtypeStruct((T, D // 2), U32),
                   jax.ShapeDtypeStruct((TOP_K, T), F32), jax.ShapeDtypeStruct((TOP_K, T), I32),
                   jax.ShapeDtypeStruct((N_EXPERTS, LANES), F32)],
        scratch_shapes=[pltpu.VMEM((N_EXPERTS, 1), F32)],
        compiler_params=pltpu.CompilerParams(
            dimension_semantics=("arbitrary",), vmem_limit_bytes=VMEM_LIMIT),
        name="out_proj_router",
    )(attn, pool, x, w_out.astype(BF16), g_moe.reshape(1, D), w_router.T, b_router.reshape(N_EXPERTS, 1), tri)


SC_ROWS_PER_COPY = 64


def _sc_workers():
    info = plsc.get_sparse_core_info()
    return info.num_cores, info.num_cores * info.num_subcores


def _sc_dispatch(rows, dest, n_out):
    T, W = rows.shape
    K = dest.shape[0]
    sub = SC_ROWS_PER_COPY
    n_cores, n_workers = _sc_workers()
    per_w = T // n_workers
    n_chunks = per_w // sub
    assert per_w * n_workers == T and n_chunks * sub == per_w and n_chunks % 2 == 0
    idx = dest.reshape(K, n_workers, n_chunks, sub).transpose(1, 2, 0, 3).reshape(n_workers, n_chunks * K, sub)
    mesh = plsc.VectorSubcoreMesh(core_axis_name="core", subcore_axis_name="subcore")

    @functools.partial(
        pl.kernel, out_type=jax.ShapeDtypeStruct((n_out, W), rows.dtype), mesh=mesh,
        scratch_types=[pltpu.VMEM((n_chunks * K, sub), I32), pltpu.VMEM((2, sub, W), rows.dtype),
                       pltpu.SemaphoreType.DMA((2,)), pltpu.SemaphoreType.DMA((2,))])
    def scatter_rows(x_hbm, i_hbm, o_hbm, idx_v, buf, rsem, wsem):
        wid = lax.axis_index("subcore") * n_cores + lax.axis_index("core")
        base = wid * per_w
        pltpu.sync_copy(i_hbm.at[wid], idx_v)

        def read(c, slot):
            return pltpu.make_async_copy(x_hbm.at[pl.ds(base + c * sub, sub)], buf.at[slot], rsem.at[slot])

        def write(c, kk, slot):
            return pltpu.make_async_copy(buf.at[slot], o_hbm.at[idx_v.at[c * K + kk]], wsem.at[slot])

        read(0, 0).start()

        @pl.loop(0, n_chunks, step=2)
        def _(c0):
            for b in range(2):
                c = c0 + b
                read(c, b).wait()
                for kk in range(K):
                    write(c, kk, b).start()

                @pl.when(c + 1 < n_chunks)
                def _():
                    @pl.when(c >= 1)
                    def _():
                        for kk in range(K):
                            write(c - 1, kk, 1 - b).wait()
                    read(c + 1, 1 - b).start()

        for kk in range(K):
            write(n_chunks - 2, kk, 0).wait()
            write(n_chunks - 1, kk, 1).wait()

    return scatter_rows(rows, idx)


def _sc_gather(table, indices):
    M = indices.shape[0]
    W = table.shape[1]
    sub = SC_ROWS_PER_COPY
    n_cores, n_workers = _sc_workers()
    per_w = M // n_workers
    n_steps = per_w // sub
    assert per_w * n_workers == M and n_steps * sub == per_w and n_steps % 2 == 0
    mesh = plsc.VectorSubcoreMesh(core_axis_name="core", subcore_axis_name="subcore")

    @functools.partial(
        pl.kernel, out_type=jax.ShapeDtypeStruct((M, W), table.dtype), mesh=mesh,
        scratch_types=[pltpu.VMEM((n_steps, sub), I32), pltpu.VMEM((2, sub, W), table.dtype),
                       pltpu.SemaphoreType.DMA((2,)), pltpu.SemaphoreType.DMA((2,))])
    def gather_rows(x_hbm, i_hbm, o_hbm, idx_v, buf, gsem, wsem):
        wid = lax.axis_index("subcore") * n_cores + lax.axis_index("core")
        base = wid * per_w
        pltpu.sync_copy(i_hbm.at[wid], idx_v)

        def gather(s, slot):
            return pltpu.make_async_copy(x_hbm.at[idx_v.at[s]], buf.at[slot], gsem.at[slot])

        def write(s, slot):
            return pltpu.make_async_copy(buf.at[slot], o_hbm.at[pl.ds(base + s * sub, sub)], wsem.at[slot])

        gather(0, 0).start()

        @pl.loop(0, n_steps, step=2)
        def _(s0):
            for b in range(2):
                s = s0 + b
                gather(s, b).wait()
                write(s, b).start()

                @pl.when(s + 1 < n_steps)
                def _():
                    @pl.when(s >= 1)
                    def _():
                        write(s - 1, 1 - b).wait()
                    gather(s + 1, 1 - b).start()

        write(n_steps - 2, 0).wait()
        write(n_steps - 1, 1).wait()

    return gather_rows(table, indices.reshape(n_workers, n_steps, sub))


def _expert_kernel(vb_ref, ve_ref, nvis_ref, nxt_ref, par_ref,
                   x_ref, wgu_hbm, bgu_ref, wd_hbm, bd_ref, y_ref,
                   wgu_stage, wd_stage, wgu_bf, wd_bf, sem):
    v = pl.program_id(0)
    real = v < nvis_ref[0]
    e = ve_ref[v]
    first_of_expert = jnp.logical_or(v == 0, e != ve_ref[jnp.maximum(v - 1, 0)])
    slot = par_ref[e]

    def weight_copies(expert, s):
        return (pltpu.make_async_copy(wgu_hbm.at[expert], wgu_stage.at[s], sem.at[0, s]),
                pltpu.make_async_copy(wd_hbm.at[expert], wd_stage.at[s], sem.at[1, s]))

    @pl.when(v == 0)
    def _():
        for c in weight_copies(e, slot):
            c.start()

    @pl.when(jnp.logical_and(first_of_expert, real))
    def _():
        for c in weight_copies(e, slot):
            c.wait()
        nxt = nxt_ref[e]

        @pl.when(nxt >= 0)
        def _():
            for c in weight_copies(nxt, 1 - slot):
                c.start()

        wgu_bf[...] = wgu_stage[slot].astype(BF16)
        wd_bf[...] = wd_stage[slot].astype(BF16)

    @pl.when(real)
    def _():
        half = wgu_bf.shape[0] // 2
        x_lo, x_hi = _unpack_rows(x_ref[...])
        gu = (jnp.dot(x_lo.astype(BF16), wgu_bf[:half, :], preferred_element_type=F32)
              + jnp.dot(x_hi.astype(BF16), wgu_bf[half:, :], preferred_element_type=F32)) + bgu_ref[0]
        gate = jnp.minimum(gu[:, :D_EXPERT], SWIGLU_LIMIT)
        up = jnp.clip(gu[:, D_EXPERT:], -SWIGLU_LIMIT, SWIGLU_LIMIT)
        act = gate * jax.nn.sigmoid(SWIGLU_ALPHA * gate) * (up + 1.0)
        y_ref[...] = _pack_rows(jnp.dot(act.astype(BF16), wd_bf[...], preferred_element_type=F32) + bd_ref[0])


def _experts(x_sorted, visits, w_gate_up, b_gate_up, w_down, b_down):
    n_rows, W = x_sorted.shape
    D = 2 * W
    tm = EXPERT_TILE
    n_visits = visits[0].shape[0]
    n_prefetch = len(visits)
    by_block = lambda v, vb, *_: (vb[v], 0)
    by_expert = lambda v, vb, ve, *_: (ve[v], 0, 0)
    grid_spec = pltpu.PrefetchScalarGridSpec(
        num_scalar_prefetch=n_prefetch,
        grid=(n_visits,),
        in_specs=[
            pl.BlockSpec((tm, W), by_block),
            pl.BlockSpec(memory_space=pl.ANY),
            pl.BlockSpec((1, 1, 2 * D_EXPERT), by_expert),
            pl.BlockSpec(memory_space=pl.ANY),
            pl.BlockSpec((1, 1, D), by_expert),
        ],
        out_specs=pl.BlockSpec((tm, W), by_block),
        scratch_shapes=[pltpu.VMEM((2, D, 2 * D_EXPERT), F32), pltpu.VMEM((2, D_EXPERT, D), F32),
                        pltpu.VMEM((D, 2 * D_EXPERT), BF16), pltpu.VMEM((D_EXPERT, D), BF16),
                        pltpu.SemaphoreType.DMA((2, 2))],
    )
    return pl.pallas_call(
        _expert_kernel,
        grid_spec=grid_spec,
        out_shape=jax.ShapeDtypeStruct((n_rows, W), U32),
        compiler_params=pltpu.CompilerParams(
            dimension_semantics=("arbitrary",), vmem_limit_bytes=VMEM_LIMIT),
        name="expert_gmm",
    )(*visits, x_sorted, w_gate_up, b_gate_up.reshape(N_EXPERTS, 1, -1), w_down, b_down.reshape(N_EXPERTS, 1, -1))


def _visit_schedule(counts, n_assign, capacity):
    tm = EXPERT_TILE
    n_visits = n_assign // tm + N_EXPERTS
    per_e = (counts + tm - 1) // tm
    vend = jnp.cumsum(per_e)
    nvis = vend[-1]
    vc = jnp.minimum(jnp.arange(n_visits, dtype=I32), nvis - 1)
    done = vc[:, None] >= vend[None, :]
    e = jnp.sum(done, axis=1).astype(I32)
    blk = e * (capacity // tm) + vc - jnp.sum(jnp.where(done, per_e[None, :], 0), axis=1)
    ids = jnp.arange(N_EXPERTS, dtype=I32)
    nonempty = counts > 0
    later = jnp.logical_and(nonempty[None, :], ids[None, :] > ids[:, None])
    nxt = jnp.where(jnp.any(later, axis=1), jnp.argmax(later, axis=1), -1).astype(I32)
    parity = ((jnp.cumsum(nonempty.astype(I32)) - nonempty.astype(I32)) % 2).astype(I32)
    return blk.astype(I32), e, nvis.reshape(1).astype(I32), nxt, parity


def _combine_kernel(y0_ref, y1_ref, y2_ref, y3_ref, x1_ref, gate_ref, g_ref, o_ref):
    half = x1_ref.shape[1] // 2
    gates = gate_ref[...]
    lo = x1_ref[:, :half]
    hi = x1_ref[:, half:]
    for kk, y_ref in enumerate((y0_ref, y1_ref, y2_ref, y3_ref)):
        y_lo, y_hi = _unpack_rows(y_ref[...])
        lo = lo + gates[:, kk:kk + 1] * y_lo
        hi = hi + gates[:, kk:kk + 1] * y_hi
    ms = (jnp.sum(lo * lo, axis=-1, keepdims=True) + jnp.sum(hi * hi, axis=-1, keepdims=True)) / (2 * half)
    scale = lax.rsqrt(ms + RMS_EPS)
    o_ref[:, :half] = lo * scale * g_ref[:, :half]
    o_ref[:, half:] = hi * scale * g_ref[:, half:]


def _combine(y_tok, x1, gates_t, g_final):
    T, D = x1.shape
    tm = COMBINE_TILE
    nt = T // tm
    y_spec = lambda kk: pl.BlockSpec((tm, D // 2), lambda i: (kk * nt + i, 0))
    return pl.pallas_call(
        _combine_kernel,
        grid=(nt,),
        in_specs=[y_spec(0), y_spec(1), y_spec(2), y_spec(3),
                  pl.BlockSpec((tm, D), lambda i: (i, 0)),
                  pl.BlockSpec((tm, TOP_K), lambda i: (i, 0)),
                  pl.BlockSpec((1, D), lambda i: (0, 0))],
        out_specs=pl.BlockSpec((tm, D), lambda i: (i, 0)),
        out_shape=jax.ShapeDtypeStruct((T, D), F32),
        compiler_params=pltpu.CompilerParams(
            dimension_semantics=("arbitrary",), vmem_limit_bytes=VMEM_LIMIT),
        name="combine_norm",
    )(y_tok, y_tok, y_tok, y_tok, x1, gates_t, g_final.reshape(1, D))


def kernel(x, g_mix, w_in, pool_w, pool_scale, w_out, g_moe, w_router, b_router, w_gate_up, b_gate_up,
           w_down, b_down, g_final):
    B, S, D = x.shape
    T = B * S
    assert g_mix.shape[0] == 1, "single-layer problem: the final norm is fused into the combine step"
    q, k, v, pool = _in_proj(x, g_mix[0], w_in[0], pool_w[0], pool_scale[0])
    attn = _attention(q, k, v)
    capacity = T
    x1, h2, gates, dest, cnt = _out_proj_router(
        attn.reshape(T, SB_WIDTH), pool.reshape(T, POOL_WIDTH), x.reshape(T, D),
        w_out[0], g_moe[0], w_router[0], b_router[0], capacity)
    x_sorted = _sc_dispatch(h2, dest, N_EXPERTS * capacity)
    visits = _visit_schedule(cnt[:, 0].astype(I32), T * TOP_K, capacity)
    y = _experts(x_sorted, visits, w_gate_up[0], b_gate_up[0], w_down[0], b_down[0])
    y_tok = _sc_gather(y, dest.reshape(-1))
    return _combine(y_tok, x1, gates.T, g_final).reshape(B, S, D)
```

```python
import functools

import jax
import jax.numpy as jnp
from jax import lax
from jax.experimental import pallas as pl
from jax.experimental.pallas import tpu as pltpu
from jax.experimental.pallas import tpu_sc as plsc

F32 = jnp.float32
BF16 = jnp.bfloat16
I32 = jnp.int32
U32 = jnp.uint32

D_MODEL = 1024
SB_HEADS = 8
SB_HEAD_DIM = 64
SB_WIDTH = SB_HEADS * SB_HEAD_DIM
POOL_WINDOWS = (2, 4, 8, 16)
POOL_WIDTH = 512
POOL_GROUP_DIM = 128
N_EXPERTS = 32
TOP_K = 4
D_EXPERT = 1024
SWIGLU_LIMIT = 7.0
SWIGLU_ALPHA = 1.702
RMS_EPS = 1e-5

LANES = 128
HALO = 32
PROJ_TILE = 512
ATTN_TILE = 256
ATTN_SUBTILES = 4
EXPERT_TILE = 1024
EXPERT_ROW_CLASSES = (1024, 512, 256)
COMBINE_TILE = 256
ATTN_SKIP_LOG2 = -160.0
LOG2_E = 1.4426950408889634
VMEM_LIMIT = 56 * 1024 * 1024

_NT = (((1,), (1,)), ((), ()))


def _rms(x, g):
    ms = jnp.mean(x * x, axis=-1, keepdims=True)
    return x * lax.rsqrt(ms + RMS_EPS) * g


def _pack_rows(x):
    n = x.shape[1] // 2
    lo = lax.bitcast_convert_type(x[:, :n].astype(BF16).astype(F32), U32)
    hi = lax.bitcast_convert_type(x[:, n:].astype(BF16).astype(F32), U32)
    return (lo >> 16) | (hi & jnp.uint32(0xFFFF0000))


def _unpack_rows(w):
    lo = lax.bitcast_convert_type(w << 16, F32)
    hi = lax.bitcast_convert_type(w & jnp.uint32(0xFFFF0000), F32)
    return lo, hi


def _in_proj_kernel(x_ref, w_ref, pw_ref, ps_ref, q_ref, k_ref, v_ref, p_ref, uext_ref, xb_ref, lvl_ref):
    s = pl.program_id(1)
    tm = x_ref.shape[1]
    x = x_ref[0]
    inv = lax.rsqrt(jnp.mean(x * x, axis=-1, keepdims=True) + RMS_EPS)
    xb_ref[...] = x.astype(BF16)
    proj = jnp.dot(xb_ref[...], w_ref[...], preferred_element_type=F32)
    q_ref[0] = (proj[:, 0:SB_WIDTH] * (inv * (LOG2_E * SB_HEAD_DIM ** -0.5))).astype(BF16)
    k_ref[0] = (proj[:, SB_WIDTH:2 * SB_WIDTH] * inv).astype(BF16)
    v_ref[0] = (proj[:, 2 * SB_WIDTH:3 * SB_WIDTH] * inv).astype(BF16)
    u = proj[:, 3 * SB_WIDTH:] * inv

    lo = HALO // 2

    @pl.when(s == 0)
    def _():
        uext_ref[0:HALO, :] = jnp.zeros((HALO, POOL_WIDTH), F32)
        lvl_ref[:, 0:lo, :] = jnp.zeros((2, lo, POOL_GROUP_DIM), F32)

    uext_ref[HALO:, :] = u
    t = s * tm + lax.broadcasted_iota(I32, (tm, 1), 0)
    for g, w in enumerate(POOL_WINDOWS):
        sl = slice(g * POOL_GROUP_DIM, (g + 1) * POOL_GROUP_DIM)
        lvl_ref[0, lo:, :] = uext_ref[lo:, sl] + uext_ref[lo - 1:HALO + tm - 1, sl]
        cur, k = 0, 2
        while k < w:
            lvl_ref[1 - cur, lo:, :] = lvl_ref[cur, lo:, :] + lvl_ref[cur, lo - k:HALO + tm - k, :]
            cur, k = 1 - cur, 2 * k
        ug = u[:, sl]
        count = jnp.minimum(t + 1, w).astype(F32)
        pooled = lvl_ref[cur, HALO:, :] / count - ug
        mixed = jnp.dot(pooled.astype(BF16), pw_ref[g], preferred_element_type=F32)
        p_ref[0, :, sl] = (mixed * ps_ref[:, sl]).astype(BF16)
    uext_ref[0:HALO, :] = u[tm - HALO:, :]


def _in_proj(x, g_mix, w_in, pool_w, pool_scale):
    B, S, D = x.shape
    tm = PROJ_TILE
    n_out = w_in.shape[1]
    out_sd = jax.ShapeDtypeStruct((B, S, SB_WIDTH), BF16)
    blk = pl.BlockSpec((1, tm, SB_WIDTH), lambda b, s: (b, s, 0))
    return pl.pallas_call(
        _in_proj_kernel,
        grid=(B, S // tm),
        in_specs=[
            pl.BlockSpec((1, tm, D), lambda b, s: (b, s, 0)),
            pl.BlockSpec((D, n_out), lambda b, s: (0, 0)),
            pl.BlockSpec(pool_w.shape, lambda b, s: (0, 0, 0)),
            pl.BlockSpec((1, POOL_WIDTH), lambda b, s: (0, 0)),
        ],
        out_specs=[blk, blk, blk, blk],
        out_shape=[out_sd, out_sd, out_sd, out_sd],
        scratch_shapes=[pltpu.VMEM((HALO + tm, POOL_WIDTH), F32), pltpu.VMEM((tm, D), BF16),
                        pltpu.VMEM((2, HALO + tm, POOL_GROUP_DIM), F32)],
        compiler_params=pltpu.CompilerParams(
            dimension_semantics=("arbitrary", "arbitrary"), vmem_limit_bytes=VMEM_LIMIT),
        name="in_proj_pool",
    )(x, (g_mix[:, None] * w_in).astype(BF16), pool_w.astype(BF16), pool_scale.reshape(1, POOL_WIDTH))


def _attn_kernel(q_ref, k_ref, v_ref, o_ref):
    tq = ATTN_TILE
    tk = tq
    qi = pl.program_id(2)
    lane = lax.broadcasted_iota(I32, (tq, LANES), 1)
    row = lax.broadcasted_iota(I32, (tq, tk), 0)
    col = lax.broadcasted_iota(I32, (tq, tk), 1)
    causal = col < row
    tri = (row > col).astype(BF16)

    def head_rows(r, h):
        rows = q_ref[0, r * tq:(r + 1) * tq, :]
        return jnp.where((lane >= h * SB_HEAD_DIM) & (lane < (h + 1) * SB_HEAD_DIM), rows, jnp.zeros_like(rows))

    def scores(qh, j):
        kb = k_ref[0, pl.ds(pl.multiple_of(j * tk, tk), tk), :]
        return lax.dot_general(qh, kb, _NT, preferred_element_type=F32)

    def values(j):
        return v_ref[0, pl.ds(pl.multiple_of(j * tk, tk), tk), :]

    def neg_log(z, mask):
        neg_abs = lax.bitcast_convert_type(lax.bitcast_convert_type(z, U32) | jnp.uint32(0x80000000), F32)
        nl = jnp.maximum(z, 0.0) + jnp.log2(1.0 + jnp.exp2(neg_abs))
        return jnp.where(causal, nl, 0.0) if mask else nl

    def tile(qh, j, mask):
        z = scores(qh, j)
        nl = neg_log(z, mask)
        ntail = jnp.dot(nl.astype(BF16), tri, preferred_element_type=F32)
        return (z - nl) - ntail, -(ntail[:, 0:1] + nl[:, 0:1]), values(j)

    nq = ATTN_SUBTILES
    has_prev = qi > 0
    qs = [head_rows(r, h) for r in range(nq) for h in range(2)]
    chains = []
    for n, qh in enumerate(qs):
        diag = nq * qi + n // 2
        chains += [(qh, diag, True), (qh, jnp.maximum(diag - 1, 0), False)]
    zs = [scores(qh, j) for qh, j, _ in chains]
    nls = [neg_log(z, is_diag) for z, (_, _, is_diag) in zip(zs, chains)]
    nts = [jnp.dot(nl.astype(BF16), tri, preferred_element_type=F32) for nl in nls]
    tots = [-(nt[:, 0:1] + nl[:, 0:1]) for nt, nl in zip(nts, nls)]
    ws = []
    for n in range(2 * nq):
        d, p = 2 * n, 2 * n + 1
        w_d = jnp.where(causal, jnp.exp2((zs[d] - nls[d]) - nts[d]), 0.0)
        w_p = jnp.exp2((zs[p] - nls[p]) - nts[p] + tots[d])
        if n < 2:
            w_p = jnp.where(has_prev, w_p, 0.0)
        ws += [w_d.astype(BF16), w_p.astype(BF16)]
    pvs = [jnp.dot(w, values(j), preferred_element_type=F32) for w, (_, j, _) in zip(ws, chains)]
    accs = [pvs[2 * n] + pvs[2 * n + 1] for n in range(2 * nq)]
    carries = [tots[2 * n] + tots[2 * n + 1] for n in range(2 * nq)]

    def cond(st):
        j, cs, _ = st
        live = functools.reduce(jnp.maximum, [jnp.max(c) for c in cs])
        return jnp.logical_and(j >= 0, live > ATTN_SKIP_LOG2)

    def body(st):
        j, cs, acs = st
        new_c, new_a = [], []
        for n, (qh, c, a) in enumerate(zip(qs, cs, acs)):
            lag = nq - 1 - n // 2
            if lag:
                c = jnp.where(j >= lag, c, -jnp.inf)
            lw, tot, vb = tile(qh, jnp.maximum(j - lag, 0), False)
            new_a.append(a + jnp.dot(jnp.exp2(lw + c).astype(BF16), vb, preferred_element_type=F32))
            new_c.append(c + tot)
        return j - 1, tuple(new_c), tuple(new_a)

    _, _, accs = lax.while_loop(cond, body, (nq * qi + nq - 3, tuple(carries), tuple(accs)))
    for r in range(nq):
        o_ref[0, r * tq:(r + 1) * tq, :] = jnp.where(
            lane < SB_HEAD_DIM, accs[2 * r], accs[2 * r + 1]).astype(o_ref.dtype)


def _attention(q, k, v):
    B, S, W = q.shape
    tq = ATTN_SUBTILES * ATTN_TILE
    n_pairs = W // LANES
    return pl.pallas_call(
        _attn_kernel,
        grid=(B, n_pairs, S // tq),
        in_specs=[
            pl.BlockSpec((1, tq, LANES), lambda b, p, i: (b, i, p)),
            pl.BlockSpec((1, S, LANES), lambda b, p, i: (b, 0, p)),
            pl.BlockSpec((1, S, LANES), lambda b, p, i: (b, 0, p)),
        ],
        out_specs=pl.BlockSpec((1, tq, LANES), lambda b, p, i: (b, i, p)),
        out_shape=jax.ShapeDtypeStruct((B, S, W), BF16),
        compiler_params=pltpu.CompilerParams(
            dimension_semantics=("arbitrary", "arbitrary", "arbitrary"), vmem_limit_bytes=VMEM_LIMIT),
        name="stickbreak_attn",
    )(q, k, v)


def _out_proj_router_kernel(attn_ref, pool_ref, x_ref, wo_ref, g_ref, wr_ref, br_ref, tri_ref,
                            x1_ref, h2_ref, gate_ref, dest_ref, cnt_ref, carry_ref, *, capacity):
    i = pl.program_id(0)
    tm = x_ref.shape[0]

    @pl.when(i == 0)
    def _():
        carry_ref[...] = jnp.zeros_like(carry_ref)

    mixed = jnp.dot(attn_ref[...], wo_ref[0:SB_WIDTH, :], preferred_element_type=F32)
    mixed = mixed + jnp.dot(pool_ref[...], wo_ref[SB_WIDTH:, :], preferred_element_type=F32)
    x1 = x_ref[...] + mixed
    h2 = _rms(x1, g_ref[...])
    x1_ref[...] = x1
    h2_ref[...] = _pack_rows(h2)

    hh = h2.astype(BF16)
    hl = (h2 - hh.astype(F32)).astype(BF16)
    wr = wr_ref[...]
    wh = wr.astype(BF16)
    wl = (wr - wh.astype(F32)).astype(BF16)
    logits = (lax.dot_general(wh, hh, _NT, preferred_element_type=F32)
              + lax.dot_general(wh, hl, _NT, preferred_element_type=F32)
              + lax.dot_general(wl, hh, _NT, preferred_element_type=F32)) + br_ref[...]

    eid = lax.broadcasted_iota(I32, (N_EXPERTS, tm), 0).astype(F32)
    work = logits
    vals, ids = [], []
    for _ in range(TOP_K):
        m = jnp.max(work, axis=0, keepdims=True)
        sel = jnp.min(jnp.where(work == m, eid, float(N_EXPERTS)), axis=0, keepdims=True)
        vals.append(m)
        ids.append(sel)
        work = jnp.where(eid == sel, -jnp.inf, work)
    exps = [jnp.exp(v - vals[0]) for v in vals]
    denom = exps[0] + exps[1] + exps[2] + exps[3]

    onehot = jnp.zeros((N_EXPERTS, tm), F32)
    for sel in ids:
        onehot = onehot + (eid == sel).astype(F32)
    before = jnp.dot(onehot.astype(BF16), tri_ref[...], preferred_element_type=F32) + carry_ref[...]
    for kk in range(TOP_K):
        gate_ref[kk:kk + 1, :] = exps[kk] / denom
        rk = jnp.sum(jnp.where(eid == ids[kk], before, 0.0), axis=0, keepdims=True)
        dest_ref[kk:kk + 1, :] = (ids[kk] * float(capacity) + rk).astype(I32)
    carry_ref[...] = carry_ref[...] + jnp.sum(onehot, axis=1, keepdims=True)
    cnt_ref[...] = jnp.broadcast_to(carry_ref[...], cnt_ref.shape)


def _out_proj_router(attn, pool, x, w_out, g_moe, w_router, b_router, capacity):
    T, D = x.shape
    tm = PROJ_TILE
    r = lax.broadcasted_iota(I32, (tm, tm), 0)
    c = lax.broadcasted_iota(I32, (tm, tm), 1)
    tri = (r < c).astype(BF16)
    row_blk = lambda w: pl.BlockSpec((tm, w), lambda i: (i, 0))
    fixed = lambda shape: pl.BlockSpec(shape, lambda i: tuple(0 for _ in shape))
    sel_blk = pl.BlockSpec((TOP_K, tm), lambda i: (0, i))
    assert N_EXPERTS * capacity < 2 ** 24
    return pl.pallas_call(
        functools.partial(_out_proj_router_kernel, capacity=capacity),
        grid=(T // tm,),
        in_specs=[row_blk(SB_WIDTH), row_blk(POOL_WIDTH), row_blk(D), fixed((D, D)), fixed((1, D)),
                  fixed((N_EXPERTS, D)), fixed((N_EXPERTS, 1)), fixed((tm, tm))],
        out_specs=[row_blk(D), row_blk(D // 2), sel_blk, sel_blk, fixed((N_EXPERTS, LANES))],
        out_shape=[jax.ShapeDtypeStruct((T, D), F32), jax.ShapeDtypeStruct((T, D // 2), U32),
                   jax.ShapeDtypeStruct((TOP_K, T), F32), jax.ShapeDtypeStruct((TOP_K, T), I32),
                   jax.ShapeDtypeStruct((N_EXPERTS, LANES), F32)],
        scratch_shapes=[pltpu.VMEM((N_EXPERTS, 1), F32)],
        compiler_params=pltpu.CompilerParams(
            dimension_semantics=("arbitrary",), vmem_limit_bytes=VMEM_LIMIT),
        name="out_proj_router",
    )(attn, pool, x, w_out.astype(BF16), g_moe.reshape(1, D), w_router.T, b_router.reshape(N_EXPERTS, 1), tri)


SC_ROWS_PER_COPY = 64


def _sc_workers():
    info = plsc.get_sparse_core_info()
    return info.num_cores, info.num_cores * info.num_subcores


def _sc_dispatch(rows, dest, n_out):
    T, W = rows.shape
    K = dest.shape[0]
    sub = SC_ROWS_PER_COPY
    n_cores, n_workers = _sc_workers()
    per_w = T // n_workers
    n_chunks = per_w // sub
    assert per_w * n_workers == T and n_chunks * sub == per_w and n_chunks % 2 == 0
    idx = dest.reshape(K, n_workers, n_chunks, sub).transpose(1, 2, 0, 3).reshape(n_workers, n_chunks * K, sub)
    mesh = plsc.VectorSubcoreMesh(core_axis_name="core", subcore_axis_name="subcore")

    @functools.partial(
        pl.kernel, out_type=jax.ShapeDtypeStruct((n_out, W), rows.dtype), mesh=mesh,
        scratch_types=[pltpu.VMEM((n_chunks * K, sub), I32), pltpu.VMEM((2, sub, W), rows.dtype),
                       pltpu.SemaphoreType.DMA((2,)), pltpu.SemaphoreType.DMA((2,))])
    def scatter_rows(x_hbm, i_hbm, o_hbm, idx_v, buf, rsem, wsem):
        wid = lax.axis_index("subcore") * n_cores + lax.axis_index("core")
        base = wid * per_w
        pltpu.sync_copy(i_hbm.at[wid], idx_v)

        def read(c, slot):
            return pltpu.make_async_copy(x_hbm.at[pl.ds(base + c * sub, sub)], buf.at[slot], rsem.at[slot])

        def write(c, kk, slot):
            return pltpu.make_async_copy(buf.at[slot], o_hbm.at[idx_v.at[c * K + kk]], wsem.at[slot])

        read(0, 0).start()

        @pl.loop(0, n_chunks, step=2)
        def _(c0):
            for b in range(2):
                c = c0 + b
                read(c, b).wait()
                for kk in range(K):
                    write(c, kk, b).start()

                @pl.when(c + 1 < n_chunks)
                def _():
                    @pl.when(c >= 1)
                    def _():
                        for kk in range(K):
                            write(c - 1, kk, 1 - b).wait()
                    read(c + 1, 1 - b).start()

        for kk in range(K):
            write(n_chunks - 2, kk, 0).wait()
            write(n_chunks - 1, kk, 1).wait()

    return scatter_rows(rows, idx)


def _sc_gather(table, indices):
    M = indices.shape[0]
    W = table.shape[1]
    sub = SC_ROWS_PER_COPY
    n_cores, n_workers = _sc_workers()
    per_w = M // n_workers
    n_steps = per_w // sub
    assert per_w * n_workers == M and n_steps * sub == per_w and n_steps % 2 == 0
    mesh = plsc.VectorSubcoreMesh(core_axis_name="core", subcore_axis_name="subcore")

    @functools.partial(
        pl.kernel, out_type=jax.ShapeDtypeStruct((M, W), table.dtype), mesh=mesh,
        scratch_types=[pltpu.VMEM((n_steps, sub), I32), pltpu.VMEM((2, sub, W), table.dtype),
                       pltpu.SemaphoreType.DMA((2,)), pltpu.SemaphoreType.DMA((2,))])
    def gather_rows(x_hbm, i_hbm, o_hbm, idx_v, buf, gsem, wsem):
        wid = lax.axis_index("subcore") * n_cores + lax.axis_index("core")
        base = wid * per_w
        pltpu.sync_copy(i_hbm.at[wid], idx_v)

        def gather(s, slot):
            return pltpu.make_async_copy(x_hbm.at[idx_v.at[s]], buf.at[slot], gsem.at[slot])

        def write(s, slot):
            return pltpu.make_async_copy(buf.at[slot], o_hbm.at[pl.ds(base + s * sub, sub)], wsem.at[slot])

        gather(0, 0).start()

        @pl.loop(0, n_steps, step=2)
        def _(s0):
            for b in range(2):
                s = s0 + b
                gather(s, b).wait()
                write(s, b).start()

                @pl.when(s + 1 < n_steps)
                def _():
                    @pl.when(s >= 1)
                    def _():
                        write(s - 1, 1 - b).wait()
                    gather(s + 1, 1 - b).start()

        write(n_steps - 2, 0).wait()
        write(n_steps - 1, 1).wait()

    return gather_rows(table, indices.reshape(n_workers, n_steps, sub))


def _expert_kernel(vb_ref, ve_ref, vrows_ref, nvis_ref, nxt_ref,
                   x_ref, wgu_hbm, bgu_ref, wd_hbm, bd_ref, y_ref,
                   wgu_stage, wd_stage, wgu_bf, wd_bf, sem):
    v = pl.program_id(0)
    real = v < nvis_ref[0]
    e = ve_ref[v]
    first_of_expert = jnp.logical_or(v == 0, e != ve_ref[jnp.maximum(v - 1, 0)])

    def weight_copies(expert):
        return (pltpu.make_async_copy(wgu_hbm.at[expert], wgu_stage, sem.at[0]),
                pltpu.make_async_copy(wd_hbm.at[expert], wd_stage, sem.at[1]))

    @pl.when(v == 0)
    def _():
        for c in weight_copies(e):
            c.start()

    @pl.when(jnp.logical_and(first_of_expert, real))
    def _():
        for c in weight_copies(e):
            c.wait()
        wgu_bf[...] = wgu_stage[...].astype(BF16)
        wd_bf[...] = wd_stage[...].astype(BF16)
        nxt = nxt_ref[e]

        @pl.when(nxt >= 0)
        def _():
            for c in weight_copies(nxt):
                c.start()

    def run(n_rows):
        half = wgu_bf.shape[0] // 2
        x_lo, x_hi = _unpack_rows(x_ref[0:n_rows, :])
        gu = (jnp.dot(x_lo.astype(BF16), wgu_bf[:half, :], preferred_element_type=F32)
              + jnp.dot(x_hi.astype(BF16), wgu_bf[half:, :], preferred_element_type=F32)) + bgu_ref[0]
        gate = jnp.minimum(gu[:, :D_EXPERT], SWIGLU_LIMIT)
        up = jnp.clip(gu[:, D_EXPERT:], -SWIGLU_LIMIT, SWIGLU_LIMIT)
        act = gate * jax.nn.sigmoid(SWIGLU_ALPHA * gate) * (up + 1.0)
        y = jnp.dot(act.astype(BF16), wd_bf[...], preferred_element_type=F32) + bd_ref[0]
        y_ref[0:n_rows, :] = _pack_rows(y)

    for n_rows in EXPERT_ROW_CLASSES:
        pl.when(jnp.logical_and(real, vrows_ref[v] == n_rows))(functools.partial(run, n_rows))


def _experts(x_sorted, visits, w_gate_up, b_gate_up, w_down, b_down):
    n_rows, W = x_sorted.shape
    D = 2 * W
    tm = EXPERT_TILE
    n_visits = visits[0].shape[0]
    n_prefetch = len(visits)
    by_block = lambda v, vb, *_: (vb[v], 0)
    by_expert = lambda v, vb, ve, *_: (ve[v], 0, 0)
    grid_spec = pltpu.PrefetchScalarGridSpec(
        num_scalar_prefetch=n_prefetch,
        grid=(n_visits,),
        in_specs=[
            pl.BlockSpec((tm, W), by_block),
            pl.BlockSpec(memory_space=pl.ANY),
            pl.BlockSpec((1, 1, 2 * D_EXPERT), by_expert),
            pl.BlockSpec(memory_space=pl.ANY),
            pl.BlockSpec((1, 1, D), by_expert),
        ],
        out_specs=pl.BlockSpec((tm, W), by_block),
        scratch_shapes=[pltpu.VMEM((D, 2 * D_EXPERT), F32), pltpu.VMEM((D_EXPERT, D), F32),
                        pltpu.VMEM((D, 2 * D_EXPERT), BF16), pltpu.VMEM((D_EXPERT, D), BF16),
                        pltpu.SemaphoreType.DMA((2,))],
    )
    return pl.pallas_call(
        _expert_kernel,
        grid_spec=grid_spec,
        out_shape=jax.ShapeDtypeStruct((n_rows, W), U32),
        compiler_params=pltpu.CompilerParams(
            dimension_semantics=("arbitrary",), vmem_limit_bytes=VMEM_LIMIT),
        name="expert_gmm",
    )(*visits, x_sorted, w_gate_up, b_gate_up.reshape(N_EXPERTS, 1, -1), w_down, b_down.reshape(N_EXPERTS, 1, -1))


def _visit_schedule(counts, n_assign, capacity):
    tm = EXPERT_TILE
    assert EXPERT_ROW_CLASSES[0] == tm and list(EXPERT_ROW_CLASSES) == sorted(EXPERT_ROW_CLASSES, reverse=True)
    n_visits = n_assign // tm + N_EXPERTS
    n_full = counts // tm
    tail = counts - n_full * tm
    tail_rows = jnp.zeros_like(tail)
    for rows in EXPERT_ROW_CLASSES:
        tail_rows = jnp.where(tail <= rows, rows, tail_rows)
    per_e = n_full + (tail > 0)
    vend = jnp.cumsum(per_e)
    nvis = vend[-1]
    vc = jnp.minimum(jnp.arange(n_visits, dtype=I32), nvis - 1)
    done = vc[:, None] >= vend[None, :]
    e = jnp.sum(done, axis=1).astype(I32)
    local = vc - jnp.sum(jnp.where(done, per_e[None, :], 0), axis=1)
    mine = e[:, None] == jnp.arange(N_EXPERTS, dtype=I32)[None, :]
    is_tail = local >= jnp.sum(jnp.where(mine, n_full[None, :], 0), axis=1)
    vrows = jnp.where(is_tail, jnp.sum(jnp.where(mine, tail_rows[None, :], 0), axis=1), tm)
    blk = e * (capacity // tm) + local
    ids = jnp.arange(N_EXPERTS, dtype=I32)
    nonempty = counts > 0
    later = jnp.logical_and(nonempty[None, :], ids[None, :] > ids[:, None])
    nxt = jnp.where(jnp.any(later, axis=1), jnp.argmax(later, axis=1), -1).astype(I32)
    return blk.astype(I32), e, vrows.astype(I32), nvis.reshape(1).astype(I32), nxt


def _combine_kernel(y0_ref, y1_ref, y2_ref, y3_ref, x1_ref, gate_ref, g_ref, o_ref):
    half = x1_ref.shape[1] // 2
    gates = gate_ref[...]
    lo = x1_ref[:, :half]
    hi = x1_ref[:, half:]
    for kk, y_ref in enumerate((y0_ref, y1_ref, y2_ref, y3_ref)):
        y_lo, y_hi = _unpack_rows(y_ref[...])
        lo = lo + gates[:, kk:kk + 1] * y_lo
        hi = hi + gates[:, kk:kk + 1] * y_hi
    ms = (jnp.sum(lo * lo, axis=-1, keepdims=True) + jnp.sum(hi * hi, axis=-1, keepdims=True)) / (2 * half)
    scale = lax.rsqrt(ms + RMS_EPS)
    o_ref[:, :half] = lo * scale * g_ref[:, :half]
    o_ref[:, half:] = hi * scale * g_ref[:, half:]


def _combine(y_tok, x1, gates_t, g_final):
    T, D = x1.shape
    tm = COMBINE_TILE
    nt = T // tm
    y_spec = lambda kk: pl.BlockSpec((tm, D // 2), lambda i: (kk * nt + i, 0))
    return pl.pallas_call(
        _combine_kernel,
        grid=(nt,),
        in_specs=[y_spec(0), y_spec(1), y_spec(2), y_spec(3),
                  pl.BlockSpec((tm, D), lambda i: (i, 0)),
                  pl.BlockSpec((tm, TOP_K), lambda i: (i, 0)),
                  pl.BlockSpec((1, D), lambda i: (0, 0))],
        out_specs=pl.BlockSpec((tm, D), lambda i: (i, 0)),
        out_shape=jax.ShapeDtypeStruct((T, D), F32),
        compiler_params=pltpu.CompilerParams(
            dimension_semantics=("arbitrary",), vmem_limit_bytes=VMEM_LIMIT),
        name="combine_norm",
    )(y_tok, y_tok, y_tok, y_tok, x1, gates_t, g_final.reshape(1, D))


def kernel(x, g_mix, w_in, pool_w, pool_scale, w_out, g_moe, w_router, b_router, w_gate_up, b_gate_up,
           w_down, b_down, g_final):
    B, S, D = x.shape
    T = B * S
    assert g_mix.shape[0] == 1, "single-layer problem: the final norm is fused into the combine step"
    q, k, v, pool = _in_proj(x, g_mix[0], w_in[0], pool_w[0], pool_scale[0])
    attn = _attention(q, k, v)
    capacity = T
    x1, h2, gates, dest, cnt = _out_proj_router(
        attn.reshape(T, SB_WIDTH), pool.reshape(T, POOL_WIDTH), x.reshape(T, D),
        w_out[0], g_moe[0], w_router[0], b_router[0], capacity)
    x_sorted = _sc_dispatch(h2, dest, N_EXPERTS * capacity)
    visits = _visit_schedule(cnt[:, 0].astype(I32), T * TOP_K, capacity)
    y = _experts(x_sorted, visits, w_gate_up[0], b_gate_up[0], w_down[0], b_down[0])
    y_tok = _sc_gather(y, dest.reshape(-1))
    return _combine(y_tok, x1, gates.T, g_final).reshape(B, S, D)
```

```python
import functools

import jax
import jax.numpy as jnp
from jax import lax
from jax.experimental import pallas as pl
from jax.experimental.pallas import tpu as pltpu
from jax.experimental.pallas import tpu_sc as plsc

F32 = jnp.float32
BF16 = jnp.bfloat16
I32 = jnp.int32
U32 = jnp.uint32

D_MODEL = 1024
SB_HEADS = 8
SB_HEAD_DIM = 64
SB_WIDTH = SB_HEADS * SB_HEAD_DIM
POOL_WINDOWS = (2, 4, 8, 16)
POOL_WIDTH = 512
POOL_GROUP_DIM = 128
N_EXPERTS = 32
TOP_K = 4
D_EXPERT = 1024
SWIGLU_LIMIT = 7.0
SWIGLU_ALPHA = 1.702
RMS_EPS = 1e-5

LANES = 128
HALO = 32
PROJ_TILE = 512
ATTN_TILE = 256
ATTN_SUBTILES = 4
EXPERT_TILE = 1024
EXPERT_ROW_CLASSES = (1024, 512, 256)
COMBINE_TILE = 512
ATTN_SKIP_LOG2 = -160.0
LOG2_E = 1.4426950408889634
VMEM_LIMIT = 56 * 1024 * 1024

_NT = (((1,), (1,)), ((), ()))


def _rms(x, g):
    ms = jnp.mean(x * x, axis=-1, keepdims=True)
    return x * lax.rsqrt(ms + RMS_EPS) * g


def _pack_rows(x):
    n = x.shape[1] // 2
    lo = lax.bitcast_convert_type(x[:, :n].astype(BF16).astype(F32), U32)
    hi = lax.bitcast_convert_type(x[:, n:].astype(BF16).astype(F32), U32)
    return (lo >> 16) | (hi & jnp.uint32(0xFFFF0000))


def _unpack_rows(w):
    lo = lax.bitcast_convert_type(w << 16, F32)
    hi = lax.bitcast_convert_type(w & jnp.uint32(0xFFFF0000), F32)
    return lo, hi


def _in_proj_kernel(x_ref, w_ref, pw_ref, ps_ref, q_ref, k_ref, v_ref, p_ref, uext_ref, xb_ref, lvl_ref):
    s = pl.program_id(1)
    tm = x_ref.shape[1]
    x = x_ref[0]
    inv = lax.rsqrt(jnp.mean(x * x, axis=-1, keepdims=True) + RMS_EPS)
    xb_ref[...] = x.astype(BF16)
    proj = jnp.dot(xb_ref[...], w_ref[...], preferred_element_type=F32)
    q_ref[0] = (proj[:, 0:SB_WIDTH] * (inv * (LOG2_E * SB_HEAD_DIM ** -0.5))).astype(BF16)
    k_ref[0] = (proj[:, SB_WIDTH:2 * SB_WIDTH] * inv).astype(BF16)
    v_ref[0] = (proj[:, 2 * SB_WIDTH:3 * SB_WIDTH] * inv).astype(BF16)
    u = proj[:, 3 * SB_WIDTH:] * inv

    lo = HALO // 2

    @pl.when(s == 0)
    def _():
        uext_ref[0:HALO, :] = jnp.zeros((HALO, POOL_WIDTH), F32)
        lvl_ref[:, 0:lo, :] = jnp.zeros((2, lo, POOL_GROUP_DIM), F32)

    uext_ref[HALO:, :] = u
    t = s * tm + lax.broadcasted_iota(I32, (tm, 1), 0)
    for g, w in enumerate(POOL_WINDOWS):
        sl = slice(g * POOL_GROUP_DIM, (g + 1) * POOL_GROUP_DIM)
        lvl_ref[0, lo:, :] = uext_ref[lo:, sl] + uext_ref[lo - 1:HALO + tm - 1, sl]
        cur, k = 0, 2
        while k < w:
            lvl_ref[1 - cur, lo:, :] = lvl_ref[cur, lo:, :] + lvl_ref[cur, lo - k:HALO + tm - k, :]
            cur, k = 1 - cur, 2 * k
        ug = u[:, sl]
        count = jnp.minimum(t + 1, w).astype(F32)
        pooled = lvl_ref[cur, HALO:, :] / count - ug
        mixed = jnp.dot(pooled.astype(BF16), pw_ref[g], preferred_element_type=F32)
        p_ref[0, :, sl] = (mixed * ps_ref[:, sl]).astype(BF16)
    uext_ref[0:HALO, :] = u[tm - HALO:, :]


def _in_proj(x, g_mix, w_in, pool_w, pool_scale):
    B, S, D = x.shape
    tm = PROJ_TILE
    n_out = w_in.shape[1]
    out_sd = jax.ShapeDtypeStruct((B, S, SB_WIDTH), BF16)
    blk = pl.BlockSpec((1, tm, SB_WIDTH), lambda b, s: (b, s, 0))
    return pl.pallas_call(
        _in_proj_kernel,
        grid=(B, S // tm),
        in_specs=[
            pl.BlockSpec((1, tm, D), lambda b, s: (b, s, 0)),
            pl.BlockSpec((D, n_out), lambda b, s: (0, 0)),
            pl.BlockSpec(pool_w.shape, lambda b, s: (0, 0, 0)),
            pl.BlockSpec((1, POOL_WIDTH), lambda b, s: (0, 0)),
        ],
        out_specs=[blk, blk, blk, blk],
        out_shape=[out_sd, out_sd, out_sd, out_sd],
        scratch_shapes=[pltpu.VMEM((HALO + tm, POOL_WIDTH), F32), pltpu.VMEM((tm, D), BF16),
                        pltpu.VMEM((2, HALO + tm, POOL_GROUP_DIM), F32)],
        compiler_params=pltpu.CompilerParams(
            dimension_semantics=("arbitrary", "arbitrary"), vmem_limit_bytes=VMEM_LIMIT),
        name="in_proj_pool",
    )(x, (g_mix[:, None] * w_in).astype(BF16), pool_w.astype(BF16), pool_scale.reshape(1, POOL_WIDTH))


def _attn_kernel(q_ref, k_ref, v_ref, o_ref):
    tq = ATTN_TILE
    tk = tq
    qi = pl.program_id(2)
    lane = lax.broadcasted_iota(I32, (tq, LANES), 1)
    row = lax.broadcasted_iota(I32, (tq, tk), 0)
    col = lax.broadcasted_iota(I32, (tq, tk), 1)
    causal = col < row
    tri = (row > col).astype(BF16)

    def head_rows(r, h):
        rows = q_ref[0, r * tq:(r + 1) * tq, :]
        return jnp.where((lane >= h * SB_HEAD_DIM) & (lane < (h + 1) * SB_HEAD_DIM), rows, jnp.zeros_like(rows))

    def scores(qh, j):
        kb = k_ref[0, pl.ds(pl.multiple_of(j * tk, tk), tk), :]
        return lax.dot_general(qh, kb, _NT, preferred_element_type=F32)

    def values(j):
        return v_ref[0, pl.ds(pl.multiple_of(j * tk, tk), tk), :]

    def neg_log(z, mask):
        neg_abs = lax.bitcast_convert_type(lax.bitcast_convert_type(z, U32) | jnp.uint32(0x80000000), F32)
        nl = jnp.maximum(z, 0.0) + jnp.log2(1.0 + jnp.exp2(neg_abs))
        return jnp.where(causal, nl, 0.0) if mask else nl

    def tile(qh, j, mask):
        z = scores(qh, j)
        nl = neg_log(z, mask)
        ntail = jnp.dot(nl.astype(BF16), tri, preferred_element_type=F32)
        return (z - nl) - ntail, -(ntail[:, 0:1] + nl[:, 0:1]), values(j)

    nq = ATTN_SUBTILES
    has_prev = qi > 0
    qs = [head_rows(r, h) for r in range(nq) for h in range(2)]
    chains = []
    for n, qh in enumerate(qs):
        diag = nq * qi + n // 2
        chains += [(qh, diag, True), (qh, jnp.maximum(diag - 1, 0), False)]
    zs = [scores(qh, j) for qh, j, _ in chains]
    nls = [neg_log(z, is_diag) for z, (_, _, is_diag) in zip(zs, chains)]
    nts = [jnp.dot(nl.astype(BF16), tri, preferred_element_type=F32) for nl in nls]
    tots = [-(nt[:, 0:1] + nl[:, 0:1]) for nt, nl in zip(nts, nls)]
    ws = []
    for n in range(2 * nq):
        d, p = 2 * n, 2 * n + 1
        w_d = jnp.where(causal, jnp.exp2((zs[d] - nls[d]) - nts[d]), 0.0)
        w_p = jnp.exp2((zs[p] - nls[p]) - nts[p] + tots[d])
        if n < 2:
            w_p = jnp.where(has_prev, w_p, 0.0)
        ws += [w_d.astype(BF16), w_p.astype(BF16)]
    pvs = [jnp.dot(w, values(j), preferred_element_type=F32) for w, (_, j, _) in zip(ws, chains)]
    accs = [pvs[2 * n] + pvs[2 * n + 1] for n in range(2 * nq)]
    carries = [tots[2 * n] + tots[2 * n + 1] for n in range(2 * nq)]

    def cond(st):
        j, cs, _ = st
        live = functools.reduce(jnp.maximum, [jnp.max(c) for c in cs])
        return jnp.logical_and(j >= 0, live > ATTN_SKIP_LOG2)

    def body(st):
        j, cs, acs = st
        new_c, new_a = [], []
        for n, (qh, c, a) in enumerate(zip(qs, cs, acs)):
            lag = nq - 1 - n // 2
            if lag:
                c = jnp.where(j >= lag, c, -jnp.inf)
            lw, tot, vb = tile(qh, jnp.maximum(j - lag, 0), False)
            new_a.append(a + jnp.dot(jnp.exp2(lw + c).astype(BF16), vb, preferred_element_type=F32))
            new_c.append(c + tot)
        return j - 1, tuple(new_c), tuple(new_a)

    _, _, accs = lax.while_loop(cond, body, (nq * qi + nq - 3, tuple(carries), tuple(accs)))
    for r in range(nq):
        o_ref[0, r * tq:(r + 1) * tq, :] = jnp.where(
            lane < SB_HEAD_DIM, accs[2 * r], accs[2 * r + 1]).astype(o_ref.dtype)


def _attention(q, k, v):
    B, S, W = q.shape
    tq = ATTN_SUBTILES * ATTN_TILE
    n_pairs = W // LANES
    return pl.pallas_call(
        _attn_kernel,
        grid=(B, n_pairs, S // tq),
        in_specs=[
            pl.BlockSpec((1, tq, LANES), lambda b, p, i: (b, i, p)),
            pl.BlockSpec((1, S, LANES), lambda b, p, i: (b, 0, p)),
            pl.BlockSpec((1, S, LANES), lambda b, p, i: (b, 0, p)),
        ],
        out_specs=pl.BlockSpec((1, tq, LANES), lambda b, p, i: (b, i, p)),
        out_shape=jax.ShapeDtypeStruct((B, S, W), BF16),
        compiler_params=pltpu.CompilerParams(
            dimension_semantics=("arbitrary", "arbitrary", "arbitrary"), vmem_limit_bytes=VMEM_LIMIT),
        name="stickbreak_attn",
    )(q, k, v)


def _out_proj_router_kernel(attn_ref, pool_ref, x_ref, wo_ref, g_ref, wr_ref, br_ref, tri_ref,
                            x1_ref, h2_ref, gate_ref, dest_ref, cnt_ref, carry_ref, *, capacity):
    i = pl.program_id(0)
    tm = x_ref.shape[0]

    @pl.when(i == 0)
    def _():
        carry_ref[...] = jnp.zeros_like(carry_ref)

    mixed = jnp.dot(attn_ref[...], wo_ref[0:SB_WIDTH, :], preferred_element_type=F32)
    mixed = mixed + jnp.dot(pool_ref[...], wo_ref[SB_WIDTH:, :], preferred_element_type=F32)
    x1 = x_ref[...] + mixed
    h2 = _rms(x1, g_ref[...])
    x1_ref[...] = x1
    h2_ref[...] = _pack_rows(h2)

    hh = h2.astype(BF16)
    hl = (h2 - hh.astype(F32)).astype(BF16)
    wr = wr_ref[...]
    wh = wr.astype(BF16)
    wl = (wr - wh.astype(F32)).astype(BF16)
    logits = (lax.dot_general(wh, hh, _NT, preferred_element_type=F32)
              + lax.dot_general(wh, hl, _NT, preferred_element_type=F32)
              + lax.dot_general(wl, hh, _NT, preferred_element_type=F32)) + br_ref[...]

    eid = lax.broadcasted_iota(I32, (N_EXPERTS, tm), 0).astype(F32)
    work = logits
    vals, ids = [], []
    for _ in range(TOP_K):
        m = jnp.max(work, axis=0, keepdims=True)
        sel = jnp.min(jnp.where(work == m, eid, float(N_EXPERTS)), axis=0, keepdims=True)
        vals.append(m)
        ids.append(sel)
        work = jnp.where(eid == sel, -jnp.inf, work)
    exps = [jnp.exp(v - vals[0]) for v in vals]
    denom = exps[0] + exps[1] + exps[2] + exps[3]

    onehot = jnp.zeros((N_EXPERTS, tm), F32)
    for sel in ids:
        onehot = onehot + (eid == sel).astype(F32)
    before = jnp.dot(onehot.astype(BF16), tri_ref[...], preferred_element_type=F32) + carry_ref[...]
    for kk in range(TOP_K):
        gate_ref[kk:kk + 1, :] = exps[kk] / denom
        rk = jnp.sum(jnp.where(eid == ids[kk], before, 0.0), axis=0, keepdims=True)
        dest_ref[kk:kk + 1, :] = (ids[kk] * float(capacity) + rk).astype(I32)
    carry_ref[...] = carry_ref[...] + jnp.sum(onehot, axis=1, keepdims=True)
    cnt_ref[...] = jnp.broadcast_to(carry_ref[...], cnt_ref.shape)


def _out_proj_router(attn, pool, x, w_out, g_moe, w_router, b_router, capacity):
    T, D = x.shape
    tm = PROJ_TILE
    r = lax.broadcasted_iota(I32, (tm, tm), 0)
    c = lax.broadcasted_iota(I32, (tm, tm), 1)
    tri = (r < c).astype(BF16)
    row_blk = lambda w: pl.BlockSpec((tm, w), lambda i: (i, 0))
    fixed = lambda shape: pl.BlockSpec(shape, lambda i: tuple(0 for _ in shape))
    sel_blk = pl.BlockSpec((TOP_K, tm), lambda i: (0, i))
    assert N_EXPERTS * capacity < 2 ** 24
    return pl.pallas_call(
        functools.partial(_out_proj_router_kernel, capacity=capacity),
        grid=(T // tm,),
        in_specs=[row_blk(SB_WIDTH), row_blk(POOL_WIDTH), row_blk(D), fixed((D, D)), fixed((1, D)),
                  fixed((N_EXPERTS, D)), fixed((N_EXPERTS, 1)), fixed((tm, tm))],
        out_specs=[row_blk(D), row_blk(D // 2), sel_blk, sel_blk, fixed((N_EXPERTS, LANES))],
        out_shape=[jax.ShapeDtypeStruct((T, D), F32), jax.ShapeDtypeStruct((T, D // 2), U32),
                   jax.ShapeDtypeStruct((TOP_K, T), F32), jax.ShapeDtypeStruct((TOP_K, T), I32),
                   jax.ShapeDtypeStruct((N_EXPERTS, LANES), F32)],
        scratch_shapes=[pltpu.VMEM((N_EXPERTS, 1), F32)],
        compiler_params=pltpu.CompilerParams(
            dimension_semantics=("arbitrary",), vmem_limit_bytes=VMEM_LIMIT),
        name="out_proj_router",
    )(attn, pool, x, w_out.astype(BF16), g_moe.reshape(1, D), w_router.T, b_router.reshape(N_EXPERTS, 1), tri)


SC_ROWS_PER_COPY = 64


def _sc_workers():
    info = plsc.get_sparse_core_info()
    return info.num_cores, info.num_cores * info.num_subcores


def _sc_dispatch(rows, dest, n_out):
    T, W = rows.shape
    K = dest.shape[0]
    sub = SC_ROWS_PER_COPY
    n_cores, n_workers = _sc_workers()
    per_w = T // n_workers
    n_chunks = per_w // sub
    assert per_w * n_workers == T and n_chunks * sub == per_w and n_chunks % 2 == 0
    idx = dest.reshape(K, n_workers, n_chunks, sub).transpose(1, 2, 0, 3).reshape(n_workers, n_chunks * K, sub)
    mesh = plsc.VectorSubcoreMesh(core_axis_name="core", subcore_axis_name="subcore")

    @functools.partial(
        pl.kernel, out_type=jax.ShapeDtypeStruct((n_out, W), rows.dtype), mesh=mesh,
        scratch_types=[pltpu.VMEM((n_chunks * K, sub), I32), pltpu.VMEM((2, sub, W), rows.dtype),
                       pltpu.SemaphoreType.DMA((2,)), pltpu.SemaphoreType.DMA((2,))])
    def scatter_rows(x_hbm, i_hbm, o_hbm, idx_v, buf, rsem, wsem):
        wid = lax.axis_index("subcore") * n_cores + lax.axis_index("core")
        base = wid * per_w
        pltpu.sync_copy(i_hbm.at[wid], idx_v)

        def read(c, slot):
            return pltpu.make_async_copy(x_hbm.at[pl.ds(base + c * sub, sub)], buf.at[slot], rsem.at[slot])

        def write(c, kk, slot):
            return pltpu.make_async_copy(buf.at[slot], o_hbm.at[idx_v.at[c * K + kk]], wsem.at[slot])

        read(0, 0).start()

        @pl.loop(0, n_chunks, step=2)
        def _(c0):
            for b in range(2):
                c = c0 + b
                read(c, b).wait()
                for kk in range(K):
                    write(c, kk, b).start()

                @pl.when(c + 1 < n_chunks)
                def _():
                    @pl.when(c >= 1)
                    def _():
                        for kk in range(K):
                            write(c - 1, kk, 1 - b).wait()
                    read(c + 1, 1 - b).start()

        for kk in range(K):
            write(n_chunks - 2, kk, 0).wait()
            write(n_chunks - 1, kk, 1).wait()

    return scatter_rows(rows, idx)


def _sc_gather(table, indices):
    M = indices.shape[0]
    W = table.shape[1]
    sub = SC_ROWS_PER_COPY
    n_cores, n_workers = _sc_workers()
    per_w = M // n_workers
    n_steps = per_w // sub
    assert per_w * n_workers == M and n_steps * sub == per_w and n_steps % 2 == 0
    mesh = plsc.VectorSubcoreMesh(core_axis_name="core", subcore_axis_name="subcore")

    @functools.partial(
        pl.kernel, out_type=jax.ShapeDtypeStruct((M, W), table.dtype), mesh=mesh,
        scratch_types=[pltpu.VMEM((n_steps, sub), I32), pltpu.VMEM((2, sub, W), table.dtype),
                       pltpu.SemaphoreType.DMA((2,)), pltpu.SemaphoreType.DMA((2,))])
    def gather_rows(x_hbm, i_hbm, o_hbm, idx_v, buf, gsem, wsem):
        wid = lax.axis_index("subcore") * n_cores + lax.axis_index("core")
        base = wid * per_w
        pltpu.sync_copy(i_hbm.at[wid], idx_v)

        def gather(s, slot):
            return pltpu.make_async_copy(x_hbm.at[idx_v.at[s]], buf.at[slot], gsem.at[slot])

        def write(s, slot):
            return pltpu.make_async_copy(buf.at[slot], o_hbm.at[pl.ds(base + s * sub, sub)], wsem.at[slot])

        gather(0, 0).start()

        @pl.loop(0, n_steps, step=2)
        def _(s0):
            for b in range(2):
                s = s0 + b
                gather(s, b).wait()
                write(s, b).start()

                @pl.when(s + 1 < n_steps)
                def _():
                    @pl.when(s >= 1)
                    def _():
                        write(s - 1, 1 - b).wait()
                    gather(s + 1, 1 - b).start()

        write(n_steps - 2, 0).wait()
        write(n_steps - 1, 1).wait()

    return gather_rows(table, indices.reshape(n_workers, n_steps, sub))


def _expert_kernel(vb_ref, ve_ref, vrows_ref, nvis_ref, nxt_ref,
                   x_ref, wgu_hbm, bgu_ref, wd_hbm, bd_ref, y_ref,
                   wgu_stage, wd_stage, wgu_bf, wd_bf, sem):
    v = pl.program_id(0)
    real = v < nvis_ref[0]
    e = ve_ref[v]
    first_of_expert = jnp.logical_or(v == 0, e != ve_ref[jnp.maximum(v - 1, 0)])

    def weight_copies(expert):
        return (pltpu.make_async_copy(wgu_hbm.at[expert], wgu_stage, sem.at[0]),
                pltpu.make_async_copy(wd_hbm.at[expert], wd_stage, sem.at[1]))

    @pl.when(v == 0)
    def _():
        for c in weight_copies(e):
            c.start()

    def load_weights():
        for c in weight_copies(e):
            c.wait()
        wgu_bf[...] = wgu_stage[...].astype(BF16)
        wd_bf[...] = wd_stage[...].astype(BF16)

    def fetch_next():
        nxt = nxt_ref[e]

        @pl.when(nxt >= 0)
        def _():
            for c in weight_copies(nxt):
                c.start()

    def run(n_rows):
        half = wgu_bf.shape[0] // 2
        x_lo, x_hi = _unpack_rows(x_ref[0:n_rows, :])
        gu = (jnp.dot(x_lo.astype(BF16), wgu_bf[:half, :], preferred_element_type=F32)
              + jnp.dot(x_hi.astype(BF16), wgu_bf[half:, :], preferred_element_type=F32)) + bgu_ref[0]
        gate = jnp.minimum(gu[:, :D_EXPERT], SWIGLU_LIMIT)
        up = jnp.clip(gu[:, D_EXPERT:], -SWIGLU_LIMIT, SWIGLU_LIMIT)
        act = gate * jax.nn.sigmoid(SWIGLU_ALPHA * gate) * (up + 1.0)
        y = jnp.dot(act.astype(BF16), wd_bf[...], preferred_element_type=F32) + bd_ref[0]
        y_ref[0:n_rows, :] = _pack_rows(y)

    rows = vrows_ref[v]
    full = EXPERT_ROW_CLASSES[0]
    first = jnp.logical_and(first_of_expert, real)

    @pl.when(jnp.logical_and(first, rows == full))
    def _():
        load_weights()
        run(full)
        fetch_next()

    @pl.when(jnp.logical_and(first, rows != full))
    def _():
        load_weights()
        fetch_next()

    later_full = jnp.logical_and(real, jnp.logical_not(first_of_expert))
    pl.when(jnp.logical_and(later_full, rows == full))(functools.partial(run, full))
    for n_rows in EXPERT_ROW_CLASSES[1:]:
        pl.when(jnp.logical_and(real, rows == n_rows))(functools.partial(run, n_rows))


def _experts(x_sorted, visits, w_gate_up, b_gate_up, w_down, b_down):
    n_rows, W = x_sorted.shape
    D = 2 * W
    tm = EXPERT_TILE
    n_visits = visits[0].shape[0]
    n_prefetch = len(visits)
    by_block = lambda v, vb, *_: (vb[v], 0)
    by_expert = lambda v, vb, ve, *_: (ve[v], 0, 0)
    grid_spec = pltpu.PrefetchScalarGridSpec(
        num_scalar_prefetch=n_prefetch,
        grid=(n_visits,),
        in_specs=[
            pl.BlockSpec((tm, W), by_block),
            pl.BlockSpec(memory_space=pl.ANY),
            pl.BlockSpec((1, 1, 2 * D_EXPERT), by_expert),
            pl.BlockSpec(memory_space=pl.ANY),
            pl.BlockSpec((1, 1, D), by_expert),
        ],
        out_specs=pl.BlockSpec((tm, W), by_block),
        scratch_shapes=[pltpu.VMEM((D, 2 * D_EXPERT), F32), pltpu.VMEM((D_EXPERT, D), F32),
                        pltpu.VMEM((D, 2 * D_EXPERT), BF16), pltpu.VMEM((D_EXPERT, D), BF16),
                        pltpu.SemaphoreType.DMA((2,))],
    )
    return pl.pallas_call(
        _expert_kernel,
        grid_spec=grid_spec,
        out_shape=jax.ShapeDtypeStruct((n_rows, W), U32),
        compiler_params=pltpu.CompilerParams(
            dimension_semantics=("arbitrary",), vmem_limit_bytes=VMEM_LIMIT),
        name="expert_gmm",
    )(*visits, x_sorted, w_gate_up, b_gate_up.reshape(N_EXPERTS, 1, -1), w_down, b_down.reshape(N_EXPERTS, 1, -1))


def _visit_schedule(counts, n_assign, capacity):
    tm = EXPERT_TILE
    assert EXPERT_ROW_CLASSES[0] == tm and list(EXPERT_ROW_CLASSES) == sorted(EXPERT_ROW_CLASSES, reverse=True)
    n_visits = n_assign // tm + N_EXPERTS
    n_full = counts // tm
    tail = counts - n_full * tm
    tail_rows = jnp.zeros_like(tail)
    for rows in EXPERT_ROW_CLASSES:
        tail_rows = jnp.where(tail <= rows, rows, tail_rows)
    per_e = n_full + (tail > 0)
    vend = jnp.cumsum(per_e)
    nvis = vend[-1]
    vc = jnp.minimum(jnp.arange(n_visits, dtype=I32), nvis - 1)
    done = vc[:, None] >= vend[None, :]
    e = jnp.sum(done, axis=1).astype(I32)
    local = vc - jnp.sum(jnp.where(done, per_e[None, :], 0), axis=1)
    mine = e[:, None] == jnp.arange(N_EXPERTS, dtype=I32)[None, :]
    is_tail = local >= jnp.sum(jnp.where(mine, n_full[None, :], 0), axis=1)
    vrows = jnp.where(is_tail, jnp.sum(jnp.where(mine, tail_rows[None, :], 0), axis=1), tm)
    blk = e * (capacity // tm) + local
    ids = jnp.arange(N_EXPERTS, dtype=I32)
    nonempty = counts > 0
    later = jnp.logical_and(nonempty[None, :], ids[None, :] > ids[:, None])
    nxt = jnp.where(jnp.any(later, axis=1), jnp.argmax(later, axis=1), -1).astype(I32)
    return blk.astype(I32), e, vrows.astype(I32), nvis.reshape(1).astype(I32), nxt


def _combine_kernel(y0_ref, y1_ref, y2_ref, y3_ref, x1_ref, gate_ref, g_ref, o_ref):
    half = x1_ref.shape[1] // 2
    gates = gate_ref[...]
    lo = x1_ref[:, :half]
    hi = x1_ref[:, half:]
    for kk, y_ref in enumerate((y0_ref, y1_ref, y2_ref, y3_ref)):
        y_lo, y_hi = _unpack_rows(y_ref[...])
        lo = lo + gates[:, kk:kk + 1] * y_lo
        hi = hi + gates[:, kk:kk + 1] * y_hi
    ms = (jnp.sum(lo * lo, axis=-1, keepdims=True) + jnp.sum(hi * hi, axis=-1, keepdims=True)) / (2 * half)
    scale = lax.rsqrt(ms + RMS_EPS)
    o_ref[:, :half] = lo * scale * g_ref[:, :half]
    o_ref[:, half:] = hi * scale * g_ref[:, half:]


def _combine(y_tok, x1, gates_t, g_final):
    T, D = x1.shape
    tm = COMBINE_TILE
    nt = T // tm
    y_spec = lambda kk: pl.BlockSpec((tm, D // 2), lambda i: (kk * nt + i, 0))
    return pl.pallas_call(
        _combine_kernel,
        grid=(nt,),
        in_specs=[y_spec(0), y_spec(1), y_spec(2), y_spec(3),
                  pl.BlockSpec((tm, D), lambda i: (i, 0)),
                  pl.BlockSpec((tm, TOP_K), lambda i: (i, 0)),
                  pl.BlockSpec((1, D), lambda i: (0, 0))],
        out_specs=pl.BlockSpec((tm, D), lambda i: (i, 0)),
        out_shape=jax.ShapeDtypeStruct((T, D), F32),
        compiler_params=pltpu.CompilerParams(
            dimension_semantics=("arbitrary",), vmem_limit_bytes=VMEM_LIMIT),
        name="combine_norm",
    )(y_tok, y_tok, y_tok, y_tok, x1, gates_t, g_final.reshape(1, D))


def kernel(x, g_mix, w_in, pool_w, pool_scale, w_out, g_moe, w_router, b_router, w_gate_up, b_gate_up,
           w_down, b_down, g_final):
    B, S, D = x.shape
    T = B * S
    assert g_mix.shape[0] == 1, "single-layer problem: the final norm is fused into the combine step"
    q, k, v, pool = _in_proj(x, g_mix[0], w_in[0], pool_w[0], pool_scale[0])
    attn = _attention(q, k, v)
    capacity = T
    x1, h2, gates, dest, cnt = _out_proj_router(
        attn.reshape(T, SB_WIDTH), pool.reshape(T, POOL_WIDTH), x.reshape(T, D),
        w_out[0], g_moe[0], w_router[0], b_router[0], capacity)
    x_sorted = _sc_dispatch(h2, dest, N_EXPERTS * capacity)
    visits = _visit_schedule(cnt[:, 0].astype(I32), T * TOP_K, capacity)
    y = _experts(x_sorted, visits, w_gate_up[0], b_gate_up[0], w_down[0], b_down[0])
    y_tok = _sc_gather(y, dest.reshape(-1))
    return _combine(y_tok, x1, gates.T, g_final).reshape(B, S, D)
```

```python
import functools

import jax
import jax.numpy as jnp
from jax import lax
from jax.experimental import pallas as pl
from jax.experimental.pallas import tpu as pltpu
from jax.experimental.pallas import tpu_sc as plsc

F32 = jnp.float32
BF16 = jnp.bfloat16
I32 = jnp.int32
U32 = jnp.uint32

D_MODEL = 1024
SB_HEADS = 8
SB_HEAD_DIM = 64
SB_WIDTH = SB_HEADS * SB_HEAD_DIM
POOL_WINDOWS = (2, 4, 8, 16)
POOL_WIDTH = 512
POOL_GROUP_DIM = 128
N_EXPERTS = 32
TOP_K = 4
D_EXPERT = 1024
SWIGLU_LIMIT = 7.0
SWIGLU_ALPHA = 1.702
RMS_EPS = 1e-5

LANES = 128
HALO = 32
PROJ_TILE = 1024
ATTN_TILE = 256
ATTN_SUBTILES = 4
EXPERT_TILE = 1024
EXPERT_ROW_CLASSES = (1024, 512, 256)
COMBINE_TILE = 1024
ATTN_SKIP_LOG2 = -160.0
LOG2_E = 1.4426950408889634
VMEM_LIMIT = 56 * 1024 * 1024

_NT = (((1,), (1,)), ((), ()))


def _rms(x, g):
    ms = jnp.mean(x * x, axis=-1, keepdims=True)
    return x * lax.rsqrt(ms + RMS_EPS) * g


def _pack_rows(x):
    n = x.shape[1] // 2
    lo = lax.bitcast_convert_type(x[:, :n].astype(BF16).astype(F32), U32)
    hi = lax.bitcast_convert_type(x[:, n:].astype(BF16).astype(F32), U32)
    return (lo >> 16) | (hi & jnp.uint32(0xFFFF0000))


def _unpack_rows(w):
    lo = lax.bitcast_convert_type(w << 16, F32)
    hi = lax.bitcast_convert_type(w & jnp.uint32(0xFFFF0000), F32)
    return lo, hi


def _in_proj_kernel(x_ref, w_ref, pw_ref, ps_ref, q_ref, k_ref, v_ref, p_ref, uext_ref, xb_ref, lvl_ref):
    s = pl.program_id(1)
    tm = x_ref.shape[1]
    x = x_ref[0]
    inv = lax.rsqrt(jnp.mean(x * x, axis=-1, keepdims=True) + RMS_EPS)
    xb_ref[...] = x.astype(BF16)
    proj = jnp.dot(xb_ref[...], w_ref[...], preferred_element_type=F32)
    q_ref[0] = (proj[:, 0:SB_WIDTH] * (inv * (LOG2_E * SB_HEAD_DIM ** -0.5))).astype(BF16)
    k_ref[0] = (proj[:, SB_WIDTH:2 * SB_WIDTH] * inv).astype(BF16)
    v_ref[0] = (proj[:, 2 * SB_WIDTH:3 * SB_WIDTH] * inv).astype(BF16)
    u = proj[:, 3 * SB_WIDTH:] * inv

    lo = HALO // 2

    @pl.when(s == 0)
    def _():
        uext_ref[0:HALO, :] = jnp.zeros((HALO, POOL_WIDTH), F32)
        lvl_ref[:, 0:lo, :] = jnp.zeros((2, lo, POOL_GROUP_DIM), F32)

    uext_ref[HALO:, :] = u
    t = s * tm + lax.broadcasted_iota(I32, (tm, 1), 0)
    for g, w in enumerate(POOL_WINDOWS):
        sl = slice(g * POOL_GROUP_DIM, (g + 1) * POOL_GROUP_DIM)
        lvl_ref[0, lo:, :] = uext_ref[lo:, sl] + uext_ref[lo - 1:HALO + tm - 1, sl]
        cur, k = 0, 2
        while k < w:
            lvl_ref[1 - cur, lo:, :] = lvl_ref[cur, lo:, :] + lvl_ref[cur, lo - k:HALO + tm - k, :]
            cur, k = 1 - cur, 2 * k
        ug = u[:, sl]
        count = jnp.minimum(t + 1, w).astype(F32)
        pooled = lvl_ref[cur, HALO:, :] / count - ug
        mixed = jnp.dot(pooled.astype(BF16), pw_ref[g], preferred_element_type=F32)
        p_ref[0, :, sl] = (mixed * ps_ref[:, sl]).astype(BF16)
    uext_ref[0:HALO, :] = u[tm - HALO:, :]


def _in_proj(x, g_mix, w_in, pool_w, pool_scale):
    B, S, D = x.shape
    tm = PROJ_TILE
    n_out = w_in.shape[1]
    out_sd = jax.ShapeDtypeStruct((B, S, SB_WIDTH), BF16)
    blk = pl.BlockSpec((1, tm, SB_WIDTH), lambda b, s: (b, s, 0))
    return pl.pallas_call(
        _in_proj_kernel,
        grid=(B, S // tm),
        in_specs=[
            pl.BlockSpec((1, tm, D), lambda b, s: (b, s, 0)),
            pl.BlockSpec((D, n_out), lambda b, s: (0, 0)),
            pl.BlockSpec(pool_w.shape, lambda b, s: (0, 0, 0)),
            pl.BlockSpec((1, POOL_WIDTH), lambda b, s: (0, 0)),
        ],
        out_specs=[blk, blk, blk, blk],
        out_shape=[out_sd, out_sd, out_sd, out_sd],
        scratch_shapes=[pltpu.VMEM((HALO + tm, POOL_WIDTH), F32), pltpu.VMEM((tm, D), BF16),
                        pltpu.VMEM((2, HALO + tm, POOL_GROUP_DIM), F32)],
        compiler_params=pltpu.CompilerParams(
            dimension_semantics=("arbitrary", "arbitrary"), vmem_limit_bytes=VMEM_LIMIT),
        name="in_proj_pool",
    )(x, (g_mix[:, None] * w_in).astype(BF16), pool_w.astype(BF16), pool_scale.reshape(1, POOL_WIDTH))


def _attn_kernel(q_ref, k_ref, v_ref, o_ref):
    tq = ATTN_TILE
    tk = tq
    qi = pl.program_id(2)
    lane = lax.broadcasted_iota(I32, (tq, LANES), 1)
    row = lax.broadcasted_iota(I32, (tq, tk), 0)
    col = lax.broadcasted_iota(I32, (tq, tk), 1)
    causal = col < row
    tri = (row > col).astype(BF16)

    def head_rows(r, h):
        rows = q_ref[0, r * tq:(r + 1) * tq, :]
        return jnp.where((lane >= h * SB_HEAD_DIM) & (lane < (h + 1) * SB_HEAD_DIM), rows, jnp.zeros_like(rows))

    def scores(qh, j):
        kb = k_ref[0, pl.ds(pl.multiple_of(j * tk, tk), tk), :]
        return lax.dot_general(qh, kb, _NT, preferred_element_type=F32)

    def values(j):
        return v_ref[0, pl.ds(pl.multiple_of(j * tk, tk), tk), :]

    def log_terms(z, mask):
        neg_abs = lax.bitcast_convert_type(lax.bitcast_convert_type(z, U32) | jnp.uint32(0x80000000), F32)
        nl = jnp.maximum(z, 0.0) + jnp.log2(1.0 + jnp.exp2(neg_abs))
        if mask:
            nl = jnp.where(causal, nl, 0.0)
        return nl.astype(BF16), z - nl, nl[:, 0:1]

    def tile(qh, j, mask):
        nl, lb, nl0 = log_terms(scores(qh, j), mask)
        ntail = jnp.dot(nl, tri, preferred_element_type=F32)
        return lb - ntail, -(ntail[:, 0:1] + nl0), values(j)

    nq = ATTN_SUBTILES
    has_prev = qi > 0
    qs = [head_rows(r, h) for r in range(nq) for h in range(2)]
    chains = []
    for n, qh in enumerate(qs):
        diag = nq * qi + n // 2
        chains += [(qh, diag, True), (qh, jnp.maximum(diag - 1, 0), False)]
    zs = [scores(qh, j) for qh, j, _ in chains]
    terms = [log_terms(z, is_diag) for z, (_, _, is_diag) in zip(zs, chains)]
    nts = [jnp.dot(nl, tri, preferred_element_type=F32) for nl, _, _ in terms]
    lbs = [lb for _, lb, _ in terms]
    tots = [-(nt[:, 0:1] + nl0) for nt, (_, _, nl0) in zip(nts, terms)]
    ws = []
    for n in range(2 * nq):
        d, p = 2 * n, 2 * n + 1
        w_d = jnp.where(causal, jnp.exp2(lbs[d] - nts[d]), 0.0)
        w_p = jnp.exp2(lbs[p] - nts[p] + tots[d])
        if n < 2:
            w_p = jnp.where(has_prev, w_p, 0.0)
        ws += [w_d.astype(BF16), w_p.astype(BF16)]
    pvs = [jnp.dot(w, values(j), preferred_element_type=F32) for w, (_, j, _) in zip(ws, chains)]
    accs = [pvs[2 * n] + pvs[2 * n + 1] for n in range(2 * nq)]
    carries = [tots[2 * n] + tots[2 * n + 1] for n in range(2 * nq)]

    def cond(st):
        j, cs, _ = st
        live = functools.reduce(jnp.maximum, [jnp.max(c) for c in cs])
        return jnp.logical_and(j >= 0, live > ATTN_SKIP_LOG2)

    def body(st):
        j, cs, acs = st
        new_c, new_a = [], []
        for n, (qh, c, a) in enumerate(zip(qs, cs, acs)):
            lag = nq - 1 - n // 2
            if lag:
                c = jnp.where(j >= lag, c, -jnp.inf)
            lw, tot, vb = tile(qh, jnp.maximum(j - lag, 0), False)
            new_a.append(a + jnp.dot(jnp.exp2(lw + c).astype(BF16), vb, preferred_element_type=F32))
            new_c.append(c + tot)
        return j - 1, tuple(new_c), tuple(new_a)

    _, _, accs = lax.while_loop(cond, body, (nq * qi + nq - 3, tuple(carries), tuple(accs)))
    for r in range(nq):
        o_ref[0, r * tq:(r + 1) * tq, :] = jnp.where(
            lane < SB_HEAD_DIM, accs[2 * r], accs[2 * r + 1]).astype(o_ref.dtype)


def _attention(q, k, v):
    B, S, W = q.shape
    tq = ATTN_SUBTILES * ATTN_TILE
    n_pairs = W // LANES
    return pl.pallas_call(
        _attn_kernel,
        grid=(B, n_pairs, S // tq),
        in_specs=[
            pl.BlockSpec((1, tq, LANES), lambda b, p, i: (b, i, p)),
            pl.BlockSpec((1, S, LANES), lambda b, p, i: (b, 0, p)),
            pl.BlockSpec((1, S, LANES), lambda b, p, i: (b, 0, p)),
        ],
        out_specs=pl.BlockSpec((1, tq, LANES), lambda b, p, i: (b, i, p)),
        out_shape=jax.ShapeDtypeStruct((B, S, W), BF16),
        compiler_params=pltpu.CompilerParams(
            dimension_semantics=("arbitrary", "arbitrary", "arbitrary"), vmem_limit_bytes=VMEM_LIMIT),
        name="stickbreak_attn",
    )(q, k, v)


def _out_proj_router_kernel(attn_ref, pool_ref, x_ref, wo_ref, g_ref, wr_ref, br_ref, tri_ref,
                            x1_ref, h2_ref, gate_ref, dest_ref, cnt_ref, carry_ref, *, capacity):
    i = pl.program_id(0)
    tm = x_ref.shape[0]

    @pl.when(i == 0)
    def _():
        carry_ref[...] = jnp.zeros_like(carry_ref)

    mixed = jnp.dot(attn_ref[...], wo_ref[0:SB_WIDTH, :], preferred_element_type=F32)
    mixed = mixed + jnp.dot(pool_ref[...], wo_ref[SB_WIDTH:, :], preferred_element_type=F32)
    x1 = x_ref[...] + mixed
    h2 = _rms(x1, g_ref[...])
    x1_ref[...] = x1
    h2_ref[...] = _pack_rows(h2)

    hh = h2.astype(BF16)
    hl = (h2 - hh.astype(F32)).astype(BF16)
    wr = wr_ref[...]
    wh = wr.astype(BF16)
    wl = (wr - wh.astype(F32)).astype(BF16)
    logits = (lax.dot_general(wh, hh, _NT, preferred_element_type=F32)
              + lax.dot_general(wh, hl, _NT, preferred_element_type=F32)
              + lax.dot_general(wl, hh, _NT, preferred_element_type=F32)) + br_ref[...]

    eid = lax.broadcasted_iota(I32, (N_EXPERTS, tm), 0).astype(F32)
    work = logits
    vals, ids = [], []
    for _ in range(TOP_K):
        m = jnp.max(work, axis=0, keepdims=True)
        sel = jnp.min(jnp.where(work == m, eid, float(N_EXPERTS)), axis=0, keepdims=True)
        vals.append(m)
        ids.append(sel)
        work = jnp.where(eid == sel, -jnp.inf, work)
    exps = [jnp.exp(v - vals[0]) for v in vals]
    denom = exps[0] + exps[1] + exps[2] + exps[3]

    onehot = jnp.zeros((N_EXPERTS, tm), F32)
    for sel in ids:
        onehot = onehot + (eid == sel).astype(F32)
    before = jnp.dot(onehot.astype(BF16), tri_ref[...], preferred_element_type=F32) + carry_ref[...]
    for kk in range(TOP_K):
        gate_ref[kk:kk + 1, :] = exps[kk] / denom
        rk = jnp.sum(jnp.where(eid == ids[kk], before, 0.0), axis=0, keepdims=True)
        dest_ref[kk:kk + 1, :] = (ids[kk] * float(capacity) + rk).astype(I32)
    carry_ref[...] = carry_ref[...] + jnp.sum(onehot, axis=1, keepdims=True)
    cnt_ref[...] = jnp.broadcast_to(carry_ref[...], cnt_ref.shape)


def _out_proj_router(attn, pool, x, w_out, g_moe, w_router, b_router, capacity):
    T, D = x.shape
    tm = PROJ_TILE
    r = lax.broadcasted_iota(I32, (tm, tm), 0)
    c = lax.broadcasted_iota(I32, (tm, tm), 1)
    tri = (r < c).astype(BF16)
    row_blk = lambda w: pl.BlockSpec((tm, w), lambda i: (i, 0))
    fixed = lambda shape: pl.BlockSpec(shape, lambda i: tuple(0 for _ in shape))
    sel_blk = pl.BlockSpec((TOP_K, tm), lambda i: (0, i))
    assert N_EXPERTS * capacity < 2 ** 24
    return pl.pallas_call(
        functools.partial(_out_proj_router_kernel, capacity=capacity),
        grid=(T // tm,),
        in_specs=[row_blk(SB_WIDTH), row_blk(POOL_WIDTH), row_blk(D), fixed((D, D)), fixed((1, D)),
                  fixed((N_EXPERTS, D)), fixed((N_EXPERTS, 1)), fixed((tm, tm))],
        out_specs=[row_blk(D), row_blk(D // 2), sel_blk, sel_blk, fixed((N_EXPERTS, LANES))],
        out_shape=[jax.ShapeDtypeStruct((T, D), F32), jax.ShapeDtypeStruct((T, D // 2), U32),
                   jax.ShapeDtypeStruct((TOP_K, T), F32), jax.ShapeDtypeStruct((TOP_K, T), I32),
                   jax.ShapeDtypeStruct((N_EXPERTS, LANES), F32)],
        scratch_shapes=[pltpu.VMEM((N_EXPERTS, 1), F32)],
        compiler_params=pltpu.CompilerParams(
            dimension_semantics=("arbitrary",), vmem_limit_bytes=VMEM_LIMIT),
        name="out_proj_router",
    )(attn, pool, x, w_out.astype(BF16), g_moe.reshape(1, D), w_router.T, b_router.reshape(N_EXPERTS, 1), tri)


SC_ROWS_PER_COPY = 64


def _sc_workers():
    info = plsc.get_sparse_core_info()
    return info.num_cores, info.num_cores * info.num_subcores


def _sc_dispatch(rows, dest, n_out):
    T, W = rows.shape
    K = dest.shape[0]
    sub = SC_ROWS_PER_COPY
    n_cores, n_workers = _sc_workers()
    per_w = T // n_workers
    n_chunks = per_w // sub
    assert per_w * n_workers == T and n_chunks * sub == per_w and n_chunks % 2 == 0
    idx = dest.reshape(K, n_workers, n_chunks, sub).transpose(1, 2, 0, 3).reshape(n_workers, n_chunks * K, sub)
    mesh = plsc.VectorSubcoreMesh(core_axis_name="core", subcore_axis_name="subcore")

    @functools.partial(
        pl.kernel, out_type=jax.ShapeDtypeStruct((n_out, W), rows.dtype), mesh=mesh,
        scratch_types=[pltpu.VMEM((n_chunks * K, sub), I32), pltpu.VMEM((2, sub, W), rows.dtype),
                       pltpu.SemaphoreType.DMA((2,)), pltpu.SemaphoreType.DMA((2,))])
    def scatter_rows(x_hbm, i_hbm, o_hbm, idx_v, buf, rsem, wsem):
        wid = lax.axis_index("subcore") * n_cores + lax.axis_index("core")
        base = wid * per_w
        pltpu.sync_copy(i_hbm.at[wid], idx_v)

        def read(c, slot):
            return pltpu.make_async_copy(x_hbm.at[pl.ds(base + c * sub, sub)], buf.at[slot], rsem.at[slot])

        def write(c, kk, slot):
            return pltpu.make_async_copy(buf.at[slot], o_hbm.at[idx_v.at[c * K + kk]], wsem.at[slot])

        read(0, 0).start()

        @pl.loop(0, n_chunks, step=2)
        def _(c0):
            for b in range(2):
                c = c0 + b
                read(c, b).wait()
                for kk in range(K):
                    write(c, kk, b).start()

                @pl.when(c + 1 < n_chunks)
                def _():
                    @pl.when(c >= 1)
                    def _():
                        for kk in range(K):
                            write(c - 1, kk, 1 - b).wait()
                    read(c + 1, 1 - b).start()

        for kk in range(K):
            write(n_chunks - 2, kk, 0).wait()
            write(n_chunks - 1, kk, 1).wait()

    return scatter_rows(rows, idx)


def _sc_gather(table, indices):
    M = indices.shape[0]
    W = table.shape[1]
    sub = SC_ROWS_PER_COPY
    n_cores, n_workers = _sc_workers()
    per_w = M // n_workers
    n_steps = per_w // sub
    assert per_w * n_workers == M and n_steps * sub == per_w and n_steps % 2 == 0
    mesh = plsc.VectorSubcoreMesh(core_axis_name="core", subcore_axis_name="subcore")

    @functools.partial(
        pl.kernel, out_type=jax.ShapeDtypeStruct((M, W), table.dtype), mesh=mesh,
        scratch_types=[pltpu.VMEM((n_steps, sub), I32), pltpu.VMEM((2, sub, W), table.dtype),
                       pltpu.SemaphoreType.DMA((2,)), pltpu.SemaphoreType.DMA((2,))])
    def gather_rows(x_hbm, i_hbm, o_hbm, idx_v, buf, gsem, wsem):
        wid = lax.axis_index("subcore") * n_cores + lax.axis_index("core")
        base = wid * per_w
        pltpu.sync_copy(i_hbm.at[wid], idx_v)

        def gather(s, slot):
            return pltpu.make_async_copy(x_hbm.at[idx_v.at[s]], buf.at[slot], gsem.at[slot])

        def write(s, slot):
            return pltpu.make_async_copy(buf.at[slot], o_hbm.at[pl.ds(base + s * sub, sub)], wsem.at[slot])

        gather(0, 0).start()

        @pl.loop(0, n_steps, step=2)
        def _(s0):
            for b in range(2):
                s = s0 + b
                gather(s, b).wait()
                write(s, b).start()

                @pl.when(s + 1 < n_steps)
                def _():
                    @pl.when(s >= 1)
                    def _():
                        write(s - 1, 1 - b).wait()
                    gather(s + 1, 1 - b).start()

        write(n_steps - 2, 0).wait()
        write(n_steps - 1, 1).wait()

    return gather_rows(table, indices.reshape(n_workers, n_steps, sub))


def _expert_kernel(vb_ref, ve_ref, vrows_ref, nvis_ref, nxt_ref,
                   x_ref, wgu_hbm, bgu_ref, wd_hbm, bd_ref, y_ref,
                   wgu_stage, wd_stage, wgu_bf, wd_bf, sem):
    v = pl.program_id(0)
    real = v < nvis_ref[0]
    e = ve_ref[v]
    first_of_expert = jnp.logical_or(v == 0, e != ve_ref[jnp.maximum(v - 1, 0)])

    def weight_copies(expert):
        return (pltpu.make_async_copy(wgu_hbm.at[expert], wgu_stage, sem.at[0]),
                pltpu.make_async_copy(wd_hbm.at[expert], wd_stage, sem.at[1]))

    @pl.when(v == 0)
    def _():
        for c in weight_copies(e):
            c.start()

    def load_weights():
        for c in weight_copies(e):
            c.wait()
        wgu_bf[...] = wgu_stage[...].astype(BF16)
        wd_bf[...] = wd_stage[...].astype(BF16)

    def fetch_next():
        nxt = nxt_ref[e]

        @pl.when(nxt >= 0)
        def _():
            for c in weight_copies(nxt):
                c.start()

    def run(n_rows):
        half = wgu_bf.shape[0] // 2
        x_lo, x_hi = _unpack_rows(x_ref[0:n_rows, :])
        gu = (jnp.dot(x_lo.astype(BF16), wgu_bf[:half, :], preferred_element_type=F32)
              + jnp.dot(x_hi.astype(BF16), wgu_bf[half:, :], preferred_element_type=F32)) + bgu_ref[0]
        gate = jnp.minimum(gu[:, :D_EXPERT], SWIGLU_LIMIT)
        up = jnp.clip(gu[:, D_EXPERT:], -SWIGLU_LIMIT, SWIGLU_LIMIT)
        act = gate * jax.nn.sigmoid(SWIGLU_ALPHA * gate) * (up + 1.0)
        y = jnp.dot(act.astype(BF16), wd_bf[...], preferred_element_type=F32) + bd_ref[0]
        y_ref[0:n_rows, :] = _pack_rows(y)

    rows = vrows_ref[v]
    full = EXPERT_ROW_CLASSES[0]
    first = jnp.logical_and(first_of_expert, real)

    @pl.when(jnp.logical_and(first, rows == full))
    def _():
        load_weights()
        run(full)
        fetch_next()

    @pl.when(jnp.logical_and(first, rows != full))
    def _():
        load_weights()
        fetch_next()

    later_full = jnp.logical_and(real, jnp.logical_not(first_of_expert))
    pl.when(jnp.logical_and(later_full, rows == full))(functools.partial(run, full))
    for n_rows in EXPERT_ROW_CLASSES[1:]:
        pl.when(jnp.logical_and(real, rows == n_rows))(functools.partial(run, n_rows))


def _experts(x_sorted, visits, w_gate_up, b_gate_up, w_down, b_down):
    n_rows, W = x_sorted.shape
    D = 2 * W
    tm = EXPERT_TILE
    n_visits = visits[0].shape[0]
    n_prefetch = len(visits)
    by_block = lambda v, vb, *_: (vb[v], 0)
    by_expert = lambda v, vb, ve, *_: (ve[v], 0, 0)
    grid_spec = pltpu.PrefetchScalarGridSpec(
        num_scalar_prefetch=n_prefetch,
        grid=(n_visits,),
        in_specs=[
            pl.BlockSpec((tm, W), by_block),
            pl.BlockSpec(memory_space=pl.ANY),
            pl.BlockSpec((1, 1, 2 * D_EXPERT), by_expert),
            pl.BlockSpec(memory_space=pl.ANY),
            pl.BlockSpec((1, 1, D), by_expert),
        ],
        out_specs=pl.BlockSpec((tm, W), by_block),
        scratch_shapes=[pltpu.VMEM((D, 2 * D_EXPERT), F32), pltpu.VMEM((D_EXPERT, D), F32),
                        pltpu.VMEM((D, 2 * D_EXPERT), BF16), pltpu.VMEM((D_EXPERT, D), BF16),
                        pltpu.SemaphoreType.DMA((2,))],
    )
    return pl.pallas_call(
        _expert_kernel,
        grid_spec=grid_spec,
        out_shape=jax.ShapeDtypeStruct((n_rows, W), U32),
        compiler_params=pltpu.CompilerParams(
            dimension_semantics=("arbitrary",), vmem_limit_bytes=VMEM_LIMIT),
        name="expert_gmm",
    )(*visits, x_sorted, w_gate_up, b_gate_up.reshape(N_EXPERTS, 1, -1), w_down, b_down.reshape(N_EXPERTS, 1, -1))


def _visit_schedule(counts, n_assign, capacity):
    tm = EXPERT_TILE
    assert EXPERT_ROW_CLASSES[0] == tm and list(EXPERT_ROW_CLASSES) == sorted(EXPERT_ROW_CLASSES, reverse=True)
    n_visits = n_assign // tm + N_EXPERTS
    n_full = counts // tm
    tail = counts - n_full * tm
    tail_rows = jnp.zeros_like(tail)
    for rows in EXPERT_ROW_CLASSES:
        tail_rows = jnp.where(tail <= rows, rows, tail_rows)
    per_e = n_full + (tail > 0)
    vend = jnp.cumsum(per_e)
    nvis = vend[-1]
    vc = jnp.minimum(jnp.arange(n_visits, dtype=I32), nvis - 1)
    done = vc[:, None] >= vend[None, :]
    e = jnp.sum(done, axis=1).astype(I32)
    local = vc - jnp.sum(jnp.where(done, per_e[None, :], 0), axis=1)
    mine = e[:, None] == jnp.arange(N_EXPERTS, dtype=I32)[None, :]
    is_tail = local >= jnp.sum(jnp.where(mine, n_full[None, :], 0), axis=1)
    vrows = jnp.where(is_tail, jnp.sum(jnp.where(mine, tail_rows[None, :], 0), axis=1), tm)
    blk = e * (capacity // tm) + local
    ids = jnp.arange(N_EXPERTS, dtype=I32)
    nonempty = counts > 0
    later = jnp.logical_and(nonempty[None, :], ids[None, :] > ids[:, None])
    nxt = jnp.where(jnp.any(later, axis=1), jnp.argmax(later, axis=1), -1).astype(I32)
    return blk.astype(I32), e, vrows.astype(I32), nvis.reshape(1).astype(I32), nxt


def _combine_kernel(y0_ref, y1_ref, y2_ref, y3_ref, x1_ref, gate_ref, g_ref, o_ref):
    half = x1_ref.shape[1] // 2
    gates = gate_ref[...]
    lo = x1_ref[:, :half]
    hi = x1_ref[:, half:]
    for kk, y_ref in enumerate((y0_ref, y1_ref, y2_ref, y3_ref)):
        y_lo, y_hi = _unpack_rows(y_ref[...])
        lo = lo + gates[:, kk:kk + 1] * y_lo
        hi = hi + gates[:, kk:kk + 1] * y_hi
    ms = (jnp.sum(lo * lo, axis=-1, keepdims=True) + jnp.sum(hi * hi, axis=-1, keepdims=True)) / (2 * half)
    scale = lax.rsqrt(ms + RMS_EPS)
    o_ref[:, :half] = lo * scale * g_ref[:, :half]
    o_ref[:, half:] = hi * scale * g_ref[:, half:]


def _combine(y_tok, x1, gates_t, g_final):
    T, D = x1.shape
    tm = COMBINE_TILE
    nt = T // tm
    y_spec = lambda kk: pl.BlockSpec((tm, D // 2), lambda i: (kk * nt + i, 0))
    return pl.pallas_call(
        _combine_kernel,
        grid=(nt,),
        in_specs=[y_spec(0), y_spec(1), y_spec(2), y_spec(3),
                  pl.BlockSpec((tm, D), lambda i: (i, 0)),
                  pl.BlockSpec((tm, TOP_K), lambda i: (i, 0)),
                  pl.BlockSpec((1, D), lambda i: (0, 0))],
        out_specs=pl.BlockSpec((tm, D), lambda i: (i, 0)),
        out_shape=jax.ShapeDtypeStruct((T, D), F32),
        compiler_params=pltpu.CompilerParams(
            dimension_semantics=("arbitrary",), vmem_limit_bytes=VMEM_LIMIT),
        name="combine_norm",
    )(y_tok, y_tok, y_tok, y_tok, x1, gates_t, g_final.reshape(1, D))


def kernel(x, g_mix, w_in, pool_w, pool_scale, w_out, g_moe, w_router, b_router, w_gate_up, b_gate_up,
           w_down, b_down, g_final):
    B, S, D = x.shape
    T = B * S
    assert g_mix.shape[0] == 1, "single-layer problem: the final norm is fused into the combine step"
    q, k, v, pool = _in_proj(x, g_mix[0], w_in[0], pool_w[0], pool_scale[0])
    attn = _attention(q, k, v)
    capacity = T
    x1, h2, gates, dest, cnt = _out_proj_router(
        attn.reshape(T, SB_WIDTH), pool.reshape(T, POOL_WIDTH), x.reshape(T, D),
        w_out[0], g_moe[0], w_router[0], b_router[0], capacity)
    x_sorted = _sc_dispatch(h2, dest, N_EXPERTS * capacity)
    visits = _visit_schedule(cnt[:, 0].astype(I32), T * TOP_K, capacity)
    y = _experts(x_sorted, visits, w_gate_up[0], b_gate_up[0], w_down[0], b_down[0])
    y_tok = _sc_gather(y, dest.reshape(-1))
    return _combine(y_tok, x1, gates.T, g_final).reshape(B, S, D)
```

```python
import functools

import jax
import jax.numpy as jnp
from jax import lax
from jax.experimental import pallas as pl
from jax.experimental.pallas import tpu as pltpu
from jax.experimental.pallas import tpu_sc as plsc

F32 = jnp.float32
BF16 = jnp.bfloat16
I32 = jnp.int32
U32 = jnp.uint32

D_MODEL = 1024
SB_HEADS = 8
SB_HEAD_DIM = 64
SB_WIDTH = SB_HEADS * SB_HEAD_DIM
POOL_WINDOWS = (2, 4, 8, 16)
POOL_WIDTH = 512
POOL_GROUP_DIM = 128
N_EXPERTS = 32
TOP_K = 4
D_EXPERT = 1024
SWIGLU_LIMIT = 7.0
SWIGLU_ALPHA = 1.702
RMS_EPS = 1e-5

LANES = 128
HALO = 32
PROJ_TILE = 1024
ATTN_TILE = 256
ATTN_SUBTILES = 4
EXPERT_TILE = 1024
EXPERT_ROW_CLASSES = (1024, 512, 256)
COMBINE_TILE = 1024
ATTN_SKIP_LOG2 = -160.0
LOG2_E = 1.4426950408889634
VMEM_LIMIT = 56 * 1024 * 1024

_NT = (((1,), (1,)), ((), ()))


def _rms(x, g):
    ms = jnp.mean(x * x, axis=-1, keepdims=True)
    return x * lax.rsqrt(ms + RMS_EPS) * g


def _pack_rows(x):
    n = x.shape[1] // 2
    lo = lax.bitcast_convert_type(x[:, :n].astype(BF16).astype(F32), U32)
    hi = lax.bitcast_convert_type(x[:, n:].astype(BF16).astype(F32), U32)
    return (lo >> 16) | (hi & jnp.uint32(0xFFFF0000))


def _unpack_rows(w):
    lo = lax.bitcast_convert_type(w << 16, F32)
    hi = lax.bitcast_convert_type(w & jnp.uint32(0xFFFF0000), F32)
    return lo, hi


def _in_proj_kernel(x_ref, w_ref, pw_ref, ps_ref, q_ref, k_ref, v_ref, p_ref, uext_ref, xb_ref, lvl_ref):
    s = pl.program_id(0)
    tm = x_ref.shape[1]
    x = x_ref[0]
    inv = lax.rsqrt(jnp.mean(x * x, axis=-1, keepdims=True) + RMS_EPS)
    xb_ref[...] = x.astype(BF16)
    proj = jnp.dot(xb_ref[...], w_ref[...], preferred_element_type=F32)
    q_ref[0] = (proj[:, 0:SB_WIDTH] * (inv * (LOG2_E * SB_HEAD_DIM ** -0.5))).astype(BF16)
    k_ref[0] = (proj[:, SB_WIDTH:2 * SB_WIDTH] * inv).astype(BF16)
    v_ref[0] = (proj[:, 2 * SB_WIDTH:3 * SB_WIDTH] * inv).astype(BF16)
    u = proj[:, 3 * SB_WIDTH:] * inv

    lo = HALO // 2

    @pl.when(s == 0)
    def _():
        uext_ref[0:HALO, :] = jnp.zeros((HALO, POOL_WIDTH), F32)
        lvl_ref[:, 0:lo, :] = jnp.zeros((2, lo, POOL_GROUP_DIM), F32)

    uext_ref[HALO:, :] = u
    t = s * tm + lax.broadcasted_iota(I32, (tm, 1), 0)
    for g, w in enumerate(POOL_WINDOWS):
        sl = slice(g * POOL_GROUP_DIM, (g + 1) * POOL_GROUP_DIM)
        lvl_ref[0, lo:, :] = uext_ref[lo:, sl] + uext_ref[lo - 1:HALO + tm - 1, sl]
        cur, k = 0, 2
        while k < w:
            lvl_ref[1 - cur, lo:, :] = lvl_ref[cur, lo:, :] + lvl_ref[cur, lo - k:HALO + tm - k, :]
            cur, k = 1 - cur, 2 * k
        ug = u[:, sl]
        count = jnp.minimum(t + 1, w).astype(F32)
        pooled = lvl_ref[cur, HALO:, :] / count - ug
        mixed = jnp.dot(pooled.astype(BF16), pw_ref[g], preferred_element_type=F32)
        p_ref[0, :, sl] = (mixed * ps_ref[:, sl]).astype(BF16)
    uext_ref[0:HALO, :] = u[tm - HALO:, :]


def _in_proj(x, b, w_folded, pool_w, pool_scale):
    _, S, D = x.shape
    tm = PROJ_TILE
    n_out = w_folded.shape[1]
    out_sd = jax.ShapeDtypeStruct((1, S, SB_WIDTH), BF16)
    blk = pl.BlockSpec((1, tm, SB_WIDTH), lambda s: (0, s, 0))
    return pl.pallas_call(
        _in_proj_kernel,
        grid=(S // tm,),
        in_specs=[
            pl.BlockSpec((1, tm, D), lambda s: (b, s, 0)),
            pl.BlockSpec((D, n_out), lambda s: (0, 0)),
            pl.BlockSpec(pool_w.shape, lambda s: (0, 0, 0)),
            pl.BlockSpec((1, POOL_WIDTH), lambda s: (0, 0)),
        ],
        out_specs=[blk, blk, blk, blk],
        out_shape=[out_sd, out_sd, out_sd, out_sd],
        scratch_shapes=[pltpu.VMEM((HALO + tm, POOL_WIDTH), F32), pltpu.VMEM((tm, D), BF16),
                        pltpu.VMEM((2, HALO + tm, POOL_GROUP_DIM), F32)],
        compiler_params=pltpu.CompilerParams(
            dimension_semantics=("arbitrary",), vmem_limit_bytes=VMEM_LIMIT),
        name="in_proj_pool",
    )(x, w_folded, pool_w, pool_scale.reshape(1, POOL_WIDTH))


def _attn_kernel(q_ref, k_ref, v_ref, o_ref):
    tq = ATTN_TILE
    tk = tq
    qi = pl.program_id(2)
    lane = lax.broadcasted_iota(I32, (tq, LANES), 1)
    row = lax.broadcasted_iota(I32, (tq, tk), 0)
    col = lax.broadcasted_iota(I32, (tq, tk), 1)
    causal = col < row
    tri = (row > col).astype(BF16)

    def head_rows(r, h):
        rows = q_ref[0, r * tq:(r + 1) * tq, :]
        return jnp.where((lane >= h * SB_HEAD_DIM) & (lane < (h + 1) * SB_HEAD_DIM), rows, jnp.zeros_like(rows))

    def scores(qh, j):
        kb = k_ref[0, pl.ds(pl.multiple_of(j * tk, tk), tk), :]
        return lax.dot_general(qh, kb, _NT, preferred_element_type=F32)

    def values(j):
        return v_ref[0, pl.ds(pl.multiple_of(j * tk, tk), tk), :]

    def log_terms(z, mask):
        neg_abs = lax.bitcast_convert_type(lax.bitcast_convert_type(z, U32) | jnp.uint32(0x80000000), F32)
        nl = jnp.maximum(z, 0.0) + jnp.log2(1.0 + jnp.exp2(neg_abs))
        if mask:
            nl = jnp.where(causal, nl, 0.0)
        return nl.astype(BF16), z - nl, nl[:, 0:1]

    def tile(qh, j, mask):
        nl, lb, nl0 = log_terms(scores(qh, j), mask)
        ntail = jnp.dot(nl, tri, preferred_element_type=F32)
        return lb - ntail, -(ntail[:, 0:1] + nl0), values(j)

    nq = ATTN_SUBTILES
    has_prev = qi > 0
    qs = [head_rows(r, h) for r in range(nq) for h in range(2)]
    chains = []
    for n, qh in enumerate(qs):
        diag = nq * qi + n // 2
        chains += [(qh, diag, True), (qh, jnp.maximum(diag - 1, 0), False)]
    zs = [scores(qh, j) for qh, j, _ in chains]
    terms = [log_terms(z, is_diag) for z, (_, _, is_diag) in zip(zs, chains)]
    nts = [jnp.dot(nl, tri, preferred_element_type=F32) for nl, _, _ in terms]
    lbs = [lb for _, lb, _ in terms]
    tots = [-(nt[:, 0:1] + nl0) for nt, (_, _, nl0) in zip(nts, terms)]
    ws = []
    for n in range(2 * nq):
        d, p = 2 * n, 2 * n + 1
        w_d = jnp.where(causal, jnp.exp2(lbs[d] - nts[d]), 0.0)
        w_p = jnp.exp2(lbs[p] - nts[p] + tots[d])
        if n < 2:
            w_p = jnp.where(has_prev, w_p, 0.0)
        ws += [w_d.astype(BF16), w_p.astype(BF16)]
    pvs = [jnp.dot(w, values(j), preferred_element_type=F32) for w, (_, j, _) in zip(ws, chains)]
    accs = [pvs[2 * n] + pvs[2 * n + 1] for n in range(2 * nq)]
    carries = [tots[2 * n] + tots[2 * n + 1] for n in range(2 * nq)]

    def cond(st):
        j, cs, _ = st
        live = functools.reduce(jnp.maximum, [jnp.max(c) for c in cs])
        return jnp.logical_and(j >= 0, live > ATTN_SKIP_LOG2)

    def body(st):
        j, cs, acs = st
        new_c, new_a = [], []
        for n, (qh, c, a) in enumerate(zip(qs, cs, acs)):
            lag = nq - 1 - n // 2
            if lag:
                c = jnp.where(j >= lag, c, -jnp.inf)
            lw, tot, vb = tile(qh, jnp.maximum(j - lag, 0), False)
            new_a.append(a + jnp.dot(jnp.exp2(lw + c).astype(BF16), vb, preferred_element_type=F32))
            new_c.append(c + tot)
        return j - 1, tuple(new_c), tuple(new_a)

    _, _, accs = lax.while_loop(cond, body, (nq * qi + nq - 3, tuple(carries), tuple(accs)))
    for r in range(nq):
        o_ref[0, r * tq:(r + 1) * tq, :] = jnp.where(
            lane < SB_HEAD_DIM, accs[2 * r], accs[2 * r + 1]).astype(o_ref.dtype)


def _attention(q, k, v):
    B, S, W = q.shape
    tq = ATTN_SUBTILES * ATTN_TILE
    n_pairs = W // LANES
    return pl.pallas_call(
        _attn_kernel,
        grid=(B, n_pairs, S // tq),
        in_specs=[
            pl.BlockSpec((1, tq, LANES), lambda b, p, i: (b, i, p)),
            pl.BlockSpec((1, S, LANES), lambda b, p, i: (b, 0, p)),
            pl.BlockSpec((1, S, LANES), lambda b, p, i: (b, 0, p)),
        ],
        out_specs=pl.BlockSpec((1, tq, LANES), lambda b, p, i: (b, i, p)),
        out_shape=jax.ShapeDtypeStruct((B, S, W), BF16),
        compiler_params=pltpu.CompilerParams(
            dimension_semantics=("arbitrary", "arbitrary", "arbitrary"), vmem_limit_bytes=VMEM_LIMIT),
        name="stickbreak_attn",
    )(q, k, v)


def _out_proj_router_kernel(attn_ref, pool_ref, x_ref, wo_ref, g_ref, wr_ref, br_ref, tri_ref,
                            x1_ref, h2_ref, gate_ref, dest_ref, cnt_ref, carry_ref, *, capacity):
    i = pl.program_id(0)
    tm = x_ref.shape[0]

    @pl.when(i == 0)
    def _():
        carry_ref[...] = jnp.zeros_like(carry_ref)

    mixed = jnp.dot(attn_ref[...], wo_ref[0:SB_WIDTH, :], preferred_element_type=F32)
    mixed = mixed + jnp.dot(pool_ref[...], wo_ref[SB_WIDTH:, :], preferred_element_type=F32)
    x1 = x_ref[...] + mixed
    h2 = _rms(x1, g_ref[...])
    x1_ref[...] = x1
    h2_ref[...] = _pack_rows(h2)

    hh = h2.astype(BF16)
    hl = (h2 - hh.astype(F32)).astype(BF16)
    wr = wr_ref[...]
    wh = wr.astype(BF16)
    wl = (wr - wh.astype(F32)).astype(BF16)
    logits = (lax.dot_general(wh, hh, _NT, preferred_element_type=F32)
              + lax.dot_general(wh, hl, _NT, preferred_element_type=F32)
              + lax.dot_general(wl, hh, _NT, preferred_element_type=F32)) + br_ref[...]

    eid = lax.broadcasted_iota(I32, (N_EXPERTS, tm), 0).astype(F32)
    work = logits
    vals, ids = [], []
    for _ in range(TOP_K):
        m = jnp.max(work, axis=0, keepdims=True)
        sel = jnp.min(jnp.where(work == m, eid, float(N_EXPERTS)), axis=0, keepdims=True)
        vals.append(m)
        ids.append(sel)
        work = jnp.where(eid == sel, -jnp.inf, work)
    exps = [jnp.exp(v - vals[0]) for v in vals]
    denom = exps[0] + exps[1] + exps[2] + exps[3]

    onehot = jnp.zeros((N_EXPERTS, tm), F32)
    for sel in ids:
        onehot = onehot + (eid == sel).astype(F32)
    before = jnp.dot(onehot.astype(BF16), tri_ref[...], preferred_element_type=F32) + carry_ref[...]
    for kk in range(TOP_K):
        gate_ref[kk:kk + 1, :] = exps[kk] / denom
        rk = jnp.sum(jnp.where(eid == ids[kk], before, 0.0), axis=0, keepdims=True)
        dest_ref[kk:kk + 1, :] = (ids[kk] * float(capacity) + rk).astype(I32)
    carry_ref[...] = carry_ref[...] + jnp.sum(onehot, axis=1, keepdims=True)
    cnt_ref[...] = jnp.broadcast_to(carry_ref[...], cnt_ref.shape)


def _out_proj_router(attn, pool, x, row_block0, w_out, g_moe, w_router, b_router, capacity):
    T, D = attn.shape[0], x.shape[1]
    tm = PROJ_TILE
    r = lax.broadcasted_iota(I32, (tm, tm), 0)
    c = lax.broadcasted_iota(I32, (tm, tm), 1)
    tri = (r < c).astype(BF16)
    row_blk = lambda w: pl.BlockSpec((tm, w), lambda i: (i, 0))
    fixed = lambda shape: pl.BlockSpec(shape, lambda i: tuple(0 for _ in shape))
    sel_blk = pl.BlockSpec((TOP_K, tm), lambda i: (0, i))
    assert N_EXPERTS * capacity < 2 ** 24
    return pl.pallas_call(
        functools.partial(_out_proj_router_kernel, capacity=capacity),
        grid=(T // tm,),
        in_specs=[row_blk(SB_WIDTH), row_blk(POOL_WIDTH), pl.BlockSpec((tm, D), lambda i: (row_block0 + i, 0)),
                  fixed((D, D)), fixed((1, D)), fixed((N_EXPERTS, D)), fixed((N_EXPERTS, 1)), fixed((tm, tm))],
        out_specs=[row_blk(D), row_blk(D // 2), sel_blk, sel_blk, fixed((N_EXPERTS, LANES))],
        out_shape=[jax.ShapeDtypeStruct((T, D), F32), jax.ShapeDtypeStruct((T, D // 2), U32),
                   jax.ShapeDtypeStruct((TOP_K, T), F32), jax.ShapeDtypeStruct((TOP_K, T), I32),
                   jax.ShapeDtypeStruct((N_EXPERTS, LANES), F32)],
        scratch_shapes=[pltpu.VMEM((N_EXPERTS, 1), F32)],
        compiler_params=pltpu.CompilerParams(
            dimension_semantics=("arbitrary",), vmem_limit_bytes=VMEM_LIMIT),
        name="out_proj_router",
    )(attn, pool, x, w_out, g_moe.reshape(1, D), w_router.T, b_router.reshape(N_EXPERTS, 1), tri)


SC_ROWS_PER_COPY = 64


def _sc_workers():
    info = plsc.get_sparse_core_info()
    return info.num_cores, info.num_cores * info.num_subcores


def _sc_dispatch(rows, dest, n_out):
    T, W = rows.shape
    K = dest.shape[0]
    sub = SC_ROWS_PER_COPY
    n_cores, n_workers = _sc_workers()
    per_w = T // n_workers
    n_chunks = per_w // sub
    assert per_w * n_workers == T and n_chunks * sub == per_w and n_chunks % 2 == 0
    idx = dest.reshape(K, n_workers, n_chunks, sub).transpose(1, 2, 0, 3).reshape(n_workers, n_chunks * K, sub)
    mesh = plsc.VectorSubcoreMesh(core_axis_name="core", subcore_axis_name="subcore")

    @functools.partial(
        pl.kernel, out_type=jax.ShapeDtypeStruct((n_out, W), rows.dtype), mesh=mesh,
        scratch_types=[pltpu.VMEM((n_chunks * K, sub), I32), pltpu.VMEM((2, sub, W), rows.dtype),
                       pltpu.SemaphoreType.DMA((2,)), pltpu.SemaphoreType.DMA((2,))])
    def scatter_rows(x_hbm, i_hbm, o_hbm, idx_v, buf, rsem, wsem):
        wid = lax.axis_index("subcore") * n_cores + lax.axis_index("core")
        base = wid * per_w
        pltpu.sync_copy(i_hbm.at[wid], idx_v)

        def read(c, slot):
            return pltpu.make_async_copy(x_hbm.at[pl.ds(base + c * sub, sub)], buf.at[slot], rsem.at[slot])

        def write(c, kk, slot):
            return pltpu.make_async_copy(buf.at[slot], o_hbm.at[idx_v.at[c * K + kk]], wsem.at[slot])

        read(0, 0).start()

        @pl.loop(0, n_chunks, step=2)
        def _(c0):
            for b in range(2):
                c = c0 + b
                read(c, b).wait()
                for kk in range(K):
                    write(c, kk, b).start()

                @pl.when(c + 1 < n_chunks)
                def _():
                    @pl.when(c >= 1)
                    def _():
                        for kk in range(K):
                            write(c - 1, kk, 1 - b).wait()
                    read(c + 1, 1 - b).start()

        for kk in range(K):
            write(n_chunks - 2, kk, 0).wait()
            write(n_chunks - 1, kk, 1).wait()

    return scatter_rows(rows, idx)


def _sc_gather(table, indices):
    M = indices.shape[0]
    W = table.shape[1]
    sub = SC_ROWS_PER_COPY
    n_cores, n_workers = _sc_workers()
    per_w = M // n_workers
    n_steps = per_w // sub
    assert per_w * n_workers == M and n_steps * sub == per_w and n_steps % 2 == 0
    mesh = plsc.VectorSubcoreMesh(core_axis_name="core", subcore_axis_name="subcore")

    @functools.partial(
        pl.kernel, out_type=jax.ShapeDtypeStruct((M, W), table.dtype), mesh=mesh,
        scratch_types=[pltpu.VMEM((n_steps, sub), I32), pltpu.VMEM((2, sub, W), table.dtype),
                       pltpu.SemaphoreType.DMA((2,)), pltpu.SemaphoreType.DMA((2,))])
    def gather_rows(x_hbm, i_hbm, o_hbm, idx_v, buf, gsem, wsem):
        wid = lax.axis_index("subcore") * n_cores + lax.axis_index("core")
        base = wid * per_w
        pltpu.sync_copy(i_hbm.at[wid], idx_v)

        def gather(s, slot):
            return pltpu.make_async_copy(x_hbm.at[idx_v.at[s]], buf.at[slot], gsem.at[slot])

        def write(s, slot):
            return pltpu.make_async_copy(buf.at[slot], o_hbm.at[pl.ds(base + s * sub, sub)], wsem.at[slot])

        gather(0, 0).start()

        @pl.loop(0, n_steps, step=2)
        def _(s0):
            for b in range(2):
                s = s0 + b
                gather(s, b).wait()
                write(s, b).start()

                @pl.when(s + 1 < n_steps)
                def _():
                    @pl.when(s >= 1)
                    def _():
                        write(s - 1, 1 - b).wait()
                    gather(s + 1, 1 - b).start()

        write(n_steps - 2, 0).wait()
        write(n_steps - 1, 1).wait()

    return gather_rows(table, indices.reshape(n_workers, n_steps, sub))


def _expert_kernel(vb_ref, ve_ref, vrows_ref, nvis_ref, nxt_ref,
                   x_ref, wgu_hbm, bgu_ref, wd_hbm, bd_ref, y_ref,
                   wgu_stage, wd_stage, wgu_bf, wd_bf, sem):
    v = pl.program_id(0)
    real = v < nvis_ref[0]
    e = ve_ref[v]
    first_of_expert = jnp.logical_or(v == 0, e != ve_ref[jnp.maximum(v - 1, 0)])

    def weight_copies(expert):
        return (pltpu.make_async_copy(wgu_hbm.at[expert], wgu_stage, sem.at[0]),
                pltpu.make_async_copy(wd_hbm.at[expert], wd_stage, sem.at[1]))

    @pl.when(v == 0)
    def _():
        for c in weight_copies(e):
            c.start()

    def load_weights():
        for c in weight_copies(e):
            c.wait()
        wgu_bf[...] = wgu_stage[...].astype(BF16)
        wd_bf[...] = wd_stage[...].astype(BF16)

    def fetch_next():
        nxt = nxt_ref[e]

        @pl.when(nxt >= 0)
        def _():
            for c in weight_copies(nxt):
                c.start()

    def run(n_rows):
        half = wgu_bf.shape[0] // 2
        x_lo, x_hi = _unpack_rows(x_ref[0:n_rows, :])
        gu = (jnp.dot(x_lo.astype(BF16), wgu_bf[:half, :], preferred_element_type=F32)
              + jnp.dot(x_hi.astype(BF16), wgu_bf[half:, :], preferred_element_type=F32)) + bgu_ref[0]
        gate = jnp.minimum(gu[:, :D_EXPERT], SWIGLU_LIMIT)
        up = jnp.clip(gu[:, D_EXPERT:], -SWIGLU_LIMIT, SWIGLU_LIMIT)
        act = gate * jax.nn.sigmoid(SWIGLU_ALPHA * gate) * (up + 1.0)
        y = jnp.dot(act.astype(BF16), wd_bf[...], preferred_element_type=F32) + bd_ref[0]
        y_ref[0:n_rows, :] = _pack_rows(y)

    rows = vrows_ref[v]
    full = EXPERT_ROW_CLASSES[0]
    first = jnp.logical_and(first_of_expert, real)

    @pl.when(jnp.logical_and(first, rows == full))
    def _():
        load_weights()
        run(full)
        fetch_next()

    @pl.when(jnp.logical_and(first, rows != full))
    def _():
        load_weights()
        fetch_next()

    later_full = jnp.logical_and(real, jnp.logical_not(first_of_expert))
    pl.when(jnp.logical_and(later_full, rows == full))(functools.partial(run, full))
    for n_rows in EXPERT_ROW_CLASSES[1:]:
        pl.when(jnp.logical_and(real, rows == n_rows))(functools.partial(run, n_rows))


def _experts(x_sorted, visits, w_gate_up, b_gate_up, w_down, b_down):
    n_rows, W = x_sorted.shape
    D = 2 * W
    tm = EXPERT_TILE
    n_visits = visits[0].shape[0]
    n_prefetch = len(visits)
    by_block = lambda v, vb, *_: (vb[v], 0)
    by_expert = lambda v, vb, ve, *_: (ve[v], 0, 0)
    grid_spec = pltpu.PrefetchScalarGridSpec(
        num_scalar_prefetch=n_prefetch,
        grid=(n_visits,),
        in_specs=[
            pl.BlockSpec((tm, W), by_block),
            pl.BlockSpec(memory_space=pl.ANY),
            pl.BlockSpec((1, 1, 2 * D_EXPERT), by_expert),
            pl.BlockSpec(memory_space=pl.ANY),
            pl.BlockSpec((1, 1, D), by_expert),
        ],
        out_specs=pl.BlockSpec((tm, W), by_block),
        scratch_shapes=[pltpu.VMEM((D, 2 * D_EXPERT), F32), pltpu.VMEM((D_EXPERT, D), F32),
                        pltpu.VMEM((D, 2 * D_EXPERT), BF16), pltpu.VMEM((D_EXPERT, D), BF16),
                        pltpu.SemaphoreType.DMA((2,))],
    )
    return pl.pallas_call(
        _expert_kernel,
        grid_spec=grid_spec,
        out_shape=jax.ShapeDtypeStruct((n_rows, W), U32),
        compiler_params=pltpu.CompilerParams(
            dimension_semantics=("arbitrary",), vmem_limit_bytes=VMEM_LIMIT),
        name="expert_gmm",
    )(*visits, x_sorted, w_gate_up, b_gate_up.reshape(N_EXPERTS, 1, -1), w_down, b_down.reshape(N_EXPERTS, 1, -1))


def _visit_schedule(counts, n_assign, capacity):
    tm = EXPERT_TILE
    assert EXPERT_ROW_CLASSES[0] == tm and list(EXPERT_ROW_CLASSES) == sorted(EXPERT_ROW_CLASSES, reverse=True)
    n_visits = n_assign // tm + N_EXPERTS
    n_full = counts // tm
    tail = counts - n_full * tm
    tail_rows = jnp.zeros_like(tail)
    for rows in EXPERT_ROW_CLASSES:
        tail_rows = jnp.where(tail <= rows, rows, tail_rows)
    per_e = n_full + (tail > 0)
    vend = jnp.cumsum(per_e)
    nvis = vend[-1]
    vc = jnp.minimum(jnp.arange(n_visits, dtype=I32), nvis - 1)
    done = vc[:, None] >= vend[None, :]
    e = jnp.sum(done, axis=1).astype(I32)
    local = vc - jnp.sum(jnp.where(done, per_e[None, :], 0), axis=1)
    mine = e[:, None] == jnp.arange(N_EXPERTS, dtype=I32)[None, :]
    is_tail = local >= jnp.sum(jnp.where(mine, n_full[None, :], 0), axis=1)
    vrows = jnp.where(is_tail, jnp.sum(jnp.where(mine, tail_rows[None, :], 0), axis=1), tm)
    blk = e * (capacity // tm) + local
    ids = jnp.arange(N_EXPERTS, dtype=I32)
    nonempty = counts > 0
    later = jnp.logical_and(nonempty[None, :], ids[None, :] > ids[:, None])
    nxt = jnp.where(jnp.any(later, axis=1), jnp.argmax(later, axis=1), -1).astype(I32)
    return blk.astype(I32), e, vrows.astype(I32), nvis.reshape(1).astype(I32), nxt


def _combine_kernel(y0_ref, y1_ref, y2_ref, y3_ref, x1_ref, gate_ref, g_ref, *out_refs):
    o_ref = out_refs[-1]
    half = x1_ref.shape[1] // 2
    gates = gate_ref[...]
    lo = x1_ref[:, :half]
    hi = x1_ref[:, half:]
    for kk, y_ref in enumerate((y0_ref, y1_ref, y2_ref, y3_ref)):
        y_lo, y_hi = _unpack_rows(y_ref[...])
        lo = lo + gates[:, kk:kk + 1] * y_lo
        hi = hi + gates[:, kk:kk + 1] * y_hi
    ms = (jnp.sum(lo * lo, axis=-1, keepdims=True) + jnp.sum(hi * hi, axis=-1, keepdims=True)) / (2 * half)
    scale = lax.rsqrt(ms + RMS_EPS)
    o_ref[:, :half] = lo * scale * g_ref[:, :half]
    o_ref[:, half:] = hi * scale * g_ref[:, half:]


def _combine(y_tok, x1, gates_t, g_final, n_total, row_block0, out_so_far):
    S, D = x1.shape
    tm = COMBINE_TILE
    nt = S // tm
    y_spec = lambda kk: pl.BlockSpec((tm, D // 2), lambda i: (kk * nt + i, 0))
    in_specs = [y_spec(0), y_spec(1), y_spec(2), y_spec(3),
                pl.BlockSpec((tm, D), lambda i: (i, 0)),
                pl.BlockSpec((tm, TOP_K), lambda i: (i, 0)),
                pl.BlockSpec((1, D), lambda i: (0, 0))]
    args = [y_tok, y_tok, y_tok, y_tok, x1, gates_t, g_final.reshape(1, D)]
    aliases = {}
    if out_so_far is not None:
        in_specs.append(pl.BlockSpec(memory_space=pl.ANY))
        args.append(out_so_far)
        aliases = {len(args) - 1: 0}
    return pl.pallas_call(
        _combine_kernel,
        grid=(nt,),
        in_specs=in_specs,
        out_specs=pl.BlockSpec((tm, D), lambda i: (row_block0 + i, 0)),
        out_shape=jax.ShapeDtypeStruct((n_total, D), F32),
        input_output_aliases=aliases,
        compiler_params=pltpu.CompilerParams(
            dimension_semantics=("arbitrary",), vmem_limit_bytes=VMEM_LIMIT),
        name="combine_norm",
    )(*args)


def kernel(x, g_mix, w_in, pool_w, pool_scale, w_out, g_moe, w_router, b_router, w_gate_up, b_gate_up,
           w_down, b_down, g_final):
    B, S, D = x.shape
    T = B * S
    assert g_mix.shape[0] == 1, "single-layer problem: the final norm is fused into the combine step"
    w_in_folded = (g_mix[0][:, None] * w_in[0]).astype(BF16)
    pool_w_bf = pool_w[0].astype(BF16)
    w_out_bf = w_out[0].astype(BF16)
    x_rows = x.reshape(T, D)
    capacity = S
    routed = []
    for b in range(B):
        q, k, v, pool = _in_proj(x, b, w_in_folded, pool_w_bf, pool_scale[0])
        attn = _attention(q, k, v)
        x1, h2, gates, dest, cnt = _out_proj_router(
            attn.reshape(S, SB_WIDTH), pool.reshape(S, POOL_WIDTH), x_rows, b * (S // PROJ_TILE),
            w_out_bf, g_moe[0], w_router[0], b_router[0], capacity)
        x_sorted = _sc_dispatch(h2, dest, N_EXPERTS * capacity)
        routed.append((x1, gates, dest, cnt, x_sorted))
    gathered = []
    for x1, gates, dest, cnt, x_sorted in routed:
        visits = _visit_schedule(cnt[:, 0].astype(I32), S * TOP_K, capacity)
        y = _experts(x_sorted, visits, w_gate_up[0], b_gate_up[0], w_down[0], b_down[0])
        gathered.append(_sc_gather(y, dest.reshape(-1)))
    out = None
    for b, ((x1, gates, _, _, _), y_tok) in enumerate(zip(routed, gathered)):
        out = _combine(y_tok, x1, gates.T, g_final, T, b * (S // COMBINE_TILE), out)
    return out.reshape(B, S, D)
```

```python
import functools

import jax
import jax.numpy as jnp
from jax import lax
from jax.experimental import pallas as pl
from jax.experimental.pallas import tpu as pltpu
from jax.experimental.pallas import tpu_sc as plsc

F32 = jnp.float32
BF16 = jnp.bfloat16
I32 = jnp.int32
U32 = jnp.uint32

D_MODEL = 1024
SB_HEADS = 8
SB_HEAD_DIM = 64
SB_WIDTH = SB_HEADS * SB_HEAD_DIM
POOL_WINDOWS = (2, 4, 8, 16)
POOL_WIDTH = 512
POOL_GROUP_DIM = 128
N_EXPERTS = 32
TOP_K = 4
D_EXPERT = 1024
SWIGLU_LIMIT = 7.0
SWIGLU_ALPHA = 1.702
RMS_EPS = 1e-5

LANES = 128
HALO = 32
PROJ_TILE = 1024
ATTN_TILE = 256
ATTN_SUBTILES = 4
EXPERT_TILE = 1024
EXPERT_ROW_CLASSES = (1024, 512, 256, 128)
COMBINE_TILE = 1024
ATTN_SKIP_LOG2 = -160.0
LOG2_E = 1.4426950408889634
VMEM_LIMIT = 56 * 1024 * 1024

_NT = (((1,), (1,)), ((), ()))


def _rms(x, g):
    ms = jnp.mean(x * x, axis=-1, keepdims=True)
    return x * lax.rsqrt(ms + RMS_EPS) * g


def _pack_rows(x):
    n = x.shape[1] // 2
    lo = lax.bitcast_convert_type(x[:, :n].astype(BF16).astype(F32), U32)
    hi = lax.bitcast_convert_type(x[:, n:].astype(BF16).astype(F32), U32)
    return (lo >> 16) | (hi & jnp.uint32(0xFFFF0000))


def _unpack_rows(w):
    lo = lax.bitcast_convert_type(w << 16, F32)
    hi = lax.bitcast_convert_type(w & jnp.uint32(0xFFFF0000), F32)
    return lo, hi


def _in_proj_kernel(x_ref, w_ref, pw_ref, ps_ref, q_ref, k_ref, v_ref, p_ref, uext_ref, xb_ref, lvl_ref):
    s = pl.program_id(1)
    tm = x_ref.shape[1]
    x = x_ref[0]
    inv = lax.rsqrt(jnp.mean(x * x, axis=-1, keepdims=True) + RMS_EPS)
    xb_ref[...] = x.astype(BF16)
    proj = jnp.dot(xb_ref[...], w_ref[...], preferred_element_type=F32)
    q_ref[0] = (proj[:, 0:SB_WIDTH] * (inv * (LOG2_E * SB_HEAD_DIM ** -0.5))).astype(BF16)
    k_ref[0] = (proj[:, SB_WIDTH:2 * SB_WIDTH] * inv).astype(BF16)
    v_ref[0] = (proj[:, 2 * SB_WIDTH:3 * SB_WIDTH] * inv).astype(BF16)
    u = proj[:, 3 * SB_WIDTH:] * inv

    lo = HALO // 2

    @pl.when(s == 0)
    def _():
        uext_ref[0:HALO, :] = jnp.zeros((HALO, POOL_WIDTH), F32)
        lvl_ref[:, 0:lo, :] = jnp.zeros((2, lo, POOL_GROUP_DIM), F32)

    uext_ref[HALO:, :] = u
    t = s * tm + lax.broadcasted_iota(I32, (tm, 1), 0)
    for g, w in enumerate(POOL_WINDOWS):
        sl = slice(g * POOL_GROUP_DIM, (g + 1) * POOL_GROUP_DIM)
        lvl_ref[0, lo:, :] = uext_ref[lo:, sl] + uext_ref[lo - 1:HALO + tm - 1, sl]
        cur, k = 0, 2
        while k < w:
            lvl_ref[1 - cur, lo:, :] = lvl_ref[cur, lo:, :] + lvl_ref[cur, lo - k:HALO + tm - k, :]
            cur, k = 1 - cur, 2 * k
        ug = u[:, sl]
        count = jnp.minimum(t + 1, w).astype(F32)
        pooled = lvl_ref[cur, HALO:, :] / count - ug
        mixed = jnp.dot(pooled.astype(BF16), pw_ref[g], preferred_element_type=F32)
        p_ref[0, :, sl] = (mixed * ps_ref[:, sl]).astype(BF16)
    uext_ref[0:HALO, :] = u[tm - HALO:, :]


def _in_proj(x, g_mix, w_in, pool_w, pool_scale):
    B, S, D = x.shape
    tm = PROJ_TILE
    n_out = w_in.shape[1]
    out_sd = jax.ShapeDtypeStruct((B, S, SB_WIDTH), BF16)
    blk = pl.BlockSpec((1, tm, SB_WIDTH), lambda b, s: (b, s, 0))
    return pl.pallas_call(
        _in_proj_kernel,
        grid=(B, S // tm),
        in_specs=[
            pl.BlockSpec((1, tm, D), lambda b, s: (b, s, 0)),
            pl.BlockSpec((D, n_out), lambda b, s: (0, 0)),
            pl.BlockSpec(pool_w.shape, lambda b, s: (0, 0, 0)),
            pl.BlockSpec((1, POOL_WIDTH), lambda b, s: (0, 0)),
        ],
        out_specs=[blk, blk, blk, blk],
        out_shape=[out_sd, out_sd, out_sd, out_sd],
        scratch_shapes=[pltpu.VMEM((HALO + tm, POOL_WIDTH), F32), pltpu.VMEM((tm, D), BF16),
                        pltpu.VMEM((2, HALO + tm, POOL_GROUP_DIM), F32)],
        compiler_params=pltpu.CompilerParams(
            dimension_semantics=("arbitrary", "arbitrary"), vmem_limit_bytes=VMEM_LIMIT),
        name="in_proj_pool",
    )(x, (g_mix[:, None] * w_in).astype(BF16), pool_w.astype(BF16), pool_scale.reshape(1, POOL_WIDTH))


def _attn_kernel(q_ref, k_ref, v_ref, o_ref):
    tq = ATTN_TILE
    tk = tq
    qi = pl.program_id(2)
    lane = lax.broadcasted_iota(I32, (tq, LANES), 1)
    row = lax.broadcasted_iota(I32, (tq, tk), 0)
    col = lax.broadcasted_iota(I32, (tq, tk), 1)
    causal = col < row
    tri = (row > col).astype(BF16)

    def head_rows(r, h):
        rows = q_ref[0, r * tq:(r + 1) * tq, :]
        return jnp.where((lane >= h * SB_HEAD_DIM) & (lane < (h + 1) * SB_HEAD_DIM), rows, jnp.zeros_like(rows))

    def scores(qh, j):
        kb = k_ref[0, pl.ds(pl.multiple_of(j * tk, tk), tk), :]
        return lax.dot_general(qh, kb, _NT, preferred_element_type=F32)

    def values(j):
        return v_ref[0, pl.ds(pl.multiple_of(j * tk, tk), tk), :]

    def log_terms(z, mask):
        neg_abs = lax.bitcast_convert_type(lax.bitcast_convert_type(z, U32) | jnp.uint32(0x80000000), F32)
        nl = jnp.maximum(z, 0.0) + jnp.log2(1.0 + jnp.exp2(neg_abs))
        if mask:
            nl = jnp.where(causal, nl, 0.0)
        return nl.astype(BF16), z - nl, nl[:, 0:1]

    def tile(qh, j, mask):
        nl, lb, nl0 = log_terms(scores(qh, j), mask)
        ntail = jnp.dot(nl, tri, preferred_element_type=F32)
        return lb - ntail, -(ntail[:, 0:1] + nl0), values(j)

    nq = ATTN_SUBTILES
    has_prev = qi > 0
    qs = [head_rows(r, h) for r in range(nq) for h in range(2)]
    chains = []
    for n, qh in enumerate(qs):
        diag = nq * qi + n // 2
        chains += [(qh, diag, True), (qh, jnp.maximum(diag - 1, 0), False)]
    zs = [scores(qh, j) for qh, j, _ in chains]
    terms = [log_terms(z, is_diag) for z, (_, _, is_diag) in zip(zs, chains)]
    nts = [jnp.dot(nl, tri, preferred_element_type=F32) for nl, _, _ in terms]
    lbs = [lb for _, lb, _ in terms]
    tots = [-(nt[:, 0:1] + nl0) for nt, (_, _, nl0) in zip(nts, terms)]
    ws = []
    for n in range(2 * nq):
        d, p = 2 * n, 2 * n + 1
        w_d = jnp.where(causal, jnp.exp2(lbs[d] - nts[d]), 0.0)
        w_p = jnp.exp2(lbs[p] - nts[p] + tots[d])
        if n < 2:
            w_p = jnp.where(has_prev, w_p, 0.0)
        ws += [w_d.astype(BF16), w_p.astype(BF16)]
    pvs = [jnp.dot(w, values(j), preferred_element_type=F32) for w, (_, j, _) in zip(ws, chains)]
    accs = [pvs[2 * n] + pvs[2 * n + 1] for n in range(2 * nq)]
    carries = [tots[2 * n] + tots[2 * n + 1] for n in range(2 * nq)]

    def cond(st):
        j, cs, _ = st
        live = functools.reduce(jnp.maximum, [jnp.max(c) for c in cs])
        return jnp.logical_and(j >= 0, live > ATTN_SKIP_LOG2)

    def body(st):
        j, cs, acs = st
        new_c, new_a = [], []
        for n, (qh, c, a) in enumerate(zip(qs, cs, acs)):
            lag = nq - 1 - n // 2
            if lag:
                c = jnp.where(j >= lag, c, -jnp.inf)
            lw, tot, vb = tile(qh, jnp.maximum(j - lag, 0), False)
            new_a.append(a + jnp.dot(jnp.exp2(lw + c).astype(BF16), vb, preferred_element_type=F32))
            new_c.append(c + tot)
        return j - 1, tuple(new_c), tuple(new_a)

    _, _, accs = lax.while_loop(cond, body, (nq * qi + nq - 3, tuple(carries), tuple(accs)))
    for r in range(nq):
        o_ref[0, r * tq:(r + 1) * tq, :] = jnp.where(
            lane < SB_HEAD_DIM, accs[2 * r], accs[2 * r + 1]).astype(o_ref.dtype)


def _attention(q, k, v):
    B, S, W = q.shape
    tq = ATTN_SUBTILES * ATTN_TILE
    n_pairs = W // LANES
    return pl.pallas_call(
        _attn_kernel,
        grid=(B, n_pairs, S // tq),
        in_specs=[
            pl.BlockSpec((1, tq, LANES), lambda b, p, i: (b, i, p)),
            pl.BlockSpec((1, S, LANES), lambda b, p, i: (b, 0, p)),
            pl.BlockSpec((1, S, LANES), lambda b, p, i: (b, 0, p)),
        ],
        out_specs=pl.BlockSpec((1, tq, LANES), lambda b, p, i: (b, i, p)),
        out_shape=jax.ShapeDtypeStruct((B, S, W), BF16),
        compiler_params=pltpu.CompilerParams(
            dimension_semantics=("arbitrary", "arbitrary", "arbitrary"), vmem_limit_bytes=VMEM_LIMIT),
        name="stickbreak_attn",
    )(q, k, v)


def _out_proj_router_kernel(attn_ref, pool_ref, x_ref, wo_ref, g_ref, wr_ref, br_ref, tri_ref,
                            x1_ref, h2_ref, gate_ref, dest_ref, cnt_ref, carry_ref, *, capacity):
    i = pl.program_id(0)
    tm = x_ref.shape[0]

    @pl.when(i == 0)
    def _():
        carry_ref[...] = jnp.zeros_like(carry_ref)

    mixed = jnp.dot(attn_ref[...], wo_ref[0:SB_WIDTH, :], preferred_element_type=F32)
    mixed = mixed + jnp.dot(pool_ref[...], wo_ref[SB_WIDTH:, :], preferred_element_type=F32)
    x1 = x_ref[...] + mixed
    h2 = _rms(x1, g_ref[...])
    x1_ref[...] = x1
    h2_ref[...] = _pack_rows(h2)

    hh = h2.astype(BF16)
    hl = (h2 - hh.astype(F32)).astype(BF16)
    wr = wr_ref[...]
    wh = wr.astype(BF16)
    wl = (wr - wh.astype(F32)).astype(BF16)
    logits = (lax.dot_general(wh, hh, _NT, preferred_element_type=F32)
              + lax.dot_general(wh, hl, _NT, preferred_element_type=F32)
              + lax.dot_general(wl, hh, _NT, preferred_element_type=F32)) + br_ref[...]

    eid = lax.broadcasted_iota(I32, (N_EXPERTS, tm), 0).astype(F32)
    work = logits
    vals, ids = [], []
    for _ in range(TOP_K):
        m = jnp.max(work, axis=0, keepdims=True)
        sel = jnp.min(jnp.where(work == m, eid, float(N_EXPERTS)), axis=0, keepdims=True)
        vals.append(m)
        ids.append(sel)
        work = jnp.where(eid == sel, -jnp.inf, work)
    exps = [jnp.exp(v - vals[0]) for v in vals]
    denom = exps[0] + exps[1] + exps[2] + exps[3]

    onehot = jnp.zeros((N_EXPERTS, tm), F32)
    for sel in ids:
        onehot = onehot + (eid == sel).astype(F32)
    before = jnp.dot(onehot.astype(BF16), tri_ref[...], preferred_element_type=F32) + carry_ref[...]
    for kk in range(TOP_K):
        gate_ref[kk:kk + 1, :] = exps[kk] / denom
        rk = jnp.sum(jnp.where(eid == ids[kk], before, 0.0), axis=0, keepdims=True)
        dest_ref[kk:kk + 1, :] = (ids[kk] * float(capacity) + rk).astype(I32)
    carry_ref[...] = carry_ref[...] + jnp.sum(onehot, axis=1, keepdims=True)
    cnt_ref[...] = jnp.broadcast_to(carry_ref[...], cnt_ref.shape)


def _out_proj_router(attn, pool, x, w_out, g_moe, w_router, b_router, capacity):
    T, D = x.shape
    tm = PROJ_TILE
    r = lax.broadcasted_iota(I32, (tm, tm), 0)
    c = lax.broadcasted_iota(I32, (tm, tm), 1)
    tri = (r < c).astype(BF16)
    row_blk = lambda w: pl.BlockSpec((tm, w), lambda i: (i, 0))
    fixed = lambda shape: pl.BlockSpec(shape, lambda i: tuple(0 for _ in shape))
    sel_blk = pl.BlockSpec((TOP_K, tm), lambda i: (0, i))
    assert N_EXPERTS * capacity < 2 ** 24
    return pl.pallas_call(
        functools.partial(_out_proj_router_kernel, capacity=capacity),
        grid=(T // tm,),
        in_specs=[row_blk(SB_WIDTH), row_blk(POOL_WIDTH), row_blk(D), fixed((D, D)), fixed((1, D)),
                  fixed((N_EXPERTS, D)), fixed((N_EXPERTS, 1)), fixed((tm, tm))],
        out_specs=[row_blk(D), row_blk(D // 2), sel_blk, sel_blk, fixed((N_EXPERTS, LANES))],
        out_shape=[jax.ShapeDtypeStruct((T, D), F32), jax.ShapeDtypeStruct((T, D // 2), U32),
                   jax.ShapeDtypeStruct((TOP_K, T), F32), jax.ShapeDtypeStruct((TOP_K, T), I32),
                   jax.ShapeDtypeStruct((N_EXPERTS, LANES), F32)],
        scratch_shapes=[pltpu.VMEM((N_EXPERTS, 1), F32)],
        compiler_params=pltpu.CompilerParams(
            dimension_semantics=("arbitrary",), vmem_limit_bytes=VMEM_LIMIT),
        name="out_proj_router",
    )(attn, pool, x, w_out.astype(BF16), g_moe.reshape(1, D), w_router.T, b_router.reshape(N_EXPERTS, 1), tri)


SC_ROWS_PER_COPY = 64


def _sc_workers():
    info = plsc.get_sparse_core_info()
    return info.num_cores, info.num_cores * info.num_subcores


def _sc_dispatch(rows, dest, n_out):
    T, W = rows.shape
    K = dest.shape[0]
    sub = SC_ROWS_PER_COPY
    n_cores, n_workers = _sc_workers()
    per_w = T // n_workers
    n_chunks = per_w // sub
    assert per_w * n_workers == T and n_chunks * sub == per_w and n_chunks % 2 == 0
    idx = dest.reshape(K, n_workers, n_chunks, sub).transpose(1, 2, 0, 3).reshape(n_workers, n_chunks * K, sub)
    mesh = plsc.VectorSubcoreMesh(core_axis_name="core", subcore_axis_name="subcore")

    @functools.partial(
        pl.kernel, out_type=jax.ShapeDtypeStruct((n_out, W), rows.dtype), mesh=mesh,
        scratch_types=[pltpu.VMEM((n_chunks * K, sub), I32), pltpu.VMEM((2, sub, W), rows.dtype),
                       pltpu.SemaphoreType.DMA((2,)), pltpu.SemaphoreType.DMA((2,))])
    def scatter_rows(x_hbm, i_hbm, o_hbm, idx_v, buf, rsem, wsem):
        wid = lax.axis_index("subcore") * n_cores + lax.axis_index("core")
        base = wid * per_w
        pltpu.sync_copy(i_hbm.at[wid], idx_v)

        def read(c, slot):
            return pltpu.make_async_copy(x_hbm.at[pl.ds(base + c * sub, sub)], buf.at[slot], rsem.at[slot])

        def write(c, kk, slot):
            return pltpu.make_async_copy(buf.at[slot], o_hbm.at[idx_v.at[c * K + kk]], wsem.at[slot])

        read(0, 0).start()

        @pl.loop(0, n_chunks, step=2)
        def _(c0):
            for b in range(2):
                c = c0 + b
                read(c, b).wait()
                for kk in range(K):
                    write(c, kk, b).start()

                @pl.when(c + 1 < n_chunks)
                def _():
                    @pl.when(c >= 1)
                    def _():
                        for kk in range(K):
                            write(c - 1, kk, 1 - b).wait()
                    read(c + 1, 1 - b).start()

        for kk in range(K):
            write(n_chunks - 2, kk, 0).wait()
            write(n_chunks - 1, kk, 1).wait()

    return scatter_rows(rows, idx)


def _sc_gather(table, indices):
    M = indices.shape[0]
    W = table.shape[1]
    sub = SC_ROWS_PER_COPY
    n_cores, n_workers = _sc_workers()
    per_w = M // n_workers
    n_steps = per_w // sub
    assert per_w * n_workers == M and n_steps * sub == per_w and n_steps % 2 == 0
    mesh = plsc.VectorSubcoreMesh(core_axis_name="core", subcore_axis_name="subcore")

    @functools.partial(
        pl.kernel, out_type=jax.ShapeDtypeStruct((M, W), table.dtype), mesh=mesh,
        scratch_types=[pltpu.VMEM((n_steps, sub), I32), pltpu.VMEM((2, sub, W), table.dtype),
                       pltpu.SemaphoreType.DMA((2,)), pltpu.SemaphoreType.DMA((2,))])
    def gather_rows(x_hbm, i_hbm, o_hbm, idx_v, buf, gsem, wsem):
        wid = lax.axis_index("subcore") * n_cores + lax.axis_index("core")
        base = wid * per_w
        pltpu.sync_copy(i_hbm.at[wid], idx_v)

        def gather(s, slot):
            return pltpu.make_async_copy(x_hbm.at[idx_v.at[s]], buf.at[slot], gsem.at[slot])

        def write(s, slot):
            return pltpu.make_async_copy(buf.at[slot], o_hbm.at[pl.ds(base + s * sub, sub)], wsem.at[slot])

        gather(0, 0).start()

        @pl.loop(0, n_steps, step=2)
        def _(s0):
            for b in range(2):
                s = s0 + b
                gather(s, b).wait()
                write(s, b).start()

                @pl.when(s + 1 < n_steps)
                def _():
                    @pl.when(s >= 1)
                    def _():
                        write(s - 1, 1 - b).wait()
                    gather(s + 1, 1 - b).start()

        write(n_steps - 2, 0).wait()
        write(n_steps - 1, 1).wait()

    return gather_rows(table, indices.reshape(n_workers, n_steps, sub))


def _expert_kernel(vb_ref, ve_ref, vrows_ref, nvis_ref, nxt_ref,
                   x_ref, wgu_hbm, bgu_ref, wd_hbm, bd_ref, y_ref,
                   wgu_stage, wd_stage, wgu_bf, wd_bf, sem):
    v = pl.program_id(0)
    real = v < nvis_ref[0]
    e = ve_ref[v]
    first_of_expert = jnp.logical_or(v == 0, e != ve_ref[jnp.maximum(v - 1, 0)])

    def weight_copies(expert):
        return (pltpu.make_async_copy(wgu_hbm.at[expert], wgu_stage, sem.at[0]),
                pltpu.make_async_copy(wd_hbm.at[expert], wd_stage, sem.at[1]))

    @pl.when(v == 0)
    def _():
        for c in weight_copies(e):
            c.start()

    def load_weights():
        for c in weight_copies(e):
            c.wait()
        wgu_bf[...] = wgu_stage[...].astype(BF16)
        wd_bf[...] = wd_stage[...].astype(BF16)

    def fetch_next():
        nxt = nxt_ref[e]

        @pl.when(nxt >= 0)
        def _():
            for c in weight_copies(nxt):
                c.start()

    def run(n_rows):
        half = wgu_bf.shape[0] // 2
        x_lo, x_hi = _unpack_rows(x_ref[0:n_rows, :])
        gu = (jnp.dot(x_lo.astype(BF16), wgu_bf[:half, :], preferred_element_type=F32)
              + jnp.dot(x_hi.astype(BF16), wgu_bf[half:, :], preferred_element_type=F32)) + bgu_ref[0]
        gate = jnp.minimum(gu[:, :D_EXPERT], SWIGLU_LIMIT)
        up = jnp.clip(gu[:, D_EXPERT:], -SWIGLU_LIMIT, SWIGLU_LIMIT)
        act = gate * jax.nn.sigmoid(SWIGLU_ALPHA * gate) * (up + 1.0)
        y = jnp.dot(act.astype(BF16), wd_bf[...], preferred_element_type=F32) + bd_ref[0]
        y_ref[0:n_rows, :] = _pack_rows(y)

    rows = vrows_ref[v]
    full = EXPERT_ROW_CLASSES[0]
    first = jnp.logical_and(first_of_expert, real)

    @pl.when(jnp.logical_and(first, rows == full))
    def _():
        load_weights()
        run(full)
        fetch_next()

    @pl.when(jnp.logical_and(first, rows != full))
    def _():
        load_weights()
        fetch_next()

    later_full = jnp.logical_and(real, jnp.logical_not(first_of_expert))
    pl.when(jnp.logical_and(later_full, rows == full))(functools.partial(run, full))
    for n_rows in EXPERT_ROW_CLASSES[1:]:
        pl.when(jnp.logical_and(real, rows == n_rows))(functools.partial(run, n_rows))


def _experts(x_sorted, visits, w_gate_up, b_gate_up, w_down, b_down):
    n_rows, W = x_sorted.shape
    D = 2 * W
    tm = EXPERT_TILE
    n_visits = visits[0].shape[0]
    n_prefetch = len(visits)
    by_block = lambda v, vb, *_: (vb[v], 0)
    by_expert = lambda v, vb, ve, *_: (ve[v], 0, 0)
    grid_spec = pltpu.PrefetchScalarGridSpec(
        num_scalar_prefetch=n_prefetch,
        grid=(n_visits,),
        in_specs=[
            pl.BlockSpec((tm, W), by_block),
            pl.BlockSpec(memory_space=pl.ANY),
            pl.BlockSpec((1, 1, 2 * D_EXPERT), by_expert),
            pl.BlockSpec(memory_space=pl.ANY),
            pl.BlockSpec((1, 1, D), by_expert),
        ],
        out_specs=pl.BlockSpec((tm, W), by_block),
        scratch_shapes=[pltpu.VMEM((D, 2 * D_EXPERT), F32), pltpu.VMEM((D_EXPERT, D), F32),
                        pltpu.VMEM((D, 2 * D_EXPERT), BF16), pltpu.VMEM((D_EXPERT, D), BF16),
                        pltpu.SemaphoreType.DMA((2,))],
    )
    return pl.pallas_call(
        _expert_kernel,
        grid_spec=grid_spec,
        out_shape=jax.ShapeDtypeStruct((n_rows, W), U32),
        compiler_params=pltpu.CompilerParams(
            dimension_semantics=("arbitrary",), vmem_limit_bytes=VMEM_LIMIT),
        name="expert_gmm",
    )(*visits, x_sorted, w_gate_up, b_gate_up.reshape(N_EXPERTS, 1, -1), w_down, b_down.reshape(N_EXPERTS, 1, -1))


def _visit_schedule(counts, n_assign, capacity):
    tm = EXPERT_TILE
    assert EXPERT_ROW_CLASSES[0] == tm and list(EXPERT_ROW_CLASSES) == sorted(EXPERT_ROW_CLASSES, reverse=True)
    n_visits = n_assign // tm + N_EXPERTS
    n_full = counts // tm
    tail = counts - n_full * tm
    tail_rows = jnp.zeros_like(tail)
    for rows in EXPERT_ROW_CLASSES:
        tail_rows = jnp.where(tail <= rows, rows, tail_rows)
    per_e = n_full + (tail > 0)
    vend = jnp.cumsum(per_e)
    nvis = vend[-1]
    vc = jnp.minimum(jnp.arange(n_visits, dtype=I32), nvis - 1)
    done = vc[:, None] >= vend[None, :]
    e = jnp.sum(done, axis=1).astype(I32)
    local = vc - jnp.sum(jnp.where(done, per_e[None, :], 0), axis=1)
    mine = e[:, None] == jnp.arange(N_EXPERTS, dtype=I32)[None, :]
    is_tail = local >= jnp.sum(jnp.where(mine, n_full[None, :], 0), axis=1)
    vrows = jnp.where(is_tail, jnp.sum(jnp.where(mine, tail_rows[None, :], 0), axis=1), tm)
    blk = e * (capacity // tm) + local
    ids = jnp.arange(N_EXPERTS, dtype=I32)
    nonempty = counts > 0
    later = jnp.logical_and(nonempty[None, :], ids[None, :] > ids[:, None])
    nxt = jnp.where(jnp.any(later, axis=1), jnp.argmax(later, axis=1), -1).astype(I32)
    return blk.astype(I32), e, vrows.astype(I32), nvis.reshape(1).astype(I32), nxt


def _combine_kernel(y0_ref, y1_ref, y2_ref, y3_ref, x1_ref, gate_ref, g_ref, o_ref):
    half = x1_ref.shape[1] // 2
    gates = jnp.transpose(gate_ref[...])
    lo = x1_ref[:, :half]
    hi = x1_ref[:, half:]
    for kk, y_ref in enumerate((y0_ref, y1_ref, y2_ref, y3_ref)):
        y_lo, y_hi = _unpack_rows(y_ref[...])
        lo = lo + gates[:, kk:kk + 1] * y_lo
        hi = hi + gates[:, kk:kk + 1] * y_hi
    ms = (jnp.sum(lo * lo, axis=-1, keepdims=True) + jnp.sum(hi * hi, axis=-1, keepdims=True)) / (2 * half)
    scale = lax.rsqrt(ms + RMS_EPS)
    o_ref[:, :half] = lo * scale * g_ref[:, :half]
    o_ref[:, half:] = hi * scale * g_ref[:, half:]


def _combine(y_tok, x1, gates, g_final):
    T, D = x1.shape
    tm = COMBINE_TILE
    nt = T // tm
    y_spec = lambda kk: pl.BlockSpec((tm, D // 2), lambda i: (kk * nt + i, 0))
    return pl.pallas_call(
        _combine_kernel,
        grid=(nt,),
        in_specs=[y_spec(0), y_spec(1), y_spec(2), y_spec(3),
                  pl.BlockSpec((tm, D), lambda i: (i, 0)),
                  pl.BlockSpec((TOP_K, tm), lambda i: (0, i)),
                  pl.BlockSpec((1, D), lambda i: (0, 0))],
        out_specs=pl.BlockSpec((tm, D), lambda i: (i, 0)),
        out_shape=jax.ShapeDtypeStruct((T, D), F32),
        compiler_params=pltpu.CompilerParams(
            dimension_semantics=("arbitrary",), vmem_limit_bytes=VMEM_LIMIT),
        name="combine_norm",
    )(y_tok, y_tok, y_tok, y_tok, x1, gates, g_final.reshape(1, D))


def kernel(x, g_mix, w_in, pool_w, pool_scale, w_out, g_moe, w_router, b_router, w_gate_up, b_gate_up,
           w_down, b_down, g_final):
    B, S, D = x.shape
    T = B * S
    assert g_mix.shape[0] == 1, "single-layer problem: the final norm is fused into the combine step"
    q, k, v, pool = _in_proj(x, g_mix[0], w_in[0], pool_w[0], pool_scale[0])
    attn = _attention(q, k, v)
    capacity = T
    x1, h2, gates, dest, cnt = _out_proj_router(
        attn.reshape(T, SB_WIDTH), pool.reshape(T, POOL_WIDTH), x.reshape(T, D),
        w_out[0], g_moe[0], w_router[0], b_router[0], capacity)
    x_sorted = _sc_dispatch(h2, dest, N_EXPERTS * capacity)
    visits = _visit_schedule(cnt[:, 0].astype(I32), T * TOP_K, capacity)
    y = _experts(x_sorted, visits, w_gate_up[0], b_gate_up[0], w_down[0], b_down[0])
    y_tok = _sc_gather(y, dest.reshape(-1))
    return _combine(y_tok, x1, gates, g_final).reshape(B, S, D)
```

```python
import functools

import jax
import jax.numpy as jnp
from jax import lax
from jax.experimental import pallas as pl
from jax.experimental.pallas import tpu as pltpu
from jax.experimental.pallas import tpu_sc as plsc

F32 = jnp.float32
BF16 = jnp.bfloat16
I32 = jnp.int32
U32 = jnp.uint32

D_MODEL = 1024
SB_HEADS = 8
SB_HEAD_DIM = 64
SB_WIDTH = SB_HEADS * SB_HEAD_DIM
POOL_WINDOWS = (2, 4, 8, 16)
POOL_WIDTH = 512
POOL_GROUP_DIM = 128
N_EXPERTS = 32
TOP_K = 4
D_EXPERT = 1024
SWIGLU_LIMIT = 7.0
SWIGLU_ALPHA = 1.702
RMS_EPS = 1e-5

LANES = 128
HALO = 32
PROJ_TILE = 1024
ATTN_TILE = 256
ATTN_SUBTILES = 4
EXPERT_TILE = 1024
EXPERT_ROW_CLASSES = (1024, 512, 256)
COMBINE_TILE = 1024
ATTN_SKIP_LOG2 = -160.0
LOG2_E = 1.4426950408889634
V7X_VMEM_BYTES = 64 * 1024 * 1024
VMEM_LIMIT = V7X_VMEM_BYTES * 7 // 8

_NT = (((1,), (1,)), ((), ()))


def _rms(x, g):
    ms = jnp.mean(x * x, axis=-1, keepdims=True)
    return x * lax.rsqrt(ms + RMS_EPS) * g


def _pack_rows(x):
    n = x.shape[1] // 2
    lo = lax.bitcast_convert_type(x[:, :n].astype(BF16).astype(F32), U32)
    hi = lax.bitcast_convert_type(x[:, n:].astype(BF16).astype(F32), U32)
    return (lo >> 16) | (hi & jnp.uint32(0xFFFF0000))


def _unpack_rows(w):
    lo = lax.bitcast_convert_type(w << 16, F32)
    hi = lax.bitcast_convert_type(w & jnp.uint32(0xFFFF0000), F32)
    return lo, hi


def _in_proj_kernel(x_ref, w_ref, pw_ref, ps_ref, q_ref, k_ref, v_ref, p_ref, uext_ref, xb_ref, lvl_ref):
    s = pl.program_id(1)
    tm = x_ref.shape[1]
    x = x_ref[0]
    inv = lax.rsqrt(jnp.mean(x * x, axis=-1, keepdims=True) + RMS_EPS)
    xb_ref[...] = x.astype(BF16)
    proj = jnp.dot(xb_ref[...], w_ref[...], preferred_element_type=F32)
    q_ref[0] = (proj[:, 0:SB_WIDTH] * (inv * (LOG2_E * SB_HEAD_DIM ** -0.5))).astype(BF16)
    k_ref[0] = (proj[:, SB_WIDTH:2 * SB_WIDTH] * inv).astype(BF16)
    v_ref[0] = (proj[:, 2 * SB_WIDTH:3 * SB_WIDTH] * inv).astype(BF16)
    u = proj[:, 3 * SB_WIDTH:] * inv

    lo = HALO // 2

    @pl.when(s == 0)
    def _():
        uext_ref[0:HALO, :] = jnp.zeros((HALO, POOL_WIDTH), F32)
        lvl_ref[:, 0:lo, :] = jnp.zeros((2, lo, POOL_GROUP_DIM), F32)

    uext_ref[HALO:, :] = u
    t = s * tm + lax.broadcasted_iota(I32, (tm, 1), 0)
    for g, w in enumerate(POOL_WINDOWS):
        sl = slice(g * POOL_GROUP_DIM, (g + 1) * POOL_GROUP_DIM)
        lvl_ref[0, lo:, :] = uext_ref[lo:, sl] + uext_ref[lo - 1:HALO + tm - 1, sl]
        cur, k = 0, 2
        while k < w:
            lvl_ref[1 - cur, lo:, :] = lvl_ref[cur, lo:, :] + lvl_ref[cur, lo - k:HALO + tm - k, :]
            cur, k = 1 - cur, 2 * k
        ug = u[:, sl]
        count = jnp.minimum(t + 1, w).astype(F32)
        pooled = lvl_ref[cur, HALO:, :] / count - ug
        mixed = jnp.dot(pooled.astype(BF16), pw_ref[g], preferred_element_type=F32)
        p_ref[0, :, sl] = (mixed * ps_ref[:, sl]).astype(BF16)
    uext_ref[0:HALO, :] = u[tm - HALO:, :]


def _in_proj(x, g_mix, w_in, pool_w, pool_scale):
    B, S, D = x.shape
    tm = PROJ_TILE
    n_out = w_in.shape[1]
    out_sd = jax.ShapeDtypeStruct((B, S, SB_WIDTH), BF16)
    blk = pl.BlockSpec((1, tm, SB_WIDTH), lambda b, s: (b, s, 0))
    return pl.pallas_call(
        _in_proj_kernel,
        grid=(B, S // tm),
        in_specs=[
            pl.BlockSpec((1, tm, D), lambda b, s: (b, s, 0)),
            pl.BlockSpec((D, n_out), lambda b, s: (0, 0)),
            pl.BlockSpec(pool_w.shape, lambda b, s: (0, 0, 0)),
            pl.BlockSpec((1, POOL_WIDTH), lambda b, s: (0, 0)),
        ],
        out_specs=[blk, blk, blk, blk],
        out_shape=[out_sd, out_sd, out_sd, out_sd],
        scratch_shapes=[pltpu.VMEM((HALO + tm, POOL_WIDTH), F32), pltpu.VMEM((tm, D), BF16),
                        pltpu.VMEM((2, HALO + tm, POOL_GROUP_DIM), F32)],
        compiler_params=pltpu.CompilerParams(
            dimension_semantics=("arbitrary", "arbitrary"), vmem_limit_bytes=VMEM_LIMIT),
        name="in_proj_pool",
    )(x, (g_mix[:, None] * w_in).astype(BF16), pool_w.astype(BF16), pool_scale.reshape(1, POOL_WIDTH))


def _attn_kernel(q_ref, k_ref, v_ref, o_ref):
    tq = ATTN_TILE
    tk = tq
    qi = pl.program_id(2)
    lane = lax.broadcasted_iota(I32, (tq, LANES), 1)
    row = lax.broadcasted_iota(I32, (tq, tk), 0)
    col = lax.broadcasted_iota(I32, (tq, tk), 1)
    causal = col < row
    tri = (row > col).astype(BF16)

    def head_rows(r, h):
        rows = q_ref[0, r * tq:(r + 1) * tq, :]
        return jnp.where((lane >= h * SB_HEAD_DIM) & (lane < (h + 1) * SB_HEAD_DIM), rows, jnp.zeros_like(rows))

    def scores(qh, j):
        kb = k_ref[0, pl.ds(pl.multiple_of(j * tk, tk), tk), :]
        return lax.dot_general(qh, kb, _NT, preferred_element_type=F32)

    def values(j):
        return v_ref[0, pl.ds(pl.multiple_of(j * tk, tk), tk), :]

    def log_terms(z, mask):
        neg_abs = lax.bitcast_convert_type(lax.bitcast_convert_type(z, U32) | jnp.uint32(0x80000000), F32)
        nl = jnp.maximum(z, 0.0) + jnp.log2(1.0 + jnp.exp2(neg_abs))
        if mask:
            nl = jnp.where(causal, nl, 0.0)
        return nl.astype(BF16), z - nl, nl[:, 0:1]

    def tile(qh, j, mask):
        nl, lb, nl0 = log_terms(scores(qh, j), mask)
        ntail = jnp.dot(nl, tri, preferred_element_type=F32)
        return lb - ntail, -(ntail[:, 0:1] + nl0), values(j)

    nq = ATTN_SUBTILES
    has_prev = qi > 0
    qs = [head_rows(r, h) for r in range(nq) for h in range(2)]
    chains = []
    for n, qh in enumerate(qs):
        diag = nq * qi + n // 2
        chains += [(qh, diag, True), (qh, jnp.maximum(diag - 1, 0), False)]
    zs = [scores(qh, j) for qh, j, _ in chains]
    terms = [log_terms(z, is_diag) for z, (_, _, is_diag) in zip(zs, chains)]
    nts = [jnp.dot(nl, tri, preferred_element_type=F32) for nl, _, _ in terms]
    lbs = [lb for _, lb, _ in terms]
    tots = [-(nt[:, 0:1] + nl0) for nt, (_, _, nl0) in zip(nts, terms)]
    ws = []
    for n in range(2 * nq):
        d, p = 2 * n, 2 * n + 1
        w_d = jnp.where(causal, jnp.exp2(lbs[d] - nts[d]), 0.0)
        w_p = jnp.exp2(lbs[p] - nts[p] + tots[d])
        if n < 2:
            w_p = jnp.where(has_prev, w_p, 0.0)
        ws += [w_d.astype(BF16), w_p.astype(BF16)]
    pvs = [jnp.dot(w, values(j), preferred_element_type=F32) for w, (_, j, _) in zip(ws, chains)]
    accs = [pvs[2 * n] + pvs[2 * n + 1] for n in range(2 * nq)]
    carries = [tots[2 * n] + tots[2 * n + 1] for n in range(2 * nq)]

    def cond(st):
        j, cs, _ = st
        live = functools.reduce(jnp.maximum, [jnp.max(c) for c in cs])
        return jnp.logical_and(j >= 0, live > ATTN_SKIP_LOG2)

    def body(st):
        j, cs, acs = st
        new_c, new_a = [], []
        for n, (qh, c, a) in enumerate(zip(qs, cs, acs)):
            lag = nq - 1 - n // 2
            if lag:
                c = jnp.where(j >= lag, c, -jnp.inf)
            lw, tot, vb = tile(qh, jnp.maximum(j - lag, 0), False)
            new_a.append(a + jnp.dot(jnp.exp2(lw + c).astype(BF16), vb, preferred_element_type=F32))
            new_c.append(c + tot)
        return j - 1, tuple(new_c), tuple(new_a)

    _, _, accs = lax.while_loop(cond, body, (nq * qi + nq - 3, tuple(carries), tuple(accs)))
    for r in range(nq):
        o_ref[0, r * tq:(r + 1) * tq, :] = jnp.where(
            lane < SB_HEAD_DIM, accs[2 * r], accs[2 * r + 1]).astype(o_ref.dtype)


def _attention(q, k, v):
    B, S, W = q.shape
    tq = ATTN_SUBTILES * ATTN_TILE
    n_pairs = W // LANES
    return pl.pallas_call(
        _attn_kernel,
        grid=(B, n_pairs, S // tq),
        in_specs=[
            pl.BlockSpec((1, tq, LANES), lambda b, p, i: (b, i, p)),
            pl.BlockSpec((1, S, LANES), lambda b, p, i: (b, 0, p)),
            pl.BlockSpec((1, S, LANES), lambda b, p, i: (b, 0, p)),
        ],
        out_specs=pl.BlockSpec((1, tq, LANES), lambda b, p, i: (b, i, p)),
        out_shape=jax.ShapeDtypeStruct((B, S, W), BF16),
        compiler_params=pltpu.CompilerParams(
            dimension_semantics=("arbitrary", "arbitrary", "arbitrary"), vmem_limit_bytes=VMEM_LIMIT),
        name="stickbreak_attn",
    )(q, k, v)


def _out_proj_router_kernel(attn_ref, pool_ref, x_ref, wo_ref, g_ref, wr_ref, br_ref, tri_ref,
                            x1_ref, h2_ref, gate_ref, dest_ref, cnt_ref, carry_ref, *, capacity):
    i = pl.program_id(0)
    tm = x_ref.shape[0]

    @pl.when(i == 0)
    def _():
        carry_ref[...] = jnp.zeros_like(carry_ref)

    mixed = jnp.dot(attn_ref[...], wo_ref[0:SB_WIDTH, :], preferred_element_type=F32)
    mixed = mixed + jnp.dot(pool_ref[...], wo_ref[SB_WIDTH:, :], preferred_element_type=F32)
    x1 = x_ref[...] + mixed
    h2 = _rms(x1, g_ref[...])
    x1_ref[...] = x1
    h2_ref[...] = _pack_rows(h2)

    hh = h2.astype(BF16)
    hl = (h2 - hh.astype(F32)).astype(BF16)
    wr = wr_ref[...]
    wh = wr.astype(BF16)
    wl = (wr - wh.astype(F32)).astype(BF16)
    logits = (lax.dot_general(wh, hh, _NT, preferred_element_type=F32)
              + lax.dot_general(wh, hl, _NT, preferred_element_type=F32)
              + lax.dot_general(wl, hh, _NT, preferred_element_type=F32)) + br_ref[...]

    eid = lax.broadcasted_iota(I32, (N_EXPERTS, tm), 0).astype(F32)
    work = logits
    vals, ids = [], []
    for _ in range(TOP_K):
        m = jnp.max(work, axis=0, keepdims=True)
        sel = jnp.min(jnp.where(work == m, eid, float(N_EXPERTS)), axis=0, keepdims=True)
        vals.append(m)
        ids.append(sel)
        work = jnp.where(eid == sel, -jnp.inf, work)
    exps = [jnp.exp(v - vals[0]) for v in vals]
    denom = exps[0] + exps[1] + exps[2] + exps[3]

    onehot = jnp.zeros((N_EXPERTS, tm), F32)
    for sel in ids:
        onehot = onehot + (eid == sel).astype(F32)
    before = jnp.dot(onehot.astype(BF16), tri_ref[...], preferred_element_type=F32) + carry_ref[...]
    for kk in range(TOP_K):
        gate_ref[kk:kk + 1, :] = exps[kk] / denom
        rk = jnp.sum(jnp.where(eid == ids[kk], before, 0.0), axis=0, keepdims=True)
        dest_ref[kk:kk + 1, :] = (ids[kk] * float(capacity) + rk).astype(I32)
    carry_ref[...] = carry_ref[...] + jnp.sum(onehot, axis=1, keepdims=True)
    cnt_ref[...] = jnp.broadcast_to(carry_ref[...], cnt_ref.shape)


def _out_proj_router(attn, pool, x, w_out, g_moe, w_router, b_router, capacity):
    T, D = x.shape
    tm = PROJ_TILE
    r = lax.broadcasted_iota(I32, (tm, tm), 0)
    c = lax.broadcasted_iota(I32, (tm, tm), 1)
    tri = (r < c).astype(BF16)
    row_blk = lambda w: pl.BlockSpec((tm, w), lambda i: (i, 0))
    fixed = lambda shape: pl.BlockSpec(shape, lambda i: tuple(0 for _ in shape))
    sel_blk = pl.BlockSpec((TOP_K, tm), lambda i: (0, i))
    assert N_EXPERTS * capacity < 2 ** 24
    return pl.pallas_call(
        functools.partial(_out_proj_router_kernel, capacity=capacity),
        grid=(T // tm,),
        in_specs=[row_blk(SB_WIDTH), row_blk(POOL_WIDTH), row_blk(D), fixed((D, D)), fixed((1, D)),
                  fixed((N_EXPERTS, D)), fixed((N_EXPERTS, 1)), fixed((tm, tm))],
        out_specs=[row_blk(D), row_blk(D // 2), sel_blk, sel_blk, fixed((N_EXPERTS, LANES))],
        out_shape=[jax.ShapeDtypeStruct((T, D), F32), jax.ShapeDtypeStruct((T, D // 2), U32),
                   jax.ShapeDtypeStruct((TOP_K, T), F32), jax.ShapeDtypeStruct((TOP_K, T), I32),
                   jax.ShapeDtypeStruct((N_EXPERTS, LANES), F32)],
        scratch_shapes=[pltpu.VMEM((N_EXPERTS, 1), F32)],
        compiler_params=pltpu.CompilerParams(
            dimension_semantics=("arbitrary",), vmem_limit_bytes=VMEM_LIMIT),
        name="out_proj_router",
    )(attn, pool, x, w_out.astype(BF16), g_moe.reshape(1, D), w_router.T, b_router.reshape(N_EXPERTS, 1), tri)


SC_ROWS_PER_COPY = 64


def _sc_workers():
    info = plsc.get_sparse_core_info()
    return info.num_cores, info.num_cores * info.num_subcores


def _sc_dispatch(rows, dest, n_out):
    T, W = rows.shape
    K = dest.shape[0]
    sub = SC_ROWS_PER_COPY
    n_cores, n_workers = _sc_workers()
    per_w = T // n_workers
    n_chunks = per_w // sub
    assert per_w * n_workers == T and n_chunks * sub == per_w and n_chunks % 2 == 0
    idx = dest.reshape(K, n_workers, n_chunks, sub).transpose(1, 2, 0, 3).reshape(n_workers, n_chunks * K, sub)
    mesh = plsc.VectorSubcoreMesh(core_axis_name="core", subcore_axis_name="subcore")

    @functools.partial(
        pl.kernel, out_type=jax.ShapeDtypeStruct((n_out, W), rows.dtype), mesh=mesh,
        scratch_types=[pltpu.VMEM((n_chunks * K, sub), I32), pltpu.VMEM((2, sub, W), rows.dtype),
                       pltpu.SemaphoreType.DMA((2,)), pltpu.SemaphoreType.DMA((2,))])
    def scatter_rows(x_hbm, i_hbm, o_hbm, idx_v, buf, rsem, wsem):
        wid = lax.axis_index("subcore") * n_cores + lax.axis_index("core")
        base = wid * per_w
        pltpu.sync_copy(i_hbm.at[wid], idx_v)

        def read(c, slot):
            return pltpu.make_async_copy(x_hbm.at[pl.ds(base + c * sub, sub)], buf.at[slot], rsem.at[slot])

        def write(c, kk, slot):
            return pltpu.make_async_copy(buf.at[slot], o_hbm.at[idx_v.at[c * K + kk]], wsem.at[slot])

        read(0, 0).start()

        @pl.loop(0, n_chunks, step=2)
        def _(c0):
            for b in range(2):
                c = c0 + b
                read(c, b).wait()
                for kk in range(K):
                    write(c, kk, b).start()

                @pl.when(c + 1 < n_chunks)
                def _():
                    @pl.when(c >= 1)
                    def _():
                        for kk in range(K):
                            write(c - 1, kk, 1 - b).wait()
                    read(c + 1, 1 - b).start()

        for kk in range(K):
            write(n_chunks - 2, kk, 0).wait()
            write(n_chunks - 1, kk, 1).wait()

    return scatter_rows(rows, idx)


def _sc_gather(table, indices):
    M = indices.shape[0]
    W = table.shape[1]
    sub = SC_ROWS_PER_COPY
    n_cores, n_workers = _sc_workers()
    per_w = M // n_workers
    n_steps = per_w // sub
    assert per_w * n_workers == M and n_steps * sub == per_w and n_steps % 2 == 0
    mesh = plsc.VectorSubcoreMesh(core_axis_name="core", subcore_axis_name="subcore")

    @functools.partial(
        pl.kernel, out_type=jax.ShapeDtypeStruct((M, W), table.dtype), mesh=mesh,
        scratch_types=[pltpu.VMEM((n_steps, sub), I32), pltpu.VMEM((2, sub, W), table.dtype),
                       pltpu.SemaphoreType.DMA((2,)), pltpu.SemaphoreType.DMA((2,))])
    def gather_rows(x_hbm, i_hbm, o_hbm, idx_v, buf, gsem, wsem):
        wid = lax.axis_index("subcore") * n_cores + lax.axis_index("core")
        base = wid * per_w
        pltpu.sync_copy(i_hbm.at[wid], idx_v)

        def gather(s, slot):
            return pltpu.make_async_copy(x_hbm.at[idx_v.at[s]], buf.at[slot], gsem.at[slot])

        def write(s, slot):
            return pltpu.make_async_copy(buf.at[slot], o_hbm.at[pl.ds(base + s * sub, sub)], wsem.at[slot])

        gather(0, 0).start()

        @pl.loop(0, n_steps, step=2)
        def _(s0):
            for b in range(2):
                s = s0 + b
                gather(s, b).wait()
                write(s, b).start()

                @pl.when(s + 1 < n_steps)
                def _():
                    @pl.when(s >= 1)
                    def _():
                        write(s - 1, 1 - b).wait()
                    gather(s + 1, 1 - b).start()

        write(n_steps - 2, 0).wait()
        write(n_steps - 1, 1).wait()

    return gather_rows(table, indices.reshape(n_workers, n_steps, sub))


def _expert_kernel(vb_ref, ve_ref, vrows_ref, nvis_ref, nxt_ref,
                   x_ref, wgu_hbm, bgu_ref, wd_hbm, bd_ref, y_ref,
                   wgu_stage, wd_stage, wgu_bf, wd_bf, sem):
    v = pl.program_id(0)
    real = v < nvis_ref[0]
    e = ve_ref[v]
    first_of_expert = jnp.logical_or(v == 0, e != ve_ref[jnp.maximum(v - 1, 0)])

    def weight_copies(expert):
        return (pltpu.make_async_copy(wgu_hbm.at[expert], wgu_stage, sem.at[0]),
                pltpu.make_async_copy(wd_hbm.at[expert], wd_stage, sem.at[1]))

    @pl.when(v == 0)
    def _():
        for c in weight_copies(e):
            c.start()

    def load_weights():
        for c in weight_copies(e):
            c.wait()
        wgu_bf[...] = wgu_stage[...].astype(BF16)
        wd_bf[...] = wd_stage[...].astype(BF16)

    def fetch_next():
        nxt = nxt_ref[e]

        @pl.when(nxt >= 0)
        def _():
            for c in weight_copies(nxt):
                c.start()

    def run(n_rows):
        half = wgu_bf.shape[0] // 2
        x_lo, x_hi = _unpack_rows(x_ref[0:n_rows, :])
        gu = (jnp.dot(x_lo.astype(BF16), wgu_bf[:half, :], preferred_element_type=F32)
              + jnp.dot(x_hi.astype(BF16), wgu_bf[half:, :], preferred_element_type=F32)) + bgu_ref[0]
        gate = jnp.minimum(gu[:, :D_EXPERT], SWIGLU_LIMIT)
        up = jnp.clip(gu[:, D_EXPERT:], -SWIGLU_LIMIT, SWIGLU_LIMIT)
        act = gate * jax.nn.sigmoid(SWIGLU_ALPHA * gate) * (up + 1.0)
        y = jnp.dot(act.astype(BF16), wd_bf[...], preferred_element_type=F32) + bd_ref[0]
        y_ref[0:n_rows, :] = _pack_rows(y)

    rows = vrows_ref[v]
    full = EXPERT_ROW_CLASSES[0]
    first = jnp.logical_and(first_of_expert, real)

    @pl.when(jnp.logical_and(first, rows == full))
    def _():
        load_weights()
        run(full)
        fetch_next()

    @pl.when(jnp.logical_and(first, rows != full))
    def _():
        load_weights()
        fetch_next()

    later_full = jnp.logical_and(real, jnp.logical_not(first_of_expert))
    pl.when(jnp.logical_and(later_full, rows == full))(functools.partial(run, full))
    for n_rows in EXPERT_ROW_CLASSES[1:]:
        pl.when(jnp.logical_and(real, rows == n_rows))(functools.partial(run, n_rows))


def _experts(x_sorted, visits, w_gate_up, b_gate_up, w_down, b_down):
    n_rows, W = x_sorted.shape
    D = 2 * W
    tm = EXPERT_TILE
    n_visits = visits[0].shape[0]
    n_prefetch = len(visits)
    by_block = lambda v, vb, *_: (vb[v], 0)
    by_expert = lambda v, vb, ve, *_: (ve[v], 0, 0)
    grid_spec = pltpu.PrefetchScalarGridSpec(
        num_scalar_prefetch=n_prefetch,
        grid=(n_visits,),
        in_specs=[
            pl.BlockSpec((tm, W), by_block),
            pl.BlockSpec(memory_space=pl.ANY),
            pl.BlockSpec((1, 1, 2 * D_EXPERT), by_expert),
            pl.BlockSpec(memory_space=pl.ANY),
            pl.BlockSpec((1, 1, D), by_expert),
        ],
        out_specs=pl.BlockSpec((tm, W), by_block),
        scratch_shapes=[pltpu.VMEM((D, 2 * D_EXPERT), F32), pltpu.VMEM((D_EXPERT, D), F32),
                        pltpu.VMEM((D, 2 * D_EXPERT), BF16), pltpu.VMEM((D_EXPERT, D), BF16),
                        pltpu.SemaphoreType.DMA((2,))],
    )
    return pl.pallas_call(
        _expert_kernel,
        grid_spec=grid_spec,
        out_shape=jax.ShapeDtypeStruct((n_rows, W), U32),
        compiler_params=pltpu.CompilerParams(
            dimension_semantics=("arbitrary",), vmem_limit_bytes=VMEM_LIMIT),
        name="expert_gmm",
    )(*visits, x_sorted, w_gate_up, b_gate_up.reshape(N_EXPERTS, 1, -1), w_down, b_down.reshape(N_EXPERTS, 1, -1))


def _visit_schedule(counts, n_assign, capacity):
    tm = EXPERT_TILE
    assert EXPERT_ROW_CLASSES[0] == tm and list(EXPERT_ROW_CLASSES) == sorted(EXPERT_ROW_CLASSES, reverse=True)
    n_visits = n_assign // tm + N_EXPERTS
    n_full = counts // tm
    tail = counts - n_full * tm
    tail_rows = jnp.zeros_like(tail)
    for rows in EXPERT_ROW_CLASSES:
        tail_rows = jnp.where(tail <= rows, rows, tail_rows)
    per_e = n_full + (tail > 0)
    vend = jnp.cumsum(per_e)
    nvis = vend[-1]
    vc = jnp.minimum(jnp.arange(n_visits, dtype=I32), nvis - 1)
    done = vc[:, None] >= vend[None, :]
    e = jnp.sum(done, axis=1).astype(I32)
    local = vc - jnp.sum(jnp.where(done, per_e[None, :], 0), axis=1)
    mine = e[:, None] == jnp.arange(N_EXPERTS, dtype=I32)[None, :]
    is_tail = local >= jnp.sum(jnp.where(mine, n_full[None, :], 0), axis=1)
    vrows = jnp.where(is_tail, jnp.sum(jnp.where(mine, tail_rows[None, :], 0), axis=1), tm)
    blk = e * (capacity // tm) + local
    ids = jnp.arange(N_EXPERTS, dtype=I32)
    nonempty = counts > 0
    later = jnp.logical_and(nonempty[None, :], ids[None, :] > ids[:, None])
    nxt = jnp.where(jnp.any(later, axis=1), jnp.argmax(later, axis=1), -1).astype(I32)
    return blk.astype(I32), e, vrows.astype(I32), nvis.reshape(1).astype(I32), nxt


def _combine_kernel(y0_ref, y1_ref, y2_ref, y3_ref, x1_ref, gate_ref, g_ref, o_ref):
    half = x1_ref.shape[1] // 2
    gates = gate_ref[...]
    lo = x1_ref[:, :half]
    hi = x1_ref[:, half:]
    for kk, y_ref in enumerate((y0_ref, y1_ref, y2_ref, y3_ref)):
        y_lo, y_hi = _unpack_rows(y_ref[...])
        lo = lo + gates[:, kk:kk + 1] * y_lo
        hi = hi + gates[:, kk:kk + 1] * y_hi
    ms = (jnp.sum(lo * lo, axis=-1, keepdims=True) + jnp.sum(hi * hi, axis=-1, keepdims=True)) / (2 * half)
    scale = lax.rsqrt(ms + RMS_EPS)
    o_ref[:, :half] = lo * scale * g_ref[:, :half]
    o_ref[:, half:] = hi * scale * g_ref[:, half:]


def _combine(y_tok, x1, gates_t, g_final):
    T, D = x1.shape
    tm = COMBINE_TILE
    nt = T // tm
    y_spec = lambda kk: pl.BlockSpec((tm, D // 2), lambda i: (kk * nt + i, 0))
    return pl.pallas_call(
        _combine_kernel,
        grid=(nt,),
        in_specs=[y_spec(0), y_spec(1), y_spec(2), y_spec(3),
                  pl.BlockSpec((tm, D), lambda i: (i, 0)),
                  pl.BlockSpec((tm, TOP_K), lambda i: (i, 0)),
                  pl.BlockSpec((1, D), lambda i: (0, 0))],
        out_specs=pl.BlockSpec((tm, D), lambda i: (i, 0)),
        out_shape=jax.ShapeDtypeStruct((T, D), F32),
        compiler_params=pltpu.CompilerParams(
            dimension_semantics=("arbitrary",), vmem_limit_bytes=VMEM_LIMIT),
        name="combine_norm",
    )(y_tok, y_tok, y_tok, y_tok, x1, gates_t, g_final.reshape(1, D))


def kernel(x, g_mix, w_in, pool_w, pool_scale, w_out, g_moe, w_router, b_router, w_gate_up, b_gate_up,
           w_down, b_down, g_final):
    B, S, D = x.shape
    T = B * S
    assert g_mix.shape[0] == 1, "single-layer problem: the final norm is fused into the combine step"
    q, k, v, pool = _in_proj(x, g_mix[0], w_in[0], pool_w[0], pool_scale[0])
    attn = _attention(q, k, v)
    capacity = T
    x1, h2, gates, dest, cnt = _out_proj_router(
        attn.reshape(T, SB_WIDTH), pool.reshape(T, POOL_WIDTH), x.reshape(T, D),
        w_out[0], g_moe[0], w_router[0], b_router[0], capacity)
    x_sorted = _sc_dispatch(h2, dest, N_EXPERTS * capacity)
    visits = _visit_schedule(cnt[:, 0].astype(I32), T * TOP_K, capacity)
    y = _experts(x_sorted, visits, w_gate_up[0], b_gate_up[0], w_down[0], b_down[0])
    y_tok = _sc_gather(y, dest.reshape(-1))
    return _combine(y_tok, x1, gates.T, g_final).reshape(B, S, D)
```

```python
import functools

import jax
import jax.numpy as jnp
from jax import lax
from jax.experimental import pallas as pl
from jax.experimental.pallas import tpu as pltpu
from jax.experimental.pallas import tpu_sc as plsc

F32 = jnp.float32
BF16 = jnp.bfloat16
I32 = jnp.int32
U32 = jnp.uint32

D_MODEL = 1024
SB_HEADS = 8
SB_HEAD_DIM = 64
SB_WIDTH = SB_HEADS * SB_HEAD_DIM
POOL_WINDOWS = (2, 4, 8, 16)
POOL_WIDTH = 512
POOL_GROUP_DIM = 128
N_EXPERTS = 32
TOP_K = 4
D_EXPERT = 1024
SWIGLU_LIMIT = 7.0
SWIGLU_ALPHA = 1.702
RMS_EPS = 1e-5

LANES = 128
HALO = 32
PROJ_TILE = 1024
ATTN_TILE = 256
EXPERT_TILE = 1024
EXPERT_ROW_CLASSES = (1024, 512, 256)
COMBINE_TILE = 1024
ATTN_SKIP_LOG2 = -160.0
LOG2_E = 1.4426950408889634
V7X_VMEM_BYTES = 64 * 1024 * 1024
VMEM_LIMIT = V7X_VMEM_BYTES * 7 // 8

_NT = (((1,), (1,)), ((), ()))


def _rms(x, g):
    ms = jnp.mean(x * x, axis=-1, keepdims=True)
    return x * lax.rsqrt(ms + RMS_EPS) * g


def _pack_rows(x):
    n = x.shape[1] // 2
    lo = lax.bitcast_convert_type(x[:, :n].astype(BF16).astype(F32), U32)
    hi = lax.bitcast_convert_type(x[:, n:].astype(BF16).astype(F32), U32)
    return (lo >> 16) | (hi & jnp.uint32(0xFFFF0000))


def _unpack_rows(w):
    lo = lax.bitcast_convert_type(w << 16, F32)
    hi = lax.bitcast_convert_type(w & jnp.uint32(0xFFFF0000), F32)
    return lo, hi


def _proj_attn_kernel(x_ref, w_ref, pw_ref, ps_ref, o_ref, p_ref,
                      xb_ref, inv_ref, q_buf, k_buf, v_buf, p_buf, uext_ref, lvl_ref):
    g = pl.program_id(1)
    n_steps = pl.num_programs(1) - 1
    n_pairs = w_ref.shape[0]
    tm = x_ref.shape[1]
    tgt = jnp.minimum(g, n_steps - 1)
    blk, p = tgt // n_pairs, tgt % n_pairs
    att = jnp.maximum(g - 1, 0)
    qi, pa = att // n_pairs, att % n_pairs
    slot = g % 2

    @pl.when(g == 0)
    def _():
        q_buf[...] = jnp.zeros_like(q_buf)
        k_buf[...] = jnp.zeros_like(k_buf)
        v_buf[...] = jnp.zeros_like(v_buf)
        p_buf[...] = jnp.zeros_like(p_buf)
        uext_ref[...] = jnp.zeros_like(uext_ref)
        lvl_ref[...] = jnp.zeros_like(lvl_ref)

    @pl.when(jnp.logical_and(p == 0, g < n_steps))
    def _():
        x = x_ref[0]
        inv_ref[...] = lax.rsqrt(jnp.mean(x * x, axis=-1, keepdims=True) + RMS_EPS)
        xb_ref[...] = x.astype(BF16)

    tq = ATTN_TILE
    tk = tq
    nq = tm // tq
    q_slot = 1 - slot
    lane = lax.broadcasted_iota(I32, (tq, LANES), 1)
    row = lax.broadcasted_iota(I32, (tq, tk), 0)
    col = lax.broadcasted_iota(I32, (tq, tk), 1)
    causal = col < row
    tri = (row > col).astype(BF16)

    def head_rows(r, h):
        rows = q_buf[q_slot, r * tq:(r + 1) * tq, :]
        return jnp.where((lane >= h * SB_HEAD_DIM) & (lane < (h + 1) * SB_HEAD_DIM), rows, jnp.zeros_like(rows))

    def scores(qh, j):
        kb = k_buf[pa, pl.ds(pl.multiple_of(j * tk, tk), tk), :]
        return lax.dot_general(qh, kb, _NT, preferred_element_type=F32)

    def values(j):
        return v_buf[pa, pl.ds(pl.multiple_of(j * tk, tk), tk), :]

    def log_terms(z, mask):
        neg_abs = lax.bitcast_convert_type(lax.bitcast_convert_type(z, U32) | jnp.uint32(0x80000000), F32)
        nl = jnp.maximum(z, 0.0) + jnp.log2(1.0 + jnp.exp2(neg_abs))
        if mask:
            nl = jnp.where(causal, nl, 0.0)
        return nl.astype(BF16), z - nl, nl[:, 0:1]

    def tile(qh, j, mask):
        nl, lb, nl0 = log_terms(scores(qh, j), mask)
        ntail = jnp.dot(nl, tri, preferred_element_type=F32)
        return lb - ntail, -(ntail[:, 0:1] + nl0), values(j)

    has_prev = qi > 0
    qs = [head_rows(r, h) for r in range(nq) for h in range(2)]
    chains = []
    for n, qh in enumerate(qs):
        diag = nq * qi + n // 2
        chains += [(qh, diag, True), (qh, jnp.maximum(diag - 1, 0), False)]
    zs = [scores(qh, j) for qh, j, _ in chains]

    inv = inv_ref[...]
    proj = jnp.dot(xb_ref[...], w_ref[p], preferred_element_type=F32)

    terms = [log_terms(z, is_diag) for z, (_, _, is_diag) in zip(zs, chains)]
    nts = [jnp.dot(nl, tri, preferred_element_type=F32) for nl, _, _ in terms]

    row0 = pl.multiple_of(blk * tm, tm)
    q_buf[slot] = (proj[:, 0:LANES] * (inv * (LOG2_E * SB_HEAD_DIM ** -0.5))).astype(BF16)
    k_buf[p, pl.ds(row0, tm), :] = (proj[:, LANES:2 * LANES] * inv).astype(BF16)
    v_buf[p, pl.ds(row0, tm), :] = (proj[:, 2 * LANES:3 * LANES] * inv).astype(BF16)
    u = proj[:, 3 * LANES:] * inv
    lo = HALO // 2
    uext_ref[p, HALO:, :] = u
    lvl_ref[0, lo:, :] = uext_ref[p, lo:, :] + uext_ref[p, lo - 1:HALO + tm - 1, :]
    window_sum = lvl_ref[0, HALO:, :]
    cur, k, level = 0, 2, 1
    while k < POOL_WINDOWS[-1]:
        lvl_ref[1 - cur, lo:, :] = lvl_ref[cur, lo:, :] + lvl_ref[cur, lo - k:HALO + tm - k, :]
        cur, k = 1 - cur, 2 * k
        window_sum = jnp.where(p >= level, lvl_ref[cur, HALO:, :], window_sum)
        level += 1
    t = blk * tm + lax.broadcasted_iota(I32, (tm, 1), 0)
    count = jnp.minimum(t + 1, 2 << p).astype(F32)
    pooled = window_sum / count - u
    mixed = jnp.dot(pooled.astype(BF16), pw_ref[p], preferred_element_type=F32)
    p_buf[slot] = (mixed * ps_ref[p]).astype(BF16)
    uext_ref[p, 0:HALO, :] = u[tm - HALO:, :]

    lbs = [lb for _, lb, _ in terms]
    tots = [-(nt[:, 0:1] + nl0) for nt, (_, _, nl0) in zip(nts, terms)]
    ws = []
    for n in range(2 * nq):
        d, e = 2 * n, 2 * n + 1
        w_d = jnp.where(causal, jnp.exp2(lbs[d] - nts[d]), 0.0)
        w_e = jnp.exp2(lbs[e] - nts[e] + tots[d])
        if n < 2:
            w_e = jnp.where(has_prev, w_e, 0.0)
        ws += [w_d.astype(BF16), w_e.astype(BF16)]
    pvs = [jnp.dot(w, values(j), preferred_element_type=F32) for w, (_, j, _) in zip(ws, chains)]
    accs = [pvs[2 * n] + pvs[2 * n + 1] for n in range(2 * nq)]
    carries = [tots[2 * n] + tots[2 * n + 1] for n in range(2 * nq)]

    def cond(st):
        j, cs, _ = st
        live = functools.reduce(jnp.maximum, [jnp.max(c) for c in cs])
        return jnp.logical_and(j >= 0, live > ATTN_SKIP_LOG2)

    def body(st):
        j, cs, acs = st
        new_c, new_a = [], []
        for n, (qh, c, a) in enumerate(zip(qs, cs, acs)):
            lag = nq - 1 - n // 2
            if lag:
                c = jnp.where(j >= lag, c, -jnp.inf)
            lw, tot, vb = tile(qh, jnp.maximum(j - lag, 0), False)
            new_a.append(a + jnp.dot(jnp.exp2(lw + c).astype(BF16), vb, preferred_element_type=F32))
            new_c.append(c + tot)
        return j - 1, tuple(new_c), tuple(new_a)

    _, _, accs = lax.while_loop(cond, body, (nq * qi + nq - 3, tuple(carries), tuple(accs)))
    for r in range(nq):
        o_ref[0, r * tq:(r + 1) * tq, :] = jnp.where(
            lane < SB_HEAD_DIM, accs[2 * r], accs[2 * r + 1]).astype(o_ref.dtype)
    p_ref[0] = p_buf[q_slot]


def _proj_attention(x, g_mix, w_in, pool_w, pool_scale):
    B, S, D = x.shape
    tm = PROJ_TILE
    n_pairs = SB_WIDTH // LANES
    assert n_pairs == len(POOL_WINDOWS) and POOL_GROUP_DIM == LANES and tm % ATTN_TILE == 0
    assert all(w == 2 << g for g, w in enumerate(POOL_WINDOWS))
    w = (g_mix[:, None] * w_in).astype(BF16).reshape(D, 4, n_pairs, LANES).transpose(2, 0, 1, 3)
    w = w.reshape(n_pairs, D, 4 * LANES)
    out_sd = jax.ShapeDtypeStruct((B, S, SB_WIDTH), BF16)
    n_steps = (S // tm) * n_pairs
    att = lambda g: jnp.maximum(g - 1, 0)
    out_blk = pl.BlockSpec((1, tm, LANES), lambda b, g: (b, att(g) // n_pairs, att(g) % n_pairs))
    return pl.pallas_call(
        _proj_attn_kernel,
        grid=(B, n_steps + 1),
        in_specs=[
            pl.BlockSpec((1, tm, D), lambda b, g: (b, jnp.minimum(g, n_steps - 1) // n_pairs, 0)),
            pl.BlockSpec(w.shape, lambda b, g: (0, 0, 0)),
            pl.BlockSpec(pool_w.shape, lambda b, g: (0, 0, 0)),
            pl.BlockSpec((n_pairs, 1, LANES), lambda b, g: (0, 0, 0)),
        ],
        out_specs=[out_blk, out_blk],
        out_shape=[out_sd, out_sd],
        scratch_shapes=[pltpu.VMEM((tm, D), BF16), pltpu.VMEM((tm, 1), F32),
                        pltpu.VMEM((2, tm, LANES), BF16),
                        pltpu.VMEM((n_pairs, S, LANES), BF16), pltpu.VMEM((n_pairs, S, LANES), BF16),
                        pltpu.VMEM((2, tm, LANES), BF16),
                        pltpu.VMEM((n_pairs, HALO + tm, LANES), F32), pltpu.VMEM((2, HALO + tm, LANES), F32)],
        compiler_params=pltpu.CompilerParams(
            dimension_semantics=("arbitrary", "arbitrary"), vmem_limit_bytes=VMEM_LIMIT),
        name="proj_attn",
    )(x, w, pool_w.astype(BF16), pool_scale.reshape(n_pairs, 1, LANES))


def _out_proj_router_kernel(attn_ref, pool_ref, x_ref, wo_ref, g_ref, wr_ref, br_ref, tri_ref,
                            x1_ref, h2_ref, gate_ref, dest_ref, cnt_ref, carry_ref, *, capacity):
    i = pl.program_id(0)
    tm = x_ref.shape[0]

    @pl.when(i == 0)
    def _():
        carry_ref[...] = jnp.zeros_like(carry_ref)

    mixed = jnp.dot(attn_ref[...], wo_ref[0:SB_WIDTH, :], preferred_element_type=F32)
    mixed = mixed + jnp.dot(pool_ref[...], wo_ref[SB_WIDTH:, :], preferred_element_type=F32)
    x1 = x_ref[...] + mixed
    h2 = _rms(x1, g_ref[...])
    x1_ref[...] = x1
    h2_ref[...] = _pack_rows(h2)

    hh = h2.astype(BF16)
    hl = (h2 - hh.astype(F32)).astype(BF16)
    wr = wr_ref[...]
    wh = wr.astype(BF16)
    wl = (wr - wh.astype(F32)).astype(BF16)
    logits = (lax.dot_general(wh, hh, _NT, preferred_element_type=F32)
              + lax.dot_general(wh, hl, _NT, preferred_element_type=F32)
              + lax.dot_general(wl, hh, _NT, preferred_element_type=F32)) + br_ref[...]

    eid = lax.broadcasted_iota(I32, (N_EXPERTS, tm), 0).astype(F32)
    work = logits
    vals, ids = [], []
    for _ in range(TOP_K):
        m = jnp.max(work, axis=0, keepdims=True)
        sel = jnp.min(jnp.where(work == m, eid, float(N_EXPERTS)), axis=0, keepdims=True)
        vals.append(m)
        ids.append(sel)
        work = jnp.where(eid == sel, -jnp.inf, work)
    exps = [jnp.exp(v - vals[0]) for v in vals]
    denom = exps[0] + exps[1] + exps[2] + exps[3]

    onehot = jnp.zeros((N_EXPERTS, tm), F32)
    for sel in ids:
        onehot = onehot + (eid == sel).astype(F32)
    before = jnp.dot(onehot.astype(BF16), tri_ref[...], preferred_element_type=F32) + carry_ref[...]
    for kk in range(TOP_K):
        gate_ref[kk:kk + 1, :] = exps[kk] / denom
        rk = jnp.sum(jnp.where(eid == ids[kk], before, 0.0), axis=0, keepdims=True)
        dest_ref[kk:kk + 1, :] = (ids[kk] * float(capacity) + rk).astype(I32)
    carry_ref[...] = carry_ref[...] + jnp.sum(onehot, axis=1, keepdims=True)
    cnt_ref[...] = jnp.broadcast_to(carry_ref[...], cnt_ref.shape)


def _out_proj_router(attn, pool, x, w_out, g_moe, w_router, b_router, capacity):
    T, D = x.shape
    tm = PROJ_TILE
    r = lax.broadcasted_iota(I32, (tm, tm), 0)
    c = lax.broadcasted_iota(I32, (tm, tm), 1)
    tri = (r < c).astype(BF16)
    row_blk = lambda w: pl.BlockSpec((tm, w), lambda i: (i, 0))
    fixed = lambda shape: pl.BlockSpec(shape, lambda i: tuple(0 for _ in shape))
    sel_blk = pl.BlockSpec((TOP_K, tm), lambda i: (0, i))
    assert N_EXPERTS * capacity < 2 ** 24
    return pl.pallas_call(
        functools.partial(_out_proj_router_kernel, capacity=capacity),
        grid=(T // tm,),
        in_specs=[row_blk(SB_WIDTH), row_blk(POOL_WIDTH), row_blk(D), fixed((D, D)), fixed((1, D)),
                  fixed((N_EXPERTS, D)), fixed((N_EXPERTS, 1)), fixed((tm, tm))],
        out_specs=[row_blk(D), row_blk(D // 2), sel_blk, sel_blk, fixed((N_EXPERTS, LANES))],
        out_shape=[jax.ShapeDtypeStruct((T, D), F32), jax.ShapeDtypeStruct((T, D // 2), U32),
                   jax.ShapeDtypeStruct((TOP_K, T), F32), jax.ShapeDtypeStruct((TOP_K, T), I32),
                   jax.ShapeDtypeStruct((N_EXPERTS, LANES), F32)],
        scratch_shapes=[pltpu.VMEM((N_EXPERTS, 1), F32)],
        compiler_params=pltpu.CompilerParams(
            dimension_semantics=("arbitrary",), vmem_limit_bytes=VMEM_LIMIT),
        name="out_proj_router",
    )(attn, pool, x, w_out.astype(BF16), g_moe.reshape(1, D), w_router.T, b_router.reshape(N_EXPERTS, 1), tri)


SC_ROWS_PER_COPY = 64


def _sc_workers():
    info = plsc.get_sparse_core_info()
    return info.num_cores, info.num_cores * info.num_subcores


def _sc_dispatch(rows, dest, n_out):
    T, W = rows.shape
    K = dest.shape[0]
    sub = SC_ROWS_PER_COPY
    n_cores, n_workers = _sc_workers()
    per_w = T // n_workers
    n_chunks = per_w // sub
    assert per_w * n_workers == T and n_chunks * sub == per_w and n_chunks % 2 == 0
    idx = dest.reshape(K, n_workers, n_chunks, sub).transpose(1, 2, 0, 3).reshape(n_workers, n_chunks * K, sub)
    mesh = plsc.VectorSubcoreMesh(core_axis_name="core", subcore_axis_name="subcore")

    @functools.partial(
        pl.kernel, out_type=jax.ShapeDtypeStruct((n_out, W), rows.dtype), mesh=mesh,
        scratch_types=[pltpu.VMEM((n_chunks * K, sub), I32), pltpu.VMEM((2, sub, W), rows.dtype),
                       pltpu.SemaphoreType.DMA((2,)), pltpu.SemaphoreType.DMA((2,))])
    def scatter_rows(x_hbm, i_hbm, o_hbm, idx_v, buf, rsem, wsem):
        wid = lax.axis_index("subcore") * n_cores + lax.axis_index("core")
        base = wid * per_w
        pltpu.sync_copy(i_hbm.at[wid], idx_v)

        def read(c, slot):
            return pltpu.make_async_copy(x_hbm.at[pl.ds(base + c * sub, sub)], buf.at[slot], rsem.at[slot])

        def write(c, kk, slot):
            return pltpu.make_async_copy(buf.at[slot], o_hbm.at[idx_v.at[c * K + kk]], wsem.at[slot])

        read(0, 0).start()

        @pl.loop(0, n_chunks, step=2)
        def _(c0):
            for b in range(2):
                c = c0 + b
                read(c, b).wait()
                for kk in range(K):
                    write(c, kk, b).start()

                @pl.when(c + 1 < n_chunks)
                def _():
                    @pl.when(c >= 1)
                    def _():
                        for kk in range(K):
                            write(c - 1, kk, 1 - b).wait()
                    read(c + 1, 1 - b).start()

        for kk in range(K):
            write(n_chunks - 2, kk, 0).wait()
            write(n_chunks - 1, kk, 1).wait()

    return scatter_rows(rows, idx)


def _sc_gather(table, indices):
    M = indices.shape[0]
    W = table.shape[1]
    sub = SC_ROWS_PER_COPY
    n_cores, n_workers = _sc_workers()
    per_w = M // n_workers
    n_steps = per_w // sub
    assert per_w * n_workers == M and n_steps * sub == per_w and n_steps % 2 == 0
    mesh = plsc.VectorSubcoreMesh(core_axis_name="core", subcore_axis_name="subcore")

    @functools.partial(
        pl.kernel, out_type=jax.ShapeDtypeStruct((M, W), table.dtype), mesh=mesh,
        scratch_types=[pltpu.VMEM((n_steps, sub), I32), pltpu.VMEM((2, sub, W), table.dtype),
                       pltpu.SemaphoreType.DMA((2,)), pltpu.SemaphoreType.DMA((2,))])
    def gather_rows(x_hbm, i_hbm, o_hbm, idx_v, buf, gsem, wsem):
        wid = lax.axis_index("subcore") * n_cores + lax.axis_index("core")
        base = wid * per_w
        pltpu.sync_copy(i_hbm.at[wid], idx_v)

        def gather(s, slot):
            return pltpu.make_async_copy(x_hbm.at[idx_v.at[s]], buf.at[slot], gsem.at[slot])

        def write(s, slot):
            return pltpu.make_async_copy(buf.at[slot], o_hbm.at[pl.ds(base + s * sub, sub)], wsem.at[slot])

        gather(0, 0).start()

        @pl.loop(0, n_steps, step=2)
        def _(s0):
            for b in range(2):
                s = s0 + b
                gather(s, b).wait()
                write(s, b).start()

                @pl.when(s + 1 < n_steps)
                def _():
                    @pl.when(s >= 1)
                    def _():
                        write(s - 1, 1 - b).wait()
                    gather(s + 1, 1 - b).start()

        write(n_steps - 2, 0).wait()
        write(n_steps - 1, 1).wait()

    return gather_rows(table, indices.reshape(n_workers, n_steps, sub))


def _expert_kernel(vb_ref, ve_ref, vrows_ref, nvis_ref, nxt_ref,
                   x_ref, wgu_hbm, bgu_ref, wd_hbm, bd_ref, y_ref,
                   wgu_stage, wd_stage, wgu_bf, wd_bf, sem):
    v = pl.program_id(0)
    real = v < nvis_ref[0]
    e = ve_ref[v]
    first_of_expert = jnp.logical_or(v == 0, e != ve_ref[jnp.maximum(v - 1, 0)])

    def weight_copies(expert):
        return (pltpu.make_async_copy(wgu_hbm.at[expert], wgu_stage, sem.at[0]),
                pltpu.make_async_copy(wd_hbm.at[expert], wd_stage, sem.at[1]))

    @pl.when(v == 0)
    def _():
        for c in weight_copies(e):
            c.start()

    def load_weights():
        for c in weight_copies(e):
            c.wait()
        wgu_bf[...] = wgu_stage[...].astype(BF16)
        wd_bf[...] = wd_stage[...].astype(BF16)

    def fetch_next():
        nxt = nxt_ref[e]

        @pl.when(nxt >= 0)
        def _():
            for c in weight_copies(nxt):
                c.start()

    def run(n_rows):
        half = wgu_bf.shape[0] // 2
        x_lo, x_hi = _unpack_rows(x_ref[0:n_rows, :])
        gu = (jnp.dot(x_lo.astype(BF16), wgu_bf[:half, :], preferred_element_type=F32)
              + jnp.dot(x_hi.astype(BF16), wgu_bf[half:, :], preferred_element_type=F32)) + bgu_ref[0]
        gate = jnp.minimum(gu[:, :D_EXPERT], SWIGLU_LIMIT)
        up = jnp.clip(gu[:, D_EXPERT:], -SWIGLU_LIMIT, SWIGLU_LIMIT)
        act = gate * jax.nn.sigmoid(SWIGLU_ALPHA * gate) * (up + 1.0)
        y = jnp.dot(act.astype(BF16), wd_bf[...], preferred_element_type=F32) + bd_ref[0]
        y_ref[0:n_rows, :] = _pack_rows(y)

    rows = vrows_ref[v]
    full = EXPERT_ROW_CLASSES[0]
    first = jnp.logical_and(first_of_expert, real)

    @pl.when(jnp.logical_and(first, rows == full))
    def _():
        load_weights()
        run(full)
        fetch_next()

    @pl.when(jnp.logical_and(first, rows != full))
    def _():
        load_weights()
        fetch_next()

    later_full = jnp.logical_and(real, jnp.logical_not(first_of_expert))
    pl.when(jnp.logical_and(later_full, rows == full))(functools.partial(run, full))
    for n_rows in EXPERT_ROW_CLASSES[1:]:
        pl.when(jnp.logical_and(real, rows == n_rows))(functools.partial(run, n_rows))


def _experts(x_sorted, visits, w_gate_up, b_gate_up, w_down, b_down):
    n_rows, W = x_sorted.shape
    D = 2 * W
    tm = EXPERT_TILE
    n_visits = visits[0].shape[0]
    n_prefetch = len(visits)
    by_block = lambda v, vb, *_: (vb[v], 0)
    by_expert = lambda v, vb, ve, *_: (ve[v], 0, 0)
    grid_spec = pltpu.PrefetchScalarGridSpec(
        num_scalar_prefetch=n_prefetch,
        grid=(n_visits,),
        in_specs=[
            pl.BlockSpec((tm, W), by_block),
            pl.BlockSpec(memory_space=pl.ANY),
            pl.BlockSpec((1, 1, 2 * D_EXPERT), by_expert),
            pl.BlockSpec(memory_space=pl.ANY),
            pl.BlockSpec((1, 1, D), by_expert),
        ],
        out_specs=pl.BlockSpec((tm, W), by_block),
        scratch_shapes=[pltpu.VMEM((D, 2 * D_EXPERT), F32), pltpu.VMEM((D_EXPERT, D), F32),
                        pltpu.VMEM((D, 2 * D_EXPERT), BF16), pltpu.VMEM((D_EXPERT, D), BF16),
                        pltpu.SemaphoreType.DMA((2,))],
    )
    return pl.pallas_call(
        _expert_kernel,
        grid_spec=grid_spec,
        out_shape=jax.ShapeDtypeStruct((n_rows, W), U32),
        compiler_params=pltpu.CompilerParams(
            dimension_semantics=("arbitrary",), vmem_limit_bytes=VMEM_LIMIT),
        name="expert_gmm",
    )(*visits, x_sorted, w_gate_up, b_gate_up.reshape(N_EXPERTS, 1, -1), w_down, b_down.reshape(N_EXPERTS, 1, -1))


def _visit_schedule(counts, n_assign, capacity):
    tm = EXPERT_TILE
    assert EXPERT_ROW_CLASSES[0] == tm and list(EXPERT_ROW_CLASSES) == sorted(EXPERT_ROW_CLASSES, reverse=True)
    n_visits = n_assign // tm + N_EXPERTS
    n_full = counts // tm
    tail = counts - n_full * tm
    tail_rows = jnp.zeros_like(tail)
    for rows in EXPERT_ROW_CLASSES:
        tail_rows = jnp.where(tail <= rows, rows, tail_rows)
    per_e = n_full + (tail > 0)
    vend = jnp.cumsum(per_e)
    nvis = vend[-1]
    vc = jnp.minimum(jnp.arange(n_visits, dtype=I32), nvis - 1)
    done = vc[:, None] >= vend[None, :]
    e = jnp.sum(done, axis=1).astype(I32)
    local = vc - jnp.sum(jnp.where(done, per_e[None, :], 0), axis=1)
    mine = e[:, None] == jnp.arange(N_EXPERTS, dtype=I32)[None, :]
    is_tail = local >= jnp.sum(jnp.where(mine, n_full[None, :], 0), axis=1)
    vrows = jnp.where(is_tail, jnp.sum(jnp.where(mine, tail_rows[None, :], 0), axis=1), tm)
    blk = e * (capacity // tm) + local
    ids = jnp.arange(N_EXPERTS, dtype=I32)
    nonempty = counts > 0
    later = jnp.logical_and(nonempty[None, :], ids[None, :] > ids[:, None])
    nxt = jnp.where(jnp.any(later, axis=1), jnp.argmax(later, axis=1), -1).astype(I32)
    return blk.astype(I32), e, vrows.astype(I32), nvis.reshape(1).astype(I32), nxt


def _combine_kernel(y0_ref, y1_ref, y2_ref, y3_ref, x1_ref, gate_ref, g_ref, o_ref):
    half = x1_ref.shape[1] // 2
    gates = gate_ref[...]
    lo = x1_ref[:, :half]
    hi = x1_ref[:, half:]
    for kk, y_ref in enumerate((y0_ref, y1_ref, y2_ref, y3_ref)):
        y_lo, y_hi = _unpack_rows(y_ref[...])
        lo = lo + gates[:, kk:kk + 1] * y_lo
        hi = hi + gates[:, kk:kk + 1] * y_hi
    ms = (jnp.sum(lo * lo, axis=-1, keepdims=True) + jnp.sum(hi * hi, axis=-1, keepdims=True)) / (2 * half)
    scale = lax.rsqrt(ms + RMS_EPS)
    o_ref[:, :half] = lo * scale * g_ref[:, :half]
    o_ref[:, half:] = hi * scale * g_ref[:, half:]


def _combine(y_tok, x1, gates_t, g_final):
    T, D = x1.shape
    tm = COMBINE_TILE
    nt = T // tm
    y_spec = lambda kk: pl.BlockSpec((tm, D // 2), lambda i: (kk * nt + i, 0))
    return pl.pallas_call(
        _combine_kernel,
        grid=(nt,),
        in_specs=[y_spec(0), y_spec(1), y_spec(2), y_spec(3),
                  pl.BlockSpec((tm, D), lambda i: (i, 0)),
                  pl.BlockSpec((tm, TOP_K), lambda i: (i, 0)),
                  pl.BlockSpec((1, D), lambda i: (0, 0))],
        out_specs=pl.BlockSpec((tm, D), lambda i: (i, 0)),
        out_shape=jax.ShapeDtypeStruct((T, D), F32),
        compiler_params=pltpu.CompilerParams(
            dimension_semantics=("arbitrary",), vmem_limit_bytes=VMEM_LIMIT),
        name="combine_norm",
    )(y_tok, y_tok, y_tok, y_tok, x1, gates_t, g_final.reshape(1, D))


def kernel(x, g_mix, w_in, pool_w, pool_scale, w_out, g_moe, w_router, b_router, w_gate_up, b_gate_up,
           w_down, b_down, g_final):
    B, S, D = x.shape
    T = B * S
    assert g_mix.shape[0] == 1, "single-layer problem: the final norm is fused into the combine step"
    attn, pool = _proj_attention(x, g_mix[0], w_in[0], pool_w[0], pool_scale[0])
    capacity = T
    x1, h2, gates, dest, cnt = _out_proj_router(
        attn.reshape(T, SB_WIDTH), pool.reshape(T, POOL_WIDTH), x.reshape(T, D),
        w_out[0], g_moe[0], w_router[0], b_router[0], capacity)
    x_sorted = _sc_dispatch(h2, dest, N_EXPERTS * capacity)
    visits = _visit_schedule(cnt[:, 0].astype(I32), T * TOP_K, capacity)
    y = _experts(x_sorted, visits, w_gate_up[0], b_gate_up[0], w_down[0], b_down[0])
    y_tok = _sc_gather(y, dest.reshape(-1))
    return _combine(y_tok, x1, gates.T, g_final).reshape(B, S, D)
```

```python
import functools

import jax
import jax.numpy as jnp
from jax import lax
from jax.experimental import pallas as pl
from jax.experimental.pallas import tpu as pltpu
from jax.experimental.pallas import tpu_sc as plsc

F32 = jnp.float32
BF16 = jnp.bfloat16
I32 = jnp.int32
U32 = jnp.uint32

D_MODEL = 1024
SB_HEADS = 8
SB_HEAD_DIM = 64
SB_WIDTH = SB_HEADS * SB_HEAD_DIM
POOL_WINDOWS = (2, 4, 8, 16)
POOL_WIDTH = 512
POOL_GROUP_DIM = 128
N_EXPERTS = 32
TOP_K = 4
D_EXPERT = 1024
SWIGLU_LIMIT = 7.0
SWIGLU_ALPHA = 1.702
RMS_EPS = 1e-5

LANES = 128
HALO = 32
PROJ_TILE = 1024
ATTN_TILE = 256
ATTN_SUBTILES = 4
EXPERT_TILE = 1024
EXPERT_ROW_CLASSES = (1024, 512, 256)
COMBINE_TILE = 1024
ATTN_SKIP_LOG2 = -160.0
LOG2_E = 1.4426950408889634
V7X_VMEM_BYTES = 64 * 1024 * 1024
VMEM_LIMIT = V7X_VMEM_BYTES * 7 // 8

_NT = (((1,), (1,)), ((), ()))


def _rms(x, g):
    ms = jnp.mean(x * x, axis=-1, keepdims=True)
    return x * lax.rsqrt(ms + RMS_EPS) * g


def _pack_rows(x):
    n = x.shape[1] // 2
    lo = lax.bitcast_convert_type(x[:, :n].astype(BF16).astype(F32), U32)
    hi = lax.bitcast_convert_type(x[:, n:].astype(BF16).astype(F32), U32)
    return (lo >> 16) | (hi & jnp.uint32(0xFFFF0000))


def _unpack_rows(w):
    lo = lax.bitcast_convert_type(w << 16, F32)
    hi = lax.bitcast_convert_type(w & jnp.uint32(0xFFFF0000), F32)
    return lo, hi


def _in_proj_kernel(x_ref, w_ref, pw_ref, ps_ref, q_ref, k_ref, v_ref, p_ref, uext_ref, xb_ref, lvl_ref):
    s = pl.program_id(1)
    tm = x_ref.shape[1]
    x = x_ref[0]
    inv = lax.rsqrt(jnp.mean(x * x, axis=-1, keepdims=True) + RMS_EPS)
    xb_ref[...] = x.astype(BF16)
    proj = jnp.dot(xb_ref[...], w_ref[...], preferred_element_type=F32)
    q_ref[0] = (proj[:, 0:SB_WIDTH] * (inv * (LOG2_E * SB_HEAD_DIM ** -0.5))).astype(BF16)
    k_ref[0] = (proj[:, SB_WIDTH:2 * SB_WIDTH] * inv).astype(BF16)
    v_ref[0] = (proj[:, 2 * SB_WIDTH:3 * SB_WIDTH] * inv).astype(BF16)
    u = proj[:, 3 * SB_WIDTH:] * inv

    lo = HALO // 2

    @pl.when(s == 0)
    def _():
        uext_ref[0:HALO, :] = jnp.zeros((HALO, POOL_WIDTH), F32)
        lvl_ref[:, 0:lo, :] = jnp.zeros((2, lo, POOL_GROUP_DIM), F32)

    uext_ref[HALO:, :] = u
    t = s * tm + lax.broadcasted_iota(I32, (tm, 1), 0)
    for g, w in enumerate(POOL_WINDOWS):
        sl = slice(g * POOL_GROUP_DIM, (g + 1) * POOL_GROUP_DIM)
        lvl_ref[0, lo:, :] = uext_ref[lo:, sl] + uext_ref[lo - 1:HALO + tm - 1, sl]
        cur, k = 0, 2
        while k < w:
            lvl_ref[1 - cur, lo:, :] = lvl_ref[cur, lo:, :] + lvl_ref[cur, lo - k:HALO + tm - k, :]
            cur, k = 1 - cur, 2 * k
        ug = u[:, sl]
        count = jnp.minimum(t + 1, w).astype(F32)
        pooled = lvl_ref[cur, HALO:, :] / count - ug
        mixed = jnp.dot(pooled.astype(BF16), pw_ref[g], preferred_element_type=F32)
        p_ref[0, :, sl] = (mixed * ps_ref[:, sl]).astype(BF16)
    uext_ref[0:HALO, :] = u[tm - HALO:, :]


def _in_proj(x, g_mix, w_in, pool_w, pool_scale):
    B, S, D = x.shape
    tm = PROJ_TILE
    n_out = w_in.shape[1]
    out_sd = jax.ShapeDtypeStruct((B, S, SB_WIDTH), BF16)
    blk = pl.BlockSpec((1, tm, SB_WIDTH), lambda b, s: (b, s, 0))
    return pl.pallas_call(
        _in_proj_kernel,
        grid=(B, S // tm),
        in_specs=[
            pl.BlockSpec((1, tm, D), lambda b, s: (b, s, 0)),
            pl.BlockSpec((D, n_out), lambda b, s: (0, 0)),
            pl.BlockSpec(pool_w.shape, lambda b, s: (0, 0, 0)),
            pl.BlockSpec((1, POOL_WIDTH), lambda b, s: (0, 0)),
        ],
        out_specs=[blk, blk, blk, blk],
        out_shape=[out_sd, out_sd, out_sd, out_sd],
        scratch_shapes=[pltpu.VMEM((HALO + tm, POOL_WIDTH), F32), pltpu.VMEM((tm, D), BF16),
                        pltpu.VMEM((2, HALO + tm, POOL_GROUP_DIM), F32)],
        compiler_params=pltpu.CompilerParams(
            dimension_semantics=("arbitrary", "arbitrary"), vmem_limit_bytes=VMEM_LIMIT),
        name="in_proj_pool",
    )(x, (g_mix[:, None] * w_in).astype(BF16), pool_w.astype(BF16), pool_scale.reshape(1, POOL_WIDTH))


def _attn_kernel(q_ref, k_ref, v_ref, o_ref):
    tq = ATTN_TILE
    tk = tq
    qi = pl.program_id(2)
    lane = lax.broadcasted_iota(I32, (tq, LANES), 1)
    row = lax.broadcasted_iota(I32, (tq, tk), 0)
    col = lax.broadcasted_iota(I32, (tq, tk), 1)
    causal = col < row
    tri = (row > col).astype(BF16)

    def head_rows(r, h):
        rows = q_ref[0, r * tq:(r + 1) * tq, :]
        return jnp.where((lane >= h * SB_HEAD_DIM) & (lane < (h + 1) * SB_HEAD_DIM), rows, jnp.zeros_like(rows))

    def scores(qh, j):
        kb = k_ref[0, pl.ds(pl.multiple_of(j * tk, tk), tk), :]
        return lax.dot_general(qh, kb, _NT, preferred_element_type=F32)

    def values(j):
        return v_ref[0, pl.ds(pl.multiple_of(j * tk, tk), tk), :]

    def log_terms(z, mask):
        neg_abs = lax.bitcast_convert_type(lax.bitcast_convert_type(z, U32) | jnp.uint32(0x80000000), F32)
        nl = jnp.maximum(z, 0.0) + jnp.log2(1.0 + jnp.exp2(neg_abs))
        if mask:
            nl = jnp.where(causal, nl, 0.0)
        return nl.astype(BF16), z - nl, nl[:, 0:1]

    def tile(qh, j, mask):
        nl, lb, nl0 = log_terms(scores(qh, j), mask)
        ntail = jnp.dot(nl, tri, preferred_element_type=F32)
        return lb - ntail, -(ntail[:, 0:1] + nl0), values(j)

    nq = ATTN_SUBTILES
    has_prev = qi > 0
    qs = [head_rows(r, h) for r in range(nq) for h in range(2)]
    chains = []
    for n, qh in enumerate(qs):
        diag = nq * qi + n // 2
        chains += [(qh, diag, True), (qh, jnp.maximum(diag - 1, 0), False)]
    zs = [scores(qh, j) for qh, j, _ in chains]
    terms = [log_terms(z, is_diag) for z, (_, _, is_diag) in zip(zs, chains)]
    nts = [jnp.dot(nl, tri, preferred_element_type=F32) for nl, _, _ in terms]
    lbs = [lb for _, lb, _ in terms]
    tots = [-(nt[:, 0:1] + nl0) for nt, (_, _, nl0) in zip(nts, terms)]
    ws = []
    for n in range(2 * nq):
        d, p = 2 * n, 2 * n + 1
        w_d = jnp.where(causal, jnp.exp2(lbs[d] - nts[d]), 0.0)
        w_p = jnp.exp2(lbs[p] - nts[p] + tots[d])
        if n < 2:
            w_p = jnp.where(has_prev, w_p, 0.0)
        ws += [w_d.astype(BF16), w_p.astype(BF16)]
    pvs = [jnp.dot(w, values(j), preferred_element_type=F32) for w, (_, j, _) in zip(ws, chains)]
    accs = [pvs[2 * n] + pvs[2 * n + 1] for n in range(2 * nq)]
    carries = [tots[2 * n] + tots[2 * n + 1] for n in range(2 * nq)]

    def cond(st):
        j, cs, _ = st
        live = functools.reduce(jnp.maximum, [jnp.max(c) for c in cs])
        return jnp.logical_and(j >= 0, live > ATTN_SKIP_LOG2)

    def body(st):
        j, cs, acs = st
        new_c, new_a = [], []
        for n, (qh, c, a) in enumerate(zip(qs, cs, acs)):
            lag = nq - 1 - n // 2
            if lag:
                c = jnp.where(j >= lag, c, -jnp.inf)
            lw, tot, vb = tile(qh, jnp.maximum(j - lag, 0), False)
            new_a.append(a + jnp.dot(jnp.exp2(lw + c).astype(BF16), vb, preferred_element_type=F32))
            new_c.append(c + tot)
        return j - 1, tuple(new_c), tuple(new_a)

    _, _, accs = lax.while_loop(cond, body, (nq * qi + nq - 3, tuple(carries), tuple(accs)))
    for r in range(nq):
        o_ref[0, r * tq:(r + 1) * tq, :] = jnp.where(
            lane < SB_HEAD_DIM, accs[2 * r], accs[2 * r + 1]).astype(o_ref.dtype)


def _attention(q, k, v):
    B, S, W = q.shape
    tq = ATTN_SUBTILES * ATTN_TILE
    n_pairs = W // LANES
    return pl.pallas_call(
        _attn_kernel,
        grid=(B, n_pairs, S // tq),
        in_specs=[
            pl.BlockSpec((1, tq, LANES), lambda b, p, i: (b, i, p)),
            pl.BlockSpec((1, S, LANES), lambda b, p, i: (b, 0, p)),
            pl.BlockSpec((1, S, LANES), lambda b, p, i: (b, 0, p)),
        ],
        out_specs=pl.BlockSpec((1, tq, LANES), lambda b, p, i: (b, i, p)),
        out_shape=jax.ShapeDtypeStruct((B, S, W), BF16),
        compiler_params=pltpu.CompilerParams(
            dimension_semantics=("arbitrary", "arbitrary", "arbitrary"), vmem_limit_bytes=VMEM_LIMIT),
        name="stickbreak_attn",
    )(q, k, v)


def _out_proj_router_kernel(attn_ref, pool_ref, x_ref, wo_ref, g_ref, wr_ref, br_ref, tri_ref,
                            x1_ref, h2_ref, gate_ref, dest_ref, cnt_ref, carry_ref, *, capacity):
    i = pl.program_id(0)
    tm = x_ref.shape[0]

    @pl.when(i == 0)
    def _():
        carry_ref[...] = jnp.zeros_like(carry_ref)

    mixed = jnp.dot(attn_ref[...], wo_ref[0:SB_WIDTH, :], preferred_element_type=F32)
    mixed = mixed + jnp.dot(pool_ref[...], wo_ref[SB_WIDTH:, :], preferred_element_type=F32)
    x1 = x_ref[...] + mixed
    h2 = _rms(x1, g_ref[...])
    x1_ref[...] = x1
    h2_ref[...] = _pack_rows(h2)

    hh = h2.astype(BF16)
    hl = (h2 - hh.astype(F32)).astype(BF16)
    wr = wr_ref[...]
    wh = wr.astype(BF16)
    wl = (wr - wh.astype(F32)).astype(BF16)
    both = lax.dot_general(jnp.concatenate([wh, wl], axis=0), hh, _NT, preferred_element_type=F32)
    logits = (both[:N_EXPERTS] + both[N_EXPERTS:]
              + lax.dot_general(wh, hl, _NT, preferred_element_type=F32)) + br_ref[...]

    eid = lax.broadcasted_iota(I32, (N_EXPERTS, tm), 0).astype(F32)
    work = logits
    vals, ids = [], []
    for _ in range(TOP_K):
        m = jnp.max(work, axis=0, keepdims=True)
        sel = jnp.min(jnp.where(work == m, eid, float(N_EXPERTS)), axis=0, keepdims=True)
        vals.append(m)
        ids.append(sel)
        work = jnp.where(eid == sel, -jnp.inf, work)
    exps = [jnp.exp(v - vals[0]) for v in vals]
    denom = exps[0] + exps[1] + exps[2] + exps[3]

    onehot = jnp.zeros((N_EXPERTS, tm), F32)
    for sel in ids:
        onehot = onehot + (eid == sel).astype(F32)
    before = jnp.dot(onehot.astype(BF16), tri_ref[...], preferred_element_type=F32) + carry_ref[...]
    for kk in range(TOP_K):
        gate_ref[kk:kk + 1, :] = exps[kk] / denom
        rk = jnp.sum(jnp.where(eid == ids[kk], before, 0.0), axis=0, keepdims=True)
        dest_ref[kk:kk + 1, :] = (ids[kk] * float(capacity) + rk).astype(I32)
    carry_ref[...] = carry_ref[...] + jnp.sum(onehot, axis=1, keepdims=True)
    cnt_ref[...] = jnp.broadcast_to(carry_ref[...], cnt_ref.shape)


def _out_proj_router(attn, pool, x, w_out, g_moe, w_router, b_router, capacity):
    T, D = x.shape
    tm = PROJ_TILE
    r = lax.broadcasted_iota(I32, (tm, tm), 0)
    c = lax.broadcasted_iota(I32, (tm, tm), 1)
    tri = (r < c).astype(BF16)
    row_blk = lambda w: pl.BlockSpec((tm, w), lambda i: (i, 0))
    fixed = lambda shape: pl.BlockSpec(shape, lambda i: tuple(0 for _ in shape))
    sel_blk = pl.BlockSpec((TOP_K, tm), lambda i: (0, i))
    assert N_EXPERTS * capacity < 2 ** 24
    return pl.pallas_call(
        functools.partial(_out_proj_router_kernel, capacity=capacity),
        grid=(T // tm,),
        in_specs=[row_blk(SB_WIDTH), row_blk(POOL_WIDTH), row_blk(D), fixed((D, D)), fixed((1, D)),
                  fixed((N_EXPERTS, D)), fixed((N_EXPERTS, 1)), fixed((tm, tm))],
        out_specs=[row_blk(D), row_blk(D // 2), sel_blk, sel_blk, fixed((N_EXPERTS, LANES))],
        out_shape=[jax.ShapeDtypeStruct((T, D), F32), jax.ShapeDtypeStruct((T, D // 2), U32),
                   jax.ShapeDtypeStruct((TOP_K, T), F32), jax.ShapeDtypeStruct((TOP_K, T), I32),
                   jax.ShapeDtypeStruct((N_EXPERTS, LANES), F32)],
        scratch_shapes=[pltpu.VMEM((N_EXPERTS, 1), F32)],
        compiler_params=pltpu.CompilerParams(
            dimension_semantics=("arbitrary",), vmem_limit_bytes=VMEM_LIMIT),
        name="out_proj_router",
    )(attn, pool, x, w_out.astype(BF16), g_moe.reshape(1, D), w_router.T, b_router.reshape(N_EXPERTS, 1), tri)


SC_ROWS_PER_COPY = 64


def _sc_workers():
    info = plsc.get_sparse_core_info()
    return info.num_cores, info.num_cores * info.num_subcores


def _sc_dispatch(rows, dest, n_out):
    T, W = rows.shape
    K = dest.shape[0]
    sub = SC_ROWS_PER_COPY
    n_cores, n_workers = _sc_workers()
    per_w = T // n_workers
    n_chunks = per_w // sub
    assert per_w * n_workers == T and n_chunks * sub == per_w and n_chunks % 2 == 0
    idx = dest.reshape(K, n_workers, n_chunks, sub).transpose(1, 2, 0, 3).reshape(n_workers, n_chunks * K, sub)
    mesh = plsc.VectorSubcoreMesh(core_axis_name="core", subcore_axis_name="subcore")

    @functools.partial(
        pl.kernel, out_type=jax.ShapeDtypeStruct((n_out, W), rows.dtype), mesh=mesh,
        scratch_types=[pltpu.VMEM((n_chunks * K, sub), I32), pltpu.VMEM((2, sub, W), rows.dtype),
                       pltpu.SemaphoreType.DMA((2,)), pltpu.SemaphoreType.DMA((2,))])
    def scatter_rows(x_hbm, i_hbm, o_hbm, idx_v, buf, rsem, wsem):
        wid = lax.axis_index("subcore") * n_cores + lax.axis_index("core")
        base = wid * per_w
        pltpu.sync_copy(i_hbm.at[wid], idx_v)

        def read(c, slot):
            return pltpu.make_async_copy(x_hbm.at[pl.ds(base + c * sub, sub)], buf.at[slot], rsem.at[slot])

        def write(c, kk, slot):
            return pltpu.make_async_copy(buf.at[slot], o_hbm.at[idx_v.at[c * K + kk]], wsem.at[slot])

        read(0, 0).start()

        @pl.loop(0, n_chunks, step=2)
        def _(c0):
            for b in range(2):
                c = c0 + b
                read(c, b).wait()
                for kk in range(K):
                    write(c, kk, b).start()

                @pl.when(c + 1 < n_chunks)
                def _():
                    @pl.when(c >= 1)
                    def _():
                        for kk in range(K):
                            write(c - 1, kk, 1 - b).wait()
                    read(c + 1, 1 - b).start()

        for kk in range(K):
            write(n_chunks - 2, kk, 0).wait()
            write(n_chunks - 1, kk, 1).wait()

    return scatter_rows(rows, idx)


def _sc_gather(table, indices):
    M = indices.shape[0]
    W = table.shape[1]
    sub = SC_ROWS_PER_COPY
    n_cores, n_workers = _sc_workers()
    per_w = M // n_workers
    n_steps = per_w // sub
    assert per_w * n_workers == M and n_steps * sub == per_w and n_steps % 2 == 0
    mesh = plsc.VectorSubcoreMesh(core_axis_name="core", subcore_axis_name="subcore")

    @functools.partial(
        pl.kernel, out_type=jax.ShapeDtypeStruct((M, W), table.dtype), mesh=mesh,
        scratch_types=[pltpu.VMEM((n_steps, sub), I32), pltpu.VMEM((2, sub, W), table.dtype),
                       pltpu.SemaphoreType.DMA((2,)), pltpu.SemaphoreType.DMA((2,))])
    def gather_rows(x_hbm, i_hbm, o_hbm, idx_v, buf, gsem, wsem):
        wid = lax.axis_index("subcore") * n_cores + lax.axis_index("core")
        base = wid * per_w
        pltpu.sync_copy(i_hbm.at[wid], idx_v)

        def gather(s, slot):
            return pltpu.make_async_copy(x_hbm.at[idx_v.at[s]], buf.at[slot], gsem.at[slot])

        def write(s, slot):
            return pltpu.make_async_copy(buf.at[slot], o_hbm.at[pl.ds(base + s * sub, sub)], wsem.at[slot])

        gather(0, 0).start()

        @pl.loop(0, n_steps, step=2)
        def _(s0):
            for b in range(2):
                s = s0 + b
                gather(s, b).wait()
                write(s, b).start()

                @pl.when(s + 1 < n_steps)
                def _():
                    @pl.when(s >= 1)
                    def _():
                        write(s - 1, 1 - b).wait()
                    gather(s + 1, 1 - b).start()

        write(n_steps - 2, 0).wait()
        write(n_steps - 1, 1).wait()

    return gather_rows(table, indices.reshape(n_workers, n_steps, sub))


def _expert_kernel(vb_ref, ve_ref, vrows_ref, nvis_ref, nxt_ref,
                   x_ref, wgu_hbm, bgu_ref, wd_hbm, bd_ref, y_ref,
                   wgu_stage, wd_stage, wgu_bf, wd_bf, sem):
    v = pl.program_id(0)
    real = v < nvis_ref[0]
    e = ve_ref[v]
    first_of_expert = jnp.logical_or(v == 0, e != ve_ref[jnp.maximum(v - 1, 0)])

    def weight_copies(expert):
        return (pltpu.make_async_copy(wgu_hbm.at[expert], wgu_stage, sem.at[0]),
                pltpu.make_async_copy(wd_hbm.at[expert], wd_stage, sem.at[1]))

    @pl.when(v == 0)
    def _():
        for c in weight_copies(e):
            c.start()

    def load_weights():
        for c in weight_copies(e):
            c.wait()
        wgu_bf[...] = wgu_stage[...].astype(BF16)
        wd_bf[...] = wd_stage[...].astype(BF16)

    def fetch_next():
        nxt = nxt_ref[e]

        @pl.when(nxt >= 0)
        def _():
            for c in weight_copies(nxt):
                c.start()

    def run(n_rows):
        half = wgu_bf.shape[0] // 2
        x_lo, x_hi = _unpack_rows(x_ref[0:n_rows, :])
        gu = (jnp.dot(x_lo.astype(BF16), wgu_bf[:half, :], preferred_element_type=F32)
              + jnp.dot(x_hi.astype(BF16), wgu_bf[half:, :], preferred_element_type=F32)) + bgu_ref[0]
        gate = jnp.minimum(gu[:, :D_EXPERT], SWIGLU_LIMIT)
        up = jnp.clip(gu[:, D_EXPERT:], -SWIGLU_LIMIT, SWIGLU_LIMIT)
        act = gate * jax.nn.sigmoid(SWIGLU_ALPHA * gate) * (up + 1.0)
        y = jnp.dot(act.astype(BF16), wd_bf[...], preferred_element_type=F32) + bd_ref[0]
        y_ref[0:n_rows, :] = _pack_rows(y)

    rows = vrows_ref[v]
    full = EXPERT_ROW_CLASSES[0]
    first = jnp.logical_and(first_of_expert, real)

    @pl.when(jnp.logical_and(first, rows == full))
    def _():
        load_weights()
        run(full)
        fetch_next()

    @pl.when(jnp.logical_and(first, rows != full))
    def _():
        load_weights()
        fetch_next()

    later_full = jnp.logical_and(real, jnp.logical_not(first_of_expert))
    pl.when(jnp.logical_and(later_full, rows == full))(functools.partial(run, full))
    for n_rows in EXPERT_ROW_CLASSES[1:]:
        pl.when(jnp.logical_and(real, rows == n_rows))(functools.partial(run, n_rows))


def _experts(x_sorted, visits, w_gate_up, b_gate_up, w_down, b_down):
    n_rows, W = x_sorted.shape
    D = 2 * W
    tm = EXPERT_TILE
    n_visits = visits[0].shape[0]
    n_prefetch = len(visits)
    by_block = lambda v, vb, *_: (vb[v], 0)
    by_expert = lambda v, vb, ve, *_: (ve[v], 0, 0)
    grid_spec = pltpu.PrefetchScalarGridSpec(
        num_scalar_prefetch=n_prefetch,
        grid=(n_visits,),
        in_specs=[
            pl.BlockSpec((tm, W), by_block),
            pl.BlockSpec(memory_space=pl.ANY),
            pl.BlockSpec((1, 1, 2 * D_EXPERT), by_expert),
            pl.BlockSpec(memory_space=pl.ANY),
            pl.BlockSpec((1, 1, D), by_expert),
        ],
        out_specs=pl.BlockSpec((tm, W), by_block),
        scratch_shapes=[pltpu.VMEM((D, 2 * D_EXPERT), F32), pltpu.VMEM((D_EXPERT, D), F32),
                        pltpu.VMEM((D, 2 * D_EXPERT), BF16), pltpu.VMEM((D_EXPERT, D), BF16),
                        pltpu.SemaphoreType.DMA((2,))],
    )
    return pl.pallas_call(
        _expert_kernel,
        grid_spec=grid_spec,
        out_shape=jax.ShapeDtypeStruct((n_rows, W), U32),
        compiler_params=pltpu.CompilerParams(
            dimension_semantics=("arbitrary",), vmem_limit_bytes=VMEM_LIMIT),
        name="expert_gmm",
    )(*visits, x_sorted, w_gate_up, b_gate_up.reshape(N_EXPERTS, 1, -1), w_down, b_down.reshape(N_EXPERTS, 1, -1))


def _visit_schedule(counts, n_assign, capacity):
    tm = EXPERT_TILE
    assert EXPERT_ROW_CLASSES[0] == tm and list(EXPERT_ROW_CLASSES) == sorted(EXPERT_ROW_CLASSES, reverse=True)
    n_visits = n_assign // tm + N_EXPERTS
    n_full = counts // tm
    tail = counts - n_full * tm
    tail_rows = jnp.zeros_like(tail)
    for rows in EXPERT_ROW_CLASSES:
        tail_rows = jnp.where(tail <= rows, rows, tail_rows)
    per_e = n_full + (tail > 0)
    vend = jnp.cumsum(per_e)
    nvis = vend[-1]
    vc = jnp.minimum(jnp.arange(n_visits, dtype=I32), nvis - 1)
    done = vc[:, None] >= vend[None, :]
    e = jnp.sum(done, axis=1).astype(I32)
    local = vc - jnp.sum(jnp.where(done, per_e[None, :], 0), axis=1)
    mine = e[:, None] == jnp.arange(N_EXPERTS, dtype=I32)[None, :]
    is_tail = local >= jnp.sum(jnp.where(mine, n_full[None, :], 0), axis=1)
    vrows = jnp.where(is_tail, jnp.sum(jnp.where(mine, tail_rows[None, :], 0), axis=1), tm)
    blk = e * (capacity // tm) + local
    ids = jnp.arange(N_EXPERTS, dtype=I32)
    nonempty = counts > 0
    later = jnp.logical_and(nonempty[None, :], ids[None, :] > ids[:, None])
    nxt = jnp.where(jnp.any(later, axis=1), jnp.argmax(later, axis=1), -1).astype(I32)
    return blk.astype(I32), e, vrows.astype(I32), nvis.reshape(1).astype(I32), nxt


def _combine_kernel(y0_ref, y1_ref, y2_ref, y3_ref, x1_ref, gate_ref, g_ref, o_ref):
    half = x1_ref.shape[1] // 2
    gates = gate_ref[...]
    lo = x1_ref[:, :half]
    hi = x1_ref[:, half:]
    for kk, y_ref in enumerate((y0_ref, y1_ref, y2_ref, y3_ref)):
        y_lo, y_hi = _unpack_rows(y_ref[...])
        lo = lo + gates[:, kk:kk + 1] * y_lo
        hi = hi + gates[:, kk:kk + 1] * y_hi
    ms = (jnp.sum(lo * lo, axis=-1, keepdims=True) + jnp.sum(hi * hi, axis=-1, keepdims=True)) / (2 * half)
    scale = lax.rsqrt(ms + RMS_EPS)
    o_ref[:, :half] = lo * scale * g_ref[:, :half]
    o_ref[:, half:] = hi * scale * g_ref[:, half:]


def _combine(y_tok, x1, gates_t, g_final):
    T, D = x1.shape
    tm = COMBINE_TILE
    nt = T // tm
    y_spec = lambda kk: pl.BlockSpec((tm, D // 2), lambda i: (kk * nt + i, 0))
    return pl.pallas_call(
        _combine_kernel,
        grid=(nt,),
        in_specs=[y_spec(0), y_spec(1), y_spec(2), y_spec(3),
                  pl.BlockSpec((tm, D), lambda i: (i, 0)),
                  pl.BlockSpec((tm, TOP_K), lambda i: (i, 0)),
                  pl.BlockSpec((1, D), lambda i: (0, 0))],
        out_specs=pl.BlockSpec((tm, D), lambda i: (i, 0)),
        out_shape=jax.ShapeDtypeStruct((T, D), F32),
        compiler_params=pltpu.CompilerParams(
            dimension_semantics=("arbitrary",), vmem_limit_bytes=VMEM_LIMIT),
        name="combine_norm",
    )(y_tok, y_tok, y_tok, y_tok, x1, gates_t, g_final.reshape(1, D))


def kernel(x, g_mix, w_in, pool_w, pool_scale, w_out, g_moe, w_router, b_router, w_gate_up, b_gate_up,
           w_down, b_down, g_final):
    B, S, D = x.shape
    T = B * S
    assert g_mix.shape[0] == 1, "single-layer problem: the final norm is fused into the combine step"
    q, k, v, pool = _in_proj(x, g_mix[0], w_in[0], pool_w[0], pool_scale[0])
    attn = _attention(q, k, v)
    capacity = T
    x1, h2, gates, dest, cnt = _out_proj_router(
        attn.reshape(T, SB_WIDTH), pool.reshape(T, POOL_WIDTH), x.reshape(T, D),
        w_out[0], g_moe[0], w_router[0], b_router[0], capacity)
    x_sorted = _sc_dispatch(h2, dest, N_EXPERTS * capacity)
    visits = _visit_schedule(cnt[:, 0].astype(I32), T * TOP_K, capacity)
    y = _experts(x_sorted, visits, w_gate_up[0], b_gate_up[0], w_down[0], b_down[0])
    y_tok = _sc_gather(y, dest.reshape(-1))
    return _combine(y_tok, x1, gates.T, g_final).reshape(B, S, D)
```

```python
import functools

import jax
import jax.numpy as jnp
from jax import lax
from jax.experimental import pallas as pl
from jax.experimental.pallas import tpu as pltpu
from jax.experimental.pallas import tpu_sc as plsc

F32 = jnp.float32
BF16 = jnp.bfloat16
I32 = jnp.int32
U32 = jnp.uint32

D_MODEL = 1024
SB_HEADS = 8
SB_HEAD_DIM = 64
SB_WIDTH = SB_HEADS * SB_HEAD_DIM
POOL_WINDOWS = (2, 4, 8, 16)
POOL_WIDTH = 512
POOL_GROUP_DIM = 128
N_EXPERTS = 32
TOP_K = 4
D_EXPERT = 1024
SWIGLU_LIMIT = 7.0
SWIGLU_ALPHA = 1.702
RMS_EPS = 1e-5

LANES = 128
HALO = 32
PROJ_TILE = 1024
ATTN_TILE = 256
ATTN_SUBTILES = 4
EXPERT_TILE = 1024
EXPERT_ROW_CLASSES = (1024, 512, 256)
COMBINE_TILE = 1024
ATTN_SKIP_LOG2 = -160.0
LOG2_E = 1.4426950408889634
V7X_VMEM_BYTES = 64 * 1024 * 1024
VMEM_LIMIT = V7X_VMEM_BYTES * 7 // 8

_NT = (((1,), (1,)), ((), ()))


def _rms(x, g):
    ms = jnp.mean(x * x, axis=-1, keepdims=True)
    return x * lax.rsqrt(ms + RMS_EPS) * g


def _pack_rows(x):
    n = x.shape[1] // 2
    lo = lax.bitcast_convert_type(x[:, :n].astype(BF16).astype(F32), U32)
    hi = lax.bitcast_convert_type(x[:, n:].astype(BF16).astype(F32), U32)
    return (lo >> 16) | (hi & jnp.uint32(0xFFFF0000))


def _unpack_rows(w):
    lo = lax.bitcast_convert_type(w << 16, F32)
    hi = lax.bitcast_convert_type(w & jnp.uint32(0xFFFF0000), F32)
    return lo, hi


def _in_proj_kernel(x_ref, w_ref, pw_ref, ps_ref, q_ref, k_ref, v_ref, p_ref, uext_ref, xb_ref, lvl_ref):
    s = pl.program_id(1)
    tm = x_ref.shape[1]
    x = x_ref[0]
    inv = lax.rsqrt(jnp.mean(x * x, axis=-1, keepdims=True) + RMS_EPS)
    xb_ref[...] = x.astype(BF16)
    proj = jnp.dot(xb_ref[...], w_ref[...], preferred_element_type=F32)
    q_ref[0] = (proj[:, 0:SB_WIDTH] * (inv * (LOG2_E * SB_HEAD_DIM ** -0.5))).astype(BF16)
    k_ref[0] = (proj[:, SB_WIDTH:2 * SB_WIDTH] * inv).astype(BF16)
    v_ref[0] = (proj[:, 2 * SB_WIDTH:3 * SB_WIDTH] * inv).astype(BF16)
    u = proj[:, 3 * SB_WIDTH:] * inv

    lo = HALO // 2

    @pl.when(s == 0)
    def _():
        uext_ref[0:HALO, :] = jnp.zeros((HALO, POOL_WIDTH), F32)
        lvl_ref[:, 0:lo, :] = jnp.zeros((2, lo, POOL_GROUP_DIM), F32)

    uext_ref[HALO:, :] = u
    t = s * tm + lax.broadcasted_iota(I32, (tm, 1), 0)
    for g, w in enumerate(POOL_WINDOWS):
        sl = slice(g * POOL_GROUP_DIM, (g + 1) * POOL_GROUP_DIM)
        lvl_ref[0, lo:, :] = uext_ref[lo:, sl] + uext_ref[lo - 1:HALO + tm - 1, sl]
        cur, k = 0, 2
        while k < w:
            lvl_ref[1 - cur, lo:, :] = lvl_ref[cur, lo:, :] + lvl_ref[cur, lo - k:HALO + tm - k, :]
            cur, k = 1 - cur, 2 * k
        ug = u[:, sl]
        count = jnp.minimum(t + 1, w).astype(F32)
        pooled = lvl_ref[cur, HALO:, :] / count - ug
        mixed = jnp.dot(pooled.astype(BF16), pw_ref[g], preferred_element_type=F32)
        p_ref[0, :, sl] = (mixed * ps_ref[:, sl]).astype(BF16)
    uext_ref[0:HALO, :] = u[tm - HALO:, :]


def _in_proj(x, g_mix, w_in, pool_w, pool_scale):
    B, S, D = x.shape
    tm = PROJ_TILE
    n_out = w_in.shape[1]
    out_sd = jax.ShapeDtypeStruct((B, S, SB_WIDTH), BF16)
    blk = pl.BlockSpec((1, tm, SB_WIDTH), lambda b, s: (b, s, 0))
    return pl.pallas_call(
        _in_proj_kernel,
        grid=(B, S // tm),
        in_specs=[
            pl.BlockSpec((1, tm, D), lambda b, s: (b, s, 0)),
            pl.BlockSpec((D, n_out), lambda b, s: (0, 0)),
            pl.BlockSpec(pool_w.shape, lambda b, s: (0, 0, 0)),
            pl.BlockSpec((1, POOL_WIDTH), lambda b, s: (0, 0)),
        ],
        out_specs=[blk, blk, blk, blk],
        out_shape=[out_sd, out_sd, out_sd, out_sd],
        scratch_shapes=[pltpu.VMEM((HALO + tm, POOL_WIDTH), F32), pltpu.VMEM((tm, D), BF16),
                        pltpu.VMEM((2, HALO + tm, POOL_GROUP_DIM), F32)],
        compiler_params=pltpu.CompilerParams(
            dimension_semantics=("arbitrary", "arbitrary"), vmem_limit_bytes=VMEM_LIMIT),
        name="in_proj_pool",
    )(x, (g_mix[:, None] * w_in).astype(BF16), pool_w.astype(BF16), pool_scale.reshape(1, POOL_WIDTH))


def _attn_kernel(q_ref, k_ref, v_ref, o_ref):
    tq = ATTN_TILE
    tk = tq
    qi = pl.program_id(2)
    lane = lax.broadcasted_iota(I32, (tq, LANES), 1)
    row = lax.broadcasted_iota(I32, (tq, tk), 0)
    col = lax.broadcasted_iota(I32, (tq, tk), 1)
    causal = col < row
    tri = (row > col).astype(BF16)

    def head_rows(r, h):
        rows = q_ref[0, r * tq:(r + 1) * tq, :]
        return jnp.where((lane >= h * SB_HEAD_DIM) & (lane < (h + 1) * SB_HEAD_DIM), rows, jnp.zeros_like(rows))

    def scores(qh, j):
        kb = k_ref[0, pl.ds(pl.multiple_of(j * tk, tk), tk), :]
        return lax.dot_general(qh, kb, _NT, preferred_element_type=F32)

    def values(j):
        return v_ref[0, pl.ds(pl.multiple_of(j * tk, tk), tk), :]

    def log_terms(z, mask):
        neg_abs = lax.bitcast_convert_type(lax.bitcast_convert_type(z, U32) | jnp.uint32(0x80000000), F32)
        nl = jnp.maximum(z, 0.0) + jnp.log2(1.0 + jnp.exp2(neg_abs))
        if mask:
            nl = jnp.where(causal, nl, 0.0)
        return nl.astype(BF16), z - nl, nl[:, 0:1]

    def tile(qh, j, mask):
        nl, lb, nl0 = log_terms(scores(qh, j), mask)
        ntail = jnp.dot(nl, tri, preferred_element_type=F32)
        return lb - ntail, -(ntail[:, 0:1] + nl0), values(j)

    nq = ATTN_SUBTILES
    has_prev = qi > 0
    qs = [head_rows(r, h) for r in range(nq) for h in range(2)]
    chains = []
    for n, qh in enumerate(qs):
        diag = nq * qi + n // 2
        chains += [(qh, diag, True), (qh, jnp.maximum(diag - 1, 0), False)]
    zs = [scores(qh, j) for qh, j, _ in chains]
    terms = [log_terms(z, is_diag) for z, (_, _, is_diag) in zip(zs, chains)]
    nts = [jnp.dot(nl, tri, preferred_element_type=F32) for nl, _, _ in terms]
    lbs = [lb for _, lb, _ in terms]
    tots = [-(nt[:, 0:1] + nl0) for nt, (_, _, nl0) in zip(nts, terms)]
    ws = []
    for n in range(2 * nq):
        d, p = 2 * n, 2 * n + 1
        w_d = jnp.where(causal, jnp.exp2(lbs[d] - nts[d]), 0.0)
        w_p = jnp.exp2(lbs[p] - nts[p] + tots[d])
        if n < 2:
            w_p = jnp.where(has_prev, w_p, 0.0)
        ws += [w_d.astype(BF16), w_p.astype(BF16)]
    pvs = [jnp.dot(w, values(j), preferred_element_type=F32) for w, (_, j, _) in zip(ws, chains)]
    accs = [pvs[2 * n] + pvs[2 * n + 1] for n in range(2 * nq)]
    carries = [tots[2 * n] + tots[2 * n + 1] for n in range(2 * nq)]

    def cond(st):
        j, cs, _ = st
        live = functools.reduce(jnp.maximum, [jnp.max(c) for c in cs])
        return jnp.logical_and(j >= 0, live > ATTN_SKIP_LOG2)

    def body(st):
        j, cs, acs = st
        new_c, new_a = [], []
        for n, (qh, c, a) in enumerate(zip(qs, cs, acs)):
            lag = nq - 1 - n // 2
            if lag:
                c = jnp.where(j >= lag, c, -jnp.inf)
            lw, tot, vb = tile(qh, jnp.maximum(j - lag, 0), False)
            new_a.append(a + jnp.dot(jnp.exp2(lw + c).astype(BF16), vb, preferred_element_type=F32))
            new_c.append(c + tot)
        return j - 1, tuple(new_c), tuple(new_a)

    _, _, accs = lax.while_loop(cond, body, (nq * qi + nq - 3, tuple(carries), tuple(accs)))
    for r in range(nq):
        o_ref[0, r * tq:(r + 1) * tq, :] = jnp.where(
            lane < SB_HEAD_DIM, accs[2 * r], accs[2 * r + 1]).astype(o_ref.dtype)


def _attention(q, k, v):
    B, S, W = q.shape
    tq = ATTN_SUBTILES * ATTN_TILE
    n_pairs = W // LANES
    return pl.pallas_call(
        _attn_kernel,
        grid=(B, n_pairs, S // tq),
        in_specs=[
            pl.BlockSpec((1, tq, LANES), lambda b, p, i: (b, i, p)),
            pl.BlockSpec((1, S, LANES), lambda b, p, i: (b, 0, p)),
            pl.BlockSpec((1, S, LANES), lambda b, p, i: (b, 0, p)),
        ],
        out_specs=pl.BlockSpec((1, tq, LANES), lambda b, p, i: (b, i, p)),
        out_shape=jax.ShapeDtypeStruct((B, S, W), BF16),
        compiler_params=pltpu.CompilerParams(
            dimension_semantics=("arbitrary", "arbitrary", "arbitrary"), vmem_limit_bytes=VMEM_LIMIT),
        name="stickbreak_attn",
    )(q, k, v)


def _out_proj_router_kernel(attn_ref, pool_ref, x_ref, wo_ref, g_ref, wr_ref, br_ref, tri_ref,
                            x1_ref, h2_ref, gate_ref, dest_ref, cnt_ref, carry_ref, *, capacity):
    i = pl.program_id(0)
    tm = x_ref.shape[0]

    @pl.when(i == 0)
    def _():
        carry_ref[...] = jnp.zeros_like(carry_ref)

    mixed = jnp.dot(attn_ref[...], wo_ref[0:SB_WIDTH, :], preferred_element_type=F32)
    mixed = mixed + jnp.dot(pool_ref[...], wo_ref[SB_WIDTH:, :], preferred_element_type=F32)
    x1 = x_ref[...] + mixed
    h2 = _rms(x1, g_ref[...])
    x1_ref[...] = _pack_rows(x1)
    h2_ref[...] = _pack_rows(h2)

    hh = h2.astype(BF16)
    hl = (h2 - hh.astype(F32)).astype(BF16)
    wr = wr_ref[...]
    wh = wr.astype(BF16)
    wl = (wr - wh.astype(F32)).astype(BF16)
    logits = (lax.dot_general(wh, hh, _NT, preferred_element_type=F32)
              + lax.dot_general(wh, hl, _NT, preferred_element_type=F32)
              + lax.dot_general(wl, hh, _NT, preferred_element_type=F32)) + br_ref[...]

    eid = lax.broadcasted_iota(I32, (N_EXPERTS, tm), 0).astype(F32)
    work = logits
    vals, ids = [], []
    for _ in range(TOP_K):
        m = jnp.max(work, axis=0, keepdims=True)
        sel = jnp.min(jnp.where(work == m, eid, float(N_EXPERTS)), axis=0, keepdims=True)
        vals.append(m)
        ids.append(sel)
        work = jnp.where(eid == sel, -jnp.inf, work)
    exps = [jnp.exp(v - vals[0]) for v in vals]
    denom = exps[0] + exps[1] + exps[2] + exps[3]

    onehot = jnp.zeros((N_EXPERTS, tm), F32)
    for sel in ids:
        onehot = onehot + (eid == sel).astype(F32)
    before = jnp.dot(onehot.astype(BF16), tri_ref[...], preferred_element_type=F32) + carry_ref[...]
    for kk in range(TOP_K):
        gate_ref[kk:kk + 1, :] = exps[kk] / denom
        rk = jnp.sum(jnp.where(eid == ids[kk], before, 0.0), axis=0, keepdims=True)
        dest_ref[kk:kk + 1, :] = (ids[kk] * float(capacity) + rk).astype(I32)
    carry_ref[...] = carry_ref[...] + jnp.sum(onehot, axis=1, keepdims=True)
    cnt_ref[...] = jnp.broadcast_to(carry_ref[...], cnt_ref.shape)


def _out_proj_router(attn, pool, x, w_out, g_moe, w_router, b_router, capacity):
    T, D = x.shape
    tm = PROJ_TILE
    r = lax.broadcasted_iota(I32, (tm, tm), 0)
    c = lax.broadcasted_iota(I32, (tm, tm), 1)
    tri = (r < c).astype(BF16)
    row_blk = lambda w: pl.BlockSpec((tm, w), lambda i: (i, 0))
    fixed = lambda shape: pl.BlockSpec(shape, lambda i: tuple(0 for _ in shape))
    sel_blk = pl.BlockSpec((TOP_K, tm), lambda i: (0, i))
    assert N_EXPERTS * capacity < 2 ** 24
    return pl.pallas_call(
        functools.partial(_out_proj_router_kernel, capacity=capacity),
        grid=(T // tm,),
        in_specs=[row_blk(SB_WIDTH), row_blk(POOL_WIDTH), row_blk(D), fixed((D, D)), fixed((1, D)),
                  fixed((N_EXPERTS, D)), fixed((N_EXPERTS, 1)), fixed((tm, tm))],
        out_specs=[row_blk(D // 2), row_blk(D // 2), sel_blk, sel_blk, fixed((N_EXPERTS, LANES))],
        out_shape=[jax.ShapeDtypeStruct((T, D // 2), U32), jax.ShapeDtypeStruct((T, D // 2), U32),
                   jax.ShapeDtypeStruct((TOP_K, T), F32), jax.ShapeDtypeStruct((TOP_K, T), I32),
                   jax.ShapeDtypeStruct((N_EXPERTS, LANES), F32)],
        scratch_shapes=[pltpu.VMEM((N_EXPERTS, 1), F32)],
        compiler_params=pltpu.CompilerParams(
            dimension_semantics=("arbitrary",), vmem_limit_bytes=VMEM_LIMIT),
        name="out_proj_router",
    )(attn, pool, x, w_out.astype(BF16), g_moe.reshape(1, D), w_router.T, b_router.reshape(N_EXPERTS, 1), tri)


SC_ROWS_PER_COPY = 64


def _sc_workers():
    info = plsc.get_sparse_core_info()
    return info.num_cores, info.num_cores * info.num_subcores


def _sc_dispatch(rows, dest, n_out):
    T, W = rows.shape
    K = dest.shape[0]
    sub = SC_ROWS_PER_COPY
    n_cores, n_workers = _sc_workers()
    per_w = T // n_workers
    n_chunks = per_w // sub
    assert per_w * n_workers == T and n_chunks * sub == per_w and n_chunks % 2 == 0
    idx = dest.reshape(K, n_workers, n_chunks, sub).transpose(1, 2, 0, 3).reshape(n_workers, n_chunks * K, sub)
    mesh = plsc.VectorSubcoreMesh(core_axis_name="core", subcore_axis_name="subcore")

    @functools.partial(
        pl.kernel, out_type=jax.ShapeDtypeStruct((n_out, W), rows.dtype), mesh=mesh,
        scratch_types=[pltpu.VMEM((n_chunks * K, sub), I32), pltpu.VMEM((2, sub, W), rows.dtype),
                       pltpu.SemaphoreType.DMA((2,)), pltpu.SemaphoreType.DMA((2,))])
    def scatter_rows(x_hbm, i_hbm, o_hbm, idx_v, buf, rsem, wsem):
        wid = lax.axis_index("subcore") * n_cores + lax.axis_index("core")
        base = wid * per_w
        pltpu.sync_copy(i_hbm.at[wid], idx_v)

        def read(c, slot):
            return pltpu.make_async_copy(x_hbm.at[pl.ds(base + c * sub, sub)], buf.at[slot], rsem.at[slot])

        def write(c, kk, slot):
            return pltpu.make_async_copy(buf.at[slot], o_hbm.at[idx_v.at[c * K + kk]], wsem.at[slot])

        read(0, 0).start()

        @pl.loop(0, n_chunks, step=2)
        def _(c0):
            for b in range(2):
                c = c0 + b
                read(c, b).wait()
                for kk in range(K):
                    write(c, kk, b).start()

                @pl.when(c + 1 < n_chunks)
                def _():
                    @pl.when(c >= 1)
                    def _():
                        for kk in range(K):
                            write(c - 1, kk, 1 - b).wait()
                    read(c + 1, 1 - b).start()

        for kk in range(K):
            write(n_chunks - 2, kk, 0).wait()
            write(n_chunks - 1, kk, 1).wait()

    return scatter_rows(rows, idx)


def _sc_gather(table, indices):
    M = indices.shape[0]
    W = table.shape[1]
    sub = SC_ROWS_PER_COPY
    n_cores, n_workers = _sc_workers()
    per_w = M // n_workers
    n_steps = per_w // sub
    assert per_w * n_workers == M and n_steps * sub == per_w and n_steps % 2 == 0
    mesh = plsc.VectorSubcoreMesh(core_axis_name="core", subcore_axis_name="subcore")

    @functools.partial(
        pl.kernel, out_type=jax.ShapeDtypeStruct((M, W), table.dtype), mesh=mesh,
        scratch_types=[pltpu.VMEM((n_steps, sub), I32), pltpu.VMEM((2, sub, W), table.dtype),
                       pltpu.SemaphoreType.DMA((2,)), pltpu.SemaphoreType.DMA((2,))])
    def gather_rows(x_hbm, i_hbm, o_hbm, idx_v, buf, gsem, wsem):
        wid = lax.axis_index("subcore") * n_cores + lax.axis_index("core")
        base = wid * per_w
        pltpu.sync_copy(i_hbm.at[wid], idx_v)

        def gather(s, slot):
            return pltpu.make_async_copy(x_hbm.at[idx_v.at[s]], buf.at[slot], gsem.at[slot])

        def write(s, slot):
            return pltpu.make_async_copy(buf.at[slot], o_hbm.at[pl.ds(base + s * sub, sub)], wsem.at[slot])

        gather(0, 0).start()

        @pl.loop(0, n_steps, step=2)
        def _(s0):
            for b in range(2):
                s = s0 + b
                gather(s, b).wait()
                write(s, b).start()

                @pl.when(s + 1 < n_steps)
                def _():
                    @pl.when(s >= 1)
                    def _():
                        write(s - 1, 1 - b).wait()
                    gather(s + 1, 1 - b).start()

        write(n_steps - 2, 0).wait()
        write(n_steps - 1, 1).wait()

    return gather_rows(table, indices.reshape(n_workers, n_steps, sub))


def _expert_kernel(vb_ref, ve_ref, vrows_ref, nvis_ref, nxt_ref,
                   x_ref, wgu_hbm, bgu_ref, wd_hbm, bd_ref, y_ref,
                   wgu_stage, wd_stage, wgu_bf, wd_bf, sem):
    v = pl.program_id(0)
    real = v < nvis_ref[0]
    e = ve_ref[v]
    first_of_expert = jnp.logical_or(v == 0, e != ve_ref[jnp.maximum(v - 1, 0)])

    def weight_copies(expert):
        return (pltpu.make_async_copy(wgu_hbm.at[expert], wgu_stage, sem.at[0]),
                pltpu.make_async_copy(wd_hbm.at[expert], wd_stage, sem.at[1]))

    @pl.when(v == 0)
    def _():
        for c in weight_copies(e):
            c.start()

    def load_weights():
        for c in weight_copies(e):
            c.wait()
        wgu_bf[...] = wgu_stage[...].astype(BF16)
        wd_bf[...] = wd_stage[...].astype(BF16)

    def fetch_next():
        nxt = nxt_ref[e]

        @pl.when(nxt >= 0)
        def _():
            for c in weight_copies(nxt):
                c.start()

    def run(n_rows):
        half = wgu_bf.shape[0] // 2
        x_lo, x_hi = _unpack_rows(x_ref[0:n_rows, :])
        gu = (jnp.dot(x_lo.astype(BF16), wgu_bf[:half, :], preferred_element_type=F32)
              + jnp.dot(x_hi.astype(BF16), wgu_bf[half:, :], preferred_element_type=F32)) + bgu_ref[0]
        gate = jnp.minimum(gu[:, :D_EXPERT], SWIGLU_LIMIT)
        up = jnp.clip(gu[:, D_EXPERT:], -SWIGLU_LIMIT, SWIGLU_LIMIT)
        act = gate * jax.nn.sigmoid(SWIGLU_ALPHA * gate) * (up + 1.0)
        y = jnp.dot(act.astype(BF16), wd_bf[...], preferred_element_type=F32) + bd_ref[0]
        y_ref[0:n_rows, :] = _pack_rows(y)

    rows = vrows_ref[v]
    full = EXPERT_ROW_CLASSES[0]
    first = jnp.logical_and(first_of_expert, real)

    @pl.when(jnp.logical_and(first, rows == full))
    def _():
        load_weights()
        run(full)
        fetch_next()

    @pl.when(jnp.logical_and(first, rows != full))
    def _():
        load_weights()
        fetch_next()

    later_full = jnp.logical_and(real, jnp.logical_not(first_of_expert))
    pl.when(jnp.logical_and(later_full, rows == full))(functools.partial(run, full))
    for n_rows in EXPERT_ROW_CLASSES[1:]:
        pl.when(jnp.logical_and(real, rows == n_rows))(functools.partial(run, n_rows))


def _experts(x_sorted, visits, w_gate_up, b_gate_up, w_down, b_down):
    n_rows, W = x_sorted.shape
    D = 2 * W
    tm = EXPERT_TILE
    n_visits = visits[0].shape[0]
    n_prefetch = len(visits)
    by_block = lambda v, vb, *_: (vb[v], 0)
    by_expert = lambda v, vb, ve, *_: (ve[v], 0, 0)
    grid_spec = pltpu.PrefetchScalarGridSpec(
        num_scalar_prefetch=n_prefetch,
        grid=(n_visits,),
        in_specs=[
            pl.BlockSpec((tm, W), by_block),
            pl.BlockSpec(memory_space=pl.ANY),
            pl.BlockSpec((1, 1, 2 * D_EXPERT), by_expert),
            pl.BlockSpec(memory_space=pl.ANY),
            pl.BlockSpec((1, 1, D), by_expert),
        ],
        out_specs=pl.BlockSpec((tm, W), by_block),
        scratch_shapes=[pltpu.VMEM((D, 2 * D_EXPERT), F32), pltpu.VMEM((D_EXPERT, D), F32),
                        pltpu.VMEM((D, 2 * D_EXPERT), BF16), pltpu.VMEM((D_EXPERT, D), BF16),
                        pltpu.SemaphoreType.DMA((2,))],
    )
    return pl.pallas_call(
        _expert_kernel,
        grid_spec=grid_spec,
        out_shape=jax.ShapeDtypeStruct((n_rows, W), U32),
        compiler_params=pltpu.CompilerParams(
            dimension_semantics=("arbitrary",), vmem_limit_bytes=VMEM_LIMIT),
        name="expert_gmm",
    )(*visits, x_sorted, w_gate_up, b_gate_up.reshape(N_EXPERTS, 1, -1), w_down, b_down.reshape(N_EXPERTS, 1, -1))


def _visit_schedule(counts, n_assign, capacity):
    tm = EXPERT_TILE
    assert EXPERT_ROW_CLASSES[0] == tm and list(EXPERT_ROW_CLASSES) == sorted(EXPERT_ROW_CLASSES, reverse=True)
    n_visits = n_assign // tm + N_EXPERTS
    n_full = counts // tm
    tail = counts - n_full * tm
    tail_rows = jnp.zeros_like(tail)
    for rows in EXPERT_ROW_CLASSES:
        tail_rows = jnp.where(tail <= rows, rows, tail_rows)
    per_e = n_full + (tail > 0)
    vend = jnp.cumsum(per_e)
    nvis = vend[-1]
    vc = jnp.minimum(jnp.arange(n_visits, dtype=I32), nvis - 1)
    done = vc[:, None] >= vend[None, :]
    e = jnp.sum(done, axis=1).astype(I32)
    local = vc - jnp.sum(jnp.where(done, per_e[None, :], 0), axis=1)
    mine = e[:, None] == jnp.arange(N_EXPERTS, dtype=I32)[None, :]
    is_tail = local >= jnp.sum(jnp.where(mine, n_full[None, :], 0), axis=1)
    vrows = jnp.where(is_tail, jnp.sum(jnp.where(mine, tail_rows[None, :], 0), axis=1), tm)
    blk = e * (capacity // tm) + local
    ids = jnp.arange(N_EXPERTS, dtype=I32)
    nonempty = counts > 0
    later = jnp.logical_and(nonempty[None, :], ids[None, :] > ids[:, None])
    nxt = jnp.where(jnp.any(later, axis=1), jnp.argmax(later, axis=1), -1).astype(I32)
    return blk.astype(I32), e, vrows.astype(I32), nvis.reshape(1).astype(I32), nxt


def _combine_kernel(y0_ref, y1_ref, y2_ref, y3_ref, x1_ref, gate_ref, g_ref, o_ref):
    half = x1_ref.shape[1]
    gates = gate_ref[...]
    lo, hi = _unpack_rows(x1_ref[...])
    for kk, y_ref in enumerate((y0_ref, y1_ref, y2_ref, y3_ref)):
        y_lo, y_hi = _unpack_rows(y_ref[...])
        lo = lo + gates[:, kk:kk + 1] * y_lo
        hi = hi + gates[:, kk:kk + 1] * y_hi
    ms = (jnp.sum(lo * lo, axis=-1, keepdims=True) + jnp.sum(hi * hi, axis=-1, keepdims=True)) / (2 * half)
    scale = lax.rsqrt(ms + RMS_EPS)
    o_ref[:, :half] = lo * scale * g_ref[:, :half]
    o_ref[:, half:] = hi * scale * g_ref[:, half:]


def _combine(y_tok, x1, gates_t, g_final):
    T, D = x1.shape[0], 2 * x1.shape[1]
    tm = COMBINE_TILE
    nt = T // tm
    y_spec = lambda kk: pl.BlockSpec((tm, D // 2), lambda i: (kk * nt + i, 0))
    return pl.pallas_call(
        _combine_kernel,
        grid=(nt,),
        in_specs=[y_spec(0), y_spec(1), y_spec(2), y_spec(3),
                  pl.BlockSpec((tm, D // 2), lambda i: (i, 0)),
                  pl.BlockSpec((tm, TOP_K), lambda i: (i, 0)),
                  pl.BlockSpec((1, D), lambda i: (0, 0))],
        out_specs=pl.BlockSpec((tm, D), lambda i: (i, 0)),
        out_shape=jax.ShapeDtypeStruct((T, D), F32),
        compiler_params=pltpu.CompilerParams(
            dimension_semantics=("arbitrary",), vmem_limit_bytes=VMEM_LIMIT),
        name="combine_norm",
    )(y_tok, y_tok, y_tok, y_tok, x1, gates_t, g_final.reshape(1, D))


def kernel(x, g_mix, w_in, pool_w, pool_scale, w_out, g_moe, w_router, b_router, w_gate_up, b_gate_up,
           w_down, b_down, g_final):
    B, S, D = x.shape
    T = B * S
    assert g_mix.shape[0] == 1, "single-layer problem: the final norm is fused into the combine step"
    q, k, v, pool = _in_proj(x, g_mix[0], w_in[0], pool_w[0], pool_scale[0])
    attn = _attention(q, k, v)
    capacity = T
    x1, h2, gates, dest, cnt = _out_proj_router(
        attn.reshape(T, SB_WIDTH), pool.reshape(T, POOL_WIDTH), x.reshape(T, D),
        w_out[0], g_moe[0], w_router[0], b_router[0], capacity)
    x_sorted = _sc_dispatch(h2, dest, N_EXPERTS * capacity)
    visits = _visit_schedule(cnt[:, 0].astype(I32), T * TOP_K, capacity)
    y = _experts(x_sorted, visits, w_gate_up[0], b_gate_up[0], w_down[0], b_down[0])
    y_tok = _sc_gather(y, dest.reshape(-1))
    return _combine(y_tok, x1, gates.T, g_final).reshape(B, S, D)
```

```python
import functools

import jax
import jax.numpy as jnp
from jax import lax
from jax.experimental import pallas as pl
from jax.experimental.pallas import tpu as pltpu
from jax.experimental.pallas import tpu_sc as plsc

F32 = jnp.float32
BF16 = jnp.bfloat16
I32 = jnp.int32
U32 = jnp.uint32

D_MODEL = 1024
SB_HEADS = 8
SB_HEAD_DIM = 64
SB_WIDTH = SB_HEADS * SB_HEAD_DIM
POOL_WINDOWS = (2, 4, 8, 16)
POOL_WIDTH = 512
POOL_GROUP_DIM = 128
N_EXPERTS = 32
TOP_K = 4
D_EXPERT = 1024
SWIGLU_LIMIT = 7.0
SWIGLU_ALPHA = 1.702
RMS_EPS = 1e-5

LANES = 128
HALO = 32
PROJ_TILE = 1024
ATTN_TILE = 256
ATTN_SUBTILES = 4
EXPERT_TILE = 1024
EXPERT_ROW_CLASSES = (1024, 512, 256)
COMBINE_TILE = 1024
ATTN_SKIP_LOG2 = -160.0
LOG2_E = 1.4426950408889634
V7X_VMEM_BYTES = 64 * 1024 * 1024
VMEM_LIMIT = V7X_VMEM_BYTES * 7 // 8

_NT = (((1,), (1,)), ((), ()))


def _rms(x, g):
    ms = jnp.mean(x * x, axis=-1, keepdims=True)
    return x * lax.rsqrt(ms + RMS_EPS) * g


def _pack_rows(x):
    n = x.shape[1] // 2
    lo = lax.bitcast_convert_type(x[:, :n].astype(BF16).astype(F32), U32)
    hi = lax.bitcast_convert_type(x[:, n:].astype(BF16).astype(F32), U32)
    return (lo >> 16) | (hi & jnp.uint32(0xFFFF0000))


def _unpack_rows(w):
    lo = lax.bitcast_convert_type(w << 16, F32)
    hi = lax.bitcast_convert_type(w & jnp.uint32(0xFFFF0000), F32)
    return lo, hi


def _in_proj_kernel(x_ref, w_ref, pw_ref, ps_ref, q_ref, k_ref, v_ref, p_ref, uext_ref, xb_ref, lvl_ref):
    s = pl.program_id(1)
    tm = x_ref.shape[1]
    x = x_ref[0]
    inv = lax.rsqrt(jnp.mean(x * x, axis=-1, keepdims=True) + RMS_EPS)
    xb_ref[...] = x.astype(BF16)
    proj = jnp.dot(xb_ref[...], w_ref[...], preferred_element_type=F32)
    q_ref[0] = (proj[:, 0:SB_WIDTH] * (inv * (LOG2_E * SB_HEAD_DIM ** -0.5))).astype(BF16)
    k_ref[0] = (proj[:, SB_WIDTH:2 * SB_WIDTH] * inv).astype(BF16)
    v_ref[0] = (proj[:, 2 * SB_WIDTH:3 * SB_WIDTH] * inv).astype(BF16)
    u = proj[:, 3 * SB_WIDTH:] * inv

    lo = HALO // 2

    @pl.when(s == 0)
    def _():
        uext_ref[0:HALO, :] = jnp.zeros((HALO, POOL_WIDTH), F32)
        lvl_ref[:, 0:lo, :] = jnp.zeros((2, lo, POOL_GROUP_DIM), F32)

    uext_ref[HALO:, :] = u
    t = s * tm + lax.broadcasted_iota(I32, (tm, 1), 0)
    for g, w in enumerate(POOL_WINDOWS):
        sl = slice(g * POOL_GROUP_DIM, (g + 1) * POOL_GROUP_DIM)
        lvl_ref[0, lo:, :] = uext_ref[lo:, sl] + uext_ref[lo - 1:HALO + tm - 1, sl]
        cur, k = 0, 2
        while k < w:
            lvl_ref[1 - cur, lo:, :] = lvl_ref[cur, lo:, :] + lvl_ref[cur, lo - k:HALO + tm - k, :]
            cur, k = 1 - cur, 2 * k
        ug = u[:, sl]
        count = jnp.minimum(t + 1, w).astype(F32)
        pooled = lvl_ref[cur, HALO:, :] / count - ug
        mixed = jnp.dot(pooled.astype(BF16), pw_ref[g], preferred_element_type=F32)
        p_ref[0, :, sl] = (mixed * ps_ref[:, sl]).astype(BF16)
    uext_ref[0:HALO, :] = u[tm - HALO:, :]


def _in_proj(x, g_mix, w_in, pool_w, pool_scale):
    B, S, D = x.shape
    tm = PROJ_TILE
    n_out = w_in.shape[1]
    out_sd = jax.ShapeDtypeStruct((B, S, SB_WIDTH), BF16)
    blk = pl.BlockSpec((1, tm, SB_WIDTH), lambda b, s: (b, s, 0))
    return pl.pallas_call(
        _in_proj_kernel,
        grid=(B, S // tm),
        in_specs=[
            pl.BlockSpec((1, tm, D), lambda b, s: (b, s, 0)),
            pl.BlockSpec((D, n_out), lambda b, s: (0, 0)),
            pl.BlockSpec(pool_w.shape, lambda b, s: (0, 0, 0)),
            pl.BlockSpec((1, POOL_WIDTH), lambda b, s: (0, 0)),
        ],
        out_specs=[blk, blk, blk, blk],
        out_shape=[out_sd, out_sd, out_sd, out_sd],
        scratch_shapes=[pltpu.VMEM((HALO + tm, POOL_WIDTH), F32), pltpu.VMEM((tm, D), BF16),
                        pltpu.VMEM((2, HALO + tm, POOL_GROUP_DIM), F32)],
        compiler_params=pltpu.CompilerParams(
            dimension_semantics=("arbitrary", "arbitrary"), vmem_limit_bytes=VMEM_LIMIT),
        name="in_proj_pool",
    )(x, (g_mix[:, None] * w_in).astype(BF16), pool_w.astype(BF16), pool_scale.reshape(1, POOL_WIDTH))


def _attn_kernel(q_ref, k_ref, v_ref, o_ref):
    tq = ATTN_TILE
    tk = tq
    qi = pl.program_id(2)
    lane = lax.broadcasted_iota(I32, (tq, LANES), 1)
    row = lax.broadcasted_iota(I32, (tq, tk), 0)
    col = lax.broadcasted_iota(I32, (tq, tk), 1)
    causal = col < row
    tri = (row > col).astype(BF16)

    def head_rows(r, h):
        rows = q_ref[0, r * tq:(r + 1) * tq, :]
        return jnp.where((lane >= h * SB_HEAD_DIM) & (lane < (h + 1) * SB_HEAD_DIM), rows, jnp.zeros_like(rows))

    def scores(qh, j):
        kb = k_ref[0, pl.ds(pl.multiple_of(j * tk, tk), tk), :]
        return lax.dot_general(qh, kb, _NT, preferred_element_type=F32)

    def values(j):
        return v_ref[0, pl.ds(pl.multiple_of(j * tk, tk), tk), :]

    def log_terms(z, mask):
        neg_abs = lax.bitcast_convert_type(lax.bitcast_convert_type(z, U32) | jnp.uint32(0x80000000), F32)
        nl = jnp.maximum(z, 0.0) + jnp.log2(1.0 + jnp.exp2(neg_abs))
        if mask:
            nl = jnp.where(causal, nl, 0.0)
        return nl.astype(BF16), z - nl, nl[:, 0:1]

    def tile(qh, j, mask):
        nl, lb, nl0 = log_terms(scores(qh, j), mask)
        ntail = jnp.dot(nl, tri, preferred_element_type=F32)
        return lb - ntail, -(ntail[:, 0:1] + nl0), values(j)

    nq = ATTN_SUBTILES
    has_prev = qi > 0
    qs = [head_rows(r, h) for r in range(nq) for h in range(2)]
    chains = []
    for n, qh in enumerate(qs):
        diag = nq * qi + n // 2
        chains += [(qh, diag, True), (qh, jnp.maximum(diag - 1, 0), False)]
    zs = [scores(qh, j) for qh, j, _ in chains]
    terms = [log_terms(z, is_diag) for z, (_, _, is_diag) in zip(zs, chains)]
    nts = [jnp.dot(nl, tri, preferred_element_type=F32) for nl, _, _ in terms]
    lbs = [lb for _, lb, _ in terms]
    tots = [-(nt[:, 0:1] + nl0) for nt, (_, _, nl0) in zip(nts, terms)]
    ws = []
    for n in range(2 * nq):
        d, p = 2 * n, 2 * n + 1
        w_d = jnp.where(causal, jnp.exp2(lbs[d] - nts[d]), 0.0)
        w_p = jnp.exp2(lbs[p] - nts[p] + tots[d])
        if n < 2:
            w_p = jnp.where(has_prev, w_p, 0.0)
        ws += [w_d.astype(BF16), w_p.astype(BF16)]
    pvs = [jnp.dot(w, values(j), preferred_element_type=F32) for w, (_, j, _) in zip(ws, chains)]
    accs = [pvs[2 * n] + pvs[2 * n + 1] for n in range(2 * nq)]
    carries = [tots[2 * n] + tots[2 * n + 1] for n in range(2 * nq)]

    def cond(st):
        j, cs, _ = st
        live = functools.reduce(jnp.maximum, [jnp.max(c) for c in cs])
        return jnp.logical_and(j >= 0, live > ATTN_SKIP_LOG2)

    def body(st):
        j, cs, acs = st
        new_c, new_a = [], []
        for n, (qh, c, a) in enumerate(zip(qs, cs, acs)):
            lag = nq - 1 - n // 2
            if lag:
                c = jnp.where(j >= lag, c, -jnp.inf)
            lw, tot, vb = tile(qh, jnp.maximum(j - lag, 0), False)
            new_a.append(a + jnp.dot(jnp.exp2(lw + c).astype(BF16), vb, preferred_element_type=F32))
            new_c.append(c + tot)
        return j - 1, tuple(new_c), tuple(new_a)

    _, _, accs = lax.while_loop(cond, body, (nq * qi + nq - 3, tuple(carries), tuple(accs)))
    for r in range(nq):
        o_ref[0, r * tq:(r + 1) * tq, :] = jnp.where(
            lane < SB_HEAD_DIM, accs[2 * r], accs[2 * r + 1]).astype(o_ref.dtype)


def _attention(q, k, v):
    B, S, W = q.shape
    tq = ATTN_SUBTILES * ATTN_TILE
    n_pairs = W // LANES
    return pl.pallas_call(
        _attn_kernel,
        grid=(B, n_pairs, S // tq),
        in_specs=[
            pl.BlockSpec((1, tq, LANES), lambda b, p, i: (b, i, p)),
            pl.BlockSpec((1, S, LANES), lambda b, p, i: (b, 0, p)),
            pl.BlockSpec((1, S, LANES), lambda b, p, i: (b, 0, p)),
        ],
        out_specs=pl.BlockSpec((1, tq, LANES), lambda b, p, i: (b, i, p)),
        out_shape=jax.ShapeDtypeStruct((B, S, W), BF16),
        compiler_params=pltpu.CompilerParams(
            dimension_semantics=("arbitrary", "arbitrary", "arbitrary"), vmem_limit_bytes=VMEM_LIMIT),
        name="stickbreak_attn",
    )(q, k, v)


def _out_proj_router_kernel(attn_ref, pool_ref, x_ref, wo_ref, g_ref, wr_ref, br_ref, tri_ref,
                            x1_ref, h2_ref, gate_ref, dest_ref, cnt_ref, carry_ref, *, capacity):
    i = pl.program_id(0)
    tm = x_ref.shape[0]

    @pl.when(i == 0)
    def _():
        carry_ref[...] = jnp.zeros_like(carry_ref)

    mixed = jnp.dot(attn_ref[...], wo_ref[0:SB_WIDTH, :], preferred_element_type=F32)
    mixed = mixed + jnp.dot(pool_ref[...], wo_ref[SB_WIDTH:, :], preferred_element_type=F32)
    x1 = x_ref[...] + mixed
    h2 = _rms(x1, g_ref[...])
    x1_ref[...] = x1
    h2_ref[...] = _pack_rows(h2)

    hh = h2.astype(BF16)
    hl = (h2 - hh.astype(F32)).astype(BF16)
    wr = wr_ref[...]
    wh = wr.astype(BF16)
    wl = (wr - wh.astype(F32)).astype(BF16)
    both = jnp.dot(hh, jnp.concatenate([wh, wl], axis=1), preferred_element_type=F32)
    by_token = both[:, :LANES] + both[:, LANES:] + jnp.dot(hl, wh, preferred_element_type=F32)
    logits = jnp.transpose(by_token)[:N_EXPERTS, :] + br_ref[...]

    eid = lax.broadcasted_iota(I32, (N_EXPERTS, tm), 0).astype(F32)
    work = logits
    vals, ids = [], []
    for _ in range(TOP_K):
        m = jnp.max(work, axis=0, keepdims=True)
        sel = jnp.min(jnp.where(work == m, eid, float(N_EXPERTS)), axis=0, keepdims=True)
        vals.append(m)
        ids.append(sel)
        work = jnp.where(eid == sel, -jnp.inf, work)
    exps = [jnp.exp(v - vals[0]) for v in vals]
    denom = exps[0] + exps[1] + exps[2] + exps[3]

    onehot = jnp.zeros((N_EXPERTS, tm), F32)
    for sel in ids:
        onehot = onehot + (eid == sel).astype(F32)
    before = jnp.dot(onehot.astype(BF16), tri_ref[...], preferred_element_type=F32) + carry_ref[...]
    for kk in range(TOP_K):
        gate_ref[kk:kk + 1, :] = exps[kk] / denom
        rk = jnp.sum(jnp.where(eid == ids[kk], before, 0.0), axis=0, keepdims=True)
        dest_ref[kk:kk + 1, :] = (ids[kk] * float(capacity) + rk).astype(I32)
    carry_ref[...] = carry_ref[...] + jnp.sum(onehot, axis=1, keepdims=True)
    cnt_ref[...] = jnp.broadcast_to(carry_ref[...], cnt_ref.shape)


def _out_proj_router(attn, pool, x, w_out, g_moe, w_router, b_router, capacity):
    T, D = x.shape
    tm = PROJ_TILE
    r = lax.broadcasted_iota(I32, (tm, tm), 0)
    c = lax.broadcasted_iota(I32, (tm, tm), 1)
    tri = (r < c).astype(BF16)
    row_blk = lambda w: pl.BlockSpec((tm, w), lambda i: (i, 0))
    fixed = lambda shape: pl.BlockSpec(shape, lambda i: tuple(0 for _ in shape))
    sel_blk = pl.BlockSpec((TOP_K, tm), lambda i: (0, i))
    assert N_EXPERTS * capacity < 2 ** 24
    return pl.pallas_call(
        functools.partial(_out_proj_router_kernel, capacity=capacity),
        grid=(T // tm,),
        in_specs=[row_blk(SB_WIDTH), row_blk(POOL_WIDTH), row_blk(D), fixed((D, D)), fixed((1, D)),
                  fixed((D, LANES)), fixed((N_EXPERTS, 1)), fixed((tm, tm))],
        out_specs=[row_blk(D), row_blk(D // 2), sel_blk, sel_blk, fixed((N_EXPERTS, LANES))],
        out_shape=[jax.ShapeDtypeStruct((T, D), F32), jax.ShapeDtypeStruct((T, D // 2), U32),
                   jax.ShapeDtypeStruct((TOP_K, T), F32), jax.ShapeDtypeStruct((TOP_K, T), I32),
                   jax.ShapeDtypeStruct((N_EXPERTS, LANES), F32)],
        scratch_shapes=[pltpu.VMEM((N_EXPERTS, 1), F32)],
        compiler_params=pltpu.CompilerParams(
            dimension_semantics=("arbitrary",), vmem_limit_bytes=VMEM_LIMIT),
        name="out_proj_router",
    )(attn, pool, x, w_out.astype(BF16), g_moe.reshape(1, D), jnp.pad(w_router, ((0, 0), (0, LANES - N_EXPERTS))),
      b_router.reshape(N_EXPERTS, 1), tri)


SC_ROWS_PER_COPY = 64


def _sc_workers():
    info = plsc.get_sparse_core_info()
    return info.num_cores, info.num_cores * info.num_subcores


def _sc_dispatch(rows, dest, n_out):
    T, W = rows.shape
    K = dest.shape[0]
    sub = SC_ROWS_PER_COPY
    n_cores, n_workers = _sc_workers()
    per_w = T // n_workers
    n_chunks = per_w // sub
    assert per_w * n_workers == T and n_chunks * sub == per_w and n_chunks % 2 == 0
    idx = dest.reshape(K, n_workers, n_chunks, sub).transpose(1, 2, 0, 3).reshape(n_workers, n_chunks * K, sub)
    mesh = plsc.VectorSubcoreMesh(core_axis_name="core", subcore_axis_name="subcore")

    @functools.partial(
        pl.kernel, out_type=jax.ShapeDtypeStruct((n_out, W), rows.dtype), mesh=mesh,
        scratch_types=[pltpu.VMEM((n_chunks * K, sub), I32), pltpu.VMEM((2, sub, W), rows.dtype),
                       pltpu.SemaphoreType.DMA((2,)), pltpu.SemaphoreType.DMA((2,))])
    def scatter_rows(x_hbm, i_hbm, o_hbm, idx_v, buf, rsem, wsem):
        wid = lax.axis_index("subcore") * n_cores + lax.axis_index("core")
        base = wid * per_w
        pltpu.sync_copy(i_hbm.at[wid], idx_v)

        def read(c, slot):
            return pltpu.make_async_copy(x_hbm.at[pl.ds(base + c * sub, sub)], buf.at[slot], rsem.at[slot])

        def write(c, kk, slot):
            return pltpu.make_async_copy(buf.at[slot], o_hbm.at[idx_v.at[c * K + kk]], wsem.at[slot])

        read(0, 0).start()

        @pl.loop(0, n_chunks, step=2)
        def _(c0):
            for b in range(2):
                c = c0 + b
                read(c, b).wait()
                for kk in range(K):
                    write(c, kk, b).start()

                @pl.when(c + 1 < n_chunks)
                def _():
                    @pl.when(c >= 1)
                    def _():
                        for kk in range(K):
                            write(c - 1, kk, 1 - b).wait()
                    read(c + 1, 1 - b).start()

        for kk in range(K):
            write(n_chunks - 2, kk, 0).wait()
            write(n_chunks - 1, kk, 1).wait()

    return scatter_rows(rows, idx)


def _sc_gather(table, indices):
    M = indices.shape[0]
    W = table.shape[1]
    sub = SC_ROWS_PER_COPY
    n_cores, n_workers = _sc_workers()
    per_w = M // n_workers
    n_steps = per_w // sub
    assert per_w * n_workers == M and n_steps * sub == per_w and n_steps % 2 == 0
    mesh = plsc.VectorSubcoreMesh(core_axis_name="core", subcore_axis_name="subcore")

    @functools.partial(
        pl.kernel, out_type=jax.ShapeDtypeStruct((M, W), table.dtype), mesh=mesh,
        scratch_types=[pltpu.VMEM((n_steps, sub), I32), pltpu.VMEM((2, sub, W), table.dtype),
                       pltpu.SemaphoreType.DMA((2,)), pltpu.SemaphoreType.DMA((2,))])
    def gather_rows(x_hbm, i_hbm, o_hbm, idx_v, buf, gsem, wsem):
        wid = lax.axis_index("subcore") * n_cores + lax.axis_index("core")
        base = wid * per_w
        pltpu.sync_copy(i_hbm.at[wid], idx_v)

        def gather(s, slot):
            return pltpu.make_async_copy(x_hbm.at[idx_v.at[s]], buf.at[slot], gsem.at[slot])

        def write(s, slot):
            return pltpu.make_async_copy(buf.at[slot], o_hbm.at[pl.ds(base + s * sub, sub)], wsem.at[slot])

        gather(0, 0).start()

        @pl.loop(0, n_steps, step=2)
        def _(s0):
            for b in range(2):
                s = s0 + b
                gather(s, b).wait()
                write(s, b).start()

                @pl.when(s + 1 < n_steps)
                def _():
                    @pl.when(s >= 1)
                    def _():
                        write(s - 1, 1 - b).wait()
                    gather(s + 1, 1 - b).start()

        write(n_steps - 2, 0).wait()
        write(n_steps - 1, 1).wait()

    return gather_rows(table, indices.reshape(n_workers, n_steps, sub))


def _expert_kernel(vb_ref, ve_ref, vrows_ref, nvis_ref, nxt_ref,
                   x_ref, wgu_hbm, bgu_ref, wd_hbm, bd_ref, y_ref,
                   wgu_stage, wd_stage, wgu_bf, wd_bf, sem):
    v = pl.program_id(0)
    real = v < nvis_ref[0]
    e = ve_ref[v]
    first_of_expert = jnp.logical_or(v == 0, e != ve_ref[jnp.maximum(v - 1, 0)])

    def weight_copies(expert):
        return (pltpu.make_async_copy(wgu_hbm.at[expert], wgu_stage, sem.at[0]),
                pltpu.make_async_copy(wd_hbm.at[expert], wd_stage, sem.at[1]))

    @pl.when(v == 0)
    def _():
        for c in weight_copies(e):
            c.start()

    def load_weights():
        for c in weight_copies(e):
            c.wait()
        wgu_bf[...] = wgu_stage[...].astype(BF16)
        wd_bf[...] = wd_stage[...].astype(BF16)

    def fetch_next():
        nxt = nxt_ref[e]

        @pl.when(nxt >= 0)
        def _():
            for c in weight_copies(nxt):
                c.start()

    def run(n_rows):
        half = wgu_bf.shape[0] // 2
        x_lo, x_hi = _unpack_rows(x_ref[0:n_rows, :])
        gu = (jnp.dot(x_lo.astype(BF16), wgu_bf[:half, :], preferred_element_type=F32)
              + jnp.dot(x_hi.astype(BF16), wgu_bf[half:, :], preferred_element_type=F32)) + bgu_ref[0]
        gate = jnp.minimum(gu[:, :D_EXPERT], SWIGLU_LIMIT)
        up = jnp.clip(gu[:, D_EXPERT:], -SWIGLU_LIMIT, SWIGLU_LIMIT)
        act = gate * jax.nn.sigmoid(SWIGLU_ALPHA * gate) * (up + 1.0)
        y = jnp.dot(act.astype(BF16), wd_bf[...], preferred_element_type=F32) + bd_ref[0]
        y_ref[0:n_rows, :] = _pack_rows(y)

    rows = vrows_ref[v]
    full = EXPERT_ROW_CLASSES[0]
    first = jnp.logical_and(first_of_expert, real)

    @pl.when(jnp.logical_and(first, rows == full))
    def _():
        load_weights()
        run(full)
        fetch_next()

    @pl.when(jnp.logical_and(first, rows != full))
    def _():
        load_weights()
        fetch_next()

    later_full = jnp.logical_and(real, jnp.logical_not(first_of_expert))
    pl.when(jnp.logical_and(later_full, rows == full))(functools.partial(run, full))
    for n_rows in EXPERT_ROW_CLASSES[1:]:
        pl.when(jnp.logical_and(real, rows == n_rows))(functools.partial(run, n_rows))


def _experts(x_sorted, visits, w_gate_up, b_gate_up, w_down, b_down):
    n_rows, W = x_sorted.shape
    D = 2 * W
    tm = EXPERT_TILE
    n_visits = visits[0].shape[0]
    n_prefetch = len(visits)
    by_block = lambda v, vb, *_: (vb[v], 0)
    by_expert = lambda v, vb, ve, *_: (ve[v], 0, 0)
    grid_spec = pltpu.PrefetchScalarGridSpec(
        num_scalar_prefetch=n_prefetch,
        grid=(n_visits,),
        in_specs=[
            pl.BlockSpec((tm, W), by_block),
            pl.BlockSpec(memory_space=pl.ANY),
            pl.BlockSpec((1, 1, 2 * D_EXPERT), by_expert),
            pl.BlockSpec(memory_space=pl.ANY),
            pl.BlockSpec((1, 1, D), by_expert),
        ],
        out_specs=pl.BlockSpec((tm, W), by_block),
        scratch_shapes=[pltpu.VMEM((D, 2 * D_EXPERT), F32), pltpu.VMEM((D_EXPERT, D), F32),
                        pltpu.VMEM((D, 2 * D_EXPERT), BF16), pltpu.VMEM((D_EXPERT, D), BF16),
                        pltpu.SemaphoreType.DMA((2,))],
    )
    return pl.pallas_call(
        _expert_kernel,
        grid_spec=grid_spec,
        out_shape=jax.ShapeDtypeStruct((n_rows, W), U32),
        compiler_params=pltpu.CompilerParams(
            dimension_semantics=("arbitrary",), vmem_limit_bytes=VMEM_LIMIT),
        name="expert_gmm",
    )(*visits, x_sorted, w_gate_up, b_gate_up.reshape(N_EXPERTS, 1, -1), w_down, b_down.reshape(N_EXPERTS, 1, -1))


def _visit_schedule(counts, n_assign, capacity):
    tm = EXPERT_TILE
    assert EXPERT_ROW_CLASSES[0] == tm and list(EXPERT_ROW_CLASSES) == sorted(EXPERT_ROW_CLASSES, reverse=True)
    n_visits = n_assign // tm + N_EXPERTS
    n_full = counts // tm
    tail = counts - n_full * tm
    tail_rows = jnp.zeros_like(tail)
    for rows in EXPERT_ROW_CLASSES:
        tail_rows = jnp.where(tail <= rows, rows, tail_rows)
    per_e = n_full + (tail > 0)
    vend = jnp.cumsum(per_e)
    nvis = vend[-1]
    vc = jnp.minimum(jnp.arange(n_visits, dtype=I32), nvis - 1)
    done = vc[:, None] >= vend[None, :]
    e = jnp.sum(done, axis=1).astype(I32)
    local = vc - jnp.sum(jnp.where(done, per_e[None, :], 0), axis=1)
    mine = e[:, None] == jnp.arange(N_EXPERTS, dtype=I32)[None, :]
    is_tail = local >= jnp.sum(jnp.where(mine, n_full[None, :], 0), axis=1)
    vrows = jnp.where(is_tail, jnp.sum(jnp.where(mine, tail_rows[None, :], 0), axis=1), tm)
    blk = e * (capacity // tm) + local
    ids = jnp.arange(N_EXPERTS, dtype=I32)
    nonempty = counts > 0
    later = jnp.logical_and(nonempty[None, :], ids[None, :] > ids[:, None])
    nxt = jnp.where(jnp.any(later, axis=1), jnp.argmax(later, axis=1), -1).astype(I32)
    return blk.astype(I32), e, vrows.astype(I32), nvis.reshape(1).astype(I32), nxt


def _combine_kernel(y0_ref, y1_ref, y2_ref, y3_ref, x1_ref, gate_ref, g_ref, o_ref):
    half = x1_ref.shape[1] // 2
    gates = gate_ref[...]
    lo = x1_ref[:, :half]
    hi = x1_ref[:, half:]
    for kk, y_ref in enumerate((y0_ref, y1_ref, y2_ref, y3_ref)):
        y_lo, y_hi = _unpack_rows(y_ref[...])
        lo = lo + gates[:, kk:kk + 1] * y_lo
        hi = hi + gates[:, kk:kk + 1] * y_hi
    ms = (jnp.sum(lo * lo, axis=-1, keepdims=True) + jnp.sum(hi * hi, axis=-1, keepdims=True)) / (2 * half)
    scale = lax.rsqrt(ms + RMS_EPS)
    o_ref[:, :half] = lo * scale * g_ref[:, :half]
    o_ref[:, half:] = hi * scale * g_ref[:, half:]


def _combine(y_tok, x1, gates_t, g_final):
    T, D = x1.shape
    tm = COMBINE_TILE
    nt = T // tm
    y_spec = lambda kk: pl.BlockSpec((tm, D // 2), lambda i: (kk * nt + i, 0))
    return pl.pallas_call(
        _combine_kernel,
        grid=(nt,),
        in_specs=[y_spec(0), y_spec(1), y_spec(2), y_spec(3),
                  pl.BlockSpec((tm, D), lambda i: (i, 0)),
                  pl.BlockSpec((tm, TOP_K), lambda i: (i, 0)),
                  pl.BlockSpec((1, D), lambda i: (0, 0))],
        out_specs=pl.BlockSpec((tm, D), lambda i: (i, 0)),
        out_shape=jax.ShapeDtypeStruct((T, D), F32),
        compiler_params=pltpu.CompilerParams(
            dimension_semantics=("arbitrary",), vmem_limit_bytes=VMEM_LIMIT),
        name="combine_norm",
    )(y_tok, y_tok, y_tok, y_tok, x1, gates_t, g_final.reshape(1, D))


def kernel(x, g_mix, w_in, pool_w, pool_scale, w_out, g_moe, w_router, b_router, w_gate_up, b_gate_up,
           w_down, b_down, g_final):
    B, S, D = x.shape
    T = B * S
    assert g_mix.shape[0] == 1, "single-layer problem: the final norm is fused into the combine step"
    q, k, v, pool = _in_proj(x, g_mix[0], w_in[0], pool_w[0], pool_scale[0])
    attn = _attention(q, k, v)
    capacity = T
    x1, h2, gates, dest, cnt = _out_proj_router(
        attn.reshape(T, SB_WIDTH), pool.reshape(T, POOL_WIDTH), x.reshape(T, D),
        w_out[0], g_moe[0], w_router[0], b_router[0], capacity)
    x_sorted = _sc_dispatch(h2, dest, N_EXPERTS * capacity)
    visits = _visit_schedule(cnt[:, 0].astype(I32), T * TOP_K, capacity)
    y = _experts(x_sorted, visits, w_gate_up[0], b_gate_up[0], w_down[0], b_down[0])
    y_tok = _sc_gather(y, dest.reshape(-1))
    return _combine(y_tok, x1, gates.T, g_final).reshape(B, S, D)
```

```python
import functools

import jax
import jax.numpy as jnp
from jax import lax
from jax.experimental import pallas as pl
from jax.experimental.pallas import tpu as pltpu
from jax.experimental.pallas import tpu_sc as plsc

F32 = jnp.float32
BF16 = jnp.bfloat16
I32 = jnp.int32
U32 = jnp.uint32

D_MODEL = 1024
SB_HEADS = 8
SB_HEAD_DIM = 64
SB_WIDTH = SB_HEADS * SB_HEAD_DIM
POOL_WINDOWS = (2, 4, 8, 16)
POOL_WIDTH = 512
POOL_GROUP_DIM = 128
N_EXPERTS = 32
TOP_K = 4
D_EXPERT = 1024
SWIGLU_LIMIT = 7.0
SWIGLU_ALPHA = 1.702
RMS_EPS = 1e-5

LANES = 128
HALO = 32
PROJ_TILE = 1024
ATTN_TILE = 256
ATTN_SUBTILES = 4
EXPERT_TILE = 1024
EXPERT_ROW_CLASSES = (1024, 512, 256)
COMBINE_TILE = 1024
ATTN_SKIP_LOG2 = -160.0
LOG2_E = 1.4426950408889634
V7X_VMEM_BYTES = 64 * 1024 * 1024
VMEM_LIMIT = V7X_VMEM_BYTES * 7 // 8

_NT = (((1,), (1,)), ((), ()))


def _rms(x, g):
    ms = jnp.mean(x * x, axis=-1, keepdims=True)
    return x * lax.rsqrt(ms + RMS_EPS) * g


def _pack_rows(x):
    n = x.shape[1] // 2
    lo = lax.bitcast_convert_type(x[:, :n].astype(BF16).astype(F32), U32)
    hi = lax.bitcast_convert_type(x[:, n:].astype(BF16).astype(F32), U32)
    return (lo >> 16) | (hi & jnp.uint32(0xFFFF0000))


def _unpack_rows(w):
    lo = lax.bitcast_convert_type(w << 16, F32)
    hi = lax.bitcast_convert_type(w & jnp.uint32(0xFFFF0000), F32)
    return lo, hi


def _in_proj_kernel(x_ref, w_ref, pw_ref, ps_ref, q_ref, k_ref, v_ref, p_ref, uext_ref, xb_ref, lvl_ref):
    s = pl.program_id(1)
    tm = x_ref.shape[1]
    x = x_ref[0]
    inv = lax.rsqrt(jnp.mean(x * x, axis=-1, keepdims=True) + RMS_EPS)
    xb_ref[...] = x.astype(BF16)
    proj = jnp.dot(xb_ref[...], w_ref[...], preferred_element_type=F32)
    q_ref[0] = (proj[:, 0:SB_WIDTH] * (inv * (LOG2_E * SB_HEAD_DIM ** -0.5))).astype(BF16)
    k_ref[0] = (proj[:, SB_WIDTH:2 * SB_WIDTH] * inv).astype(BF16)
    v_ref[0] = (proj[:, 2 * SB_WIDTH:3 * SB_WIDTH] * inv).astype(BF16)
    u = proj[:, 3 * SB_WIDTH:] * inv

    lo = HALO // 2

    @pl.when(s == 0)
    def _():
        uext_ref[0:HALO, :] = jnp.zeros((HALO, POOL_WIDTH), F32)
        lvl_ref[:, 0:lo, :] = jnp.zeros((2, lo, POOL_GROUP_DIM), F32)

    uext_ref[HALO:, :] = u
    t = s * tm + lax.broadcasted_iota(I32, (tm, 1), 0)
    for g, w in enumerate(POOL_WINDOWS):
        sl = slice(g * POOL_GROUP_DIM, (g + 1) * POOL_GROUP_DIM)
        lvl_ref[0, lo:, :] = uext_ref[lo:, sl] + uext_ref[lo - 1:HALO + tm - 1, sl]
        cur, k = 0, 2
        while k < w:
            lvl_ref[1 - cur, lo:, :] = lvl_ref[cur, lo:, :] + lvl_ref[cur, lo - k:HALO + tm - k, :]
            cur, k = 1 - cur, 2 * k
        ug = u[:, sl]
        count = jnp.minimum(t + 1, w).astype(F32)
        pooled = lvl_ref[cur, HALO:, :] / count - ug
        mixed = jnp.dot(pooled.astype(BF16), pw_ref[g], preferred_element_type=F32)
        p_ref[0, :, sl] = (mixed * ps_ref[:, sl]).astype(BF16)
    uext_ref[0:HALO, :] = u[tm - HALO:, :]


def _in_proj(x, g_mix, w_in, pool_w, pool_scale):
    B, S, D = x.shape
    tm = PROJ_TILE
    n_out = w_in.shape[1]
    out_sd = jax.ShapeDtypeStruct((B, S, SB_WIDTH), BF16)
    blk = pl.BlockSpec((1, tm, SB_WIDTH), lambda b, s: (b, s, 0))
    return pl.pallas_call(
        _in_proj_kernel,
        grid=(B, S // tm),
        in_specs=[
            pl.BlockSpec((1, tm, D), lambda b, s: (b, s, 0)),
            pl.BlockSpec((D, n_out), lambda b, s: (0, 0)),
            pl.BlockSpec(pool_w.shape, lambda b, s: (0, 0, 0)),
            pl.BlockSpec((1, POOL_WIDTH), lambda b, s: (0, 0)),
        ],
        out_specs=[blk, blk, blk, blk],
        out_shape=[out_sd, out_sd, out_sd, out_sd],
        scratch_shapes=[pltpu.VMEM((HALO + tm, POOL_WIDTH), F32), pltpu.VMEM((tm, D), BF16),
                        pltpu.VMEM((2, HALO + tm, POOL_GROUP_DIM), F32)],
        compiler_params=pltpu.CompilerParams(
            dimension_semantics=("arbitrary", "arbitrary"), vmem_limit_bytes=VMEM_LIMIT),
        name="in_proj_pool",
    )(x, (g_mix[:, None] * w_in).astype(BF16), pool_w.astype(BF16), pool_scale.reshape(1, POOL_WIDTH))


def _attn_kernel(q_ref, k_ref, v_ref, o_ref):
    tq = ATTN_TILE
    tk = tq
    qi = pl.program_id(2)
    lane = lax.broadcasted_iota(I32, (tq, LANES), 1)
    row = lax.broadcasted_iota(I32, (tq, tk), 0)
    col = lax.broadcasted_iota(I32, (tq, tk), 1)
    causal = col < row
    tri = (row > col).astype(BF16)

    def head_rows(r, h):
        rows = q_ref[0, r * tq:(r + 1) * tq, :]
        return jnp.where((lane >= h * SB_HEAD_DIM) & (lane < (h + 1) * SB_HEAD_DIM), rows, jnp.zeros_like(rows))

    def scores(qh, j):
        kb = k_ref[0, pl.ds(pl.multiple_of(j * tk, tk), tk), :]
        return lax.dot_general(qh, kb, _NT, preferred_element_type=F32)

    def values(j):
        return v_ref[0, pl.ds(pl.multiple_of(j * tk, tk), tk), :]

    def log_terms(z, mask):
        neg_abs = lax.bitcast_convert_type(lax.bitcast_convert_type(z, U32) | jnp.uint32(0x80000000), F32)
        nl = jnp.maximum(z, 0.0) + jnp.log2(1.0 + jnp.exp2(neg_abs))
        if mask:
            nl = jnp.where(causal, nl, 0.0)
        return nl.astype(BF16), z - nl, nl[:, 0:1]

    def tile(qh, j, mask):
        nl, lb, nl0 = log_terms(scores(qh, j), mask)
        ntail = jnp.dot(nl, tri, preferred_element_type=F32)
        return lb - ntail, -(ntail[:, 0:1] + nl0), values(j)

    nq = ATTN_SUBTILES
    has_prev = qi > 0
    qs = [head_rows(r, h) for r in range(nq) for h in range(2)]
    chains = []
    for n, qh in enumerate(qs):
        diag = nq * qi + n // 2
        chains += [(qh, diag, True), (qh, jnp.maximum(diag - 1, 0), False)]
    zs = [scores(qh, j) for qh, j, _ in chains]
    terms = [log_terms(z, is_diag) for z, (_, _, is_diag) in zip(zs, chains)]
    nts = [jnp.dot(nl, tri, preferred_element_type=F32) for nl, _, _ in terms]
    lbs = [lb for _, lb, _ in terms]
    tots = [-(nt[:, 0:1] + nl0) for nt, (_, _, nl0) in zip(nts, terms)]
    ws = []
    for n in range(2 * nq):
        d, p = 2 * n, 2 * n + 1
        w_d = jnp.where(causal, jnp.exp2(lbs[d] - nts[d]), 0.0)
        w_p = jnp.exp2(lbs[p] - nts[p] + tots[d])
        if n < 2:
            w_p = jnp.where(has_prev, w_p, 0.0)
        ws += [w_d.astype(BF16), w_p.astype(BF16)]
    pvs = [jnp.dot(w, values(j), preferred_element_type=F32) for w, (_, j, _) in zip(ws, chains)]
    accs = [pvs[2 * n] + pvs[2 * n + 1] for n in range(2 * nq)]
    carries = [tots[2 * n] + tots[2 * n + 1] for n in range(2 * nq)]

    def cond(st):
        j, cs, _ = st
        live = functools.reduce(jnp.maximum, [jnp.max(c) for c in cs])
        return jnp.logical_and(j >= 0, live > ATTN_SKIP_LOG2)

    def body(st):
        j, cs, acs = st
        new_c, new_a = [], []
        for n, (qh, c, a) in enumerate(zip(qs, cs, acs)):
            lag = nq - 1 - n // 2
            if lag:
                c = jnp.where(j >= lag, c, -jnp.inf)
            lw, tot, vb = tile(qh, jnp.maximum(j - lag, 0), False)
            new_a.append(a + jnp.dot(jnp.exp2(lw + c).astype(BF16), vb, preferred_element_type=F32))
            new_c.append(c + tot)
        return j - 1, tuple(new_c), tuple(new_a)

    _, _, accs = lax.while_loop(cond, body, (nq * qi + nq - 3, tuple(carries), tuple(accs)))
    for r in range(nq):
        o_ref[0, r * tq:(r + 1) * tq, :] = jnp.where(
            lane < SB_HEAD_DIM, accs[2 * r], accs[2 * r + 1]).astype(o_ref.dtype)


def _attention(q, k, v):
    B, S, W = q.shape
    tq = ATTN_SUBTILES * ATTN_TILE
    n_pairs = W // LANES
    return pl.pallas_call(
        _attn_kernel,
        grid=(B, n_pairs, S // tq),
        in_specs=[
            pl.BlockSpec((1, tq, LANES), lambda b, p, i: (b, i, p)),
            pl.BlockSpec((1, S, LANES), lambda b, p, i: (b, 0, p)),
            pl.BlockSpec((1, S, LANES), lambda b, p, i: (b, 0, p)),
        ],
        out_specs=pl.BlockSpec((1, tq, LANES), lambda b, p, i: (b, i, p)),
        out_shape=jax.ShapeDtypeStruct((B, S, W), BF16),
        compiler_params=pltpu.CompilerParams(
            dimension_semantics=("arbitrary", "arbitrary", "arbitrary"), vmem_limit_bytes=VMEM_LIMIT),
        name="stickbreak_attn",
    )(q, k, v)


def _out_proj_router_kernel(attn_ref, pool_ref, x_ref, wo_ref, g_ref, wr_ref, br_ref, tri_ref,
                            x1_ref, h2_ref, gate_ref, dest_ref, cnt_ref, carry_ref, *, capacity):
    i = pl.program_id(0)
    tm = x_ref.shape[0]

    @pl.when(i == 0)
    def _():
        carry_ref[...] = jnp.zeros_like(carry_ref)

    mixed = jnp.dot(attn_ref[...], wo_ref[0:SB_WIDTH, :], preferred_element_type=F32)
    mixed = mixed + jnp.dot(pool_ref[...], wo_ref[SB_WIDTH:, :], preferred_element_type=F32)
    x1 = x_ref[...] + mixed
    h2 = _rms(x1, g_ref[...])
    x1_ref[...] = x1
    h2_ref[...] = _pack_rows(h2)

    hh = h2.astype(BF16)
    hl = (h2 - hh.astype(F32)).astype(BF16)
    wr = wr_ref[...]
    wh = wr.astype(BF16)
    wl = (wr - wh.astype(F32)).astype(BF16)
    logits = (lax.dot_general(wh, hh, _NT, preferred_element_type=F32)
              + lax.dot_general(wh, hl, _NT, preferred_element_type=F32)
              + lax.dot_general(wl, hh, _NT, preferred_element_type=F32)) + br_ref[...]

    eid = lax.broadcasted_iota(I32, (N_EXPERTS, tm), 0).astype(F32)
    work = logits
    vals, ids = [], []
    for _ in range(TOP_K):
        m = jnp.max(work, axis=0, keepdims=True)
        sel = jnp.min(jnp.where(work == m, eid, float(N_EXPERTS)), axis=0, keepdims=True)
        vals.append(m)
        ids.append(sel)
        work = jnp.where(eid == sel, -jnp.inf, work)
    exps = [jnp.exp(v - vals[0]) for v in vals]
    denom = exps[0] + exps[1] + exps[2] + exps[3]

    onehot = jnp.zeros((N_EXPERTS, tm), F32)
    for sel in ids:
        onehot = onehot + (eid == sel).astype(F32)
    before = jnp.dot(onehot.astype(BF16), tri_ref[...], preferred_element_type=F32) + carry_ref[...]
    for kk in range(TOP_K):
        gate_ref[kk:kk + 1, :] = exps[kk] / denom
        rk = jnp.sum(jnp.where(eid == ids[kk], before, 0.0), axis=0, keepdims=True)
        dest_ref[kk:kk + 1, :] = (ids[kk] * float(capacity) + rk).astype(I32)
    carry_ref[...] = carry_ref[...] + jnp.sum(onehot, axis=1, keepdims=True)
    cnt_ref[...] = jnp.broadcast_to(carry_ref[...], cnt_ref.shape)


def _out_proj_router(attn, pool, x, w_out, g_moe, w_router, b_router, capacity):
    T, D = x.shape
    tm = PROJ_TILE
    r = lax.broadcasted_iota(I32, (tm, tm), 0)
    c = lax.broadcasted_iota(I32, (tm, tm), 1)
    tri = (r < c).astype(BF16)
    row_blk = lambda w: pl.BlockSpec((tm, w), lambda i: (i, 0))
    fixed = lambda shape: pl.BlockSpec(shape, lambda i: tuple(0 for _ in shape))
    sel_blk = pl.BlockSpec((TOP_K, tm), lambda i: (0, i))
    assert N_EXPERTS * capacity < 2 ** 24
    return pl.pallas_call(
        functools.partial(_out_proj_router_kernel, capacity=capacity),
        grid=(T // tm,),
        in_specs=[row_blk(SB_WIDTH), row_blk(POOL_WIDTH), row_blk(D), fixed((D, D)), fixed((1, D)),
                  fixed((N_EXPERTS, D)), fixed((N_EXPERTS, 1)), fixed((tm, tm))],
        out_specs=[row_blk(D), row_blk(D // 2), sel_blk, sel_blk, fixed((N_EXPERTS, LANES))],
        out_shape=[jax.ShapeDtypeStruct((T, D), F32), jax.ShapeDtypeStruct((T, D // 2), U32),
                   jax.ShapeDtypeStruct((TOP_K, T), F32), jax.ShapeDtypeStruct((TOP_K, T), I32),
                   jax.ShapeDtypeStruct((N_EXPERTS, LANES), F32)],
        scratch_shapes=[pltpu.VMEM((N_EXPERTS, 1), F32)],
        compiler_params=pltpu.CompilerParams(
            dimension_semantics=("arbitrary",), vmem_limit_bytes=VMEM_LIMIT),
        name="out_proj_router",
    )(attn, pool, x, w_out.astype(BF16), g_moe.reshape(1, D), w_router.T, b_router.reshape(N_EXPERTS, 1), tri)


SC_ROWS_PER_COPY = 64


def _sc_workers():
    info = plsc.get_sparse_core_info()
    return info.num_cores, info.num_cores * info.num_subcores


def _sc_dispatch(rows, dest, n_out):
    T, W = rows.shape
    K = dest.shape[0]
    sub = SC_ROWS_PER_COPY
    n_cores, n_workers = _sc_workers()
    per_w = T // n_workers
    n_chunks = per_w // sub
    assert per_w * n_workers == T and n_chunks * sub == per_w and n_chunks % 2 == 0
    idx = dest.reshape(K, n_workers, n_chunks, sub).transpose(1, 2, 0, 3).reshape(n_workers, n_chunks * K, sub)
    mesh = plsc.VectorSubcoreMesh(core_axis_name="core", subcore_axis_name="subcore")

    @functools.partial(
        pl.kernel, out_type=jax.ShapeDtypeStruct((n_out, W), rows.dtype), mesh=mesh,
        scratch_types=[pltpu.VMEM((n_chunks * K, sub), I32), pltpu.VMEM((2, sub, W), rows.dtype),
                       pltpu.SemaphoreType.DMA((2,)), pltpu.SemaphoreType.DMA((2,))])
    def scatter_rows(x_hbm, i_hbm, o_hbm, idx_v, buf, rsem, wsem):
        wid = lax.axis_index("subcore") * n_cores + lax.axis_index("core")
        base = wid * per_w
        pltpu.sync_copy(i_hbm.at[wid], idx_v)

        def read(c, slot):
            return pltpu.make_async_copy(x_hbm.at[pl.ds(base + c * sub, sub)], buf.at[slot], rsem.at[slot])

        def write(c, kk, slot):
            return pltpu.make_async_copy(buf.at[slot], o_hbm.at[idx_v.at[c * K + kk]], wsem.at[slot])

        read(0, 0).start()

        @pl.loop(0, n_chunks, step=2)
        def _(c0):
            for b in range(2):
                c = c0 + b
                read(c, b).wait()
                for kk in range(K):
                    write(c, kk, b).start()

                @pl.when(c + 1 < n_chunks)
                def _():
                    @pl.when(c >= 1)
                    def _():
                        for kk in range(K):
                            write(c - 1, kk, 1 - b).wait()
                    read(c + 1, 1 - b).start()

        for kk in range(K):
            write(n_chunks - 2, kk, 0).wait()
            write(n_chunks - 1, kk, 1).wait()

    return scatter_rows(rows, idx)


def _sc_gather(table, indices):
    M = indices.shape[0]
    W = table.shape[1]
    sub = SC_ROWS_PER_COPY
    n_cores, n_workers = _sc_workers()
    per_w = M // n_workers
    n_steps = per_w // sub
    assert per_w * n_workers == M and n_steps * sub == per_w and n_steps % 2 == 0
    mesh = plsc.VectorSubcoreMesh(core_axis_name="core", subcore_axis_name="subcore")

    @functools.partial(
        pl.kernel, out_type=jax.ShapeDtypeStruct((M, W), table.dtype), mesh=mesh,
        scratch_types=[pltpu.VMEM((n_steps, sub), I32), pltpu.VMEM((2, sub, W), table.dtype),
                       pltpu.SemaphoreType.DMA((2,)), pltpu.SemaphoreType.DMA((2,))])
    def gather_rows(x_hbm, i_hbm, o_hbm, idx_v, buf, gsem, wsem):
        wid = lax.axis_index("subcore") * n_cores + lax.axis_index("core")
        base = wid * per_w
        pltpu.sync_copy(i_hbm.at[wid], idx_v)

        def gather(s, slot):
            return pltpu.make_async_copy(x_hbm.at[idx_v.at[s]], buf.at[slot], gsem.at[slot])

        def write(s, slot):
            return pltpu.make_async_copy(buf.at[slot], o_hbm.at[pl.ds(base + s * sub, sub)], wsem.at[slot])

        gather(0, 0).start()

        @pl.loop(0, n_steps, step=2)
        def _(s0):
            for b in range(2):
                s = s0 + b
                gather(s, b).wait()
                write(s, b).start()

                @pl.when(s + 1 < n_steps)
                def _():
                    @pl.when(s >= 1)
                    def _():
                        write(s - 1, 1 - b).wait()
                    gather(s + 1, 1 - b).start()

        write(n_steps - 2, 0).wait()
        write(n_steps - 1, 1).wait()

    return gather_rows(table, indices.reshape(n_workers, n_steps, sub))


def _expert_kernel(vb_ref, ve_ref, vrows_ref, nvis_ref, nxt_ref,
                   x_ref, wgu_hbm, bgu_ref, wd_hbm, bd_ref, y_ref,
                   wgu_stage, wd_stage, wgu_bf, wd_bf, sem):
    v = pl.program_id(0)
    real = v < nvis_ref[0]
    e = ve_ref[v]
    first_of_expert = jnp.logical_or(v == 0, e != ve_ref[jnp.maximum(v - 1, 0)])

    def weight_copies(expert):
        return (pltpu.make_async_copy(wgu_hbm.at[expert], wgu_stage, sem.at[0]),
                pltpu.make_async_copy(wd_hbm.at[expert], wd_stage, sem.at[1]))

    @pl.when(v == 0)
    def _():
        for c in weight_copies(e):
            c.start()

    def load_weights():
        for c in weight_copies(e):
            c.wait()
        wgu_bf[...] = wgu_stage[...].astype(BF16)
        wd_bf[...] = wd_stage[...].astype(BF16)

    def fetch_next():
        nxt = nxt_ref[e]

        @pl.when(nxt >= 0)
        def _():
            for c in weight_copies(nxt):
                c.start()

    def run(n_rows):
        x_lo, x_hi = _unpack_rows(x_ref[0:n_rows, :])
        xb = jnp.concatenate([x_lo.astype(BF16), x_hi.astype(BF16)], axis=1)
        gu = jnp.dot(xb, wgu_bf[...], preferred_element_type=F32) + bgu_ref[0]
        gate = jnp.minimum(gu[:, :D_EXPERT], SWIGLU_LIMIT)
        up = jnp.clip(gu[:, D_EXPERT:], -SWIGLU_LIMIT, SWIGLU_LIMIT)
        act = gate * jax.nn.sigmoid(SWIGLU_ALPHA * gate) * (up + 1.0)
        y = jnp.dot(act.astype(BF16), wd_bf[...], preferred_element_type=F32) + bd_ref[0]
        y_ref[0:n_rows, :] = _pack_rows(y)

    rows = vrows_ref[v]
    full = EXPERT_ROW_CLASSES[0]
    first = jnp.logical_and(first_of_expert, real)

    @pl.when(jnp.logical_and(first, rows == full))
    def _():
        load_weights()
        run(full)
        fetch_next()

    @pl.when(jnp.logical_and(first, rows != full))
    def _():
        load_weights()
        fetch_next()

    later_full = jnp.logical_and(real, jnp.logical_not(first_of_expert))
    pl.when(jnp.logical_and(later_full, rows == full))(functools.partial(run, full))
    for n_rows in EXPERT_ROW_CLASSES[1:]:
        pl.when(jnp.logical_and(real, rows == n_rows))(functools.partial(run, n_rows))


def _experts(x_sorted, visits, w_gate_up, b_gate_up, w_down, b_down):
    n_rows, W = x_sorted.shape
    D = 2 * W
    tm = EXPERT_TILE
    n_visits = visits[0].shape[0]
    n_prefetch = len(visits)
    by_block = lambda v, vb, *_: (vb[v], 0)
    by_expert = lambda v, vb, ve, *_: (ve[v], 0, 0)
    grid_spec = pltpu.PrefetchScalarGridSpec(
        num_scalar_prefetch=n_prefetch,
        grid=(n_visits,),
        in_specs=[
            pl.BlockSpec((tm, W), by_block),
            pl.BlockSpec(memory_space=pl.ANY),
            pl.BlockSpec((1, 1, 2 * D_EXPERT), by_expert),
            pl.BlockSpec(memory_space=pl.ANY),
            pl.BlockSpec((1, 1, D), by_expert),
        ],
        out_specs=pl.BlockSpec((tm, W), by_block),
        scratch_shapes=[pltpu.VMEM((D, 2 * D_EXPERT), F32), pltpu.VMEM((D_EXPERT, D), F32),
                        pltpu.VMEM((D, 2 * D_EXPERT), BF16), pltpu.VMEM((D_EXPERT, D), BF16),
                        pltpu.SemaphoreType.DMA((2,))],
    )
    return pl.pallas_call(
        _expert_kernel,
        grid_spec=grid_spec,
        out_shape=jax.ShapeDtypeStruct((n_rows, W), U32),
        compiler_params=pltpu.CompilerParams(
            dimension_semantics=("arbitrary",), vmem_limit_bytes=VMEM_LIMIT),
        name="expert_gmm",
    )(*visits, x_sorted, w_gate_up, b_gate_up.reshape(N_EXPERTS, 1, -1), w_down, b_down.reshape(N_EXPERTS, 1, -1))


def _visit_schedule(counts, n_assign, capacity):
    tm = EXPERT_TILE
    assert EXPERT_ROW_CLASSES[0] == tm and list(EXPERT_ROW_CLASSES) == sorted(EXPERT_ROW_CLASSES, reverse=True)
    n_visits = n_assign // tm + N_EXPERTS
    n_full = counts // tm
    tail = counts - n_full * tm
    tail_rows = jnp.zeros_like(tail)
    for rows in EXPERT_ROW_CLASSES:
        tail_rows = jnp.where(tail <= rows, rows, tail_rows)
    per_e = n_full + (tail > 0)
    vend = jnp.cumsum(per_e)
    nvis = vend[-1]
    vc = jnp.minimum(jnp.arange(n_visits, dtype=I32), nvis - 1)
    done = vc[:, None] >= vend[None, :]
    e = jnp.sum(done, axis=1).astype(I32)
    local = vc - jnp.sum(jnp.where(done, per_e[None, :], 0), axis=1)
    mine = e[:, None] == jnp.arange(N_EXPERTS, dtype=I32)[None, :]
    is_tail = local >= jnp.sum(jnp.where(mine, n_full[None, :], 0), axis=1)
    vrows = jnp.where(is_tail, jnp.sum(jnp.where(mine, tail_rows[None, :], 0), axis=1), tm)
    blk = e * (capacity // tm) + local
    ids = jnp.arange(N_EXPERTS, dtype=I32)
    nonempty = counts > 0
    later = jnp.logical_and(nonempty[None, :], ids[None, :] > ids[:, None])
    nxt = jnp.where(jnp.any(later, axis=1), jnp.argmax(later, axis=1), -1).astype(I32)
    return blk.astype(I32), e, vrows.astype(I32), nvis.reshape(1).astype(I32), nxt


def _combine_kernel(y0_ref, y1_ref, y2_ref, y3_ref, x1_ref, gate_ref, g_ref, o_ref):
    half = x1_ref.shape[1] // 2
    gates = gate_ref[...]
    lo = x1_ref[:, :half]
    hi = x1_ref[:, half:]
    for kk, y_ref in enumerate((y0_ref, y1_ref, y2_ref, y3_ref)):
        y_lo, y_hi = _unpack_rows(y_ref[...])
        lo = lo + gates[:, kk:kk + 1] * y_lo
        hi = hi + gates[:, kk:kk + 1] * y_hi
    ms = (jnp.sum(lo * lo, axis=-1, keepdims=True) + jnp.sum(hi * hi, axis=-1, keepdims=True)) / (2 * half)
    scale = lax.rsqrt(ms + RMS_EPS)
    o_ref[:, :half] = lo * scale * g_ref[:, :half]
    o_ref[:, half:] = hi * scale * g_ref[:, half:]


def _combine(y_tok, x1, gates_t, g_final):
    T, D = x1.shape
    tm = COMBINE_TILE
    nt = T // tm
    y_spec = lambda kk: pl.BlockSpec((tm, D // 2), lambda i: (kk * nt + i, 0))
    return pl.pallas_call(
        _combine_kernel,
        grid=(nt,),
        in_specs=[y_spec(0), y_spec(1), y_spec(2), y_spec(3),
                  pl.BlockSpec((tm, D), lambda i: (i, 0)),
                  pl.BlockSpec((tm, TOP_K), lambda i: (i, 0)),
                  pl.BlockSpec((1, D), lambda i: (0, 0))],
        out_specs=pl.BlockSpec((tm, D), lambda i: (i, 0)),
        out_shape=jax.ShapeDtypeStruct((T, D), F32),
        compiler_params=pltpu.CompilerParams(
            dimension_semantics=("arbitrary",), vmem_limit_bytes=VMEM_LIMIT),
        name="combine_norm",
    )(y_tok, y_tok, y_tok, y_tok, x1, gates_t, g_final.reshape(1, D))


def kernel(x, g_mix, w_in, pool_w, pool_scale, w_out, g_moe, w_router, b_router, w_gate_up, b_gate_up,
           w_down, b_down, g_final):
    B, S, D = x.shape
    T = B * S
    assert g_mix.shape[0] == 1, "single-layer problem: the final norm is fused into the combine step"
    q, k, v, pool = _in_proj(x, g_mix[0], w_in[0], pool_w[0], pool_scale[0])
    attn = _attention(q, k, v)
    capacity = T
    x1, h2, gates, dest, cnt = _out_proj_router(
        attn.reshape(T, SB_WIDTH), pool.reshape(T, POOL_WIDTH), x.reshape(T, D),
        w_out[0], g_moe[0], w_router[0], b_router[0], capacity)
    x_sorted = _sc_dispatch(h2, dest, N_EXPERTS * capacity)
    visits = _visit_schedule(cnt[:, 0].astype(I32), T * TOP_K, capacity)
    y = _experts(x_sorted, visits, w_gate_up[0], b_gate_up[0], w_down[0], b_down[0])
    y_tok = _sc_gather(y, dest.reshape(-1))
    return _combine(y_tok, x1, gates.T, g_final).reshape(B, S, D)
```

```python
import functools

import jax
import jax.numpy as jnp
from jax import lax
from jax.experimental import pallas as pl
from jax.experimental.pallas import tpu as pltpu
from jax.experimental.pallas import tpu_sc as plsc

F32 = jnp.float32
BF16 = jnp.bfloat16
I32 = jnp.int32
U32 = jnp.uint32

D_MODEL = 1024
SB_HEADS = 8
SB_HEAD_DIM = 64
SB_WIDTH = SB_HEADS * SB_HEAD_DIM
POOL_WINDOWS = (2, 4, 8, 16)
POOL_WIDTH = 512
POOL_GROUP_DIM = 128
N_EXPERTS = 32
TOP_K = 4
D_EXPERT = 1024
SWIGLU_LIMIT = 7.0
SWIGLU_ALPHA = 1.702
RMS_EPS = 1e-5

LANES = 128
HALO = 32
PROJ_TILE = 1024
ROUTER_PARTS = 4
ATTN_TILE = 256
ATTN_SUBTILES = 4
EXPERT_TILE = 1024
EXPERT_ROW_CLASSES = (1024, 512, 256)
COMBINE_TILE = 1024
ATTN_SKIP_LOG2 = -160.0
LOG2_E = 1.4426950408889634
V7X_VMEM_BYTES = 64 * 1024 * 1024
VMEM_LIMIT = V7X_VMEM_BYTES * 7 // 8

_NT = (((1,), (1,)), ((), ()))


def _rms(x, g):
    ms = jnp.mean(x * x, axis=-1, keepdims=True)
    return x * lax.rsqrt(ms + RMS_EPS) * g


def _pack_rows(x):
    n = x.shape[1] // 2
    lo = lax.bitcast_convert_type(x[:, :n].astype(BF16).astype(F32), U32)
    hi = lax.bitcast_convert_type(x[:, n:].astype(BF16).astype(F32), U32)
    return (lo >> 16) | (hi & jnp.uint32(0xFFFF0000))


def _unpack_rows(w):
    lo = lax.bitcast_convert_type(w << 16, F32)
    hi = lax.bitcast_convert_type(w & jnp.uint32(0xFFFF0000), F32)
    return lo, hi


def _in_proj_kernel(x_ref, w_ref, pw_ref, ps_ref, q_ref, k_ref, v_ref, p_ref, uext_ref, xb_ref, lvl_ref):
    s = pl.program_id(1)
    tm = x_ref.shape[1]
    x = x_ref[0]
    inv = lax.rsqrt(jnp.mean(x * x, axis=-1, keepdims=True) + RMS_EPS)
    xb_ref[...] = x.astype(BF16)
    proj = jnp.dot(xb_ref[...], w_ref[...], preferred_element_type=F32)
    q_ref[0] = (proj[:, 0:SB_WIDTH] * (inv * (LOG2_E * SB_HEAD_DIM ** -0.5))).astype(BF16)
    k_ref[0] = (proj[:, SB_WIDTH:2 * SB_WIDTH] * inv).astype(BF16)
    v_ref[0] = (proj[:, 2 * SB_WIDTH:3 * SB_WIDTH] * inv).astype(BF16)
    u = proj[:, 3 * SB_WIDTH:] * inv

    lo = HALO // 2

    @pl.when(s == 0)
    def _():
        uext_ref[0:HALO, :] = jnp.zeros((HALO, POOL_WIDTH), F32)
        lvl_ref[:, 0:lo, :] = jnp.zeros((2, lo, POOL_GROUP_DIM), F32)

    uext_ref[HALO:, :] = u
    t = s * tm + lax.broadcasted_iota(I32, (tm, 1), 0)
    for g, w in enumerate(POOL_WINDOWS):
        sl = slice(g * POOL_GROUP_DIM, (g + 1) * POOL_GROUP_DIM)
        lvl_ref[0, lo:, :] = uext_ref[lo:, sl] + uext_ref[lo - 1:HALO + tm - 1, sl]
        cur, k = 0, 2
        while k < w:
            lvl_ref[1 - cur, lo:, :] = lvl_ref[cur, lo:, :] + lvl_ref[cur, lo - k:HALO + tm - k, :]
            cur, k = 1 - cur, 2 * k
        ug = u[:, sl]
        count = jnp.minimum(t + 1, w).astype(F32)
        pooled = lvl_ref[cur, HALO:, :] / count - ug
        mixed = jnp.dot(pooled.astype(BF16), pw_ref[g], preferred_element_type=F32)
        p_ref[0, :, sl] = (mixed * ps_ref[:, sl]).astype(BF16)
    uext_ref[0:HALO, :] = u[tm - HALO:, :]


def _in_proj(x, g_mix, w_in, pool_w, pool_scale):
    B, S, D = x.shape
    tm = PROJ_TILE
    n_out = w_in.shape[1]
    out_sd = jax.ShapeDtypeStruct((B, S, SB_WIDTH), BF16)
    blk = pl.BlockSpec((1, tm, SB_WIDTH), lambda b, s: (b, s, 0))
    return pl.pallas_call(
        _in_proj_kernel,
        grid=(B, S // tm),
        in_specs=[
            pl.BlockSpec((1, tm, D), lambda b, s: (b, s, 0)),
            pl.BlockSpec((D, n_out), lambda b, s: (0, 0)),
            pl.BlockSpec(pool_w.shape, lambda b, s: (0, 0, 0)),
            pl.BlockSpec((1, POOL_WIDTH), lambda b, s: (0, 0)),
        ],
        out_specs=[blk, blk, blk, blk],
        out_shape=[out_sd, out_sd, out_sd, out_sd],
        scratch_shapes=[pltpu.VMEM((HALO + tm, POOL_WIDTH), F32), pltpu.VMEM((tm, D), BF16),
                        pltpu.VMEM((2, HALO + tm, POOL_GROUP_DIM), F32)],
        compiler_params=pltpu.CompilerParams(
            dimension_semantics=("arbitrary", "arbitrary"), vmem_limit_bytes=VMEM_LIMIT),
        name="in_proj_pool",
    )(x, (g_mix[:, None] * w_in).astype(BF16), pool_w.astype(BF16), pool_scale.reshape(1, POOL_WIDTH))


def _attn_kernel(q_ref, k_ref, v_ref, o_ref):
    tq = ATTN_TILE
    tk = tq
    qi = pl.program_id(2)
    lane = lax.broadcasted_iota(I32, (tq, LANES), 1)
    row = lax.broadcasted_iota(I32, (tq, tk), 0)
    col = lax.broadcasted_iota(I32, (tq, tk), 1)
    causal = col < row
    tri = (row > col).astype(BF16)

    def head_rows(r, h):
        rows = q_ref[0, r * tq:(r + 1) * tq, :]
        return jnp.where((lane >= h * SB_HEAD_DIM) & (lane < (h + 1) * SB_HEAD_DIM), rows, jnp.zeros_like(rows))

    def scores(qh, j):
        kb = k_ref[0, pl.ds(pl.multiple_of(j * tk, tk), tk), :]
        return lax.dot_general(qh, kb, _NT, preferred_element_type=F32)

    def values(j):
        return v_ref[0, pl.ds(pl.multiple_of(j * tk, tk), tk), :]

    def log_terms(z, mask):
        neg_abs = lax.bitcast_convert_type(lax.bitcast_convert_type(z, U32) | jnp.uint32(0x80000000), F32)
        nl = jnp.maximum(z, 0.0) + jnp.log2(1.0 + jnp.exp2(neg_abs))
        if mask:
            nl = jnp.where(causal, nl, 0.0)
        return nl.astype(BF16), z - nl, nl[:, 0:1]

    def tile(qh, j, mask):
        nl, lb, nl0 = log_terms(scores(qh, j), mask)
        ntail = jnp.dot(nl, tri, preferred_element_type=F32)
        return lb - ntail, -(ntail[:, 0:1] + nl0), values(j)

    nq = ATTN_SUBTILES
    has_prev = qi > 0
    qs = [head_rows(r, h) for r in range(nq) for h in range(2)]
    chains = []
    for n, qh in enumerate(qs):
        diag = nq * qi + n // 2
        chains += [(qh, diag, True), (qh, jnp.maximum(diag - 1, 0), False)]
    zs = [scores(qh, j) for qh, j, _ in chains]
    terms = [log_terms(z, is_diag) for z, (_, _, is_diag) in zip(zs, chains)]
    nts = [jnp.dot(nl, tri, preferred_element_type=F32) for nl, _, _ in terms]
    lbs = [lb for _, lb, _ in terms]
    tots = [-(nt[:, 0:1] + nl0) for nt, (_, _, nl0) in zip(nts, terms)]
    ws = []
    for n in range(2 * nq):
        d, p = 2 * n, 2 * n + 1
        w_d = jnp.where(causal, jnp.exp2(lbs[d] - nts[d]), 0.0)
        w_p = jnp.exp2(lbs[p] - nts[p] + tots[d])
        if n < 2:
            w_p = jnp.where(has_prev, w_p, 0.0)
        ws += [w_d.astype(BF16), w_p.astype(BF16)]
    pvs = [jnp.dot(w, values(j), preferred_element_type=F32) for w, (_, j, _) in zip(ws, chains)]
    accs = [pvs[2 * n] + pvs[2 * n + 1] for n in range(2 * nq)]
    carries = [tots[2 * n] + tots[2 * n + 1] for n in range(2 * nq)]

    def cond(st):
        j, cs, _ = st
        live = functools.reduce(jnp.maximum, [jnp.max(c) for c in cs])
        return jnp.logical_and(j >= 0, live > ATTN_SKIP_LOG2)

    def body(st):
        j, cs, acs = st
        new_c, new_a = [], []
        for n, (qh, c, a) in enumerate(zip(qs, cs, acs)):
            lag = nq - 1 - n // 2
            if lag:
                c = jnp.where(j >= lag, c, -jnp.inf)
            lw, tot, vb = tile(qh, jnp.maximum(j - lag, 0), False)
            new_a.append(a + jnp.dot(jnp.exp2(lw + c).astype(BF16), vb, preferred_element_type=F32))
            new_c.append(c + tot)
        return j - 1, tuple(new_c), tuple(new_a)

    _, _, accs = lax.while_loop(cond, body, (nq * qi + nq - 3, tuple(carries), tuple(accs)))
    for r in range(nq):
        o_ref[0, r * tq:(r + 1) * tq, :] = jnp.where(
            lane < SB_HEAD_DIM, accs[2 * r], accs[2 * r + 1]).astype(o_ref.dtype)


def _attention(q, k, v):
    B, S, W = q.shape
    tq = ATTN_SUBTILES * ATTN_TILE
    n_pairs = W // LANES
    return pl.pallas_call(
        _attn_kernel,
        grid=(B, n_pairs, S // tq),
        in_specs=[
            pl.BlockSpec((1, tq, LANES), lambda b, p, i: (b, i, p)),
            pl.BlockSpec((1, S, LANES), lambda b, p, i: (b, 0, p)),
            pl.BlockSpec((1, S, LANES), lambda b, p, i: (b, 0, p)),
        ],
        out_specs=pl.BlockSpec((1, tq, LANES), lambda b, p, i: (b, i, p)),
        out_shape=jax.ShapeDtypeStruct((B, S, W), BF16),
        compiler_params=pltpu.CompilerParams(
            dimension_semantics=("arbitrary", "arbitrary", "arbitrary"), vmem_limit_bytes=VMEM_LIMIT),
        name="stickbreak_attn",
    )(q, k, v)


def _out_proj_router_kernel(attn_ref, pool_ref, x_ref, wo_ref, g_ref, wr_ref, br_ref, tri_ref, cin_ref,
                            *rest, capacity):
    x1_ref, h2_ref, gate_ref, dest_ref, cnt_ref, carry_ref = rest[-6:]
    i = pl.program_id(0)
    tm = x_ref.shape[0]

    @pl.when(i == 0)
    def _():
        carry_ref[...] = cin_ref[...]

    mixed = jnp.dot(attn_ref[...], wo_ref[0:SB_WIDTH, :], preferred_element_type=F32)
    mixed = mixed + jnp.dot(pool_ref[...], wo_ref[SB_WIDTH:, :], preferred_element_type=F32)
    x1 = x_ref[...] + mixed
    h2 = _rms(x1, g_ref[...])
    x1_ref[...] = x1
    h2_ref[...] = _pack_rows(h2)

    hh = h2.astype(BF16)
    hl = (h2 - hh.astype(F32)).astype(BF16)
    wr = wr_ref[...]
    wh = wr.astype(BF16)
    wl = (wr - wh.astype(F32)).astype(BF16)
    logits = (lax.dot_general(wh, hh, _NT, preferred_element_type=F32)
              + lax.dot_general(wh, hl, _NT, preferred_element_type=F32)
              + lax.dot_general(wl, hh, _NT, preferred_element_type=F32)) + br_ref[...]

    eid = lax.broadcasted_iota(I32, (N_EXPERTS, tm), 0).astype(F32)
    work = logits
    vals, ids = [], []
    for _ in range(TOP_K):
        m = jnp.max(work, axis=0, keepdims=True)
        sel = jnp.min(jnp.where(work == m, eid, float(N_EXPERTS)), axis=0, keepdims=True)
        vals.append(m)
        ids.append(sel)
        work = jnp.where(eid == sel, -jnp.inf, work)
    exps = [jnp.exp(v - vals[0]) for v in vals]
    denom = exps[0] + exps[1] + exps[2] + exps[3]

    onehot = jnp.zeros((N_EXPERTS, tm), F32)
    for sel in ids:
        onehot = onehot + (eid == sel).astype(F32)
    before = jnp.dot(onehot.astype(BF16), tri_ref[...], preferred_element_type=F32) + carry_ref[...]
    for kk in range(TOP_K):
        gate_ref[kk:kk + 1, :] = exps[kk] / denom
        rk = jnp.sum(jnp.where(eid == ids[kk], before, 0.0), axis=0, keepdims=True)
        dest_ref[kk:kk + 1, :] = (ids[kk] * float(capacity) + rk).astype(I32)
    carry_ref[...] = carry_ref[...] + jnp.sum(onehot, axis=1, keepdims=True)
    cnt_ref[...] = jnp.broadcast_to(carry_ref[...], cnt_ref.shape)


def _out_proj_router(attn, pool, x, weights, capacity, part, counts_in, x1_buf):
    T, D = x.shape
    tm = PROJ_TILE
    steps = T // tm // ROUTER_PARTS
    base = part * steps
    tp = steps * tm
    w_out, g_moe, w_router, b_router, tri = weights
    in_blk = lambda w: pl.BlockSpec((tm, w), lambda i: (i + base, 0))
    fixed = lambda shape: pl.BlockSpec(shape, lambda i: tuple(0 for _ in shape))
    sel_blk = pl.BlockSpec((TOP_K, tm), lambda i: (0, i))
    assert N_EXPERTS * capacity < 2 ** 24
    in_specs = [in_blk(SB_WIDTH), in_blk(POOL_WIDTH), in_blk(D), fixed((D, D)), fixed((1, D)),
                fixed((N_EXPERTS, D)), fixed((N_EXPERTS, 1)), fixed((tm, tm)), fixed((N_EXPERTS, 1))]
    args = [attn, pool, x, w_out, g_moe, w_router, b_router, tri, counts_in]
    aliases = {}
    if x1_buf is not None:
        in_specs.append(pl.BlockSpec(memory_space=pl.ANY))
        args.append(x1_buf)
        aliases = {len(args) - 1: 0}
    return pl.pallas_call(
        functools.partial(_out_proj_router_kernel, capacity=capacity),
        grid=(steps,),
        in_specs=in_specs,
        out_specs=[in_blk(D), pl.BlockSpec((tm, D // 2), lambda i: (i, 0)), sel_blk, sel_blk,
                   fixed((N_EXPERTS, LANES))],
        out_shape=[jax.ShapeDtypeStruct((T, D), F32), jax.ShapeDtypeStruct((tp, D // 2), U32),
                   jax.ShapeDtypeStruct((TOP_K, tp), F32), jax.ShapeDtypeStruct((TOP_K, tp), I32),
                   jax.ShapeDtypeStruct((N_EXPERTS, LANES), F32)],
        scratch_shapes=[pltpu.VMEM((N_EXPERTS, 1), F32)],
        input_output_aliases=aliases,
        compiler_params=pltpu.CompilerParams(
            dimension_semantics=("arbitrary",), vmem_limit_bytes=VMEM_LIMIT),
        name="out_proj_router",
    )(*args)


SC_ROWS_PER_COPY = 64


def _sc_workers():
    info = plsc.get_sparse_core_info()
    return info.num_cores, info.num_cores * info.num_subcores


def _sc_dispatch(rows, dest, n_out, into=None):
    T, W = rows.shape
    K = dest.shape[0]
    sub = SC_ROWS_PER_COPY
    n_cores, n_workers = _sc_workers()
    per_w = T // n_workers
    n_chunks = per_w // sub
    assert per_w * n_workers == T and n_chunks * sub == per_w and n_chunks % 2 == 0
    idx = dest.reshape(K, n_workers, n_chunks, sub).transpose(1, 2, 0, 3).reshape(n_workers, n_chunks * K, sub)
    mesh = plsc.VectorSubcoreMesh(core_axis_name="core", subcore_axis_name="subcore")

    scratch = [pltpu.VMEM((n_chunks * K, sub), I32), pltpu.VMEM((2, sub, W), rows.dtype),
               pltpu.SemaphoreType.DMA((2,)), pltpu.SemaphoreType.DMA((2,))]

    def scatter_rows(x_hbm, i_hbm, o_hbm, idx_v, buf, rsem, wsem):
        wid = lax.axis_index("subcore") * n_cores + lax.axis_index("core")
        base = wid * per_w
        pltpu.sync_copy(i_hbm.at[wid], idx_v)

        def read(c, slot):
            return pltpu.make_async_copy(x_hbm.at[pl.ds(base + c * sub, sub)], buf.at[slot], rsem.at[slot])

        def write(c, kk, slot):
            return pltpu.make_async_copy(buf.at[slot], o_hbm.at[idx_v.at[c * K + kk]], wsem.at[slot])

        read(0, 0).start()

        @pl.loop(0, n_chunks, step=2)
        def _(c0):
            for b in range(2):
                c = c0 + b
                read(c, b).wait()
                for kk in range(K):
                    write(c, kk, b).start()

                @pl.when(c + 1 < n_chunks)
                def _():
                    @pl.when(c >= 1)
                    def _():
                        for kk in range(K):
                            write(c - 1, kk, 1 - b).wait()
                    read(c + 1, 1 - b).start()

        for kk in range(K):
            write(n_chunks - 2, kk, 0).wait()
            write(n_chunks - 1, kk, 1).wait()

    if into is None:
        out_type = jax.ShapeDtypeStruct((n_out, W), rows.dtype)
        return pl.kernel(scatter_rows, out_type=out_type, mesh=mesh, scratch_types=scratch)(rows, idx)
    assert into.shape == (n_out, W) and into.dtype == rows.dtype
    out_ref = jax.new_ref(into)
    pl.kernel(scatter_rows, out_type=(), mesh=mesh, scratch_types=scratch)(rows, idx, out_ref)
    return jax.freeze(out_ref)


def _sc_gather(table, indices):
    M = indices.shape[0]
    W = table.shape[1]
    sub = SC_ROWS_PER_COPY
    n_cores, n_workers = _sc_workers()
    per_w = M // n_workers
    n_steps = per_w // sub
    assert per_w * n_workers == M and n_steps * sub == per_w and n_steps % 2 == 0
    mesh = plsc.VectorSubcoreMesh(core_axis_name="core", subcore_axis_name="subcore")

    @functools.partial(
        pl.kernel, out_type=jax.ShapeDtypeStruct((M, W), table.dtype), mesh=mesh,
        scratch_types=[pltpu.VMEM((n_steps, sub), I32), pltpu.VMEM((2, sub, W), table.dtype),
                       pltpu.SemaphoreType.DMA((2,)), pltpu.SemaphoreType.DMA((2,))])
    def gather_rows(x_hbm, i_hbm, o_hbm, idx_v, buf, gsem, wsem):
        wid = lax.axis_index("subcore") * n_cores + lax.axis_index("core")
        base = wid * per_w
        pltpu.sync_copy(i_hbm.at[wid], idx_v)

        def gather(s, slot):
            return pltpu.make_async_copy(x_hbm.at[idx_v.at[s]], buf.at[slot], gsem.at[slot])

        def write(s, slot):
            return pltpu.make_async_copy(buf.at[slot], o_hbm.at[pl.ds(base + s * sub, sub)], wsem.at[slot])

        gather(0, 0).start()

        @pl.loop(0, n_steps, step=2)
        def _(s0):
            for b in range(2):
                s = s0 + b
                gather(s, b).wait()
                write(s, b).start()

                @pl.when(s + 1 < n_steps)
                def _():
                    @pl.when(s >= 1)
                    def _():
                        write(s - 1, 1 - b).wait()
                    gather(s + 1, 1 - b).start()

        write(n_steps - 2, 0).wait()
        write(n_steps - 1, 1).wait()

    return gather_rows(table, indices.reshape(n_workers, n_steps, sub))


def _expert_kernel(vb_ref, ve_ref, vrows_ref, nvis_ref, nxt_ref,
                   x_ref, wgu_hbm, bgu_ref, wd_hbm, bd_ref, y_ref,
                   wgu_stage, wd_stage, wgu_bf, wd_bf, sem):
    v = pl.program_id(0)
    real = v < nvis_ref[0]
    e = ve_ref[v]
    first_of_expert = jnp.logical_or(v == 0, e != ve_ref[jnp.maximum(v - 1, 0)])

    def weight_copies(expert):
        return (pltpu.make_async_copy(wgu_hbm.at[expert], wgu_stage, sem.at[0]),
                pltpu.make_async_copy(wd_hbm.at[expert], wd_stage, sem.at[1]))

    @pl.when(v == 0)
    def _():
        for c in weight_copies(e):
            c.start()

    def load_weights():
        for c in weight_copies(e):
            c.wait()
        wgu_bf[...] = wgu_stage[...].astype(BF16)
        wd_bf[...] = wd_stage[...].astype(BF16)

    def fetch_next():
        nxt = nxt_ref[e]

        @pl.when(nxt >= 0)
        def _():
            for c in weight_copies(nxt):
                c.start()

    def run(n_rows):
        x_lo, x_hi = _unpack_rows(x_ref[0:n_rows, :])
        xb = jnp.concatenate([x_lo.astype(BF16), x_hi.astype(BF16)], axis=1)
        gu = jnp.dot(xb, wgu_bf[...], preferred_element_type=F32) + bgu_ref[0]
        gate = jnp.minimum(gu[:, :D_EXPERT], SWIGLU_LIMIT)
        up = jnp.clip(gu[:, D_EXPERT:], -SWIGLU_LIMIT, SWIGLU_LIMIT)
        act = gate * jax.nn.sigmoid(SWIGLU_ALPHA * gate) * (up + 1.0)
        y = jnp.dot(act.astype(BF16), wd_bf[...], preferred_element_type=F32) + bd_ref[0]
        y_ref[0:n_rows, :] = _pack_rows(y)

    rows = vrows_ref[v]
    full = EXPERT_ROW_CLASSES[0]
    first = jnp.logical_and(first_of_expert, real)

    @pl.when(jnp.logical_and(first, rows == full))
    def _():
        load_weights()
        run(full)
        fetch_next()

    @pl.when(jnp.logical_and(first, rows != full))
    def _():
        load_weights()
        fetch_next()

    later_full = jnp.logical_and(real, jnp.logical_not(first_of_expert))
    pl.when(jnp.logical_and(later_full, rows == full))(functools.partial(run, full))
    for n_rows in EXPERT_ROW_CLASSES[1:]:
        pl.when(jnp.logical_and(real, rows == n_rows))(functools.partial(run, n_rows))


def _experts(x_sorted, visits, w_gate_up, b_gate_up, w_down, b_down):
    n_rows, W = x_sorted.shape
    D = 2 * W
    tm = EXPERT_TILE
    n_visits = visits[0].shape[0]
    n_prefetch = len(visits)
    by_block = lambda v, vb, *_: (vb[v], 0)
    by_expert = lambda v, vb, ve, *_: (ve[v], 0, 0)
    grid_spec = pltpu.PrefetchScalarGridSpec(
        num_scalar_prefetch=n_prefetch,
        grid=(n_visits,),
        in_specs=[
            pl.BlockSpec((tm, W), by_block),
            pl.BlockSpec(memory_space=pl.ANY),
            pl.BlockSpec((1, 1, 2 * D_EXPERT), by_expert),
            pl.BlockSpec(memory_space=pl.ANY),
            pl.BlockSpec((1, 1, D), by_expert),
        ],
        out_specs=pl.BlockSpec((tm, W), by_block),
        scratch_shapes=[pltpu.VMEM((D, 2 * D_EXPERT), F32), pltpu.VMEM((D_EXPERT, D), F32),
                        pltpu.VMEM((D, 2 * D_EXPERT), BF16), pltpu.VMEM((D_EXPERT, D), BF16),
                        pltpu.SemaphoreType.DMA((2,))],
    )
    return pl.pallas_call(
        _expert_kernel,
        grid_spec=grid_spec,
        out_shape=jax.ShapeDtypeStruct((n_rows, W), U32),
        compiler_params=pltpu.CompilerParams(
            dimension_semantics=("arbitrary",), vmem_limit_bytes=VMEM_LIMIT),
        name="expert_gmm",
    )(*visits, x_sorted, w_gate_up, b_gate_up.reshape(N_EXPERTS, 1, -1), w_down, b_down.reshape(N_EXPERTS, 1, -1))


def _visit_schedule(counts, n_assign, capacity):
    tm = EXPERT_TILE
    assert EXPERT_ROW_CLASSES[0] == tm and list(EXPERT_ROW_CLASSES) == sorted(EXPERT_ROW_CLASSES, reverse=True)
    n_visits = n_assign // tm + N_EXPERTS
    n_full = counts // tm
    tail = counts - n_full * tm
    tail_rows = jnp.zeros_like(tail)
    for rows in EXPERT_ROW_CLASSES:
        tail_rows = jnp.where(tail <= rows, rows, tail_rows)
    per_e = n_full + (tail > 0)
    vend = jnp.cumsum(per_e)
    nvis = vend[-1]
    vc = jnp.minimum(jnp.arange(n_visits, dtype=I32), nvis - 1)
    done = vc[:, None] >= vend[None, :]
    e = jnp.sum(done, axis=1).astype(I32)
    local = vc - jnp.sum(jnp.where(done, per_e[None, :], 0), axis=1)
    mine = e[:, None] == jnp.arange(N_EXPERTS, dtype=I32)[None, :]
    is_tail = local >= jnp.sum(jnp.where(mine, n_full[None, :], 0), axis=1)
    vrows = jnp.where(is_tail, jnp.sum(jnp.where(mine, tail_rows[None, :], 0), axis=1), tm)
    blk = e * (capacity // tm) + local
    ids = jnp.arange(N_EXPERTS, dtype=I32)
    nonempty = counts > 0
    later = jnp.logical_and(nonempty[None, :], ids[None, :] > ids[:, None])
    nxt = jnp.where(jnp.any(later, axis=1), jnp.argmax(later, axis=1), -1).astype(I32)
    return blk.astype(I32), e, vrows.astype(I32), nvis.reshape(1).astype(I32), nxt


def _combine_kernel(y0_ref, y1_ref, y2_ref, y3_ref, x1_ref, gate_ref, g_ref, o_ref):
    half = x1_ref.shape[1] // 2
    gates = gate_ref[...]
    lo = x1_ref[:, :half]
    hi = x1_ref[:, half:]
    for kk, y_ref in enumerate((y0_ref, y1_ref, y2_ref, y3_ref)):
        y_lo, y_hi = _unpack_rows(y_ref[...])
        lo = lo + gates[:, kk:kk + 1] * y_lo
        hi = hi + gates[:, kk:kk + 1] * y_hi
    ms = (jnp.sum(lo * lo, axis=-1, keepdims=True) + jnp.sum(hi * hi, axis=-1, keepdims=True)) / (2 * half)
    scale = lax.rsqrt(ms + RMS_EPS)
    o_ref[:, :half] = lo * scale * g_ref[:, :half]
    o_ref[:, half:] = hi * scale * g_ref[:, half:]


def _combine(y_tok, x1, gates_t, g_final):
    T, D = x1.shape
    tm = COMBINE_TILE
    nt = T // tm
    y_spec = lambda kk: pl.BlockSpec((tm, D // 2), lambda i: (kk * nt + i, 0))
    return pl.pallas_call(
        _combine_kernel,
        grid=(nt,),
        in_specs=[y_spec(0), y_spec(1), y_spec(2), y_spec(3),
                  pl.BlockSpec((tm, D), lambda i: (i, 0)),
                  pl.BlockSpec((tm, TOP_K), lambda i: (i, 0)),
                  pl.BlockSpec((1, D), lambda i: (0, 0))],
        out_specs=pl.BlockSpec((tm, D), lambda i: (i, 0)),
        out_shape=jax.ShapeDtypeStruct((T, D), F32),
        compiler_params=pltpu.CompilerParams(
            dimension_semantics=("arbitrary",), vmem_limit_bytes=VMEM_LIMIT),
        name="combine_norm",
    )(y_tok, y_tok, y_tok, y_tok, x1, gates_t, g_final.reshape(1, D))


def kernel(x, g_mix, w_in, pool_w, pool_scale, w_out, g_moe, w_router, b_router, w_gate_up, b_gate_up,
           w_down, b_down, g_final):
    B, S, D = x.shape
    T = B * S
    assert g_mix.shape[0] == 1, "single-layer problem: the final norm is fused into the combine step"
    q, k, v, pool = _in_proj(x, g_mix[0], w_in[0], pool_w[0], pool_scale[0])
    attn = _attention(q, k, v)
    capacity = T
    tm = PROJ_TILE
    tri = (lax.broadcasted_iota(I32, (tm, tm), 0) < lax.broadcasted_iota(I32, (tm, tm), 1)).astype(BF16)
    weights = (w_out[0].astype(BF16), g_moe[0].reshape(1, D), w_router[0].T, b_router[0].reshape(N_EXPERTS, 1), tri)
    x1 = x_sorted = None
    counts = jnp.zeros((N_EXPERTS, 1), F32)
    gates, dest = [], []
    for part in range(ROUTER_PARTS):
        x1, h2, gates_p, dest_p, cnt = _out_proj_router(
            attn.reshape(T, SB_WIDTH), pool.reshape(T, POOL_WIDTH), x.reshape(T, D), weights, capacity,
            part, counts, x1)
        x_sorted = _sc_dispatch(h2, dest_p, N_EXPERTS * capacity, x_sorted)
        counts = cnt[:, :1]
        gates.append(gates_p)
        dest.append(dest_p)
    gates = jnp.concatenate(gates, axis=1)
    dest = jnp.concatenate(dest, axis=1)
    visits = _visit_schedule(cnt[:, 0].astype(I32), T * TOP_K, capacity)
    y = _experts(x_sorted, visits, w_gate_up[0], b_gate_up[0], w_down[0], b_down[0])
    y_tok = _sc_gather(y, dest.reshape(-1))
    return _combine(y_tok, x1, gates.T, g_final).reshape(B, S, D)
```

```python
import functools

import jax
import jax.numpy as jnp
from jax import lax
from jax.experimental import pallas as pl
from jax.experimental.pallas import tpu as pltpu
from jax.experimental.pallas import tpu_sc as plsc

F32 = jnp.float32
BF16 = jnp.bfloat16
I32 = jnp.int32
U32 = jnp.uint32

D_MODEL = 1024
SB_HEADS = 8
SB_HEAD_DIM = 64
SB_WIDTH = SB_HEADS * SB_HEAD_DIM
POOL_WINDOWS = (2, 4, 8, 16)
POOL_WIDTH = 512
POOL_GROUP_DIM = 128
N_EXPERTS = 32
TOP_K = 4
D_EXPERT = 1024
SWIGLU_LIMIT = 7.0
SWIGLU_ALPHA = 1.702
RMS_EPS = 1e-5

LANES = 128
HALO = 32
PROJ_TILE = 1024
ROUTER_PARTS = 4
ATTN_TILE = 256
ATTN_SUBTILES = 4
EXPERT_TILE = 1024
EXPERT_ROW_CLASSES = (1024, 512, 256)
COMBINE_TILE = 1024
ATTN_SKIP_LOG2 = -160.0
LOG2_E = 1.4426950408889634
V7X_VMEM_BYTES = 64 * 1024 * 1024
VMEM_LIMIT = V7X_VMEM_BYTES * 7 // 8

_NT = (((1,), (1,)), ((), ()))


def _rms(x, g):
    ms = jnp.mean(x * x, axis=-1, keepdims=True)
    return x * lax.rsqrt(ms + RMS_EPS) * g


def _pack_rows(x):
    n = x.shape[1] // 2
    lo = lax.bitcast_convert_type(x[:, :n].astype(BF16).astype(F32), U32)
    hi = lax.bitcast_convert_type(x[:, n:].astype(BF16).astype(F32), U32)
    return (lo >> 16) | (hi & jnp.uint32(0xFFFF0000))


def _unpack_rows(w):
    lo = lax.bitcast_convert_type(w << 16, F32)
    hi = lax.bitcast_convert_type(w & jnp.uint32(0xFFFF0000), F32)
    return lo, hi


def _in_proj_kernel(x_ref, w_ref, pw_ref, ps_ref, q_ref, k_ref, v_ref, p_ref, uext_ref, xb_ref, lvl_ref, unew_ref, *, n_seq):
    i = pl.program_id(0)
    n_tiles = pl.num_programs(0) - 1
    tm = x_ref.shape[0]
    lo = HALO // 2

    @pl.when(i == 0)
    def _():
        uext_ref[...] = jnp.zeros_like(uext_ref)
        lvl_ref[:, 0:lo, :] = jnp.zeros((2, lo, POOL_GROUP_DIM), F32)

    t = lax.rem(i + n_seq - 1, n_seq) * tm + lax.broadcasted_iota(I32, (tm, 1), 0)

    def pool_group(g):
        w = POOL_WINDOWS[g]
        sl = slice(g * POOL_GROUP_DIM, (g + 1) * POOL_GROUP_DIM)
        lvl_ref[0, lo:, :] = uext_ref[lo:, sl] + uext_ref[lo - 1:HALO + tm - 1, sl]
        cur, k = 0, 2
        while k < w:
            lvl_ref[1 - cur, lo:, :] = lvl_ref[cur, lo:, :] + lvl_ref[cur, lo - k:HALO + tm - k, :]
            cur, k = 1 - cur, 2 * k
        count = jnp.minimum(t + 1, w).astype(F32)
        pooled = lvl_ref[cur, HALO:, :] / count - uext_ref[HALO:, sl]
        mixed = jnp.dot(pooled.astype(BF16), pw_ref[g], preferred_element_type=F32)
        p_ref[:, sl] = (mixed * ps_ref[:, sl]).astype(BF16)

    def shift_history():
        uext_ref[0:HALO, :] = jnp.where(lax.rem(i, n_seq) == 0, 0.0, uext_ref[tm:tm + HALO, :])

    @pl.when(i < n_tiles)
    def _():
        x = x_ref[...]
        inv = lax.rsqrt(jnp.mean(x * x, axis=-1, keepdims=True) + RMS_EPS)
        xb_ref[...] = x.astype(BF16)
        scales = (inv * (LOG2_E * SB_HEAD_DIM ** -0.5), inv, inv)
        for g, (o_ref, scale) in enumerate(zip((q_ref, k_ref, v_ref), scales)):
            cols = slice(g * SB_WIDTH, (g + 1) * SB_WIDTH)
            o_ref[...] = (jnp.dot(xb_ref[...], w_ref[:, cols], preferred_element_type=F32) * scale).astype(BF16)
            pool_group(g)
        unew_ref[...] = jnp.dot(xb_ref[...], w_ref[:, 3 * SB_WIDTH:], preferred_element_type=F32) * inv
        pool_group(3)
        shift_history()
        uext_ref[HALO:, :] = unew_ref[...]

    @pl.when(i == n_tiles)
    def _():
        for g in range(len(POOL_WINDOWS)):
            pool_group(g)


def _in_proj(x, g_mix, w_in, pool_w, pool_scale):
    B, S, D = x.shape
    tm = PROJ_TILE
    n_out = w_in.shape[1]
    n_tiles = B * S // tm
    out_sd = jax.ShapeDtypeStruct((B * S, SB_WIDTH), BF16)
    cur = lambda i: (jnp.minimum(i, n_tiles - 1), 0)
    prev = lambda i: (jnp.maximum(i - 1, 0), 0)
    blk = pl.BlockSpec((tm, SB_WIDTH), cur)
    q, k, v, p = pl.pallas_call(
        functools.partial(_in_proj_kernel, n_seq=S // tm),
        grid=(n_tiles + 1,),
        in_specs=[
            pl.BlockSpec((tm, D), cur),
            pl.BlockSpec((D, n_out), lambda i: (0, 0)),
            pl.BlockSpec(pool_w.shape, lambda i: (0, 0, 0)),
            pl.BlockSpec((1, POOL_WIDTH), lambda i: (0, 0)),
        ],
        out_specs=[blk, blk, blk, pl.BlockSpec((tm, POOL_WIDTH), prev)],
        out_shape=[out_sd, out_sd, out_sd, out_sd],
        scratch_shapes=[pltpu.VMEM((HALO + tm, POOL_WIDTH), F32), pltpu.VMEM((tm, D), BF16),
                        pltpu.VMEM((2, HALO + tm, POOL_GROUP_DIM), F32), pltpu.VMEM((tm, POOL_WIDTH), F32)],
        compiler_params=pltpu.CompilerParams(
            dimension_semantics=("arbitrary",), vmem_limit_bytes=VMEM_LIMIT),
        name="in_proj_pool",
    )(x.reshape(B * S, D), (g_mix[:, None] * w_in).astype(BF16), pool_w.astype(BF16),
      pool_scale.reshape(1, POOL_WIDTH))
    return (q.reshape(B, S, SB_WIDTH), k.reshape(B, S, SB_WIDTH), v.reshape(B, S, SB_WIDTH),
            p.reshape(B, S, POOL_WIDTH))


def _attn_kernel(q_ref, k_ref, v_ref, o_ref):
    tq = ATTN_TILE
    tk = tq
    qi = pl.program_id(2)
    lane = lax.broadcasted_iota(I32, (tq, LANES), 1)
    row = lax.broadcasted_iota(I32, (tq, tk), 0)
    col = lax.broadcasted_iota(I32, (tq, tk), 1)
    causal = col < row
    tri = (row > col).astype(BF16)

    def head_rows(r, h):
        rows = q_ref[0, r * tq:(r + 1) * tq, :]
        return jnp.where((lane >= h * SB_HEAD_DIM) & (lane < (h + 1) * SB_HEAD_DIM), rows, jnp.zeros_like(rows))

    def scores(qh, j):
        kb = k_ref[0, pl.ds(pl.multiple_of(j * tk, tk), tk), :]
        return lax.dot_general(qh, kb, _NT, preferred_element_type=F32)

    def values(j):
        return v_ref[0, pl.ds(pl.multiple_of(j * tk, tk), tk), :]

    def log_terms(z, mask):
        neg_abs = lax.bitcast_convert_type(lax.bitcast_convert_type(z, U32) | jnp.uint32(0x80000000), F32)
        nl = jnp.maximum(z, 0.0) + jnp.log2(1.0 + jnp.exp2(neg_abs))
        if mask:
            nl = jnp.where(causal, nl, 0.0)
        return nl.astype(BF16), z - nl, nl[:, 0:1]

    def tile(qh, j, mask):
        nl, lb, nl0 = log_terms(scores(qh, j), mask)
        ntail = jnp.dot(nl, tri, preferred_element_type=F32)
        return lb - ntail, -(ntail[:, 0:1] + nl0), values(j)

    nq = ATTN_SUBTILES
    has_prev = qi > 0
    qs = [head_rows(r, h) for r in range(nq) for h in range(2)]
    chains = []
    for n, qh in enumerate(qs):
        diag = nq * qi + n // 2
        chains += [(qh, diag, True), (qh, jnp.maximum(diag - 1, 0), False)]
    zs = [scores(qh, j) for qh, j, _ in chains]
    terms = [log_terms(z, is_diag) for z, (_, _, is_diag) in zip(zs, chains)]
    nts = [jnp.dot(nl, tri, preferred_element_type=F32) for nl, _, _ in terms]
    lbs = [lb for _, lb, _ in terms]
    tots = [-(nt[:, 0:1] + nl0) for nt, (_, _, nl0) in zip(nts, terms)]
    ws = []
    for n in range(2 * nq):
        d, p = 2 * n, 2 * n + 1
        w_d = jnp.where(causal, jnp.exp2(lbs[d] - nts[d]), 0.0)
        w_p = jnp.exp2(lbs[p] - nts[p] + tots[d])
        if n < 2:
            w_p = jnp.where(has_prev, w_p, 0.0)
        ws += [w_d.astype(BF16), w_p.astype(BF16)]
    pvs = [jnp.dot(w, values(j), preferred_element_type=F32) for w, (_, j, _) in zip(ws, chains)]
    accs = [pvs[2 * n] + pvs[2 * n + 1] for n in range(2 * nq)]
    carries = [tots[2 * n] + tots[2 * n + 1] for n in range(2 * nq)]

    def cond(st):
        j, cs, _ = st
        live = functools.reduce(jnp.maximum, [jnp.max(c) for c in cs])
        return jnp.logical_and(j >= 0, live > ATTN_SKIP_LOG2)

    def body(st):
        j, cs, acs = st
        new_c, new_a = [], []
        for n, (qh, c, a) in enumerate(zip(qs, cs, acs)):
            lag = nq - 1 - n // 2
            if lag:
                c = jnp.where(j >= lag, c, -jnp.inf)
            lw, tot, vb = tile(qh, jnp.maximum(j - lag, 0), False)
            new_a.append(a + jnp.dot(jnp.exp2(lw + c).astype(BF16), vb, preferred_element_type=F32))
            new_c.append(c + tot)
        return j - 1, tuple(new_c), tuple(new_a)

    _, _, accs = lax.while_loop(cond, body, (nq * qi + nq - 3, tuple(carries), tuple(accs)))
    for r in range(nq):
        o_ref[0, r * tq:(r + 1) * tq, :] = jnp.where(
            lane < SB_HEAD_DIM, accs[2 * r], accs[2 * r + 1]).astype(o_ref.dtype)


def _attention(q, k, v):
    B, S, W = q.shape
    tq = ATTN_SUBTILES * ATTN_TILE
    n_pairs = W // LANES
    return pl.pallas_call(
        _attn_kernel,
        grid=(B, n_pairs, S // tq),
        in_specs=[
            pl.BlockSpec((1, tq, LANES), lambda b, p, i: (b, i, p)),
            pl.BlockSpec((1, S, LANES), lambda b, p, i: (b, 0, p)),
            pl.BlockSpec((1, S, LANES), lambda b, p, i: (b, 0, p)),
        ],
        out_specs=pl.BlockSpec((1, tq, LANES), lambda b, p, i: (b, i, p)),
        out_shape=jax.ShapeDtypeStruct((B, S, W), BF16),
        compiler_params=pltpu.CompilerParams(
            dimension_semantics=("arbitrary", "arbitrary", "arbitrary"), vmem_limit_bytes=VMEM_LIMIT),
        name="stickbreak_attn",
    )(q, k, v)


def _out_proj_router_kernel(attn_ref, pool_ref, x_ref, wo_ref, g_ref, wr_ref, br_ref, tri_ref, cin_ref,
                            *rest, capacity):
    x1_ref, h2_ref, gate_ref, dest_ref, cnt_ref, carry_ref = rest[-6:]
    i = pl.program_id(0)
    tm = x_ref.shape[0]

    @pl.when(i == 0)
    def _():
        carry_ref[...] = cin_ref[...]

    mixed = jnp.dot(attn_ref[...], wo_ref[0:SB_WIDTH, :], preferred_element_type=F32)
    mixed = mixed + jnp.dot(pool_ref[...], wo_ref[SB_WIDTH:, :], preferred_element_type=F32)
    x1 = x_ref[...] + mixed
    h2 = _rms(x1, g_ref[...])
    x1_ref[...] = x1
    h2_ref[...] = _pack_rows(h2)

    hh = h2.astype(BF16)
    hl = (h2 - hh.astype(F32)).astype(BF16)
    wr = wr_ref[...]
    wh = wr.astype(BF16)
    wl = (wr - wh.astype(F32)).astype(BF16)
    logits = (lax.dot_general(wh, hh, _NT, preferred_element_type=F32)
              + lax.dot_general(wh, hl, _NT, preferred_element_type=F32)
              + lax.dot_general(wl, hh, _NT, preferred_element_type=F32)) + br_ref[...]

    eid = lax.broadcasted_iota(I32, (N_EXPERTS, tm), 0).astype(F32)
    work = logits
    vals, ids = [], []
    for _ in range(TOP_K):
        m = jnp.max(work, axis=0, keepdims=True)
        sel = jnp.min(jnp.where(work == m, eid, float(N_EXPERTS)), axis=0, keepdims=True)
        vals.append(m)
        ids.append(sel)
        work = jnp.where(eid == sel, -jnp.inf, work)
    exps = [jnp.exp(v - vals[0]) for v in vals]
    denom = exps[0] + exps[1] + exps[2] + exps[3]

    onehot = jnp.zeros((N_EXPERTS, tm), F32)
    for sel in ids:
        onehot = onehot + (eid == sel).astype(F32)
    before = jnp.dot(onehot.astype(BF16), tri_ref[...], preferred_element_type=F32) + carry_ref[...]
    for kk in range(TOP_K):
        gate_ref[kk:kk + 1, :] = exps[kk] / denom
        rk = jnp.sum(jnp.where(eid == ids[kk], before, 0.0), axis=0, keepdims=True)
        dest_ref[kk:kk + 1, :] = (ids[kk] * float(capacity) + rk).astype(I32)
    carry_ref[...] = carry_ref[...] + jnp.sum(onehot, axis=1, keepdims=True)
    cnt_ref[...] = jnp.broadcast_to(carry_ref[...], cnt_ref.shape)


def _out_proj_router(attn, pool, x, weights, capacity, part, counts_in, x1_buf):
    T, D = x.shape
    tm = PROJ_TILE
    steps = T // tm // ROUTER_PARTS
    base = part * steps
    tp = steps * tm
    w_out, g_moe, w_router, b_router, tri = weights
    in_blk = lambda w: pl.BlockSpec((tm, w), lambda i: (i + base, 0))
    fixed = lambda shape: pl.BlockSpec(shape, lambda i: tuple(0 for _ in shape))
    sel_blk = pl.BlockSpec((TOP_K, tm), lambda i: (0, i))
    assert N_EXPERTS * capacity < 2 ** 24
    in_specs = [in_blk(SB_WIDTH), in_blk(POOL_WIDTH), in_blk(D), fixed((D, D)), fixed((1, D)),
                fixed((N_EXPERTS, D)), fixed((N_EXPERTS, 1)), fixed((tm, tm)), fixed((N_EXPERTS, 1))]
    args = [attn, pool, x, w_out, g_moe, w_router, b_router, tri, counts_in]
    aliases = {}
    if x1_buf is not None:
        in_specs.append(pl.BlockSpec(memory_space=pl.ANY))
        args.append(x1_buf)
        aliases = {len(args) - 1: 0}
    return pl.pallas_call(
        functools.partial(_out_proj_router_kernel, capacity=capacity),
        grid=(steps,),
        in_specs=in_specs,
        out_specs=[in_blk(D), pl.BlockSpec((tm, D // 2), lambda i: (i, 0)), sel_blk, sel_blk,
                   fixed((N_EXPERTS, LANES))],
        out_shape=[jax.ShapeDtypeStruct((T, D), F32), jax.ShapeDtypeStruct((tp, D // 2), U32),
                   jax.ShapeDtypeStruct((TOP_K, tp), F32), jax.ShapeDtypeStruct((TOP_K, tp), I32),
                   jax.ShapeDtypeStruct((N_EXPERTS, LANES), F32)],
        scratch_shapes=[pltpu.VMEM((N_EXPERTS, 1), F32)],
        input_output_aliases=aliases,
        compiler_params=pltpu.CompilerParams(
            dimension_semantics=("arbitrary",), vmem_limit_bytes=VMEM_LIMIT),
        name="out_proj_router",
    )(*args)


SC_ROWS_PER_COPY = 64


def _sc_workers():
    info = plsc.get_sparse_core_info()
    return info.num_cores, info.num_cores * info.num_subcores


def _sc_dispatch(rows, dest, n_out, into=None):
    T, W = rows.shape
    K = dest.shape[0]
    sub = SC_ROWS_PER_COPY
    n_cores, n_workers = _sc_workers()
    per_w = T // n_workers
    n_chunks = per_w // sub
    assert per_w * n_workers == T and n_chunks * sub == per_w and n_chunks % 2 == 0
    idx = dest.reshape(K, n_workers, n_chunks, sub).transpose(1, 2, 0, 3).reshape(n_workers, n_chunks * K, sub)
    mesh = plsc.VectorSubcoreMesh(core_axis_name="core", subcore_axis_name="subcore")

    scratch = [pltpu.VMEM((n_chunks * K, sub), I32), pltpu.VMEM((2, sub, W), rows.dtype),
               pltpu.SemaphoreType.DMA((2,)), pltpu.SemaphoreType.DMA((2,))]

    def scatter_rows(x_hbm, i_hbm, o_hbm, idx_v, buf, rsem, wsem):
        wid = lax.axis_index("subcore") * n_cores + lax.axis_index("core")
        base = wid * per_w
        pltpu.sync_copy(i_hbm.at[wid], idx_v)

        def read(c, slot):
            return pltpu.make_async_copy(x_hbm.at[pl.ds(base + c * sub, sub)], buf.at[slot], rsem.at[slot])

        def write(c, kk, slot):
            return pltpu.make_async_copy(buf.at[slot], o_hbm.at[idx_v.at[c * K + kk]], wsem.at[slot])

        read(0, 0).start()

        @pl.loop(0, n_chunks, step=2)
        def _(c0):
            for b in range(2):
                c = c0 + b
                read(c, b).wait()
                for kk in range(K):
                    write(c, kk, b).start()

                @pl.when(c + 1 < n_chunks)
                def _():
                    @pl.when(c >= 1)
                    def _():
                        for kk in range(K):
                            write(c - 1, kk, 1 - b).wait()
                    read(c + 1, 1 - b).start()

        for kk in range(K):
            write(n_chunks - 2, kk, 0).wait()
            write(n_chunks - 1, kk, 1).wait()

    if into is None:
        out_type = jax.ShapeDtypeStruct((n_out, W), rows.dtype)
        return pl.kernel(scatter_rows, out_type=out_type, mesh=mesh, scratch_types=scratch)(rows, idx)
    assert into.shape == (n_out, W) and into.dtype == rows.dtype
    out_ref = jax.new_ref(into)
    pl.kernel(scatter_rows, out_type=(), mesh=mesh, scratch_types=scratch)(rows, idx, out_ref)
    return jax.freeze(out_ref)


def _sc_gather(table, indices):
    M = indices.shape[0]
    W = table.shape[1]
    sub = SC_ROWS_PER_COPY
    n_cores, n_workers = _sc_workers()
    per_w = M // n_workers
    n_steps = per_w // sub
    assert per_w * n_workers == M and n_steps * sub == per_w and n_steps % 2 == 0
    mesh = plsc.VectorSubcoreMesh(core_axis_name="core", subcore_axis_name="subcore")

    @functools.partial(
        pl.kernel, out_type=jax.ShapeDtypeStruct((M, W), table.dtype), mesh=mesh,
        scratch_types=[pltpu.VMEM((n_steps, sub), I32), pltpu.VMEM((2, sub, W), table.dtype),
                       pltpu.SemaphoreType.DMA((2,)), pltpu.SemaphoreType.DMA((2,))])
    def gather_rows(x_hbm, i_hbm, o_hbm, idx_v, buf, gsem, wsem):
        wid = lax.axis_index("subcore") * n_cores + lax.axis_index("core")
        base = wid * per_w
        pltpu.sync_copy(i_hbm.at[wid], idx_v)

        def gather(s, slot):
            return pltpu.make_async_copy(x_hbm.at[idx_v.at[s]], buf.at[slot], gsem.at[slot])

        def write(s, slot):
            return pltpu.make_async_copy(buf.at[slot], o_hbm.at[pl.ds(base + s * sub, sub)], wsem.at[slot])

        gather(0, 0).start()

        @pl.loop(0, n_steps, step=2)
        def _(s0):
            for b in range(2):
                s = s0 + b
                gather(s, b).wait()
                write(s, b).start()

                @pl.when(s + 1 < n_steps)
                def _():
                    @pl.when(s >= 1)
                    def _():
                        write(s - 1, 1 - b).wait()
                    gather(s + 1, 1 - b).start()

        write(n_steps - 2, 0).wait()
        write(n_steps - 1, 1).wait()

    return gather_rows(table, indices.reshape(n_workers, n_steps, sub))


def _expert_kernel(vb_ref, ve_ref, vrows_ref, nvis_ref, nxt_ref,
                   x_ref, wgu_hbm, bgu_ref, wd_hbm, bd_ref, y_ref,
                   wgu_stage, wd_stage, wgu_bf, wd_bf, sem):
    v = pl.program_id(0)
    real = v < nvis_ref[0]
    e = ve_ref[v]
    first_of_expert = jnp.logical_or(v == 0, e != ve_ref[jnp.maximum(v - 1, 0)])

    def weight_copies(expert):
        return (pltpu.make_async_copy(wgu_hbm.at[expert], wgu_stage, sem.at[0]),
                pltpu.make_async_copy(wd_hbm.at[expert], wd_stage, sem.at[1]))

    @pl.when(v == 0)
    def _():
        for c in weight_copies(e):
            c.start()

    def load_weights():
        for c in weight_copies(e):
            c.wait()
        wgu_bf[...] = wgu_stage[...].astype(BF16)
        wd_bf[...] = wd_stage[...].astype(BF16)

    def fetch_next():
        nxt = nxt_ref[e]

        @pl.when(nxt >= 0)
        def _():
            for c in weight_copies(nxt):
                c.start()

    def run(n_rows):
        x_lo, x_hi = _unpack_rows(x_ref[0:n_rows, :])
        xb = jnp.concatenate([x_lo.astype(BF16), x_hi.astype(BF16)], axis=1)
        gu = jnp.dot(xb, wgu_bf[...], preferred_element_type=F32) + bgu_ref[0]
        gate = jnp.minimum(gu[:, :D_EXPERT], SWIGLU_LIMIT)
        up = jnp.clip(gu[:, D_EXPERT:], -SWIGLU_LIMIT, SWIGLU_LIMIT)
        act = gate * jax.nn.sigmoid(SWIGLU_ALPHA * gate) * (up + 1.0)
        y = jnp.dot(act.astype(BF16), wd_bf[...], preferred_element_type=F32) + bd_ref[0]
        y_ref[0:n_rows, :] = _pack_rows(y)

    rows = vrows_ref[v]
    full = EXPERT_ROW_CLASSES[0]
    first = jnp.logical_and(first_of_expert, real)

    @pl.when(jnp.logical_and(first, rows == full))
    def _():
        load_weights()
        run(full)
        fetch_next()

    @pl.when(jnp.logical_and(first, rows != full))
    def _():
        load_weights()
        fetch_next()

    later_full = jnp.logical_and(real, jnp.logical_not(first_of_expert))
    pl.when(jnp.logical_and(later_full, rows == full))(functools.partial(run, full))
    for n_rows in EXPERT_ROW_CLASSES[1:]:
        pl.when(jnp.logical_and(real, rows == n_rows))(functools.partial(run, n_rows))


def _experts(x_sorted, visits, w_gate_up, b_gate_up, w_down, b_down):
    n_rows, W = x_sorted.shape
    D = 2 * W
    tm = EXPERT_TILE
    n_visits = visits[0].shape[0]
    n_prefetch = len(visits)
    by_block = lambda v, vb, *_: (vb[v], 0)
    by_expert = lambda v, vb, ve, *_: (ve[v], 0, 0)
    grid_spec = pltpu.PrefetchScalarGridSpec(
        num_scalar_prefetch=n_prefetch,
        grid=(n_visits,),
        in_specs=[
            pl.BlockSpec((tm, W), by_block),
            pl.BlockSpec(memory_space=pl.ANY),
            pl.BlockSpec((1, 1, 2 * D_EXPERT), by_expert),
            pl.BlockSpec(memory_space=pl.ANY),
            pl.BlockSpec((1, 1, D), by_expert),
        ],
        out_specs=pl.BlockSpec((tm, W), by_block),
        scratch_shapes=[pltpu.VMEM((D, 2 * D_EXPERT), F32), pltpu.VMEM((D_EXPERT, D), F32),
                        pltpu.VMEM((D, 2 * D_EXPERT), BF16), pltpu.VMEM((D_EXPERT, D), BF16),
                        pltpu.SemaphoreType.DMA((2,))],
    )
    return pl.pallas_call(
        _expert_kernel,
        grid_spec=grid_spec,
        out_shape=jax.ShapeDtypeStruct((n_rows, W), U32),
        compiler_params=pltpu.CompilerParams(
            dimension_semantics=("arbitrary",), vmem_limit_bytes=VMEM_LIMIT),
        name="expert_gmm",
    )(*visits, x_sorted, w_gate_up, b_gate_up.reshape(N_EXPERTS, 1, -1), w_down, b_down.reshape(N_EXPERTS, 1, -1))


def _visit_schedule(counts, n_assign, capacity):
    tm = EXPERT_TILE
    assert EXPERT_ROW_CLASSES[0] == tm and list(EXPERT_ROW_CLASSES) == sorted(EXPERT_ROW_CLASSES, reverse=True)
    n_visits = n_assign // tm + N_EXPERTS
    n_full = counts // tm
    tail = counts - n_full * tm
    tail_rows = jnp.zeros_like(tail)
    for rows in EXPERT_ROW_CLASSES:
        tail_rows = jnp.where(tail <= rows, rows, tail_rows)
    per_e = n_full + (tail > 0)
    vend = jnp.cumsum(per_e)
    nvis = vend[-1]
    vc = jnp.minimum(jnp.arange(n_visits, dtype=I32), nvis - 1)
    done = vc[:, None] >= vend[None, :]
    e = jnp.sum(done, axis=1).astype(I32)
    local = vc - jnp.sum(jnp.where(done, per_e[None, :], 0), axis=1)
    mine = e[:, None] == jnp.arange(N_EXPERTS, dtype=I32)[None, :]
    is_tail = local >= jnp.sum(jnp.where(mine, n_full[None, :], 0), axis=1)
    vrows = jnp.where(is_tail, jnp.sum(jnp.where(mine, tail_rows[None, :], 0), axis=1), tm)
    blk = e * (capacity // tm) + local
    ids = jnp.arange(N_EXPERTS, dtype=I32)
    nonempty = counts > 0
    later = jnp.logical_and(nonempty[None, :], ids[None, :] > ids[:, None])
    nxt = jnp.where(jnp.any(later, axis=1), jnp.argmax(later, axis=1), -1).astype(I32)
    return blk.astype(I32), e, vrows.astype(I32), nvis.reshape(1).astype(I32), nxt


def _combine_kernel(y0_ref, y1_ref, y2_ref, y3_ref, x1_ref, gate_ref, g_ref, o_ref):
    half = x1_ref.shape[1] // 2
    gates = gate_ref[...]
    lo = x1_ref[:, :half]
    hi = x1_ref[:, half:]
    for kk, y_ref in enumerate((y0_ref, y1_ref, y2_ref, y3_ref)):
        y_lo, y_hi = _unpack_rows(y_ref[...])
        lo = lo + gates[:, kk:kk + 1] * y_lo
        hi = hi + gates[:, kk:kk + 1] * y_hi
    ms = (jnp.sum(lo * lo, axis=-1, keepdims=True) + jnp.sum(hi * hi, axis=-1, keepdims=True)) / (2 * half)
    scale = lax.rsqrt(ms + RMS_EPS)
    o_ref[:, :half] = lo * scale * g_ref[:, :half]
    o_ref[:, half:] = hi * scale * g_ref[:, half:]


def _combine(y_tok, x1, gates_t, g_final):
    T, D = x1.shape
    tm = COMBINE_TILE
    nt = T // tm
    y_spec = lambda kk: pl.BlockSpec((tm, D // 2), lambda i: (kk * nt + i, 0))
    return pl.pallas_call(
        _combine_kernel,
        grid=(nt,),
        in_specs=[y_spec(0), y_spec(1), y_spec(2), y_spec(3),
                  pl.BlockSpec((tm, D), lambda i: (i, 0)),
                  pl.BlockSpec((tm, TOP_K), lambda i: (i, 0)),
                  pl.BlockSpec((1, D), lambda i: (0, 0))],
        out_specs=pl.BlockSpec((tm, D), lambda i: (i, 0)),
        out_shape=jax.ShapeDtypeStruct((T, D), F32),
        compiler_params=pltpu.CompilerParams(
            dimension_semantics=("arbitrary",), vmem_limit_bytes=VMEM_LIMIT),
        name="combine_norm",
    )(y_tok, y_tok, y_tok, y_tok, x1, gates_t, g_final.reshape(1, D))


def kernel(x, g_mix, w_in, pool_w, pool_scale, w_out, g_moe, w_router, b_router, w_gate_up, b_gate_up,
           w_down, b_down, g_final):
    B, S, D = x.shape
    T = B * S
    assert g_mix.shape[0] == 1, "single-layer problem: the final norm is fused into the combine step"
    q, k, v, pool = _in_proj(x, g_mix[0], w_in[0], pool_w[0], pool_scale[0])
    attn = _attention(q, k, v)
    capacity = T
    tm = PROJ_TILE
    tri = (lax.broadcasted_iota(I32, (tm, tm), 0) < lax.broadcasted_iota(I32, (tm, tm), 1)).astype(BF16)
    weights = (w_out[0].astype(BF16), g_moe[0].reshape(1, D), w_router[0].T, b_router[0].reshape(N_EXPERTS, 1), tri)
    x1 = x_sorted = None
    counts = jnp.zeros((N_EXPERTS, 1), F32)
    gates, dest = [], []
    for part in range(ROUTER_PARTS):
        x1, h2, gates_p, dest_p, cnt = _out_proj_router(
            attn.reshape(T, SB_WIDTH), pool.reshape(T, POOL_WIDTH), x.reshape(T, D), weights, capacity,
            part, counts, x1)
        x_sorted = _sc_dispatch(h2, dest_p, N_EXPERTS * capacity, x_sorted)
        counts = cnt[:, :1]
        gates.append(gates_p)
        dest.append(dest_p)
    gates = jnp.concatenate(gates, axis=1)
    dest = jnp.concatenate(dest, axis=1)
    visits = _visit_schedule(cnt[:, 0].astype(I32), T * TOP_K, capacity)
    y = _experts(x_sorted, visits, w_gate_up[0], b_gate_up[0], w_down[0], b_down[0])
    y_tok = _sc_gather(y, dest.reshape(-1))
    return _combine(y_tok, x1, gates.T, g_final).reshape(B, S, D)
```

```python
import functools

import jax
import jax.numpy as jnp
from jax import lax
from jax.experimental import pallas as pl
from jax.experimental.pallas import tpu as pltpu
from jax.experimental.pallas import tpu_sc as plsc

F32 = jnp.float32
BF16 = jnp.bfloat16
I32 = jnp.int32
U32 = jnp.uint32

D_MODEL = 1024
SB_HEADS = 8
SB_HEAD_DIM = 64
SB_WIDTH = SB_HEADS * SB_HEAD_DIM
POOL_WINDOWS = (2, 4, 8, 16)
POOL_WIDTH = 512
POOL_GROUP_DIM = 128
N_EXPERTS = 32
TOP_K = 4
D_EXPERT = 1024
SWIGLU_LIMIT = 7.0
SWIGLU_ALPHA = 1.702
RMS_EPS = 1e-5

LANES = 128
HALO = 32
PROJ_TILE = 1024
ROUTER_PARTS = 4
ATTN_TILE = 256
ATTN_SUBTILES = 4
EXPERT_TILE = 1024
EXPERT_ROW_CLASSES = (1024, 512, 256)
COMBINE_TILE = 1024
ATTN_SKIP_LOG2 = -160.0
LOG2_E = 1.4426950408889634
V7X_VMEM_BYTES = 64 * 1024 * 1024
VMEM_LIMIT = V7X_VMEM_BYTES * 7 // 8

_NT = (((1,), (1,)), ((), ()))


def _rms(x, g):
    ms = jnp.mean(x * x, axis=-1, keepdims=True)
    return x * lax.rsqrt(ms + RMS_EPS) * g


def _pack_rows(x):
    n = x.shape[1] // 2
    lo = lax.bitcast_convert_type(x[:, :n].astype(BF16).astype(F32), U32)
    hi = lax.bitcast_convert_type(x[:, n:].astype(BF16).astype(F32), U32)
    return (lo >> 16) | (hi & jnp.uint32(0xFFFF0000))


def _unpack_rows(w):
    lo = lax.bitcast_convert_type(w << 16, F32)
    hi = lax.bitcast_convert_type(w & jnp.uint32(0xFFFF0000), F32)
    return lo, hi


def _in_proj_kernel(x_ref, w_ref, gain_ref, pw_ref, ps_ref, q_ref, k_ref, v_ref, p_ref, uext_ref, xb_ref, lvl_ref, unew_ref,
                    wbf_ref, *, n_seq):
    i = pl.program_id(0)
    n_tiles = pl.num_programs(0) - 1
    tm = x_ref.shape[0]
    lo = HALO // 2

    @pl.when(i == 0)
    def _():
        uext_ref[...] = jnp.zeros_like(uext_ref)
        wbf_ref[...] = (w_ref[...] * gain_ref[...]).astype(BF16)
        lvl_ref[:, 0:lo, :] = jnp.zeros((2, lo, POOL_GROUP_DIM), F32)

    t = lax.rem(i + n_seq - 1, n_seq) * tm + lax.broadcasted_iota(I32, (tm, 1), 0)

    def pool_group(g):
        w = POOL_WINDOWS[g]
        sl = slice(g * POOL_GROUP_DIM, (g + 1) * POOL_GROUP_DIM)
        lvl_ref[0, lo:, :] = uext_ref[lo:, sl] + uext_ref[lo - 1:HALO + tm - 1, sl]
        cur, k = 0, 2
        while k < w:
            lvl_ref[1 - cur, lo:, :] = lvl_ref[cur, lo:, :] + lvl_ref[cur, lo - k:HALO + tm - k, :]
            cur, k = 1 - cur, 2 * k
        count = jnp.minimum(t + 1, w).astype(F32)
        pooled = lvl_ref[cur, HALO:, :] / count - uext_ref[HALO:, sl]
        mixed = jnp.dot(pooled.astype(BF16), pw_ref[g].astype(BF16), preferred_element_type=F32)
        p_ref[:, sl] = (mixed * ps_ref[:, sl]).astype(BF16)

    def shift_history():
        uext_ref[0:HALO, :] = jnp.where(lax.rem(i, n_seq) == 0, 0.0, uext_ref[tm:tm + HALO, :])

    @pl.when(i < n_tiles)
    def _():
        x = x_ref[...]
        inv = lax.rsqrt(jnp.mean(x * x, axis=-1, keepdims=True) + RMS_EPS)
        xb_ref[...] = x.astype(BF16)
        scales = (inv * (LOG2_E * SB_HEAD_DIM ** -0.5), inv, inv)
        for g, (o_ref, scale) in enumerate(zip((q_ref, k_ref, v_ref), scales)):
            cols = slice(g * SB_WIDTH, (g + 1) * SB_WIDTH)
            o_ref[...] = (jnp.dot(xb_ref[...], wbf_ref[:, cols], preferred_element_type=F32) * scale).astype(BF16)
            pool_group(g)
        unew_ref[...] = jnp.dot(xb_ref[...], wbf_ref[:, 3 * SB_WIDTH:], preferred_element_type=F32) * inv
        pool_group(3)
        shift_history()
        uext_ref[HALO:, :] = unew_ref[...]

    @pl.when(i == n_tiles)
    def _():
        for g in range(len(POOL_WINDOWS)):
            pool_group(g)


def _in_proj(x, g_mix, w_in, pool_w, pool_scale):
    B, S, D = x.shape
    tm = PROJ_TILE
    n_out = w_in.shape[1]
    n_tiles = B * S // tm
    out_sd = jax.ShapeDtypeStruct((B * S, SB_WIDTH), BF16)
    cur = lambda i: (jnp.minimum(i, n_tiles - 1), 0)
    prev = lambda i: (jnp.maximum(i - 1, 0), 0)
    blk = pl.BlockSpec((tm, SB_WIDTH), cur)
    q, k, v, p = pl.pallas_call(
        functools.partial(_in_proj_kernel, n_seq=S // tm),
        grid=(n_tiles + 1,),
        in_specs=[
            pl.BlockSpec((tm, D), cur),
            pl.BlockSpec((D, n_out), lambda i: (0, 0)),
            pl.BlockSpec((D, 1), lambda i: (0, 0)),
            pl.BlockSpec(pool_w.shape, lambda i: (0, 0, 0)),
            pl.BlockSpec((1, POOL_WIDTH), lambda i: (0, 0)),
        ],
        out_specs=[blk, blk, blk, pl.BlockSpec((tm, POOL_WIDTH), prev)],
        out_shape=[out_sd, out_sd, out_sd, out_sd],
        scratch_shapes=[pltpu.VMEM((HALO + tm, POOL_WIDTH), F32), pltpu.VMEM((tm, D), BF16),
                        pltpu.VMEM((2, HALO + tm, POOL_GROUP_DIM), F32), pltpu.VMEM((tm, POOL_WIDTH), F32),
                        pltpu.VMEM((D, n_out), BF16)],
        compiler_params=pltpu.CompilerParams(
            dimension_semantics=("arbitrary",), vmem_limit_bytes=VMEM_LIMIT),
        name="in_proj_pool",
    )(x.reshape(B * S, D), w_in, g_mix.reshape(D, 1), pool_w,
      pool_scale.reshape(1, POOL_WIDTH))
    return (q.reshape(B, S, SB_WIDTH), k.reshape(B, S, SB_WIDTH), v.reshape(B, S, SB_WIDTH),
            p.reshape(B, S, POOL_WIDTH))


def _attn_kernel(q_ref, k_ref, v_ref, o_ref):
    tq = ATTN_TILE
    tk = tq
    qi = pl.program_id(2)
    lane = lax.broadcasted_iota(I32, (tq, LANES), 1)
    row = lax.broadcasted_iota(I32, (tq, tk), 0)
    col = lax.broadcasted_iota(I32, (tq, tk), 1)
    causal = col < row
    tri = (row > col).astype(BF16)

    def head_rows(r, h):
        rows = q_ref[0, r * tq:(r + 1) * tq, :]
        return jnp.where((lane >= h * SB_HEAD_DIM) & (lane < (h + 1) * SB_HEAD_DIM), rows, jnp.zeros_like(rows))

    def scores(qh, j):
        kb = k_ref[0, pl.ds(pl.multiple_of(j * tk, tk), tk), :]
        return lax.dot_general(qh, kb, _NT, preferred_element_type=F32)

    def values(j):
        return v_ref[0, pl.ds(pl.multiple_of(j * tk, tk), tk), :]

    def log_terms(z, mask):
        neg_abs = lax.bitcast_convert_type(lax.bitcast_convert_type(z, U32) | jnp.uint32(0x80000000), F32)
        nl = jnp.maximum(z, 0.0) + jnp.log2(1.0 + jnp.exp2(neg_abs))
        if mask:
            nl = jnp.where(causal, nl, 0.0)
        return nl.astype(BF16), z - nl, nl[:, 0:1]

    def tile(qh, j, mask):
        nl, lb, nl0 = log_terms(scores(qh, j), mask)
        ntail = jnp.dot(nl, tri, preferred_element_type=F32)
        return lb - ntail, -(ntail[:, 0:1] + nl0), values(j)

    nq = ATTN_SUBTILES
    has_prev = qi > 0
    qs = [head_rows(r, h) for r in range(nq) for h in range(2)]
    chains = []
    for n, qh in enumerate(qs):
        diag = nq * qi + n // 2
        chains += [(qh, diag, True), (qh, jnp.maximum(diag - 1, 0), False)]
    zs = [scores(qh, j) for qh, j, _ in chains]
    terms = [log_terms(z, is_diag) for z, (_, _, is_diag) in zip(zs, chains)]
    nts = [jnp.dot(nl, tri, preferred_element_type=F32) for nl, _, _ in terms]
    lbs = [lb for _, lb, _ in terms]
    tots = [-(nt[:, 0:1] + nl0) for nt, (_, _, nl0) in zip(nts, terms)]
    ws = []
    for n in range(2 * nq):
        d, p = 2 * n, 2 * n + 1
        w_d = jnp.where(causal, jnp.exp2(lbs[d] - nts[d]), 0.0)
        w_p = jnp.exp2(lbs[p] - nts[p] + tots[d])
        if n < 2:
            w_p = jnp.where(has_prev, w_p, 0.0)
        ws += [w_d.astype(BF16), w_p.astype(BF16)]
    pvs = [jnp.dot(w, values(j), preferred_element_type=F32) for w, (_, j, _) in zip(ws, chains)]
    accs = [pvs[2 * n] + pvs[2 * n + 1] for n in range(2 * nq)]
    carries = [tots[2 * n] + tots[2 * n + 1] for n in range(2 * nq)]

    def cond(st):
        j, cs, _ = st
        live = functools.reduce(jnp.maximum, [jnp.max(c) for c in cs])
        return jnp.logical_and(j >= 0, live > ATTN_SKIP_LOG2)

    def body(st):
        j, cs, acs = st
        new_c, new_a = [], []
        for n, (qh, c, a) in enumerate(zip(qs, cs, acs)):
            lag = nq - 1 - n // 2
            if lag:
                c = jnp.where(j >= lag, c, -jnp.inf)
            lw, tot, vb = tile(qh, jnp.maximum(j - lag, 0), False)
            new_a.append(a + jnp.dot(jnp.exp2(lw + c).astype(BF16), vb, preferred_element_type=F32))
            new_c.append(c + tot)
        return j - 1, tuple(new_c), tuple(new_a)

    _, _, accs = lax.while_loop(cond, body, (nq * qi + nq - 3, tuple(carries), tuple(accs)))
    for r in range(nq):
        o_ref[0, r * tq:(r + 1) * tq, :] = jnp.where(
            lane < SB_HEAD_DIM, accs[2 * r], accs[2 * r + 1]).astype(o_ref.dtype)


def _attention(q, k, v):
    B, S, W = q.shape
    tq = ATTN_SUBTILES * ATTN_TILE
    n_pairs = W // LANES
    return pl.pallas_call(
        _attn_kernel,
        grid=(B, n_pairs, S // tq),
        in_specs=[
            pl.BlockSpec((1, tq, LANES), lambda b, p, i: (b, i, p)),
            pl.BlockSpec((1, S, LANES), lambda b, p, i: (b, 0, p)),
            pl.BlockSpec((1, S, LANES), lambda b, p, i: (b, 0, p)),
        ],
        out_specs=pl.BlockSpec((1, tq, LANES), lambda b, p, i: (b, i, p)),
        out_shape=jax.ShapeDtypeStruct((B, S, W), BF16),
        compiler_params=pltpu.CompilerParams(
            dimension_semantics=("arbitrary", "arbitrary", "arbitrary"), vmem_limit_bytes=VMEM_LIMIT),
        name="stickbreak_attn",
    )(q, k, v)


def _out_proj_router_kernel(attn_ref, pool_ref, x_ref, wo_ref, g_ref, wr_ref, br_ref, tri_ref, cin_ref,
                            *rest, capacity):
    x1_ref, h2_ref, gate_ref, dest_ref, cnt_ref, carry_ref = rest[-6:]
    i = pl.program_id(0)
    tm = x_ref.shape[0]

    @pl.when(i == 0)
    def _():
        carry_ref[...] = cin_ref[...]

    mixed = jnp.dot(attn_ref[...], wo_ref[0:SB_WIDTH, :], preferred_element_type=F32)
    mixed = mixed + jnp.dot(pool_ref[...], wo_ref[SB_WIDTH:, :], preferred_element_type=F32)
    x1 = x_ref[...] + mixed
    h2 = _rms(x1, g_ref[...])
    x1_ref[...] = x1
    h2_ref[...] = _pack_rows(h2)

    hh = h2.astype(BF16)
    hl = (h2 - hh.astype(F32)).astype(BF16)
    wr = wr_ref[...]
    wh = wr.astype(BF16)
    wl = (wr - wh.astype(F32)).astype(BF16)
    logits = (lax.dot_general(wh, hh, _NT, preferred_element_type=F32)
              + lax.dot_general(wh, hl, _NT, preferred_element_type=F32)
              + lax.dot_general(wl, hh, _NT, preferred_element_type=F32)) + br_ref[...]

    eid = lax.broadcasted_iota(I32, (N_EXPERTS, tm), 0).astype(F32)
    work = logits
    vals, ids = [], []
    for _ in range(TOP_K):
        m = jnp.max(work, axis=0, keepdims=True)
        sel = jnp.min(jnp.where(work == m, eid, float(N_EXPERTS)), axis=0, keepdims=True)
        vals.append(m)
        ids.append(sel)
        work = jnp.where(eid == sel, -jnp.inf, work)
    exps = [jnp.exp(v - vals[0]) for v in vals]
    denom = exps[0] + exps[1] + exps[2] + exps[3]

    onehot = jnp.zeros((N_EXPERTS, tm), F32)
    for sel in ids:
        onehot = onehot + (eid == sel).astype(F32)
    before = jnp.dot(onehot.astype(BF16), tri_ref[...], preferred_element_type=F32) + carry_ref[...]
    for kk in range(TOP_K):
        gate_ref[kk:kk + 1, :] = exps[kk] / denom
        rk = jnp.sum(jnp.where(eid == ids[kk], before, 0.0), axis=0, keepdims=True)
        dest_ref[kk:kk + 1, :] = (ids[kk] * float(capacity) + rk).astype(I32)
    carry_ref[...] = carry_ref[...] + jnp.sum(onehot, axis=1, keepdims=True)
    cnt_ref[...] = jnp.broadcast_to(carry_ref[...], cnt_ref.shape)


def _out_proj_router(attn, pool, x, weights, capacity, part, counts_in, x1_buf):
    T, D = x.shape
    tm = PROJ_TILE
    steps = T // tm // ROUTER_PARTS
    base = part * steps
    tp = steps * tm
    w_out, g_moe, w_router, b_router, tri = weights
    in_blk = lambda w: pl.BlockSpec((tm, w), lambda i: (i + base, 0))
    fixed = lambda shape: pl.BlockSpec(shape, lambda i: tuple(0 for _ in shape))
    sel_blk = pl.BlockSpec((TOP_K, tm), lambda i: (0, i))
    assert N_EXPERTS * capacity < 2 ** 24
    in_specs = [in_blk(SB_WIDTH), in_blk(POOL_WIDTH), in_blk(D), fixed((D, D)), fixed((1, D)),
                fixed((N_EXPERTS, D)), fixed((N_EXPERTS, 1)), fixed((tm, tm)), fixed((N_EXPERTS, 1))]
    args = [attn, pool, x, w_out, g_moe, w_router, b_router, tri, counts_in]
    aliases = {}
    if x1_buf is not None:
        in_specs.append(pl.BlockSpec(memory_space=pl.ANY))
        args.append(x1_buf)
        aliases = {len(args) - 1: 0}
    return pl.pallas_call(
        functools.partial(_out_proj_router_kernel, capacity=capacity),
        grid=(steps,),
        in_specs=in_specs,
        out_specs=[in_blk(D), pl.BlockSpec((tm, D // 2), lambda i: (i, 0)), sel_blk, sel_blk,
                   fixed((N_EXPERTS, LANES))],
        out_shape=[jax.ShapeDtypeStruct((T, D), F32), jax.ShapeDtypeStruct((tp, D // 2), U32),
                   jax.ShapeDtypeStruct((TOP_K, tp), F32), jax.ShapeDtypeStruct((TOP_K, tp), I32),
                   jax.ShapeDtypeStruct((N_EXPERTS, LANES), F32)],
        scratch_shapes=[pltpu.VMEM((N_EXPERTS, 1), F32)],
        input_output_aliases=aliases,
        compiler_params=pltpu.CompilerParams(
            dimension_semantics=("arbitrary",), vmem_limit_bytes=VMEM_LIMIT),
        name="out_proj_router",
    )(*args)


SC_ROWS_PER_COPY = 64


def _sc_workers():
    info = plsc.get_sparse_core_info()
    return info.num_cores, info.num_cores * info.num_subcores


def _sc_dispatch(rows, dest, n_out, into=None):
    T, W = rows.shape
    K = dest.shape[0]
    sub = SC_ROWS_PER_COPY
    n_cores, n_workers = _sc_workers()
    per_w = T // n_workers
    n_chunks = per_w // sub
    assert per_w * n_workers == T and n_chunks * sub == per_w and n_chunks % 2 == 0
    idx = dest.reshape(K, n_workers, n_chunks, sub).transpose(1, 2, 0, 3).reshape(n_workers, n_chunks * K, sub)
    mesh = plsc.VectorSubcoreMesh(core_axis_name="core", subcore_axis_name="subcore")

    scratch = [pltpu.VMEM((n_chunks * K, sub), I32), pltpu.VMEM((2, sub, W), rows.dtype),
               pltpu.SemaphoreType.DMA((2,)), pltpu.SemaphoreType.DMA((2,))]

    def scatter_rows(x_hbm, i_hbm, o_hbm, idx_v, buf, rsem, wsem):
        wid = lax.axis_index("subcore") * n_cores + lax.axis_index("core")
        base = wid * per_w
        pltpu.sync_copy(i_hbm.at[wid], idx_v)

        def read(c, slot):
            return pltpu.make_async_copy(x_hbm.at[pl.ds(base + c * sub, sub)], buf.at[slot], rsem.at[slot])

        def write(c, kk, slot):
            return pltpu.make_async_copy(buf.at[slot], o_hbm.at[idx_v.at[c * K + kk]], wsem.at[slot])

        read(0, 0).start()

        @pl.loop(0, n_chunks, step=2)
        def _(c0):
            for b in range(2):
                c = c0 + b
                read(c, b).wait()
                for kk in range(K):
                    write(c, kk, b).start()

                @pl.when(c + 1 < n_chunks)
                def _():
                    @pl.when(c >= 1)
                    def _():
                        for kk in range(K):
                            write(c - 1, kk, 1 - b).wait()
                    read(c + 1, 1 - b).start()

        for kk in range(K):
            write(n_chunks - 2, kk, 0).wait()
            write(n_chunks - 1, kk, 1).wait()

    if into is None:
        out_type = jax.ShapeDtypeStruct((n_out, W), rows.dtype)
        return pl.kernel(scatter_rows, out_type=out_type, mesh=mesh, scratch_types=scratch)(rows, idx)
    assert into.shape == (n_out, W) and into.dtype == rows.dtype
    out_ref = jax.new_ref(into)
    pl.kernel(scatter_rows, out_type=(), mesh=mesh, scratch_types=scratch)(rows, idx, out_ref)
    return jax.freeze(out_ref)


def _sc_gather(table, indices):
    M = indices.shape[0]
    W = table.shape[1]
    sub = SC_ROWS_PER_COPY
    n_cores, n_workers = _sc_workers()
    per_w = M // n_workers
    n_steps = per_w // sub
    assert per_w * n_workers == M and n_steps * sub == per_w and n_steps % 2 == 0
    mesh = plsc.VectorSubcoreMesh(core_axis_name="core", subcore_axis_name="subcore")

    @functools.partial(
        pl.kernel, out_type=jax.ShapeDtypeStruct((M, W), table.dtype), mesh=mesh,
        scratch_types=[pltpu.VMEM((n_steps, sub), I32), pltpu.VMEM((2, sub, W), table.dtype),
                       pltpu.SemaphoreType.DMA((2,)), pltpu.SemaphoreType.DMA((2,))])
    def gather_rows(x_hbm, i_hbm, o_hbm, idx_v, buf, gsem, wsem):
        wid = lax.axis_index("subcore") * n_cores + lax.axis_index("core")
        base = wid * per_w
        pltpu.sync_copy(i_hbm.at[wid], idx_v)

        def gather(s, slot):
            return pltpu.make_async_copy(x_hbm.at[idx_v.at[s]], buf.at[slot], gsem.at[slot])

        def write(s, slot):
            return pltpu.make_async_copy(buf.at[slot], o_hbm.at[pl.ds(base + s * sub, sub)], wsem.at[slot])

        gather(0, 0).start()

        @pl.loop(0, n_steps, step=2)
        def _(s0):
            for b in range(2):
                s = s0 + b
                gather(s, b).wait()
                write(s, b).start()

                @pl.when(s + 1 < n_steps)
                def _():
                    @pl.when(s >= 1)
                    def _():
                        write(s - 1, 1 - b).wait()
                    gather(s + 1, 1 - b).start()

        write(n_steps - 2, 0).wait()
        write(n_steps - 1, 1).wait()

    return gather_rows(table, indices.reshape(n_workers, n_steps, sub))


def _expert_kernel(vb_ref, ve_ref, vrows_ref, nvis_ref, nxt_ref,
                   x_ref, wgu_hbm, bgu_ref, wd_hbm, bd_ref, y_ref,
                   wgu_stage, wd_stage, wgu_bf, wd_bf, sem):
    v = pl.program_id(0)
    real = v < nvis_ref[0]
    e = ve_ref[v]
    first_of_expert = jnp.logical_or(v == 0, e != ve_ref[jnp.maximum(v - 1, 0)])

    def weight_copies(expert):
        return (pltpu.make_async_copy(wgu_hbm.at[expert], wgu_stage, sem.at[0]),
                pltpu.make_async_copy(wd_hbm.at[expert], wd_stage, sem.at[1]))

    @pl.when(v == 0)
    def _():
        for c in weight_copies(e):
            c.start()

    def load_weights():
        for c in weight_copies(e):
            c.wait()
        wgu_bf[...] = wgu_stage[...].astype(BF16)
        wd_bf[...] = wd_stage[...].astype(BF16)

    def fetch_next():
        nxt = nxt_ref[e]

        @pl.when(nxt >= 0)
        def _():
            for c in weight_copies(nxt):
                c.start()

    def run(n_rows):
        x_lo, x_hi = _unpack_rows(x_ref[0:n_rows, :])
        xb = jnp.concatenate([x_lo.astype(BF16), x_hi.astype(BF16)], axis=1)
        gu = jnp.dot(xb, wgu_bf[...], preferred_element_type=F32) + bgu_ref[0]
        gate = jnp.minimum(gu[:, :D_EXPERT], SWIGLU_LIMIT)
        up = jnp.clip(gu[:, D_EXPERT:], -SWIGLU_LIMIT, SWIGLU_LIMIT)
        act = gate * jax.nn.sigmoid(SWIGLU_ALPHA * gate) * (up + 1.0)
        y = jnp.dot(act.astype(BF16), wd_bf[...], preferred_element_type=F32) + bd_ref[0]
        y_ref[0:n_rows, :] = _pack_rows(y)

    rows = vrows_ref[v]
    full = EXPERT_ROW_CLASSES[0]
    first = jnp.logical_and(first_of_expert, real)

    @pl.when(jnp.logical_and(first, rows == full))
    def _():
        load_weights()
        run(full)
        fetch_next()

    @pl.when(jnp.logical_and(first, rows != full))
    def _():
        load_weights()
        fetch_next()

    later_full = jnp.logical_and(real, jnp.logical_not(first_of_expert))
    pl.when(jnp.logical_and(later_full, rows == full))(functools.partial(run, full))
    for n_rows in EXPERT_ROW_CLASSES[1:]:
        pl.when(jnp.logical_and(real, rows == n_rows))(functools.partial(run, n_rows))


def _experts(x_sorted, visits, w_gate_up, b_gate_up, w_down, b_down):
    n_rows, W = x_sorted.shape
    D = 2 * W
    tm = EXPERT_TILE
    n_visits = visits[0].shape[0]
    n_prefetch = len(visits)
    by_block = lambda v, vb, *_: (vb[v], 0)
    by_expert = lambda v, vb, ve, *_: (ve[v], 0, 0)
    grid_spec = pltpu.PrefetchScalarGridSpec(
        num_scalar_prefetch=n_prefetch,
        grid=(n_visits,),
        in_specs=[
            pl.BlockSpec((tm, W), by_block),
            pl.BlockSpec(memory_space=pl.ANY),
            pl.BlockSpec((1, 1, 2 * D_EXPERT), by_expert),
            pl.BlockSpec(memory_space=pl.ANY),
            pl.BlockSpec((1, 1, D), by_expert),
        ],
        out_specs=pl.BlockSpec((tm, W), by_block),
        scratch_shapes=[pltpu.VMEM((D, 2 * D_EXPERT), F32), pltpu.VMEM((D_EXPERT, D), F32),
                        pltpu.VMEM((D, 2 * D_EXPERT), BF16), pltpu.VMEM((D_EXPERT, D), BF16),
                        pltpu.SemaphoreType.DMA((2,))],
    )
    return pl.pallas_call(
        _expert_kernel,
        grid_spec=grid_spec,
        out_shape=jax.ShapeDtypeStruct((n_rows, W), U32),
        compiler_params=pltpu.CompilerParams(
            dimension_semantics=("arbitrary",), vmem_limit_bytes=VMEM_LIMIT),
        name="expert_gmm",
    )(*visits, x_sorted, w_gate_up, b_gate_up.reshape(N_EXPERTS, 1, -1), w_down, b_down.reshape(N_EXPERTS, 1, -1))


def _visit_schedule(counts, n_assign, capacity):
    tm = EXPERT_TILE
    assert EXPERT_ROW_CLASSES[0] == tm and list(EXPERT_ROW_CLASSES) == sorted(EXPERT_ROW_CLASSES, reverse=True)
    n_visits = n_assign // tm + N_EXPERTS
    n_full = counts // tm
    tail = counts - n_full * tm
    tail_rows = jnp.zeros_like(tail)
    for rows in EXPERT_ROW_CLASSES:
        tail_rows = jnp.where(tail <= rows, rows, tail_rows)
    per_e = n_full + (tail > 0)
    vend = jnp.cumsum(per_e)
    nvis = vend[-1]
    vc = jnp.minimum(jnp.arange(n_visits, dtype=I32), nvis - 1)
    done = vc[:, None] >= vend[None, :]
    e = jnp.sum(done, axis=1).astype(I32)
    local = vc - jnp.sum(jnp.where(done, per_e[None, :], 0), axis=1)
    mine = e[:, None] == jnp.arange(N_EXPERTS, dtype=I32)[None, :]
    is_tail = local >= jnp.sum(jnp.where(mine, n_full[None, :], 0), axis=1)
    vrows = jnp.where(is_tail, jnp.sum(jnp.where(mine, tail_rows[None, :], 0), axis=1), tm)
    blk = e * (capacity // tm) + local
    ids = jnp.arange(N_EXPERTS, dtype=I32)
    nonempty = counts > 0
    later = jnp.logical_and(nonempty[None, :], ids[None, :] > ids[:, None])
    nxt = jnp.where(jnp.any(later, axis=1), jnp.argmax(later, axis=1), -1).astype(I32)
    return blk.astype(I32), e, vrows.astype(I32), nvis.reshape(1).astype(I32), nxt


def _combine_kernel(y0_ref, y1_ref, y2_ref, y3_ref, x1_ref, gate_ref, g_ref, o_ref):
    half = x1_ref.shape[1] // 2
    gates = gate_ref[...]
    lo = x1_ref[:, :half]
    hi = x1_ref[:, half:]
    for kk, y_ref in enumerate((y0_ref, y1_ref, y2_ref, y3_ref)):
        y_lo, y_hi = _unpack_rows(y_ref[...])
        lo = lo + gates[:, kk:kk + 1] * y_lo
        hi = hi + gates[:, kk:kk + 1] * y_hi
    ms = (jnp.sum(lo * lo, axis=-1, keepdims=True) + jnp.sum(hi * hi, axis=-1, keepdims=True)) / (2 * half)
    scale = lax.rsqrt(ms + RMS_EPS)
    o_ref[:, :half] = lo * scale * g_ref[:, :half]
    o_ref[:, half:] = hi * scale * g_ref[:, half:]


def _combine(y_tok, x1, gates_t, g_final):
    T, D = x1.shape
    tm = COMBINE_TILE
    nt = T // tm
    y_spec = lambda kk: pl.BlockSpec((tm, D // 2), lambda i: (kk * nt + i, 0))
    return pl.pallas_call(
        _combine_kernel,
        grid=(nt,),
        in_specs=[y_spec(0), y_spec(1), y_spec(2), y_spec(3),
                  pl.BlockSpec((tm, D), lambda i: (i, 0)),
                  pl.BlockSpec((tm, TOP_K), lambda i: (i, 0)),
                  pl.BlockSpec((1, D), lambda i: (0, 0))],
        out_specs=pl.BlockSpec((tm, D), lambda i: (i, 0)),
        out_shape=jax.ShapeDtypeStruct((T, D), F32),
        compiler_params=pltpu.CompilerParams(
            dimension_semantics=("arbitrary",), vmem_limit_bytes=VMEM_LIMIT),
        name="combine_norm",
    )(y_tok, y_tok, y_tok, y_tok, x1, gates_t, g_final.reshape(1, D))


def kernel(x, g_mix, w_in, pool_w, pool_scale, w_out, g_moe, w_router, b_router, w_gate_up, b_gate_up,
           w_down, b_down, g_final):
    B, S, D = x.shape
    T = B * S
    assert g_mix.shape[0] == 1, "single-layer problem: the final norm is fused into the combine step"
    q, k, v, pool = _in_proj(x, g_mix[0], w_in[0], pool_w[0], pool_scale[0])
    attn = _attention(q, k, v)
    capacity = T
    tm = PROJ_TILE
    tri = (lax.broadcasted_iota(I32, (tm, tm), 0) < lax.broadcasted_iota(I32, (tm, tm), 1)).astype(BF16)
    weights = (w_out[0].astype(BF16), g_moe[0].reshape(1, D), w_router[0].T, b_router[0].reshape(N_EXPERTS, 1), tri)
    x1 = x_sorted = None
    counts = jnp.zeros((N_EXPERTS, 1), F32)
    gates, dest = [], []
    for part in range(ROUTER_PARTS):
        x1, h2, gates_p, dest_p, cnt = _out_proj_router(
            attn.reshape(T, SB_WIDTH), pool.reshape(T, POOL_WIDTH), x.reshape(T, D), weights, capacity,
            part, counts, x1)
        x_sorted = _sc_dispatch(h2, dest_p, N_EXPERTS * capacity, x_sorted)
        counts = cnt[:, :1]
        gates.append(gates_p)
        dest.append(dest_p)
    gates = jnp.concatenate(gates, axis=1)
    dest = jnp.concatenate(dest, axis=1)
    visits = _visit_schedule(cnt[:, 0].astype(I32), T * TOP_K, capacity)
    y = _experts(x_sorted, visits, w_gate_up[0], b_gate_up[0], w_down[0], b_down[0])
    y_tok = _sc_gather(y, dest.reshape(-1))
    return _combine(y_tok, x1, gates.T, g_final).reshape(B, S, D)
```

```python
import functools

import jax
import jax.numpy as jnp
from jax import lax
from jax.experimental import pallas as pl
from jax.experimental.pallas import tpu as pltpu
from jax.experimental.pallas import tpu_sc as plsc

F32 = jnp.float32
BF16 = jnp.bfloat16
I32 = jnp.int32
U32 = jnp.uint32

D_MODEL = 1024
SB_HEADS = 8
SB_HEAD_DIM = 64
SB_WIDTH = SB_HEADS * SB_HEAD_DIM
POOL_WINDOWS = (2, 4, 8, 16)
POOL_WIDTH = 512
POOL_GROUP_DIM = 128
N_EXPERTS = 32
TOP_K = 4
D_EXPERT = 1024
SWIGLU_LIMIT = 7.0
SWIGLU_ALPHA = 1.702
RMS_EPS = 1e-5

LANES = 128
HALO = 32
PROJ_TILE = 1024
ROUTER_PARTS = 4
ATTN_TILE = 256
ATTN_SUBTILES = 4
EXPERT_TILE = 1024
EXPERT_ROW_CLASSES = (1024, 512, 256)
COMBINE_TILE = 1024
ATTN_SKIP_LOG2 = -160.0
LOG2_E = 1.4426950408889634
V7X_VMEM_BYTES = 64 * 1024 * 1024
VMEM_LIMIT = V7X_VMEM_BYTES * 7 // 8

_NT = (((1,), (1,)), ((), ()))


def _rms(x, g):
    ms = jnp.mean(x * x, axis=-1, keepdims=True)
    return x * lax.rsqrt(ms + RMS_EPS) * g


def _pack_rows(x):
    n = x.shape[1] // 2
    lo = lax.bitcast_convert_type(x[:, :n].astype(BF16).astype(F32), U32)
    hi = lax.bitcast_convert_type(x[:, n:].astype(BF16).astype(F32), U32)
    return (lo >> 16) | (hi & jnp.uint32(0xFFFF0000))


def _unpack_rows(w):
    lo = lax.bitcast_convert_type(w << 16, F32)
    hi = lax.bitcast_convert_type(w & jnp.uint32(0xFFFF0000), F32)
    return lo, hi


def _in_proj_kernel(x_ref, w_ref, gain_ref, pw_ref, ps_ref, q_ref, k_ref, v_ref, p_ref, uext_ref, xb_ref, lvl_ref, unew_ref,
                    wbf_ref, *, n_seq):
    i = pl.program_id(0)
    n_tiles = pl.num_programs(0) - 1
    tm = x_ref.shape[0]
    lo = HALO // 2

    @pl.when(i == 0)
    def _():
        uext_ref[...] = jnp.zeros_like(uext_ref)
        wbf_ref[...] = (w_ref[...] * gain_ref[...]).astype(BF16)
        lvl_ref[:, 0:lo, :] = jnp.zeros((2, lo, POOL_GROUP_DIM), F32)

    t = lax.rem(i + n_seq - 1, n_seq) * tm + lax.broadcasted_iota(I32, (tm, 1), 0)

    def pool_group(g):
        w = POOL_WINDOWS[g]
        sl = slice(g * POOL_GROUP_DIM, (g + 1) * POOL_GROUP_DIM)
        lvl_ref[0, lo:, :] = uext_ref[lo:, sl] + uext_ref[lo - 1:HALO + tm - 1, sl]
        cur, k = 0, 2
        while k < w:
            lvl_ref[1 - cur, lo:, :] = lvl_ref[cur, lo:, :] + lvl_ref[cur, lo - k:HALO + tm - k, :]
            cur, k = 1 - cur, 2 * k
        count = jnp.minimum(t + 1, w).astype(F32)
        pooled = lvl_ref[cur, HALO:, :] / count - uext_ref[HALO:, sl]
        mixed = jnp.dot(pooled.astype(BF16), pw_ref[g].astype(BF16), preferred_element_type=F32)
        p_ref[:, sl] = (mixed * ps_ref[:, sl]).astype(BF16)

    def shift_history():
        uext_ref[0:HALO, :] = jnp.where(lax.rem(i, n_seq) == 0, 0.0, uext_ref[tm:tm + HALO, :])

    @pl.when(i < n_tiles)
    def _():
        x = x_ref[...]
        inv = lax.rsqrt(jnp.mean(x * x, axis=-1, keepdims=True) + RMS_EPS)
        xb_ref[...] = x.astype(BF16)
        scales = (inv * (LOG2_E * SB_HEAD_DIM ** -0.5), inv, inv)
        for g, (o_ref, scale) in enumerate(zip((q_ref, k_ref, v_ref), scales)):
            cols = slice(g * SB_WIDTH, (g + 1) * SB_WIDTH)
            o_ref[...] = (jnp.dot(xb_ref[...], wbf_ref[:, cols], preferred_element_type=F32) * scale).astype(BF16)
            pool_group(g)
        unew_ref[...] = jnp.dot(xb_ref[...], wbf_ref[:, 3 * SB_WIDTH:], preferred_element_type=F32) * inv
        pool_group(3)
        shift_history()
        uext_ref[HALO:, :] = unew_ref[...]

    @pl.when(i == n_tiles)
    def _():
        for g in range(len(POOL_WINDOWS)):
            pool_group(g)


def _in_proj(x, g_mix, w_in, pool_w, pool_scale):
    B, S, D = x.shape
    tm = PROJ_TILE
    n_out = w_in.shape[1]
    n_tiles = B * S // tm
    out_sd = jax.ShapeDtypeStruct((B * S, SB_WIDTH), BF16)
    cur = lambda i: (jnp.minimum(i, n_tiles - 1), 0)
    prev = lambda i: (jnp.maximum(i - 1, 0), 0)
    blk = pl.BlockSpec((tm, SB_WIDTH), cur)
    q, k, v, p = pl.pallas_call(
        functools.partial(_in_proj_kernel, n_seq=S // tm),
        grid=(n_tiles + 1,),
        in_specs=[
            pl.BlockSpec((tm, D), cur),
            pl.BlockSpec((D, n_out), lambda i: (0, 0)),
            pl.BlockSpec((D, 1), lambda i: (0, 0)),
            pl.BlockSpec(pool_w.shape, lambda i: (0, 0, 0)),
            pl.BlockSpec((1, POOL_WIDTH), lambda i: (0, 0)),
        ],
        out_specs=[blk, blk, blk, pl.BlockSpec((tm, POOL_WIDTH), prev)],
        out_shape=[out_sd, out_sd, out_sd, out_sd],
        scratch_shapes=[pltpu.VMEM((HALO + tm, POOL_WIDTH), F32), pltpu.VMEM((tm, D), BF16),
                        pltpu.VMEM((2, HALO + tm, POOL_GROUP_DIM), F32), pltpu.VMEM((tm, POOL_WIDTH), F32),
                        pltpu.VMEM((D, n_out), BF16)],
        compiler_params=pltpu.CompilerParams(
            dimension_semantics=("arbitrary",), vmem_limit_bytes=VMEM_LIMIT),
        name="in_proj_pool",
    )(x.reshape(B * S, D), w_in, g_mix.reshape(D, 1), pool_w,
      pool_scale.reshape(1, POOL_WIDTH))
    return (q.reshape(B, S, SB_WIDTH), k.reshape(B, S, SB_WIDTH), v.reshape(B, S, SB_WIDTH),
            p.reshape(B, S, POOL_WIDTH))


def _attn_kernel(q_ref, k_ref, v_ref, o_ref):
    tq = ATTN_TILE
    tk = tq
    qi = pl.program_id(2)
    lane = lax.broadcasted_iota(I32, (tq, LANES), 1)
    row = lax.broadcasted_iota(I32, (tq, tk), 0)
    col = lax.broadcasted_iota(I32, (tq, tk), 1)
    causal = col < row
    tri = (row > col).astype(BF16)

    def head_rows(r, h):
        rows = q_ref[0, r * tq:(r + 1) * tq, :]
        return jnp.where((lane >= h * SB_HEAD_DIM) & (lane < (h + 1) * SB_HEAD_DIM), rows, jnp.zeros_like(rows))

    def scores(qh, j):
        kb = k_ref[0, pl.ds(pl.multiple_of(j * tk, tk), tk), :]
        return lax.dot_general(qh, kb, _NT, preferred_element_type=F32)

    def values(j):
        return v_ref[0, pl.ds(pl.multiple_of(j * tk, tk), tk), :]

    def log_terms(z, mask):
        neg_abs = lax.bitcast_convert_type(lax.bitcast_convert_type(z, U32) | jnp.uint32(0x80000000), F32)
        nl = jnp.maximum(z, 0.0) + jnp.log2(1.0 + jnp.exp2(neg_abs))
        if mask:
            nl = jnp.where(causal, nl, 0.0)
        return nl.astype(BF16), z - nl, nl[:, 0:1]

    def tile(qh, j, mask):
        nl, lb, nl0 = log_terms(scores(qh, j), mask)
        ntail = jnp.dot(nl, tri, preferred_element_type=F32)
        return lb - ntail, -(ntail[:, 0:1] + nl0), values(j)

    nq = ATTN_SUBTILES
    has_prev = qi > 0
    qs = [head_rows(r, h) for r in range(nq) for h in range(2)]
    chains = []
    for n, qh in enumerate(qs):
        diag = nq * qi + n // 2
        chains += [(qh, diag, True), (qh, jnp.maximum(diag - 1, 0), False)]
    zs = [scores(qh, j) for qh, j, _ in chains]
    terms = [log_terms(z, is_diag) for z, (_, _, is_diag) in zip(zs, chains)]
    nts = [jnp.dot(nl, tri, preferred_element_type=F32) for nl, _, _ in terms]
    lbs = [lb for _, lb, _ in terms]
    tots = [-(nt[:, 0:1] + nl0) for nt, (_, _, nl0) in zip(nts, terms)]
    ws = []
    for n in range(2 * nq):
        d, p = 2 * n, 2 * n + 1
        w_d = jnp.where(causal, jnp.exp2(lbs[d] - nts[d]), 0.0)
        w_p = jnp.exp2(lbs[p] - nts[p] + tots[d])
        if n < 2:
            w_p = jnp.where(has_prev, w_p, 0.0)
        ws += [w_d.astype(BF16), w_p.astype(BF16)]
    pvs = [jnp.dot(w, values(j), preferred_element_type=F32) for w, (_, j, _) in zip(ws, chains)]
    accs = [pvs[2 * n] + pvs[2 * n + 1] for n in range(2 * nq)]
    carries = [tots[2 * n] + tots[2 * n + 1] for n in range(2 * nq)]

    def cond(st):
        j, cs, _ = st
        live = functools.reduce(jnp.maximum, [jnp.max(c) for c in cs])
        return jnp.logical_and(j >= 0, live > ATTN_SKIP_LOG2)

    def body(st):
        j, cs, acs = st
        new_c, new_a = [], []
        for n, (qh, c, a) in enumerate(zip(qs, cs, acs)):
            lag = nq - 1 - n // 2
            if lag:
                c = jnp.where(j >= lag, c, -jnp.inf)
            lw, tot, vb = tile(qh, jnp.maximum(j - lag, 0), False)
            new_a.append(a + jnp.dot(jnp.exp2(lw + c).astype(BF16), vb, preferred_element_type=F32))
            new_c.append(c + tot)
        return j - 1, tuple(new_c), tuple(new_a)

    _, _, accs = lax.while_loop(cond, body, (nq * qi + nq - 3, tuple(carries), tuple(accs)))
    for r in range(nq):
        o_ref[0, r * tq:(r + 1) * tq, :] = jnp.where(
            lane < SB_HEAD_DIM, accs[2 * r], accs[2 * r + 1]).astype(o_ref.dtype)


def _attention(q, k, v):
    B, S, W = q.shape
    tq = ATTN_SUBTILES * ATTN_TILE
    n_pairs = W // LANES
    return pl.pallas_call(
        _attn_kernel,
        grid=(B, n_pairs, S // tq),
        in_specs=[
            pl.BlockSpec((1, tq, LANES), lambda b, p, i: (b, i, p)),
            pl.BlockSpec((1, S, LANES), lambda b, p, i: (b, 0, p)),
            pl.BlockSpec((1, S, LANES), lambda b, p, i: (b, 0, p)),
        ],
        out_specs=pl.BlockSpec((1, tq, LANES), lambda b, p, i: (b, i, p)),
        out_shape=jax.ShapeDtypeStruct((B, S, W), BF16),
        compiler_params=pltpu.CompilerParams(
            dimension_semantics=("arbitrary", "arbitrary", "arbitrary"), vmem_limit_bytes=VMEM_LIMIT),
        name="stickbreak_attn",
    )(q, k, v)


def _out_proj_router_kernel(attn_ref, pool_ref, x_ref, wo_ref, g_ref, wr_ref, br_ref, tri_ref, cin_ref,
                            *rest, capacity):
    x1_ref, h2_ref, gate_ref, dest_ref, cnt_ref, carry_ref = rest[-6:]
    i = pl.program_id(0)
    tm = x_ref.shape[0]

    @pl.when(i == 0)
    def _():
        carry_ref[...] = cin_ref[...]

    mixed = jnp.dot(attn_ref[...], wo_ref[0:SB_WIDTH, :], preferred_element_type=F32)
    mixed = mixed + jnp.dot(pool_ref[...], wo_ref[SB_WIDTH:, :], preferred_element_type=F32)
    x1 = x_ref[...] + mixed
    h2 = _rms(x1, g_ref[...])
    x1_ref[...] = x1
    h2_ref[...] = _pack_rows(h2)

    hh = h2.astype(BF16)
    hl = (h2 - hh.astype(F32)).astype(BF16)
    wr = wr_ref[...]
    wh = wr.astype(BF16)
    wl = (wr - wh.astype(F32)).astype(BF16)
    both = lax.dot_general(jnp.concatenate([wh, wl], axis=0), hh, _NT, preferred_element_type=F32)
    logits = (both[:N_EXPERTS] + both[N_EXPERTS:]
              + lax.dot_general(wh, hl, _NT, preferred_element_type=F32)) + br_ref[...]

    eid = lax.broadcasted_iota(I32, (N_EXPERTS, tm), 0).astype(F32)
    work = logits
    vals, ids = [], []
    for _ in range(TOP_K):
        m = jnp.max(work, axis=0, keepdims=True)
        sel = jnp.min(jnp.where(work == m, eid, float(N_EXPERTS)), axis=0, keepdims=True)
        vals.append(m)
        ids.append(sel)
        work = jnp.where(eid == sel, -jnp.inf, work)
    exps = [jnp.exp(v - vals[0]) for v in vals]
    denom = exps[0] + exps[1] + exps[2] + exps[3]

    onehot = jnp.zeros((N_EXPERTS, tm), F32)
    for sel in ids:
        onehot = onehot + (eid == sel).astype(F32)
    before = jnp.dot(onehot.astype(BF16), tri_ref[...], preferred_element_type=F32) + carry_ref[...]
    for kk in range(TOP_K):
        gate_ref[kk:kk + 1, :] = exps[kk] / denom
        rk = jnp.sum(jnp.where(eid == ids[kk], before, 0.0), axis=0, keepdims=True)
        dest_ref[kk:kk + 1, :] = (ids[kk] * float(capacity) + rk).astype(I32)
    carry_ref[...] = carry_ref[...] + jnp.sum(onehot, axis=1, keepdims=True)
    cnt_ref[...] = jnp.broadcast_to(carry_ref[...], cnt_ref.shape)


def _out_proj_router(attn, pool, x, weights, capacity, part, counts_in, x1_buf):
    T, D = x.shape
    tm = PROJ_TILE
    steps = T // tm // ROUTER_PARTS
    base = part * steps
    tp = steps * tm
    w_out, g_moe, w_router, b_router, tri = weights
    in_blk = lambda w: pl.BlockSpec((tm, w), lambda i: (i + base, 0))
    fixed = lambda shape: pl.BlockSpec(shape, lambda i: tuple(0 for _ in shape))
    sel_blk = pl.BlockSpec((TOP_K, tm), lambda i: (0, i))
    assert N_EXPERTS * capacity < 2 ** 24
    in_specs = [in_blk(SB_WIDTH), in_blk(POOL_WIDTH), in_blk(D), fixed((D, D)), fixed((1, D)),
                fixed((N_EXPERTS, D)), fixed((N_EXPERTS, 1)), fixed((tm, tm)), fixed((N_EXPERTS, 1))]
    args = [attn, pool, x, w_out, g_moe, w_router, b_router, tri, counts_in]
    aliases = {}
    if x1_buf is not None:
        in_specs.append(pl.BlockSpec(memory_space=pl.ANY))
        args.append(x1_buf)
        aliases = {len(args) - 1: 0}
    return pl.pallas_call(
        functools.partial(_out_proj_router_kernel, capacity=capacity),
        grid=(steps,),
        in_specs=in_specs,
        out_specs=[in_blk(D), pl.BlockSpec((tm, D // 2), lambda i: (i, 0)), sel_blk, sel_blk,
                   fixed((N_EXPERTS, LANES))],
        out_shape=[jax.ShapeDtypeStruct((T, D), F32), jax.ShapeDtypeStruct((tp, D // 2), U32),
                   jax.ShapeDtypeStruct((TOP_K, tp), F32), jax.ShapeDtypeStruct((TOP_K, tp), I32),
                   jax.ShapeDtypeStruct((N_EXPERTS, LANES), F32)],
        scratch_shapes=[pltpu.VMEM((N_EXPERTS, 1), F32)],
        input_output_aliases=aliases,
        compiler_params=pltpu.CompilerParams(
            dimension_semantics=("arbitrary",), vmem_limit_bytes=VMEM_LIMIT),
        name="out_proj_router",
    )(*args)


SC_ROWS_PER_COPY = 64


def _sc_workers():
    info = plsc.get_sparse_core_info()
    return info.num_cores, info.num_cores * info.num_subcores


def _sc_dispatch(rows, dest, n_out, into=None):
    T, W = rows.shape
    K = dest.shape[0]
    sub = SC_ROWS_PER_COPY
    n_cores, n_workers = _sc_workers()
    per_w = T // n_workers
    n_chunks = per_w // sub
    assert per_w * n_workers == T and n_chunks * sub == per_w and n_chunks % 2 == 0
    idx = dest.reshape(K, n_workers, n_chunks, sub).transpose(1, 2, 0, 3).reshape(n_workers, n_chunks * K, sub)
    mesh = plsc.VectorSubcoreMesh(core_axis_name="core", subcore_axis_name="subcore")

    scratch = [pltpu.VMEM((n_chunks * K, sub), I32), pltpu.VMEM((2, sub, W), rows.dtype),
               pltpu.SemaphoreType.DMA((2,)), pltpu.SemaphoreType.DMA((2,))]

    def scatter_rows(x_hbm, i_hbm, o_hbm, idx_v, buf, rsem, wsem):
        wid = lax.axis_index("subcore") * n_cores + lax.axis_index("core")
        base = wid * per_w
        pltpu.sync_copy(i_hbm.at[wid], idx_v)

        def read(c, slot):
            return pltpu.make_async_copy(x_hbm.at[pl.ds(base + c * sub, sub)], buf.at[slot], rsem.at[slot])

        def write(c, kk, slot):
            return pltpu.make_async_copy(buf.at[slot], o_hbm.at[idx_v.at[c * K + kk]], wsem.at[slot])

        read(0, 0).start()

        @pl.loop(0, n_chunks, step=2)
        def _(c0):
            for b in range(2):
                c = c0 + b
                read(c, b).wait()
                for kk in range(K):
                    write(c, kk, b).start()

                @pl.when(c + 1 < n_chunks)
                def _():
                    @pl.when(c >= 1)
                    def _():
                        for kk in range(K):
                            write(c - 1, kk, 1 - b).wait()
                    read(c + 1, 1 - b).start()

        for kk in range(K):
            write(n_chunks - 2, kk, 0).wait()
            write(n_chunks - 1, kk, 1).wait()

    if into is None:
        out_type = jax.ShapeDtypeStruct((n_out, W), rows.dtype)
        return pl.kernel(scatter_rows, out_type=out_type, mesh=mesh, scratch_types=scratch)(rows, idx)
    assert into.shape == (n_out, W) and into.dtype == rows.dtype
    out_ref = jax.new_ref(into)
    pl.kernel(scatter_rows, out_type=(), mesh=mesh, scratch_types=scratch)(rows, idx, out_ref)
    return jax.freeze(out_ref)


def _sc_gather(table, indices):
    M = indices.shape[0]
    W = table.shape[1]
    sub = SC_ROWS_PER_COPY
    n_cores, n_workers = _sc_workers()
    per_w = M // n_workers
    n_steps = per_w // sub
    assert per_w * n_workers == M and n_steps * sub == per_w and n_steps % 2 == 0
    mesh = plsc.VectorSubcoreMesh(core_axis_name="core", subcore_axis_name="subcore")

    @functools.partial(
        pl.kernel, out_type=jax.ShapeDtypeStruct((M, W), table.dtype), mesh=mesh,
        scratch_types=[pltpu.VMEM((n_steps, sub), I32), pltpu.VMEM((2, sub, W), table.dtype),
                       pltpu.SemaphoreType.DMA((2,)), pltpu.SemaphoreType.DMA((2,))])
    def gather_rows(x_hbm, i_hbm, o_hbm, idx_v, buf, gsem, wsem):
        wid = lax.axis_index("subcore") * n_cores + lax.axis_index("core")
        base = wid * per_w
        pltpu.sync_copy(i_hbm.at[wid], idx_v)

        def gather(s, slot):
            return pltpu.make_async_copy(x_hbm.at[idx_v.at[s]], buf.at[slot], gsem.at[slot])

        def write(s, slot):
            return pltpu.make_async_copy(buf.at[slot], o_hbm.at[pl.ds(base + s * sub, sub)], wsem.at[slot])

        gather(0, 0).start()

        @pl.loop(0, n_steps, step=2)
        def _(s0):
            for b in range(2):
                s = s0 + b
                gather(s, b).wait()
                write(s, b).start()

                @pl.when(s + 1 < n_steps)
                def _():
                    @pl.when(s >= 1)
                    def _():
                        write(s - 1, 1 - b).wait()
                    gather(s + 1, 1 - b).start()

        write(n_steps - 2, 0).wait()
        write(n_steps - 1, 1).wait()

    return gather_rows(table, indices.reshape(n_workers, n_steps, sub))


def _expert_kernel(vb_ref, ve_ref, vrows_ref, nvis_ref, nxt_ref,
                   x_ref, wgu_hbm, bgu_ref, wd_hbm, bd_ref, y_ref,
                   wgu_stage, wd_stage, wgu_bf, wd_bf, sem):
    v = pl.program_id(0)
    real = v < nvis_ref[0]
    e = ve_ref[v]
    first_of_expert = jnp.logical_or(v == 0, e != ve_ref[jnp.maximum(v - 1, 0)])

    def weight_copies(expert):
        return (pltpu.make_async_copy(wgu_hbm.at[expert], wgu_stage, sem.at[0]),
                pltpu.make_async_copy(wd_hbm.at[expert], wd_stage, sem.at[1]))

    @pl.when(v == 0)
    def _():
        for c in weight_copies(e):
            c.start()

    def load_weights():
        for c in weight_copies(e):
            c.wait()
        wgu_bf[...] = wgu_stage[...].astype(BF16)
        wd_bf[...] = wd_stage[...].astype(BF16)

    def fetch_next():
        nxt = nxt_ref[e]

        @pl.when(nxt >= 0)
        def _():
            for c in weight_copies(nxt):
                c.start()

    def run(n_rows):
        x_lo, x_hi = _unpack_rows(x_ref[0:n_rows, :])
        xb = jnp.concatenate([x_lo.astype(BF16), x_hi.astype(BF16)], axis=1)
        gu = jnp.dot(xb, wgu_bf[...], preferred_element_type=F32) + bgu_ref[0]
        gate = jnp.minimum(gu[:, :D_EXPERT], SWIGLU_LIMIT)
        up = jnp.clip(gu[:, D_EXPERT:], -SWIGLU_LIMIT, SWIGLU_LIMIT)
        act = gate * jax.nn.sigmoid(SWIGLU_ALPHA * gate) * (up + 1.0)
        y = jnp.dot(act.astype(BF16), wd_bf[...], preferred_element_type=F32) + bd_ref[0]
        y_ref[0:n_rows, :] = _pack_rows(y)

    rows = vrows_ref[v]
    full = EXPERT_ROW_CLASSES[0]
    first = jnp.logical_and(first_of_expert, real)

    @pl.when(jnp.logical_and(first, rows == full))
    def _():
        load_weights()
        run(full)
        fetch_next()

    @pl.when(jnp.logical_and(first, rows != full))
    def _():
        load_weights()
        fetch_next()

    later_full = jnp.logical_and(real, jnp.logical_not(first_of_expert))
    pl.when(jnp.logical_and(later_full, rows == full))(functools.partial(run, full))
    for n_rows in EXPERT_ROW_CLASSES[1:]:
        pl.when(jnp.logical_and(real, rows == n_rows))(functools.partial(run, n_rows))


def _experts(x_sorted, visits, w_gate_up, b_gate_up, w_down, b_down):
    n_rows, W = x_sorted.shape
    D = 2 * W
    tm = EXPERT_TILE
    n_visits = visits[0].shape[0]
    n_prefetch = len(visits)
    by_block = lambda v, vb, *_: (vb[v], 0)
    by_expert = lambda v, vb, ve, *_: (ve[v], 0, 0)
    grid_spec = pltpu.PrefetchScalarGridSpec(
        num_scalar_prefetch=n_prefetch,
        grid=(n_visits,),
        in_specs=[
            pl.BlockSpec((tm, W), by_block),
            pl.BlockSpec(memory_space=pl.ANY),
            pl.BlockSpec((1, 1, 2 * D_EXPERT), by_expert),
            pl.BlockSpec(memory_space=pl.ANY),
            pl.BlockSpec((1, 1, D), by_expert),
        ],
        out_specs=pl.BlockSpec((tm, W), by_block),
        scratch_shapes=[pltpu.VMEM((D, 2 * D_EXPERT), F32), pltpu.VMEM((D_EXPERT, D), F32),
                        pltpu.VMEM((D, 2 * D_EXPERT), BF16), pltpu.VMEM((D_EXPERT, D), BF16),
                        pltpu.SemaphoreType.DMA((2,))],
    )
    return pl.pallas_call(
        _expert_kernel,
        grid_spec=grid_spec,
        out_shape=jax.ShapeDtypeStruct((n_rows, W), U32),
        compiler_params=pltpu.CompilerParams(
            dimension_semantics=("arbitrary",), vmem_limit_bytes=VMEM_LIMIT),
        name="expert_gmm",
    )(*visits, x_sorted, w_gate_up, b_gate_up.reshape(N_EXPERTS, 1, -1), w_down, b_down.reshape(N_EXPERTS, 1, -1))


def _visit_schedule(counts, n_assign, capacity):
    tm = EXPERT_TILE
    assert EXPERT_ROW_CLASSES[0] == tm and list(EXPERT_ROW_CLASSES) == sorted(EXPERT_ROW_CLASSES, reverse=True)
    n_visits = n_assign // tm + N_EXPERTS
    n_full = counts // tm
    tail = counts - n_full * tm
    tail_rows = jnp.zeros_like(tail)
    for rows in EXPERT_ROW_CLASSES:
        tail_rows = jnp.where(tail <= rows, rows, tail_rows)
    per_e = n_full + (tail > 0)
    vend = jnp.cumsum(per_e)
    nvis = vend[-1]
    vc = jnp.minimum(jnp.arange(n_visits, dtype=I32), nvis - 1)
    done = vc[:, None] >= vend[None, :]
    e = jnp.sum(done, axis=1).astype(I32)
    local = vc - jnp.sum(jnp.where(done, per_e[None, :], 0), axis=1)
    mine = e[:, None] == jnp.arange(N_EXPERTS, dtype=I32)[None, :]
    is_tail = local >= jnp.sum(jnp.where(mine, n_full[None, :], 0), axis=1)
    vrows = jnp.where(is_tail, jnp.sum(jnp.where(mine, tail_rows[None, :], 0), axis=1), tm)
    blk = e * (capacity // tm) + local
    ids = jnp.arange(N_EXPERTS, dtype=I32)
    nonempty = counts > 0
    later = jnp.logical_and(nonempty[None, :], ids[None, :] > ids[:, None])
    nxt = jnp.where(jnp.any(later, axis=1), jnp.argmax(later, axis=1), -1).astype(I32)
    return blk.astype(I32), e, vrows.astype(I32), nvis.reshape(1).astype(I32), nxt


def _combine_kernel(y0_ref, y1_ref, y2_ref, y3_ref, x1_ref, gate_ref, g_ref, o_ref):
    half = x1_ref.shape[1] // 2
    gates = gate_ref[...]
    lo = x1_ref[:, :half]
    hi = x1_ref[:, half:]
    for kk, y_ref in enumerate((y0_ref, y1_ref, y2_ref, y3_ref)):
        y_lo, y_hi = _unpack_rows(y_ref[...])
        lo = lo + gates[:, kk:kk + 1] * y_lo
        hi = hi + gates[:, kk:kk + 1] * y_hi
    ms = (jnp.sum(lo * lo, axis=-1, keepdims=True) + jnp.sum(hi * hi, axis=-1, keepdims=True)) / (2 * half)
    scale = lax.rsqrt(ms + RMS_EPS)
    o_ref[:, :half] = lo * scale * g_ref[:, :half]
    o_ref[:, half:] = hi * scale * g_ref[:, half:]


def _combine(y_tok, x1, gates_t, g_final):
    T, D = x1.shape
    tm = COMBINE_TILE
    nt = T // tm
    y_spec = lambda kk: pl.BlockSpec((tm, D // 2), lambda i: (kk * nt + i, 0))
    return pl.pallas_call(
        _combine_kernel,
        grid=(nt,),
        in_specs=[y_spec(0), y_spec(1), y_spec(2), y_spec(3),
                  pl.BlockSpec((tm, D), lambda i: (i, 0)),
                  pl.BlockSpec((tm, TOP_K), lambda i: (i, 0)),
                  pl.BlockSpec((1, D), lambda i: (0, 0))],
        out_specs=pl.BlockSpec((tm, D), lambda i: (i, 0)),
        out_shape=jax.ShapeDtypeStruct((T, D), F32),
        compiler_params=pltpu.CompilerParams(
            dimension_semantics=("arbitrary",), vmem_limit_bytes=VMEM_LIMIT),
        name="combine_norm",
    )(y_tok, y_tok, y_tok, y_tok, x1, gates_t, g_final.reshape(1, D))


def kernel(x, g_mix, w_in, pool_w, pool_scale, w_out, g_moe, w_router, b_router, w_gate_up, b_gate_up,
           w_down, b_down, g_final):
    B, S, D = x.shape
    T = B * S
    assert g_mix.shape[0] == 1, "single-layer problem: the final norm is fused into the combine step"
    q, k, v, pool = _in_proj(x, g_mix[0], w_in[0], pool_w[0], pool_scale[0])
    attn = _attention(q, k, v)
    capacity = T
    tm = PROJ_TILE
    tri = (lax.broadcasted_iota(I32, (tm, tm), 0) < lax.broadcasted_iota(I32, (tm, tm), 1)).astype(BF16)
    weights = (w_out[0].astype(BF16), g_moe[0].reshape(1, D), w_router[0].T, b_router[0].reshape(N_EXPERTS, 1), tri)
    x1 = x_sorted = None
    counts = jnp.zeros((N_EXPERTS, 1), F32)
    gates, dest = [], []
    for part in range(ROUTER_PARTS):
        x1, h2, gates_p, dest_p, cnt = _out_proj_router(
            attn.reshape(T, SB_WIDTH), pool.reshape(T, POOL_WIDTH), x.reshape(T, D), weights, capacity,
            part, counts, x1)
        x_sorted = _sc_dispatch(h2, dest_p, N_EXPERTS * capacity, x_sorted)
        counts = cnt[:, :1]
        gates.append(gates_p)
        dest.append(dest_p)
    gates = jnp.concatenate(gates, axis=1)
    dest = jnp.concatenate(dest, axis=1)
    visits = _visit_schedule(cnt[:, 0].astype(I32), T * TOP_K, capacity)
    y = _experts(x_sorted, visits, w_gate_up[0], b_gate_up[0], w_down[0], b_down[0])
    y_tok = _sc_gather(y, dest.reshape(-1))
    return _combine(y_tok, x1, gates.T, g_final).reshape(B, S, D)
```

```python
import functools

import jax
import jax.numpy as jnp
from jax import lax
from jax.experimental import pallas as pl
from jax.experimental.pallas import tpu as pltpu
from jax.experimental.pallas import tpu_sc as plsc

F32 = jnp.float32
BF16 = jnp.bfloat16
I32 = jnp.int32
U32 = jnp.uint32

D_MODEL = 1024
SB_HEADS = 8
SB_HEAD_DIM = 64
SB_WIDTH = SB_HEADS * SB_HEAD_DIM
POOL_WINDOWS = (2, 4, 8, 16)
POOL_WIDTH = 512
POOL_GROUP_DIM = 128
N_EXPERTS = 32
TOP_K = 4
D_EXPERT = 1024
SWIGLU_LIMIT = 7.0
SWIGLU_ALPHA = 1.702
RMS_EPS = 1e-5

LANES = 128
HALO = 32
PROJ_TILE = 1024
ROUTER_PARTS = 4
ATTN_TILE = 256
ATTN_SUBTILES = 4
EXPERT_TILE = 1024
EXPERT_ROW_CLASSES = (1024, 512, 256)
COMBINE_TILE = 1024
ATTN_SKIP_LOG2 = -160.0
LOG2_E = 1.4426950408889634
V7X_VMEM_BYTES = 64 * 1024 * 1024
VMEM_LIMIT = V7X_VMEM_BYTES * 7 // 8

_NT = (((1,), (1,)), ((), ()))


def _rms(x, g):
    ms = jnp.mean(x * x, axis=-1, keepdims=True)
    return x * lax.rsqrt(ms + RMS_EPS) * g


def _pack_rows(x):
    n = x.shape[1] // 2
    lo = lax.bitcast_convert_type(x[:, :n].astype(BF16).astype(F32), U32)
    hi = lax.bitcast_convert_type(x[:, n:].astype(BF16).astype(F32), U32)
    return (lo >> 16) | (hi & jnp.uint32(0xFFFF0000))


def _unpack_rows(w):
    lo = lax.bitcast_convert_type(w << 16, F32)
    hi = lax.bitcast_convert_type(w & jnp.uint32(0xFFFF0000), F32)
    return lo, hi


def _in_proj_kernel(x_ref, w_ref, gain_ref, pw_ref, ps_ref, q_ref, k_ref, v_ref, p_ref, uext_ref, xb_ref, lvl_ref, unew_ref,
                    wbf_ref, *, n_seq):
    i = pl.program_id(0)
    n_tiles = pl.num_programs(0) - 1
    tm = x_ref.shape[0]
    lo = HALO // 2

    @pl.when(i == 0)
    def _():
        uext_ref[...] = jnp.zeros_like(uext_ref)
        wbf_ref[...] = (w_ref[...] * gain_ref[...]).astype(BF16)
        lvl_ref[:, 0:lo, :] = jnp.zeros((2, lo, POOL_GROUP_DIM), F32)

    t = lax.rem(i + n_seq - 1, n_seq) * tm + lax.broadcasted_iota(I32, (tm, 1), 0)

    def pool_group(g):
        w = POOL_WINDOWS[g]
        sl = slice(g * POOL_GROUP_DIM, (g + 1) * POOL_GROUP_DIM)
        lvl_ref[0, lo:, :] = uext_ref[lo:, sl] + uext_ref[lo - 1:HALO + tm - 1, sl]
        cur, k = 0, 2
        while k < w:
            lvl_ref[1 - cur, lo:, :] = lvl_ref[cur, lo:, :] + lvl_ref[cur, lo - k:HALO + tm - k, :]
            cur, k = 1 - cur, 2 * k
        count = jnp.minimum(t + 1, w).astype(F32)
        pooled = lvl_ref[cur, HALO:, :] / count - uext_ref[HALO:, sl]
        mixed = jnp.dot(pooled.astype(BF16), pw_ref[g].astype(BF16), preferred_element_type=F32)
        p_ref[:, sl] = (mixed * ps_ref[:, sl]).astype(BF16)

    def shift_history():
        uext_ref[0:HALO, :] = jnp.where(lax.rem(i, n_seq) == 0, 0.0, uext_ref[tm:tm + HALO, :])

    @pl.when(i < n_tiles)
    def _():
        x = x_ref[...]
        inv = lax.rsqrt(jnp.mean(x * x, axis=-1, keepdims=True) + RMS_EPS)
        xb_ref[...] = x.astype(BF16)
        scales = (inv * (LOG2_E * SB_HEAD_DIM ** -0.5), inv, inv)
        for g, (o_ref, scale) in enumerate(zip((q_ref, k_ref, v_ref), scales)):
            cols = slice(g * SB_WIDTH, (g + 1) * SB_WIDTH)
            o_ref[...] = (jnp.dot(xb_ref[...], wbf_ref[:, cols], preferred_element_type=F32) * scale).astype(BF16)
            pool_group(g)
        unew_ref[...] = jnp.dot(xb_ref[...], wbf_ref[:, 3 * SB_WIDTH:], preferred_element_type=F32) * inv
        pool_group(3)
        shift_history()
        uext_ref[HALO:, :] = unew_ref[...]

    @pl.when(i == n_tiles)
    def _():
        for g in range(len(POOL_WINDOWS)):
            pool_group(g)


def _in_proj(x, g_mix, w_in, pool_w, pool_scale):
    B, S, D = x.shape
    tm = PROJ_TILE
    n_out = w_in.shape[1]
    n_tiles = B * S // tm
    out_sd = jax.ShapeDtypeStruct((B * S, SB_WIDTH), BF16)
    cur = lambda i: (jnp.minimum(i, n_tiles - 1), 0)
    prev = lambda i: (jnp.maximum(i - 1, 0), 0)
    blk = pl.BlockSpec((tm, SB_WIDTH), cur)
    q, k, v, p = pl.pallas_call(
        functools.partial(_in_proj_kernel, n_seq=S // tm),
        grid=(n_tiles + 1,),
        in_specs=[
            pl.BlockSpec((tm, D), cur),
            pl.BlockSpec((D, n_out), lambda i: (0, 0)),
            pl.BlockSpec((D, 1), lambda i: (0, 0)),
            pl.BlockSpec(pool_w.shape, lambda i: (0, 0, 0)),
            pl.BlockSpec((1, POOL_WIDTH), lambda i: (0, 0)),
        ],
        out_specs=[blk, blk, blk, pl.BlockSpec((tm, POOL_WIDTH), prev)],
        out_shape=[out_sd, out_sd, out_sd, out_sd],
        scratch_shapes=[pltpu.VMEM((HALO + tm, POOL_WIDTH), F32), pltpu.VMEM((tm, D), BF16),
                        pltpu.VMEM((2, HALO + tm, POOL_GROUP_DIM), F32), pltpu.VMEM((tm, POOL_WIDTH), F32),
                        pltpu.VMEM((D, n_out), BF16)],
        compiler_params=pltpu.CompilerParams(
            dimension_semantics=("arbitrary",), vmem_limit_bytes=VMEM_LIMIT),
        name="in_proj_pool",
    )(x.reshape(B * S, D), w_in, g_mix.reshape(D, 1), pool_w,
      pool_scale.reshape(1, POOL_WIDTH))
    return (q.reshape(B, S, SB_WIDTH), k.reshape(B, S, SB_WIDTH), v.reshape(B, S, SB_WIDTH),
            p.reshape(B, S, POOL_WIDTH))


def _attn_kernel(q_ref, k_ref, v_ref, o_ref):
    tq = ATTN_TILE
    tk = tq
    qi = pl.program_id(2)
    lane = lax.broadcasted_iota(I32, (tq, LANES), 1)
    row = lax.broadcasted_iota(I32, (tq, tk), 0)
    col = lax.broadcasted_iota(I32, (tq, tk), 1)
    causal = col < row
    tri = (row > col).astype(BF16)

    def head_rows(r, h):
        rows = q_ref[0, r * tq:(r + 1) * tq, :]
        return jnp.where((lane >= h * SB_HEAD_DIM) & (lane < (h + 1) * SB_HEAD_DIM), rows, jnp.zeros_like(rows))

    def scores(qh, j):
        kb = k_ref[0, pl.ds(pl.multiple_of(j * tk, tk), tk), :]
        return lax.dot_general(qh, kb, _NT, preferred_element_type=F32)

    def values(j):
        return v_ref[0, pl.ds(pl.multiple_of(j * tk, tk), tk), :]

    def log_terms(z, mask):
        neg_abs = lax.bitcast_convert_type(lax.bitcast_convert_type(z, U32) | jnp.uint32(0x80000000), F32)
        nl = jnp.maximum(z, 0.0) + jnp.log2(1.0 + jnp.exp2(neg_abs))
        if mask:
            nl = jnp.where(causal, nl, 0.0)
        return nl.astype(BF16), z - nl, nl[:, 0:1]

    def tile(qh, j, mask):
        nl, lb, nl0 = log_terms(scores(qh, j), mask)
        ntail = jnp.dot(nl, tri, preferred_element_type=F32)
        return lb - ntail, -(ntail[:, 0:1] + nl0), values(j)

    nq = ATTN_SUBTILES
    has_prev = qi > 0
    qs = [head_rows(r, h) for r in range(nq) for h in range(2)]
    chains = []
    for n, qh in enumerate(qs):
        diag = nq * qi + n // 2
        chains += [(qh, diag, True), (qh, jnp.maximum(diag - 1, 0), False)]
    zs = [scores(qh, j) for qh, j, _ in chains]
    terms = [log_terms(z, is_diag) for z, (_, _, is_diag) in zip(zs, chains)]
    nts = [jnp.dot(nl, tri, preferred_element_type=F32) for nl, _, _ in terms]
    lbs = [lb for _, lb, _ in terms]
    tots = [-(nt[:, 0:1] + nl0) for nt, (_, _, nl0) in zip(nts, terms)]
    ws = []
    for n in range(2 * nq):
        d, p = 2 * n, 2 * n + 1
        w_d = jnp.where(causal, jnp.exp2(lbs[d] - nts[d]), 0.0)
        w_p = jnp.exp2(lbs[p] - nts[p] + tots[d])
        if n < 2:
            w_p = jnp.where(has_prev, w_p, 0.0)
        ws += [w_d.astype(BF16), w_p.astype(BF16)]
    pvs = [jnp.dot(w, values(j), preferred_element_type=F32) for w, (_, j, _) in zip(ws, chains)]
    accs = [pvs[2 * n] + pvs[2 * n + 1] for n in range(2 * nq)]
    carries = [tots[2 * n] + tots[2 * n + 1] for n in range(2 * nq)]

    def cond(st):
        j, cs, _ = st
        live = functools.reduce(jnp.maximum, [jnp.max(c) for c in cs])
        return jnp.logical_and(j >= 0, live > ATTN_SKIP_LOG2)

    def body(st):
        j, cs, acs = st
        new_c, new_a = [], []
        for n, (qh, c, a) in enumerate(zip(qs, cs, acs)):
            lag = nq - 1 - n // 2
            if lag:
                c = jnp.where(j >= lag, c, -jnp.inf)
            lw, tot, vb = tile(qh, jnp.maximum(j - lag, 0), False)
            new_a.append(a + jnp.dot(jnp.exp2(lw + c).astype(BF16), vb, preferred_element_type=F32))
            new_c.append(c + tot)
        return j - 1, tuple(new_c), tuple(new_a)

    _, _, accs = lax.while_loop(cond, body, (nq * qi + nq - 3, tuple(carries), tuple(accs)))
    for r in range(nq):
        o_ref[0, r * tq:(r + 1) * tq, :] = jnp.where(
            lane < SB_HEAD_DIM, accs[2 * r], accs[2 * r + 1]).astype(o_ref.dtype)


def _attention(q, k, v):
    B, S, W = q.shape
    tq = ATTN_SUBTILES * ATTN_TILE
    n_pairs = W // LANES
    return pl.pallas_call(
        _attn_kernel,
        grid=(B, n_pairs, S // tq),
        in_specs=[
            pl.BlockSpec((1, tq, LANES), lambda b, p, i: (b, i, p)),
            pl.BlockSpec((1, S, LANES), lambda b, p, i: (b, 0, p)),
            pl.BlockSpec((1, S, LANES), lambda b, p, i: (b, 0, p)),
        ],
        out_specs=pl.BlockSpec((1, tq, LANES), lambda b, p, i: (b, i, p)),
        out_shape=jax.ShapeDtypeStruct((B, S, W), BF16),
        compiler_params=pltpu.CompilerParams(
            dimension_semantics=("arbitrary", "arbitrary", "arbitrary"), vmem_limit_bytes=VMEM_LIMIT),
        name="stickbreak_attn",
    )(q, k, v)


def _out_proj_router_kernel(attn_ref, pool_ref, x_ref, wo_ref, g_ref, wr_ref, br_ref, tri_ref, cin_ref,
                            *rest, capacity):
    x1_ref, h2_ref, gate_ref, dest_ref, cnt_ref, carry_ref = rest[-6:]
    i = pl.program_id(0)
    tm = x_ref.shape[0]

    @pl.when(i == 0)
    def _():
        carry_ref[...] = cin_ref[...]

    mixed = jnp.dot(attn_ref[...], wo_ref[0:SB_WIDTH, :], preferred_element_type=F32)
    mixed = mixed + jnp.dot(pool_ref[...], wo_ref[SB_WIDTH:, :], preferred_element_type=F32)
    x1 = x_ref[...] + mixed
    h2 = _rms(x1, g_ref[...])
    x1_ref[...] = _pack_rows(x1)
    h2_ref[...] = _pack_rows(h2)

    hh = h2.astype(BF16)
    hl = (h2 - hh.astype(F32)).astype(BF16)
    wr = wr_ref[...]
    wh = wr.astype(BF16)
    wl = (wr - wh.astype(F32)).astype(BF16)
    both = lax.dot_general(jnp.concatenate([wh, wl], axis=0), hh, _NT, preferred_element_type=F32)
    logits = (both[:N_EXPERTS] + both[N_EXPERTS:]
              + lax.dot_general(wh, hl, _NT, preferred_element_type=F32)) + br_ref[...]

    eid = lax.broadcasted_iota(I32, (N_EXPERTS, tm), 0).astype(F32)
    work = logits
    vals, ids = [], []
    for _ in range(TOP_K):
        m = jnp.max(work, axis=0, keepdims=True)
        sel = jnp.min(jnp.where(work == m, eid, float(N_EXPERTS)), axis=0, keepdims=True)
        vals.append(m)
        ids.append(sel)
        work = jnp.where(eid == sel, -jnp.inf, work)
    exps = [jnp.exp(v - vals[0]) for v in vals]
    denom = exps[0] + exps[1] + exps[2] + exps[3]

    onehot = jnp.zeros((N_EXPERTS, tm), F32)
    for sel in ids:
        onehot = onehot + (eid == sel).astype(F32)
    before = jnp.dot(onehot.astype(BF16), tri_ref[...], preferred_element_type=F32) + carry_ref[...]
    for kk in range(TOP_K):
        gate_ref[kk:kk + 1, :] = exps[kk] / denom
        rk = jnp.sum(jnp.where(eid == ids[kk], before, 0.0), axis=0, keepdims=True)
        dest_ref[kk:kk + 1, :] = (ids[kk] * float(capacity) + rk).astype(I32)
    carry_ref[...] = carry_ref[...] + jnp.sum(onehot, axis=1, keepdims=True)
    cnt_ref[...] = jnp.broadcast_to(carry_ref[...], cnt_ref.shape)


def _out_proj_router(attn, pool, x, weights, capacity, part, counts_in, x1_buf):
    T, D = x.shape
    tm = PROJ_TILE
    steps = T // tm // ROUTER_PARTS
    base = part * steps
    tp = steps * tm
    w_out, g_moe, w_router, b_router, tri = weights
    in_blk = lambda w: pl.BlockSpec((tm, w), lambda i: (i + base, 0))
    fixed = lambda shape: pl.BlockSpec(shape, lambda i: tuple(0 for _ in shape))
    sel_blk = pl.BlockSpec((TOP_K, tm), lambda i: (0, i))
    assert N_EXPERTS * capacity < 2 ** 24
    in_specs = [in_blk(SB_WIDTH), in_blk(POOL_WIDTH), in_blk(D), fixed((D, D)), fixed((1, D)),
                fixed((N_EXPERTS, D)), fixed((N_EXPERTS, 1)), fixed((tm, tm)), fixed((N_EXPERTS, 1))]
    args = [attn, pool, x, w_out, g_moe, w_router, b_router, tri, counts_in]
    aliases = {}
    if x1_buf is not None:
        in_specs.append(pl.BlockSpec(memory_space=pl.ANY))
        args.append(x1_buf)
        aliases = {len(args) - 1: 0}
    return pl.pallas_call(
        functools.partial(_out_proj_router_kernel, capacity=capacity),
        grid=(steps,),
        in_specs=in_specs,
        out_specs=[in_blk(D // 2), pl.BlockSpec((tm, D // 2), lambda i: (i, 0)), sel_blk, sel_blk,
                   fixed((N_EXPERTS, LANES))],
        out_shape=[jax.ShapeDtypeStruct((T, D // 2), U32), jax.ShapeDtypeStruct((tp, D // 2), U32),
                   jax.ShapeDtypeStruct((TOP_K, tp), F32), jax.ShapeDtypeStruct((TOP_K, tp), I32),
                   jax.ShapeDtypeStruct((N_EXPERTS, LANES), F32)],
        scratch_shapes=[pltpu.VMEM((N_EXPERTS, 1), F32)],
        input_output_aliases=aliases,
        compiler_params=pltpu.CompilerParams(
            dimension_semantics=("arbitrary",), vmem_limit_bytes=VMEM_LIMIT),
        name="out_proj_router",
    )(*args)


SC_ROWS_PER_COPY = 64


def _sc_workers():
    info = plsc.get_sparse_core_info()
    return info.num_cores, info.num_cores * info.num_subcores


def _sc_dispatch(rows, dest, n_out, into=None):
    T, W = rows.shape
    K = dest.shape[0]
    sub = SC_ROWS_PER_COPY
    n_cores, n_workers = _sc_workers()
    per_w = T // n_workers
    n_chunks = per_w // sub
    assert per_w * n_workers == T and n_chunks * sub == per_w and n_chunks % 2 == 0
    idx = dest.reshape(K, n_workers, n_chunks, sub).transpose(1, 2, 0, 3).reshape(n_workers, n_chunks * K, sub)
    mesh = plsc.VectorSubcoreMesh(core_axis_name="core", subcore_axis_name="subcore")

    scratch = [pltpu.VMEM((n_chunks * K, sub), I32), pltpu.VMEM((2, sub, W), rows.dtype),
               pltpu.SemaphoreType.DMA((2,)), pltpu.SemaphoreType.DMA((2,))]

    def scatter_rows(x_hbm, i_hbm, o_hbm, idx_v, buf, rsem, wsem):
        wid = lax.axis_index("subcore") * n_cores + lax.axis_index("core")
        base = wid * per_w
        pltpu.sync_copy(i_hbm.at[wid], idx_v)

        def read(c, slot):
            return pltpu.make_async_copy(x_hbm.at[pl.ds(base + c * sub, sub)], buf.at[slot], rsem.at[slot])

        def write(c, kk, slot):
            return pltpu.make_async_copy(buf.at[slot], o_hbm.at[idx_v.at[c * K + kk]], wsem.at[slot])

        read(0, 0).start()

        @pl.loop(0, n_chunks, step=2)
        def _(c0):
            for b in range(2):
                c = c0 + b
                read(c, b).wait()
                for kk in range(K):
                    write(c, kk, b).start()

                @pl.when(c + 1 < n_chunks)
                def _():
                    @pl.when(c >= 1)
                    def _():
                        for kk in range(K):
                            write(c - 1, kk, 1 - b).wait()
                    read(c + 1, 1 - b).start()

        for kk in range(K):
            write(n_chunks - 2, kk, 0).wait()
            write(n_chunks - 1, kk, 1).wait()

    if into is None:
        out_type = jax.ShapeDtypeStruct((n_out, W), rows.dtype)
        return pl.kernel(scatter_rows, out_type=out_type, mesh=mesh, scratch_types=scratch)(rows, idx)
    assert into.shape == (n_out, W) and into.dtype == rows.dtype
    out_ref = jax.new_ref(into)
    pl.kernel(scatter_rows, out_type=(), mesh=mesh, scratch_types=scratch)(rows, idx, out_ref)
    return jax.freeze(out_ref)


def _sc_gather(table, indices):
    M = indices.shape[0]
    W = table.shape[1]
    sub = SC_ROWS_PER_COPY
    n_cores, n_workers = _sc_workers()
    per_w = M // n_workers
    n_steps = per_w // sub
    assert per_w * n_workers == M and n_steps * sub == per_w and n_steps % 2 == 0
    mesh = plsc.VectorSubcoreMesh(core_axis_name="core", subcore_axis_name="subcore")

    @functools.partial(
        pl.kernel, out_type=jax.ShapeDtypeStruct((M, W), table.dtype), mesh=mesh,
        scratch_types=[pltpu.VMEM((n_steps, sub), I32), pltpu.VMEM((2, sub, W), table.dtype),
                       pltpu.SemaphoreType.DMA((2,)), pltpu.SemaphoreType.DMA((2,))])
    def gather_rows(x_hbm, i_hbm, o_hbm, idx_v, buf, gsem, wsem):
        wid = lax.axis_index("subcore") * n_cores + lax.axis_index("core")
        base = wid * per_w
        pltpu.sync_copy(i_hbm.at[wid], idx_v)

        def gather(s, slot):
            return pltpu.make_async_copy(x_hbm.at[idx_v.at[s]], buf.at[slot], gsem.at[slot])

        def write(s, slot):
            return pltpu.make_async_copy(buf.at[slot], o_hbm.at[pl.ds(base + s * sub, sub)], wsem.at[slot])

        gather(0, 0).start()

        @pl.loop(0, n_steps, step=2)
        def _(s0):
            for b in range(2):
                s = s0 + b
                gather(s, b).wait()
                write(s, b).start()

                @pl.when(s + 1 < n_steps)
                def _():
                    @pl.when(s >= 1)
                    def _():
                        write(s - 1, 1 - b).wait()
                    gather(s + 1, 1 - b).start()

        write(n_steps - 2, 0).wait()
        write(n_steps - 1, 1).wait()

    return gather_rows(table, indices.reshape(n_workers, n_steps, sub))


def _expert_kernel(vb_ref, ve_ref, vrows_ref, nvis_ref, nxt_ref,
                   x_ref, wgu_hbm, bgu_ref, wd_hbm, bd_ref, y_ref,
                   wgu_stage, wd_stage, wgu_bf, wd_bf, sem):
    v = pl.program_id(0)
    real = v < nvis_ref[0]
    e = ve_ref[v]
    first_of_expert = jnp.logical_or(v == 0, e != ve_ref[jnp.maximum(v - 1, 0)])

    def weight_copies(expert):
        return (pltpu.make_async_copy(wgu_hbm.at[expert], wgu_stage, sem.at[0]),
                pltpu.make_async_copy(wd_hbm.at[expert], wd_stage, sem.at[1]))

    @pl.when(v == 0)
    def _():
        for c in weight_copies(e):
            c.start()

    def load_weights():
        for c in weight_copies(e):
            c.wait()
        wgu_bf[...] = wgu_stage[...].astype(BF16)
        wd_bf[...] = wd_stage[...].astype(BF16)

    def fetch_next():
        nxt = nxt_ref[e]

        @pl.when(nxt >= 0)
        def _():
            for c in weight_copies(nxt):
                c.start()

    def run(n_rows):
        x_lo, x_hi = _unpack_rows(x_ref[0:n_rows, :])
        xb = jnp.concatenate([x_lo.astype(BF16), x_hi.astype(BF16)], axis=1)
        gu = jnp.dot(xb, wgu_bf[...], preferred_element_type=F32) + bgu_ref[0]
        gate = jnp.minimum(gu[:, :D_EXPERT], SWIGLU_LIMIT)
        up = jnp.clip(gu[:, D_EXPERT:], -SWIGLU_LIMIT, SWIGLU_LIMIT)
        act = gate * jax.nn.sigmoid(SWIGLU_ALPHA * gate) * (up + 1.0)
        y = jnp.dot(act.astype(BF16), wd_bf[...], preferred_element_type=F32) + bd_ref[0]
        y_ref[0:n_rows, :] = _pack_rows(y)

    rows = vrows_ref[v]
    full = EXPERT_ROW_CLASSES[0]
    first = jnp.logical_and(first_of_expert, real)

    @pl.when(jnp.logical_and(first, rows == full))
    def _():
        load_weights()
        run(full)
        fetch_next()

    @pl.when(jnp.logical_and(first, rows != full))
    def _():
        load_weights()
        fetch_next()

    later_full = jnp.logical_and(real, jnp.logical_not(first_of_expert))
    pl.when(jnp.logical_and(later_full, rows == full))(functools.partial(run, full))
    for n_rows in EXPERT_ROW_CLASSES[1:]:
        pl.when(jnp.logical_and(real, rows == n_rows))(functools.partial(run, n_rows))


def _experts(x_sorted, visits, w_gate_up, b_gate_up, w_down, b_down):
    n_rows, W = x_sorted.shape
    D = 2 * W
    tm = EXPERT_TILE
    n_visits = visits[0].shape[0]
    n_prefetch = len(visits)
    by_block = lambda v, vb, *_: (vb[v], 0)
    by_expert = lambda v, vb, ve, *_: (ve[v], 0, 0)
    grid_spec = pltpu.PrefetchScalarGridSpec(
        num_scalar_prefetch=n_prefetch,
        grid=(n_visits,),
        in_specs=[
            pl.BlockSpec((tm, W), by_block),
            pl.BlockSpec(memory_space=pl.ANY),
            pl.BlockSpec((1, 1, 2 * D_EXPERT), by_expert),
            pl.BlockSpec(memory_space=pl.ANY),
            pl.BlockSpec((1, 1, D), by_expert),
        ],
        out_specs=pl.BlockSpec((tm, W), by_block),
        scratch_shapes=[pltpu.VMEM((D, 2 * D_EXPERT), F32), pltpu.VMEM((D_EXPERT, D), F32),
                        pltpu.VMEM((D, 2 * D_EXPERT), BF16), pltpu.VMEM((D_EXPERT, D), BF16),
                        pltpu.SemaphoreType.DMA((2,))],
    )
    return pl.pallas_call(
        _expert_kernel,
        grid_spec=grid_spec,
        out_shape=jax.ShapeDtypeStruct((n_rows, W), U32),
        compiler_params=pltpu.CompilerParams(
            dimension_semantics=("arbitrary",), vmem_limit_bytes=VMEM_LIMIT),
        name="expert_gmm",
    )(*visits, x_sorted, w_gate_up, b_gate_up.reshape(N_EXPERTS, 1, -1), w_down, b_down.reshape(N_EXPERTS, 1, -1))


def _visit_schedule(counts, n_assign, capacity):
    tm = EXPERT_TILE
    assert EXPERT_ROW_CLASSES[0] == tm and list(EXPERT_ROW_CLASSES) == sorted(EXPERT_ROW_CLASSES, reverse=True)
    n_visits = n_assign // tm + N_EXPERTS
    n_full = counts // tm
    tail = counts - n_full * tm
    tail_rows = jnp.zeros_like(tail)
    for rows in EXPERT_ROW_CLASSES:
        tail_rows = jnp.where(tail <= rows, rows, tail_rows)
    per_e = n_full + (tail > 0)
    vend = jnp.cumsum(per_e)
    nvis = vend[-1]
    vc = jnp.minimum(jnp.arange(n_visits, dtype=I32), nvis - 1)
    done = vc[:, None] >= vend[None, :]
    e = jnp.sum(done, axis=1).astype(I32)
    local = vc - jnp.sum(jnp.where(done, per_e[None, :], 0), axis=1)
    mine = e[:, None] == jnp.arange(N_EXPERTS, dtype=I32)[None, :]
    is_tail = local >= jnp.sum(jnp.where(mine, n_full[None, :], 0), axis=1)
    vrows = jnp.where(is_tail, jnp.sum(jnp.where(mine, tail_rows[None, :], 0), axis=1), tm)
    blk = e * (capacity // tm) + local
    ids = jnp.arange(N_EXPERTS, dtype=I32)
    nonempty = counts > 0
    later = jnp.logical_and(nonempty[None, :], ids[None, :] > ids[:, None])
    nxt = jnp.where(jnp.any(later, axis=1), jnp.argmax(later, axis=1), -1).astype(I32)
    return blk.astype(I32), e, vrows.astype(I32), nvis.reshape(1).astype(I32), nxt


def _combine_kernel(y0_ref, y1_ref, y2_ref, y3_ref, x1_ref, gate_ref, g_ref, o_ref):
    half = x1_ref.shape[1]
    gates = gate_ref[...]
    lo, hi = _unpack_rows(x1_ref[...])
    for kk, y_ref in enumerate((y0_ref, y1_ref, y2_ref, y3_ref)):
        y_lo, y_hi = _unpack_rows(y_ref[...])
        lo = lo + gates[:, kk:kk + 1] * y_lo
        hi = hi + gates[:, kk:kk + 1] * y_hi
    ms = (jnp.sum(lo * lo, axis=-1, keepdims=True) + jnp.sum(hi * hi, axis=-1, keepdims=True)) / (2 * half)
    scale = lax.rsqrt(ms + RMS_EPS)
    o_ref[:, :half] = lo * scale * g_ref[:, :half]
    o_ref[:, half:] = hi * scale * g_ref[:, half:]


def _combine(y_tok, x1, gates_t, g_final):
    T, D = x1.shape[0], 2 * x1.shape[1]
    tm = COMBINE_TILE
    nt = T // tm
    y_spec = lambda kk: pl.BlockSpec((tm, D // 2), lambda i: (kk * nt + i, 0))
    return pl.pallas_call(
        _combine_kernel,
        grid=(nt,),
        in_specs=[y_spec(0), y_spec(1), y_spec(2), y_spec(3),
                  pl.BlockSpec((tm, D // 2), lambda i: (i, 0)),
                  pl.BlockSpec((tm, TOP_K), lambda i: (i, 0)),
                  pl.BlockSpec((1, D), lambda i: (0, 0))],
        out_specs=pl.BlockSpec((tm, D), lambda i: (i, 0)),
        out_shape=jax.ShapeDtypeStruct((T, D), F32),
        compiler_params=pltpu.CompilerParams(
            dimension_semantics=("arbitrary",), vmem_limit_bytes=VMEM_LIMIT),
        name="combine_norm",
    )(y_tok, y_tok, y_tok, y_tok, x1, gates_t, g_final.reshape(1, D))


def kernel(x, g_mix, w_in, pool_w, pool_scale, w_out, g_moe, w_router, b_router, w_gate_up, b_gate_up,
           w_down, b_down, g_final):
    B, S, D = x.shape
    T = B * S
    assert g_mix.shape[0] == 1, "single-layer problem: the final norm is fused into the combine step"
    q, k, v, pool = _in_proj(x, g_mix[0], w_in[0], pool_w[0], pool_scale[0])
    attn = _attention(q, k, v)
    capacity = T
    tm = PROJ_TILE
    tri = (lax.broadcasted_iota(I32, (tm, tm), 0) < lax.broadcasted_iota(I32, (tm, tm), 1)).astype(BF16)
    weights = (w_out[0].astype(BF16), g_moe[0].reshape(1, D), w_router[0].T, b_router[0].reshape(N_EXPERTS, 1), tri)
    x1 = x_sorted = None
    counts = jnp.zeros((N_EXPERTS, 1), F32)
    gates, dest = [], []
    for part in range(ROUTER_PARTS):
        x1, h2, gates_p, dest_p, cnt = _out_proj_router(
            attn.reshape(T, SB_WIDTH), pool.reshape(T, POOL_WIDTH), x.reshape(T, D), weights, capacity,
            part, counts, x1)
        x_sorted = _sc_dispatch(h2, dest_p, N_EXPERTS * capacity, x_sorted)
        counts = cnt[:, :1]
        gates.append(gates_p)
        dest.append(dest_p)
    gates = jnp.concatenate(gates, axis=1)
    dest = jnp.concatenate(dest, axis=1)
    visits = _visit_schedule(cnt[:, 0].astype(I32), T * TOP_K, capacity)
    y = _experts(x_sorted, visits, w_gate_up[0], b_gate_up[0], w_down[0], b_down[0])
    y_tok = _sc_gather(y, dest.reshape(-1))
    return _combine(y_tok, x1, gates.T, g_final).reshape(B, S, D)
```
